```python
import math
import jax
import jax.numpy as jnp
from jax import lax
import numpy as np

D_MODEL = 1024
BATCH = 32
SEQ = 256
DEPTH = 1
DEC_BATCH = 4
DEC_SEQ = 1024
PAST_LEN = 256

GRID_W = 64
RET_HEADS = 4
RET_DK = D_MODEL // 8
RET_DV = D_MODEL // 8
RET_QK = RET_HEADS * RET_DK
RET_WIDTH = RET_HEADS * RET_DV
RET_CHUNK = 128
HY_WIDTH = D_MODEL // 2
HY_ORDER = 2
HY_EMB_BANDS = 16
HY_EMB = 1 + 2 * HY_EMB_BANDS
HY_FILTER_HIDDEN = 64
HY_FAST_DECAY = 0.3
HY_SLOW_DECAY = 1.5
HY_TARGET = 1e-2
SHORT_CONV = 3
D_FF = -(-8 * D_MODEL // (3 * 256)) * 256
N_IN = 2 * RET_QK + 2 * RET_WIDTH + 3 * HY_WIDTH + 2 * D_MODEL
N_MOD = 6
EPS = 1e-6

kernel_name = 'retention_hyena_flow_step'


def rmsnorm(x, g):
    x32 = x.astype(jnp.float32)
    y = x32 * lax.rsqrt(jnp.mean(x32 * x32, axis=-1, keepdims=True) + EPS)
    return (y * g.astype(jnp.float32)).astype(x.dtype)


def ada_mod(cond, w, b):
    m = jax.nn.silu(cond) @ w + b
    return jnp.split(m[..., None, :], N_MOD, axis=-1)


def retention_chunkwise(q, k, v, gamma, s0):
    B, H, L, _ = q.shape
    n = L // RET_CHUNK

    def chunks(t):
        return jnp.moveaxis(t.reshape(B, H, n, RET_CHUNK, t.shape[-1]), 2, 0)

    log_g = jnp.log(gamma)
    pos = jnp.arange(RET_CHUNK, dtype=jnp.float32)
    rel = pos[:, None] - pos[None, :]
    lower = rel >= 0
    dmat = jnp.where(lower[None], jnp.exp(jnp.where(lower, rel, 0.0)[None] * log_g[:, None, None]), 0.0)
    q_decay = jnp.exp((pos + 1.0)[None] * log_g[:, None])
    k_decay = jnp.exp((RET_CHUNK - 1.0 - pos)[None] * log_g[:, None])
    chunk_decay = jnp.exp(RET_CHUNK * log_g)

    def step(s, qkv):
        qc, kc, vc = qkv
        att = jnp.einsum('bhid,bhjd->bhij', qc, kc) * dmat
        o = (jnp.einsum('bhij,bhjv->bhiv', att, vc)
             + jnp.einsum('bhid,bhdv->bhiv', qc, s) * q_decay[None, :, :, None])
        s = (s * chunk_decay[None, :, None, None]
             + jnp.einsum('bhjd,bhjv->bhdv', kc * k_decay[None, :, :, None], vc))
        return s, o

    s_final, o = lax.scan(step, s0, (chunks(q), chunks(k), chunks(v)))
    o = jnp.moveaxis(o, 0, 2).reshape(B, H, L, v.shape[-1])
    return o, s_final


def bidir_retention(q, k, v, gamma_f, gamma_b, s0_f, s0_b):
    o_f, s_f = retention_chunkwise(q, k, v, gamma_f, s0_f)
    o_b, s_b = retention_chunkwise(jnp.flip(q, 2), jnp.flip(k, 2), jnp.flip(v, 2), gamma_b, s0_b)
    return o_f + jnp.flip(o_b, 2), s_f, s_b


def head_groupnorm(o):
    mu = jnp.mean(o, axis=-1, keepdims=True)
    var = jnp.mean(jnp.square(o - mu), axis=-1, keepdims=True)
    return (o - mu) * lax.rsqrt(var + EPS)


def short_conv3(u, w, b, grid_w):
    B, L, C = u.shape
    if grid_w is not None:
        rows = L // grid_w
        u = u.reshape(B, rows, grid_w, C)
    pad = [(0, 0)] * (u.ndim - 2) + [(1, 1), (0, 0)]
    up = jnp.pad(u, pad)
    n = u.shape[-2]
    y = up[..., 0:n, :] * w[0] + up[..., 1:n + 1, :] * w[1] + up[..., 2:n + 2, :] * w[2] + b
    return y.reshape(B, L, C)


def hyena_filters(L, w1, b1, w2, b2, w3, freq):
    t = jnp.linspace(0.0, 1.0, L, dtype=jnp.float32)[:, None]
    ang = 2.0 * math.pi * jnp.arange(L, dtype=jnp.float32)[:, None] / L
    bands = jnp.linspace(1e-4, HY_EMB_BANDS - 1, HY_EMB_BANDS, dtype=jnp.float32)[None]
    z = jnp.concatenate([t, jnp.cos(bands * ang), -jnp.sin(bands * ang)], axis=-1)
    h = jnp.sin(freq * (z @ w1 + b1))
    h = jnp.sin(freq * (h @ w2 + b2))
    h = (h @ w3).reshape(L, HY_ORDER, 2, HY_WIDTH)
    max_decay = math.log(HY_TARGET) / HY_FAST_DECAY
    min_decay = math.log(HY_TARGET) / HY_SLOW_DECAY
    deltas = jnp.linspace(min_decay, max_decay, HY_WIDTH, dtype=jnp.float32)
    window = jnp.exp(-t * jnp.abs(deltas))
    h = h * window[:, None, None, :]
    fwd = h[:, :, 0]
    bwd = h[:, :, 1]
    k = jnp.concatenate([fwd, jnp.zeros_like(fwd[:1]), jnp.flip(bwd[1:], axis=0)], axis=0)
    k = k / jnp.sum(jnp.abs(k), axis=0, keepdims=True)
    return jnp.moveaxis(k, 1, 0)


def long_conv(u, k, bias):
    L = u.shape[1]
    uf = jnp.fft.rfft(u, n=2 * L, axis=1)
    kf = jnp.fft.rfft(k, n=2 * L, axis=0)
    y = jnp.fft.irfft(uf * kf[None], n=2 * L, axis=1)[:, :L]
    return y + u * bias


def trunk_layer(x, shift1, scale1, gate1, shift2, scale2, gate2, s0_fwd, s0_bwd, grid_w,
                norm1, norm2, w_in, decay_fwd, decay_bwd, conv_w, conv_b,
                pos_w1, pos_b1, pos_w2, pos_b2, pos_w3, sin_freq, hy_bias,
                w_ret_o, w_hy_o, w_out, w_ffn_in, w_ffn_out):
    B, L, _ = x.shape
    f32 = jnp.float32
    h = rmsnorm(x, norm1) * (1 + scale1) + shift1
    proj = h @ w_in
    splits = list(np.cumsum([RET_QK, RET_QK, RET_WIDTH, RET_WIDTH, 3 * HY_WIDTH, D_MODEL]))
    q, k, v, g, hu, g_ret, g_hy = jnp.split(proj, splits, axis=-1)

    def heads(t, d):
        return t.reshape(B, L, RET_HEADS, d).transpose(0, 2, 1, 3).astype(f32)
    qh = heads(q, RET_DK)
    kh = heads(k, RET_DK) * (RET_DK ** -0.5)
    vh = heads(v, RET_DV)
    gamma_f = jax.nn.sigmoid(decay_fwd.astype(f32))
    gamma_b = jax.nn.sigmoid(decay_bwd.astype(f32))
    o, s_f, s_b = bidir_retention(qh, kh, vh, gamma_f, gamma_b, s0_fwd.astype(f32), s0_bwd.astype(f32))
    o = head_groupnorm(o).transpose(0, 2, 1, 3).reshape(B, L, RET_WIDTH).astype(x.dtype)
    y_ret = (jax.nn.silu(g) * o) @ w_ret_o

    u = short_conv3(hu, conv_w, conv_b, grid_w).astype(f32)
    hv, hx1, hx2 = jnp.split(u, 3, axis=-1)
    kers = hyena_filters(L, pos_w1.astype(f32), pos_b1.astype(f32), pos_w2.astype(f32),
                         pos_b2.astype(f32), pos_w3.astype(f32), sin_freq.astype(f32))
    hb = hy_bias.astype(f32)
    z = hx1 * long_conv(hv, kers[0], hb[0])
    z = hx2 * long_conv(z, kers[1], hb[1])
    y_hy = z.astype(x.dtype) @ w_hy_o

    mix = jax.nn.sigmoid(g_ret) * y_ret + jax.nn.sigmoid(g_hy) * y_hy
    x = x + gate1 * (mix @ w_out)

    h2 = rmsnorm(x, norm2) * (1 + scale2) + shift2
    a, bgt = jnp.split(h2 @ w_ffn_in, 2, axis=-1)
    x = x + gate2 * ((jax.nn.silu(a) * bgt) @ w_ffn_out)
    return x, s_f, s_b


def setup_inputs(seed: int = 0) -> dict:
    key = jax.random.key(seed)
    ks = jax.random.split(key, 32)
    f32 = jnp.float32

    def nrm(k, shape, scale):
        return jax.random.normal(k, shape, f32) * scale

    st_shape = (DEC_BATCH, DEPTH, RET_HEADS, RET_DK, RET_DV)
    decay_init = jnp.log(2.0 ** (5.0 + jnp.arange(RET_HEADS, dtype=f32)) - 1.0)
    return {
        'x_prompt': nrm(ks[0], (BATCH, SEQ, D_MODEL), 1.0),
        'x_sample': nrm(ks[1], (DEC_BATCH, DEC_SEQ, D_MODEL), 1.0),
        'state_ret_fwd': nrm(ks[2], st_shape, 0.5),
        'state_ret_bwd': nrm(ks[3], st_shape, 0.5),
        'c': nrm(ks[4], (DEC_BATCH, D_MODEL), 1.0),
        'c_ctx': nrm(ks[5], (D_MODEL,), 1.0),
        'norm1_g': 1.0 + nrm(ks[6], (DEPTH, D_MODEL), 0.02),
        'norm2_g': 1.0 + nrm(ks[7], (DEPTH, D_MODEL), 0.02),
        'w_ada': nrm(ks[8], (DEPTH, D_MODEL, N_MOD * D_MODEL), 0.3 * D_MODEL ** -0.5),
        'b_ada': nrm(ks[9], (DEPTH, N_MOD * D_MODEL), 0.02),
        'w_in': nrm(ks[10], (DEPTH, D_MODEL, N_IN), D_MODEL ** -0.5),
        'ret_decay_fwd': decay_init[None] + nrm(ks[11], (DEPTH, RET_HEADS), 0.1),
        'ret_decay_bwd': decay_init[None] + nrm(ks[12], (DEPTH, RET_HEADS), 0.1),
        'hy_conv_w': nrm(ks[13], (DEPTH, SHORT_CONV, 3 * HY_WIDTH), SHORT_CONV ** -0.5),
        'hy_conv_b': nrm(ks[14], (DEPTH, 3 * HY_WIDTH), 0.02),
        'hy_pos_w1': nrm(ks[15], (DEPTH, HY_EMB, HY_FILTER_HIDDEN), HY_EMB ** -0.5),
        'hy_pos_b1': nrm(ks[16], (DEPTH, HY_FILTER_HIDDEN), 0.1),
        'hy_pos_w2': nrm(ks[17], (DEPTH, HY_FILTER_HIDDEN, HY_FILTER_HIDDEN), HY_FILTER_HIDDEN ** -0.5),
        'hy_pos_b2': nrm(ks[18], (DEPTH, HY_FILTER_HIDDEN), 0.1),
        'hy_pos_w3': nrm(ks[19], (DEPTH, HY_FILTER_HIDDEN, HY_ORDER * 2 * HY_WIDTH), HY_FILTER_HIDDEN ** -0.5),
        'hy_sin_freq': 1.0 + nrm(ks[20], (DEPTH, HY_FILTER_HIDDEN), 0.1),
        'hy_bias': nrm(ks[21], (DEPTH, HY_ORDER, HY_WIDTH), 0.1),
        'w_ret_o': nrm(ks[22], (DEPTH, RET_WIDTH, D_MODEL), RET_WIDTH ** -0.5),
        'w_hy_o': nrm(ks[23], (DEPTH, HY_WIDTH, D_MODEL), HY_WIDTH ** -0.5),
        'w_out': nrm(ks[24], (DEPTH, D_MODEL, D_MODEL), D_MODEL ** -0.5),
        'w_ffn_in': nrm(ks[25], (DEPTH, D_MODEL, 2 * D_FF), D_MODEL ** -0.5),
        'w_ffn_out': nrm(ks[26], (DEPTH, D_FF, D_MODEL), D_FF ** -0.5),
        'final_g': 1.0 + nrm(ks[27], (D_MODEL,), 0.02),
    }


def reference(x_prompt, x_sample, state_ret_fwd, state_ret_bwd, c, c_ctx,
              norm1_g, norm2_g, w_ada, b_ada, w_in, ret_decay_fwd, ret_decay_bwd,
              hy_conv_w, hy_conv_b, hy_pos_w1, hy_pos_b1, hy_pos_w2, hy_pos_b2, hy_pos_w3,
              hy_sin_freq, hy_bias, w_ret_o, w_hy_o, w_out, w_ffn_in, w_ffn_out, final_g):
    zero_state = jnp.zeros((x_prompt.shape[0], RET_HEADS, RET_DK, RET_DV), jnp.float32)
    xp = x_prompt
    new_f = []
    new_b = []
    for l in range(DEPTH):
        sh1, sc1, g1, sh2, sc2, g2 = ada_mod(c_ctx, w_ada[l], b_ada[l])
        xp, s_f, s_b = trunk_layer(
            xp, sh1, sc1, g1, sh2, sc2, g2, zero_state, zero_state, None,
            norm1_g[l], norm2_g[l], w_in[l], ret_decay_fwd[l], ret_decay_bwd[l],
            hy_conv_w[l], hy_conv_b[l], hy_pos_w1[l], hy_pos_b1[l], hy_pos_w2[l], hy_pos_b2[l],
            hy_pos_w3[l], hy_sin_freq[l], hy_bias[l], w_ret_o[l], w_hy_o[l], w_out[l],
            w_ffn_in[l], w_ffn_out[l])
        new_f.append(s_f)
        new_b.append(s_b)
    y_prompt = rmsnorm(xp, final_g)
    new_state_ret_fwd = jnp.stack(new_f, axis=1).astype(x_prompt.dtype)
    new_state_ret_bwd = jnp.stack(new_b, axis=1).astype(x_prompt.dtype)

    xs = x_sample
    for l in range(DEPTH):
        sh1, sc1, g1, sh2, sc2, g2 = ada_mod(c, w_ada[l], b_ada[l])
        xs, _, _ = trunk_layer(
            xs, sh1, sc1, g1, sh2, sc2, g2, state_ret_fwd[:, l], state_ret_bwd[:, l], GRID_W,
            norm1_g[l], norm2_g[l], w_in[l], ret_decay_fwd[l], ret_decay_bwd[l],
            hy_conv_w[l], hy_conv_b[l], hy_pos_w1[l], hy_pos_b1[l], hy_pos_w2[l], hy_pos_b2[l],
            hy_pos_w3[l], hy_sin_freq[l], hy_bias[l], w_ret_o[l], w_hy_o[l], w_out[l],
            w_ffn_in[l], w_ffn_out[l])
    y_sample = rmsnorm(xs, final_g)
    return (y_prompt, y_sample, new_state_ret_fwd, new_state_ret_bwd)
```

```python
import functools
import math

import numpy as np
import jax
import jax.numpy as jnp
from jax import lax
from jax.experimental import pallas as pl
from jax.experimental.pallas import tpu as pltpu

F32 = jnp.float32
BF16 = jnp.bfloat16

D_MODEL = 1024
RET_HEADS = 4
HEAD_DIM = 128
RET_W = RET_HEADS * HEAD_DIM
HY_W = 512
HY_ORDER = 2
HY_BANDS = 16
HY_EMB = 1 + 2 * HY_BANDS
HY_EMB_PAD = 40
HY_HIDDEN = 64
HY_FAST_DECAY = 0.3
HY_SLOW_DECAY = 1.5
HY_TARGET = 1e-2
D_FF = 2816
N_QKVG = 4 * RET_W
N_HY = 3 * HY_W
N_GATE = 2 * D_MODEL
N_MOD = 6
EPS = 1e-6
GRID_W = 64
RET_CHUNK = 256
HY_CBLK = 256
MLP_ROWS = 256
ADA_COLS = 768
VMEM_LIMIT = 56 * 1024 * 1024


def _const_spec(shape):
    nd = len(shape)
    return pl.BlockSpec(shape, lambda *_: (0,) * nd, pipeline_mode=pl.Buffered(1))


def _params(n_axes):
    return pltpu.CompilerParams(dimension_semantics=("arbitrary",) * n_axes,
                                vmem_limit_bytes=VMEM_LIMIT)


def _modnorm(x, g, scale, shift):
    ms = jnp.mean(x * x, axis=-1, keepdims=True)
    return (x * lax.rsqrt(ms + EPS) * g) * (1.0 + scale) + shift


def _dot(a, b):
    return jnp.dot(a, b, preferred_element_type=F32)


def _ada_kernel(c_ref, w_ref, b_ref, o_ref):
    c = c_ref[...]
    s = (c * jax.nn.sigmoid(c)).astype(BF16)
    o_ref[...] = _dot(s, w_ref[...].astype(BF16)) + b_ref[...]


def _ada(cond8, w, b):
    n = w.shape[1]
    return pl.pallas_call(
        _ada_kernel,
        grid=(n // ADA_COLS,),
        in_specs=[pl.BlockSpec((8, D_MODEL), lambda j: (0, 0)),
                  pl.BlockSpec((D_MODEL, ADA_COLS), lambda j: (0, j)),
                  pl.BlockSpec((1, ADA_COLS), lambda j: (0, j))],
        out_specs=pl.BlockSpec((8, ADA_COLS), lambda j: (0, j)),
        out_shape=jax.ShapeDtypeStruct((8, n), F32),
        compiler_params=_params(1),
        name="ada",
    )(cond8, w, b)


@functools.lru_cache(maxsize=None)
def _dft_mats(L):
    n = 2 * L
    t = np.arange(L, dtype=np.int64)
    f = np.arange(L, dtype=np.int64)
    ang = 2.0 * np.pi * ((f[:, None] * t[None, :]) % n).astype(np.float64) / n
    cos = np.cos(ang)
    sin = np.sin(ang)
    nyq = np.where(t % 2 == 0, 1.0, -1.0)
    fwd = np.concatenate([cos, -sin], axis=0)
    fwd[L] = nyq
    wre = np.full((L,), 2.0 / n)
    wre[0] = 1.0 / n
    inv = np.concatenate([cos.T * wre[None, :], -sin.T * (2.0 / n)], axis=1)
    inv[:, L] = nyq / n
    sgn = np.broadcast_to(nyq[:, None], (L, HY_W))
    return (jnp.asarray(fwd, dtype=BF16), jnp.asarray(inv, dtype=BF16),
            np.asarray(sgn, dtype=np.float32))


@functools.lru_cache(maxsize=None)
def _filter_consts(L):
    t = np.linspace(0.0, 1.0, L)[:, None]
    ang = 2.0 * np.pi * np.arange(L, dtype=np.float64)[:, None] / L
    bands = np.linspace(1e-4, HY_BANDS - 1, HY_BANDS)[None]
    z = np.concatenate([t, np.cos(bands * ang), -np.sin(bands * ang)], axis=-1)
    z = np.pad(z, ((0, 0), (0, HY_EMB_PAD - HY_EMB)))
    max_decay = math.log(HY_TARGET) / HY_FAST_DECAY
    min_decay = math.log(HY_TARGET) / HY_SLOW_DECAY
    deltas = np.linspace(min_decay, max_decay, HY_W)
    tdel = t * np.abs(deltas)[None, :]
    return np.asarray(z, np.float32), np.asarray(tdel, np.float32)


def _filter_kernel(z_ref, tdel_ref, sgn_ref, w1_ref, b1_ref, w2_ref, b2_ref, w3_ref, fr_ref,
                   fw_ref, o_ref, *, L):
    hi = lax.Precision.HIGHEST
    fr = fr_ref[...]
    h = jnp.sin(fr * (jnp.dot(z_ref[...], w1_ref[...], precision=hi,
                              preferred_element_type=F32) + b1_ref[...]))
    h = jnp.sin(fr * (jnp.dot(h, w2_ref[...], precision=hi,
                              preferred_element_type=F32) + b2_ref[...]))
    h = jnp.dot(h, w3_ref[...], precision=hi, preferred_element_type=F32)
    win = jnp.exp(-tdel_ref[...])
    sgn = sgn_ref[...]
    row0 = lax.broadcasted_iota(jnp.int32, (L, HY_W), 0) == 0
    for o in range(HY_ORDER):
        base = o * 2 * HY_W
        fwd = h[:, base:base + HY_W] * win
        bwd = jnp.where(row0, 0.0, h[:, base + HY_W:base + 2 * HY_W] * win)
        nrm = (jnp.sum(jnp.abs(fwd), axis=0, keepdims=True)
               + jnp.sum(jnp.abs(bwd), axis=0, keepdims=True))
        inv = 1.0 / nrm
        sm = (fwd + bwd) * inv
        df = (fwd - bwd) * inv
        top = _dot(fw_ref[0:L, :], sm.astype(BF16))
        bot = _dot(fw_ref[L:2 * L, :], df.astype(BF16))
        nyq = jnp.sum(sgn * sm, axis=0, keepdims=True)
        o_ref[o, 0] = top
        o_ref[o, 1] = jnp.where(row0, 0.0, bot)
        o_ref[o, 2] = jnp.where(row0, nyq, top)


def _filters(L, fw, sgn, w1, b1, w2, b2, w3, freq):
    z, tdel = _filter_consts(L)
    args = (jnp.asarray(z), jnp.asarray(tdel), jnp.asarray(sgn), w1, b1, w2, b2, w3, freq, fw)
    return pl.pallas_call(
        functools.partial(_filter_kernel, L=L),
        out_shape=jax.ShapeDtypeStruct((HY_ORDER, 3, L, HY_W), F32),
        compiler_params=pltpu.CompilerParams(vmem_limit_bytes=VMEM_LIMIT),
        name=f"filters{L}",
    )(*args)


def _ret_kernel(*refs, L, C, has_init, emit_state):
    it = iter(refs)
    x_ref, mod_ref, n1_ref, w_ref, dec_ref = (next(it) for _ in range(5))
    s0f_ref = s0b_ref = sf_ref = sb_ref = None
    if has_init:
        s0f_ref, s0b_ref = next(it), next(it)
    wo_ref, y_ref = next(it), next(it)
    if emit_state:
        sf_ref, sb_ref = next(it), next(it)
    mask_scr, vec_scr, cd_scr, g_scr = (next(it) for _ in range(4))
    n = L // C
    H, E = RET_HEADS, HEAD_DIM
    scale = float(E) ** -0.5

    @pl.when(pl.program_id(0) == 0)
    def _():
        lg = jnp.log(jax.nn.sigmoid(dec_ref[...]))
        cd_scr[...] = jnp.exp(float(C) * lg[:, 0:E])
        ii = lax.broadcasted_iota(jnp.int32, (C, C), 0)
        jj = lax.broadcasted_iota(jnp.int32, (C, C), 1)
        rel = (ii - jj).astype(F32)
        ri = lax.broadcasted_iota(jnp.int32, (C, E), 0).astype(F32)
        for h in range(H):
            lf = lg[h:h + 1, :]
            lb = lg[H + h:H + h + 1, :]
            mf = jnp.where(rel >= 0, jnp.exp(jnp.maximum(rel, 0.0) * lf), 0.0)
            mb = jnp.where(rel <= 0, jnp.exp(jnp.maximum(-rel, 0.0) * lb), 0.0)
            mask_scr[h] = scale * (mf + mb)
            lfe, lbe = lf[:, 0:E], lb[:, 0:E]
            vec_scr[h, 0] = jnp.exp((ri + 1.0) * lfe)
            vec_scr[h, 1] = jnp.exp((float(C) - ri) * lbe)
            vec_scr[h, 2] = scale * jnp.exp((float(C) - 1.0 - ri) * lfe)
            vec_scr[h, 3] = scale * jnp.exp(ri * lbe)

    m = mod_ref[0]
    hn = _modnorm(x_ref[0], n1_ref[...], m[1:2], m[0:1]).astype(BF16)
    qkvg = _dot(hn, w_ref[...])
    tdims = (((0,), (0,)), ((), ()))
    ndims = (((1,), (1,)), ((), ()))

    for h in range(H):
        q = qkvg[:, h * E:(h + 1) * E]
        k = qkvg[:, RET_W + h * E:RET_W + (h + 1) * E]
        v = qkvg[:, 2 * RET_W + h * E:2 * RET_W + (h + 1) * E]
        g = qkvg[:, 3 * RET_W + h * E:3 * RET_W + (h + 1) * E]
        qb = q.astype(BF16)
        kb = k.astype(BF16)
        vb = v.astype(BF16)
        mask = mask_scr[h]
        dqf, dqb, dkf, dkb = vec_scr[h, 0], vec_scr[h, 1], vec_scr[h, 2], vec_scr[h, 3]
        cdf = cd_scr[h:h + 1, :]
        cdb = cd_scr[H + h:H + h + 1, :]
        o_c, kvf, kvb = [], [], []
        for c in range(n):
            r = slice(c * C, (c + 1) * C)
            a = lax.dot_general(qb[r], kb[r], ndims, preferred_element_type=F32)
            o_c.append(_dot((a * mask).astype(BF16), vb[r]))
            kvf.append(lax.dot_general((k[r] * dkf).astype(BF16), vb[r], tdims,
                                       preferred_element_type=F32))
            kvb.append(lax.dot_general((k[r] * dkb).astype(BF16), vb[r], tdims,
                                       preferred_element_type=F32))
        s = s0f_ref[0, h] if has_init else None
        for c in range(n):
            if s is not None:
                o_c[c] = o_c[c] + _dot(qb[c * C:(c + 1) * C], s.astype(BF16)) * dqf
            s = kvf[c] if s is None else s * cdf + kvf[c]
        if emit_state:
            sf_ref[0, h] = s
        s = s0b_ref[0, h] if has_init else None
        for c in range(n - 1, -1, -1):
            if s is not None:
                o_c[c] = o_c[c] + _dot(qb[c * C:(c + 1) * C], s.astype(BF16)) * dqb
            s = kvb[c] if s is None else s * cdb + kvb[c]
        if emit_state:
            sb_ref[0, h] = s
        for c in range(n):
            r = slice(c * C, (c + 1) * C)
            o = o_c[c]
            mu = jnp.mean(o, axis=-1, keepdims=True)
            d = o - mu
            var = jnp.mean(d * d, axis=-1, keepdims=True)
            on = d * lax.rsqrt(var + EPS)
            gg = g[r]
            g_scr[r, h * E:(h + 1) * E] = (gg * jax.nn.sigmoid(gg) * on).astype(BF16)

    y_ref[0] = _dot(g_scr[...], wo_ref[...])


def _retention(x, mods3, mod_row, norm1, w_qkvg, dec8, s0f, s0b, w_o, *, emit_state):
    B, L, D = x.shape
    C = min(RET_CHUNK, L)
    has_init = s0f is not None
    H, E = RET_HEADS, HEAD_DIM
    in_specs = [pl.BlockSpec((1, L, D), lambda b: (b, 0, 0)),
                pl.BlockSpec((1, N_MOD, D), lambda b: (mod_row(b), 0, 0)),
                _const_spec((1, D)),
                _const_spec((D, N_QKVG)),
                _const_spec((8, C))]
    args = [x, mods3, norm1, w_qkvg, dec8[:, :C]]
    st_spec = pl.BlockSpec((1, H, E, E), lambda b: (b, 0, 0, 0))
    if has_init:
        in_specs += [st_spec, st_spec]
        args += [s0f, s0b]
    in_specs.append(_const_spec((RET_W, D)))
    args.append(w_o)
    out_specs = [pl.BlockSpec((1, L, D), lambda b: (b, 0, 0))]
    out_shape = [jax.ShapeDtypeStruct((B, L, D), F32)]
    if emit_state:
        out_specs += [st_spec, st_spec]
        out_shape += [jax.ShapeDtypeStruct((B, H, E, E), F32)] * 2
    return pl.pallas_call(
        functools.partial(_ret_kernel, L=L, C=C, has_init=has_init, emit_state=emit_state),
        grid=(B,),
        in_specs=in_specs,
        out_specs=out_specs,
        out_shape=out_shape,
        scratch_shapes=[pltpu.VMEM((H, C, C), F32),
                        pltpu.VMEM((H, 4, C, E), F32),
                        pltpu.VMEM((8, E), F32),
                        pltpu.VMEM((L, RET_W), BF16)],
        compiler_params=_params(1),
        name=f"retention{L}",
    )(*args)


def _hy_kernel(x_ref, mod_ref, n1_ref, w_ref, cw_ref, cb_ref, fw_ref, bw_ref, filt_ref,
               hb_ref, wo_ref, y_ref, *, L, W):
    m = mod_ref[0]
    hn = _modnorm(x_ref[0], n1_ref[...], m[1:2], m[0:1]).astype(BF16)
    CB = HY_CBLK
    pos = lax.broadcasted_iota(jnp.int32, (L, CB), 0) % W
    first = pos == 0
    last = pos == W - 1

    def short_conv(cs):
        ug = _dot(hn, w_ref[:, cs])
        prev = jnp.where(first, 0.0, pltpu.roll(ug, 1, axis=0))
        nxt = jnp.where(last, 0.0, pltpu.roll(ug, L - 1, axis=0))
        return (prev * cw_ref[0:1, cs] + ug * cw_ref[1:2, cs] + nxt * cw_ref[2:3, cs]
                + cb_ref[:, cs])

    def long_conv(sig, o, cs):
        spec = _dot(fw_ref[...], sig.astype(BF16))
        sre, sim = spec[0:L], spec[L:2 * L]
        ka, kb, kd = filt_ref[o, 0, :, cs], filt_ref[o, 1, :, cs], filt_ref[o, 2, :, cs]
        yre = (sre * ka - sim * kb).astype(BF16)
        yim = (sre * kb + sim * kd).astype(BF16)
        return _dot(bw_ref[:, 0:L], yre) + _dot(bw_ref[:, L:2 * L], yim)

    for blk in range(HY_W // CB):
        cs = slice(blk * CB, (blk + 1) * CB)
        hv = short_conv(slice(blk * CB, (blk + 1) * CB))
        hx1 = short_conv(slice(HY_W + blk * CB, HY_W + (blk + 1) * CB))
        hx2 = short_conv(slice(2 * HY_W + blk * CB, 2 * HY_W + (blk + 1) * CB))
        z = hx1 * (long_conv(hv, 0, cs) + hv * hb_ref[0:1, cs])
        z = hx2 * (long_conv(z, 1, cs) + z * hb_ref[1:2, cs])
        yb = _dot(z.astype(BF16), wo_ref[cs, :])
        if blk == 0:
            y_ref[0] = yb
        else:
            y_ref[0] += yb


def _hyena(x, mods3, mod_row, norm1, w_hy, conv_w, conv_b, fw, bw, filt, hy_bias, w_o, *, W):
    B, L, D = x.shape
    return pl.pallas_call(
        functools.partial(_hy_kernel, L=L, W=W),
        grid=(B,),
        in_specs=[pl.BlockSpec((1, L, D), lambda b: (b, 0, 0)),
                  pl.BlockSpec((1, N_MOD, D), lambda b: (mod_row(b), 0, 0)),
                  _const_spec((1, D)),
                  _const_spec((D, N_HY)),
                  _const_spec((3, N_HY)),
                  _const_spec((1, N_HY)),
                  _const_spec((2 * L, L)),
                  _const_spec((L, 2 * L)),
                  _const_spec((HY_ORDER, 3, L, HY_W)),
                  _const_spec((HY_ORDER, HY_W)),
                  _const_spec((HY_W, D))],
        out_specs=pl.BlockSpec((1, L, D), lambda b: (b, 0, 0)),
        out_shape=jax.ShapeDtypeStruct((B, L, D), F32),
        compiler_params=_params(1),
        name=f"hyena{L}",
    )(x, mods3, norm1, w_hy, conv_w, conv_b, fw, bw, filt, hy_bias, w_o)


def _mlp_kernel(x_ref, yr_ref, yh_ref, mod_ref, n1_ref, n2_ref, fg_ref, wg_ref, wout_ref,
                wfi_ref, wfo_ref, y_ref):
    m = mod_ref[0]
    x = x_ref[0]
    hn = _modnorm(x, n1_ref[...], m[1:2], m[0:1]).astype(BF16)
    gates = _dot(hn, wg_ref[...])
    mix = (jax.nn.sigmoid(gates[:, 0:D_MODEL]) * yr_ref[0]
           + jax.nn.sigmoid(gates[:, D_MODEL:2 * D_MODEL]) * yh_ref[0])
    x1 = x + m[2:3] * _dot(mix.astype(BF16), wout_ref[...])
    h2 = _modnorm(x1, n2_ref[...], m[4:5], m[3:4]).astype(BF16)
    ab = _dot(h2, wfi_ref[...])
    a = ab[:, 0:D_FF]
    ff = (a * jax.nn.sigmoid(a) * ab[:, D_FF:2 * D_FF]).astype(BF16)
    x2 = x1 + m[5:6] * _dot(ff, wfo_ref[...])
    ms = jnp.mean(x2 * x2, axis=-1, keepdims=True)
    y_ref[0] = x2 * lax.rsqrt(ms + EPS) * fg_ref[...]


def _mlp(x, y_ret, y_hy, mods3, mod_row, norm1, norm2, final_g, w_gate, w_out, w_fi, w_fo):
    B, L, D = x.shape
    T = MLP_ROWS
    act = pl.BlockSpec((1, T, D), lambda b, i: (b, i, 0))
    return pl.pallas_call(
        _mlp_kernel,
        grid=(B, L // T),
        in_specs=[act, act, act,
                  pl.BlockSpec((1, N_MOD, D), lambda b, i: (mod_row(b), 0, 0)),
                  _const_spec((1, D)), _const_spec((1, D)), _const_spec((1, D)),
                  _const_spec((D, N_GATE)),
                  _const_spec((D, D)),
                  _const_spec((D, 2 * D_FF)),
                  _const_spec((D_FF, D))],
        out_specs=act,
        out_shape=jax.ShapeDtypeStruct((B, L, D), F32),
        compiler_params=_params(2),
        name=f"mlp{L}",
    )(x, y_ret, y_hy, mods3, norm1, norm2, final_g, w_gate, w_out, w_fi, w_fo)


def kernel(x_prompt, x_sample, state_ret_fwd, state_ret_bwd, c, c_ctx, norm1_g, norm2_g, w_ada,
           b_ada, w_in, ret_decay_fwd, ret_decay_bwd, hy_conv_w, hy_conv_b, hy_pos_w1, hy_pos_b1,
           hy_pos_w2, hy_pos_b2, hy_pos_w3, hy_sin_freq, hy_bias, w_ret_o, w_hy_o, w_out,
           w_ffn_in, w_ffn_out, final_g):
    assert w_in.shape[0] == 1, "single-layer configuration"
    nb_lat = x_sample.shape[0]
    l_ctx, l_lat = x_prompt.shape[1], x_sample.shape[1]

    cond8 = jnp.zeros((8, D_MODEL), F32).at[0].set(c_ctx).at[1:1 + nb_lat].set(c)
    mods3 = _ada(cond8, w_ada[0], b_ada).reshape(8, N_MOD, D_MODEL)

    w_in_b = w_in[0].astype(BF16)
    w_qkvg = w_in_b[:, 0:N_QKVG]
    w_hy = w_in_b[:, N_QKVG:N_QKVG + N_HY]
    w_gate = w_in_b[:, N_QKVG + N_HY:]
    w_ret_o_b = w_ret_o[0].astype(BF16)
    w_hy_o_b = w_hy_o[0].astype(BF16)
    w_out_b = w_out[0].astype(BF16)
    w_fi_b = w_ffn_in[0].astype(BF16)
    w_fo_b = w_ffn_out[0].astype(BF16)
    norm1 = norm1_g[0][None, :]
    norm2 = norm2_g[0][None, :]
    fg = final_g[None, :]
    dec8 = jnp.broadcast_to(jnp.concatenate([ret_decay_fwd[0], ret_decay_bwd[0]])[:, None],
                            (8, RET_CHUNK))
    w1 = jnp.pad(hy_pos_w1[0], ((0, HY_EMB_PAD - HY_EMB), (0, 0)))
    b1, b2 = hy_pos_b1[0][None, :], hy_pos_b2[0][None, :]
    freq = hy_sin_freq[0][None, :]
    conv_b = hy_conv_b[0][None, :]

    def group(x, mod_row, s0f, s0b, grid_w, emit_state):
        L = x.shape[1]
        fw, bw, sgn = _dft_mats(L)
        filt = _filters(L, fw, sgn, w1, b1, hy_pos_w2[0], b2, hy_pos_w3[0], freq)
        ret = _retention(x, mods3, mod_row, norm1, w_qkvg, dec8, s0f, s0b, w_ret_o_b,
                         emit_state=emit_state)
        y_ret = ret[0]
        y_hy = _hyena(x, mods3, mod_row, norm1, w_hy, hy_conv_w[0], conv_b, fw, bw, filt,
                      hy_bias[0], w_hy_o_b, W=grid_w)
        y = _mlp(x, y_ret, y_hy, mods3, mod_row, norm1, norm2, fg, w_gate, w_out_b, w_fi_b,
                 w_fo_b)
        return y, ret[1:]

    y_prompt, (s_f, s_b) = group(x_prompt, lambda b: 0, None, None, l_ctx, True)
    y_sample, _ = group(x_sample, lambda b: b + 1, state_ret_fwd[:, 0], state_ret_bwd[:, 0],
                        GRID_W, False)
    return (y_prompt, y_sample, s_f[:, None], s_b[:, None])
```

```python
import functools
import math

import numpy as np
import jax
import jax.numpy as jnp
from jax import lax
from jax.experimental import pallas as pl
from jax.experimental.pallas import tpu as pltpu

F32 = jnp.float32
BF16 = jnp.bfloat16

D_MODEL = 1024
RET_HEADS = 4
HEAD_DIM = 128
RET_W = RET_HEADS * HEAD_DIM
HY_W = 512
HY_ORDER = 2
HY_BANDS = 16
HY_EMB = 1 + 2 * HY_BANDS
HY_EMB_PAD = 40
HY_HIDDEN = 64
HY_FAST_DECAY = 0.3
HY_SLOW_DECAY = 1.5
HY_TARGET = 1e-2
D_FF = 2816
N_QKVG = 4 * RET_W
N_HY = 3 * HY_W
N_GATE = 2 * D_MODEL
N_MOD = 6
EPS = 1e-6
GRID_W = 64
RET_CHUNK = 256
HY_CBLK = 256
HY_TBLK = 512
HY_CTX_SEQS = 2
MLP_ROWS = 256
ADA_COLS = 768
VMEM_LIMIT = 56 * 1024 * 1024


def _const_spec(shape):
    nd = len(shape)
    return pl.BlockSpec(shape, lambda *_: (0,) * nd, pipeline_mode=pl.Buffered(1))


def _params(n_axes):
    return pltpu.CompilerParams(dimension_semantics=("arbitrary",) * n_axes,
                                vmem_limit_bytes=VMEM_LIMIT)


def _modnorm(x, g, scale, shift):
    ms = jnp.mean(x * x, axis=-1, keepdims=True)
    return (x * lax.rsqrt(ms + EPS) * g) * (1.0 + scale) + shift


def _dot(a, b):
    return jnp.dot(a, b, preferred_element_type=F32)


def _ada_kernel(c_ref, w_ref, b_ref, o_ref):
    c = c_ref[...]
    s = (c * jax.nn.sigmoid(c)).astype(BF16)
    o_ref[...] = _dot(s, w_ref[...].astype(BF16)) + b_ref[...]


def _ada(cond8, w, b):
    n = w.shape[1]
    return pl.pallas_call(
        _ada_kernel,
        grid=(n // ADA_COLS,),
        in_specs=[pl.BlockSpec((8, D_MODEL), lambda j: (0, 0)),
                  pl.BlockSpec((D_MODEL, ADA_COLS), lambda j: (0, j)),
                  pl.BlockSpec((1, ADA_COLS), lambda j: (0, j))],
        out_specs=pl.BlockSpec((8, ADA_COLS), lambda j: (0, j)),
        out_shape=jax.ShapeDtypeStruct((8, n), F32),
        compiler_params=_params(1),
        name="ada",
    )(cond8, w, b)


@functools.lru_cache(maxsize=None)
def _dft_mats(L):
    n = 2 * L
    t = np.arange(L, dtype=np.int64)
    f = np.arange(L, dtype=np.int64)
    ang = 2.0 * np.pi * ((f[:, None] * t[None, :]) % n).astype(np.float64) / n
    cos = np.cos(ang)
    sin = np.sin(ang)
    nyq = np.where(t % 2 == 0, 1.0, -1.0)
    fwd = np.concatenate([cos, -sin], axis=0)
    fwd[L] = nyq
    wre = np.full((L,), 2.0 / n)
    wre[0] = 1.0 / n
    inv = np.concatenate([cos.T * wre[None, :], -sin.T * (2.0 / n)], axis=1)
    inv[:, L] = nyq / n
    sgn = np.broadcast_to(nyq[:, None], (L, HY_W))
    return (jnp.asarray(fwd, dtype=BF16), jnp.asarray(inv, dtype=BF16),
            np.asarray(sgn, dtype=np.float32))


@functools.lru_cache(maxsize=None)
def _filter_consts(L):
    t = np.linspace(0.0, 1.0, L)[:, None]
    ang = 2.0 * np.pi * np.arange(L, dtype=np.float64)[:, None] / L
    bands = np.linspace(1e-4, HY_BANDS - 1, HY_BANDS)[None]
    z = np.concatenate([t, np.cos(bands * ang), -np.sin(bands * ang)], axis=-1)
    z = np.pad(z, ((0, 0), (0, HY_EMB_PAD - HY_EMB)))
    max_decay = math.log(HY_TARGET) / HY_FAST_DECAY
    min_decay = math.log(HY_TARGET) / HY_SLOW_DECAY
    deltas = np.linspace(min_decay, max_decay, HY_W)
    tdel = t * np.abs(deltas)[None, :]
    return np.asarray(z, np.float32), np.asarray(tdel, np.float32)


def _filter_kernel(z_ref, tdel_ref, sgn_ref, w1_ref, b1_ref, w2_ref, b2_ref, w3_ref, fr_ref,
                   fw_ref, oa_ref, ob_ref, od_ref, *, L, b):
    m = L // b
    hi = lax.Precision.HIGHEST
    fr = fr_ref[...]
    h = jnp.sin(fr * (jnp.dot(z_ref[...], w1_ref[...], precision=hi,
                              preferred_element_type=F32) + b1_ref[...]))
    h = jnp.sin(fr * (jnp.dot(h, w2_ref[...], precision=hi,
                              preferred_element_type=F32) + b2_ref[...]))
    h = jnp.dot(h, w3_ref[...], precision=hi, preferred_element_type=F32)
    win = jnp.exp(-tdel_ref[...])
    sg = sgn_ref[...]
    row0_l = lax.broadcasted_iota(jnp.int32, (L, HY_W), 0) == 0
    row0_b = lax.broadcasted_iota(jnp.int32, (b, HY_W), 0) == 0
    row0_8 = lax.broadcasted_iota(jnp.int32, (8, HY_W), 0) == 0
    for o in range(HY_ORDER):
        base = o * 2 * HY_W
        fwd = h[:, base:base + HY_W] * win
        bwd = jnp.where(row0_l, 0.0, h[:, base + HY_W:base + 2 * HY_W] * win)
        nrm = (jnp.sum(jnp.abs(fwd), axis=0, keepdims=True)
               + jnp.sum(jnp.abs(bwd), axis=0, keepdims=True))
        inv = 1.0 / nrm
        fn = fwd * inv
        bn = bwd * inv
        xr, xn, xi, wr, wn, wi = [], [], [], [], [], []
        for r in range(m):
            p = _dot(fw_ref[...], fn[r * b:(r + 1) * b].astype(BF16))
            q = _dot(fw_ref[...], bn[r * b:(r + 1) * b].astype(BF16))
            xr.append(p[0:b])
            xn.append(p[b:b + 1])
            xi.append(jnp.where(row0_b, 0.0, p[b:2 * b]))
            wr.append(q[0:b])
            wn.append(q[b:b + 1])
            wi.append(jnp.where(row0_b, 0.0, -q[b:2 * b]))

        def emit(d, ka, kn, kb):
            oa_ref[o, d + m - 1] = ka
            ob_ref[o, d + m - 1] = kb
            od_ref[o, d + m - 1] = jnp.where(row0_8, kn, ka[0:8])

        emit(0, xr[0] + wr[0], xn[0] + wn[0], xi[0] + wi[0])
        for d in range(1, m):
            f0 = fn[(d - 1) * b:(d - 1) * b + 1]
            b0 = bn[(d - 1) * b:(d - 1) * b + 1]
            emit(d, xr[d] + sg * (xr[d - 1] - f0), xn[d] + (xn[d - 1] - f0),
                 xi[d] + sg * xi[d - 1])
            emit(-d, wr[d] + sg * (wr[d - 1] - b0), wn[d] + (wn[d - 1] - b0),
                 wi[d] + sg * wi[d - 1])


def _filters(L, b, fw, sgn, w1, b1, w2, b2, w3, freq):
    z, tdel = _filter_consts(L)
    nd = 2 * (L // b) - 1
    args = (jnp.asarray(z), jnp.asarray(tdel), jnp.asarray(sgn), w1, b1, w2, b2, w3, freq, fw)
    return pl.pallas_call(
        functools.partial(_filter_kernel, L=L, b=b),
        out_shape=[jax.ShapeDtypeStruct((HY_ORDER, nd, b, HY_W), F32),
                   jax.ShapeDtypeStruct((HY_ORDER, nd, b, HY_W), F32),
                   jax.ShapeDtypeStruct((HY_ORDER, nd, 8, HY_W), F32)],
        compiler_params=pltpu.CompilerParams(vmem_limit_bytes=VMEM_LIMIT),
        name=f"filters{L}",
    )(*args)


def _ret_kernel(*refs, L, C, has_init, emit_state):
    it = iter(refs)
    x_ref, mod_ref, n1_ref, w_ref, dec_ref = (next(it) for _ in range(5))
    s0f_ref = s0b_ref = sf_ref = sb_ref = None
    if has_init:
        s0f_ref, s0b_ref = next(it), next(it)
    wo_ref, y_ref = next(it), next(it)
    if emit_state:
        sf_ref, sb_ref = next(it), next(it)
    mask_scr, vec_scr, cd_scr, g_scr = (next(it) for _ in range(4))
    n = L // C
    H, E = RET_HEADS, HEAD_DIM
    scale = float(E) ** -0.5

    @pl.when(pl.program_id(0) == 0)
    def _():
        lg = jnp.log(jax.nn.sigmoid(dec_ref[...]))
        cd_scr[...] = jnp.exp(float(C) * lg[:, 0:E])
        ii = lax.broadcasted_iota(jnp.int32, (C, C), 0)
        jj = lax.broadcasted_iota(jnp.int32, (C, C), 1)
        rel = (ii - jj).astype(F32)
        ri = lax.broadcasted_iota(jnp.int32, (C, E), 0).astype(F32)
        for h in range(H):
            lf = lg[h:h + 1, :]
            lb = lg[H + h:H + h + 1, :]
            mf = jnp.where(rel >= 0, jnp.exp(jnp.maximum(rel, 0.0) * lf), 0.0)
            mb = jnp.where(rel <= 0, jnp.exp(jnp.maximum(-rel, 0.0) * lb), 0.0)
            mask_scr[h] = scale * (mf + mb)
            lfe, lbe = lf[:, 0:E], lb[:, 0:E]
            vec_scr[h, 0] = jnp.exp((ri + 1.0) * lfe)
            vec_scr[h, 1] = jnp.exp((float(C) - ri) * lbe)
            vec_scr[h, 2] = scale * jnp.exp((float(C) - 1.0 - ri) * lfe)
            vec_scr[h, 3] = scale * jnp.exp(ri * lbe)

    m = mod_ref[0]
    hn = _modnorm(x_ref[0], n1_ref[...], m[1:2], m[0:1]).astype(BF16)
    qkvg = _dot(hn, w_ref[...])
    tdims = (((0,), (0,)), ((), ()))
    ndims = (((1,), (1,)), ((), ()))

    for h in range(H):
        q = qkvg[:, h * E:(h + 1) * E]
        k = qkvg[:, RET_W + h * E:RET_W + (h + 1) * E]
        v = qkvg[:, 2 * RET_W + h * E:2 * RET_W + (h + 1) * E]
        g = qkvg[:, 3 * RET_W + h * E:3 * RET_W + (h + 1) * E]
        qb = q.astype(BF16)
        kb = k.astype(BF16)
        vb = v.astype(BF16)
        mask = mask_scr[h]
        dqf, dqb, dkf, dkb = vec_scr[h, 0], vec_scr[h, 1], vec_scr[h, 2], vec_scr[h, 3]
        cdf = cd_scr[h:h + 1, :]
        cdb = cd_scr[H + h:H + h + 1, :]
        o_c, kvf, kvb = [], [], []
        for c in range(n):
            r = slice(c * C, (c + 1) * C)
            a = lax.dot_general(qb[r], kb[r], ndims, preferred_element_type=F32)
            o_c.append(_dot((a * mask).astype(BF16), vb[r]))
            kvf.append(lax.dot_general((k[r] * dkf).astype(BF16), vb[r], tdims,
                                       preferred_element_type=F32))
            kvb.append(lax.dot_general((k[r] * dkb).astype(BF16), vb[r], tdims,
                                       preferred_element_type=F32))
        s = s0f_ref[0, h] if has_init else None
        for c in range(n):
            if s is not None:
                o_c[c] = o_c[c] + _dot(qb[c * C:(c + 1) * C], s.astype(BF16)) * dqf
            s = kvf[c] if s is None else s * cdf + kvf[c]
        if emit_state:
            sf_ref[0, h] = s
        s = s0b_ref[0, h] if has_init else None
        for c in range(n - 1, -1, -1):
            if s is not None:
                o_c[c] = o_c[c] + _dot(qb[c * C:(c + 1) * C], s.astype(BF16)) * dqb
            s = kvb[c] if s is None else s * cdb + kvb[c]
        if emit_state:
            sb_ref[0, h] = s
        for c in range(n):
            r = slice(c * C, (c + 1) * C)
            o = o_c[c]
            mu = jnp.mean(o, axis=-1, keepdims=True)
            d = o - mu
            var = jnp.mean(d * d, axis=-1, keepdims=True)
            on = d * lax.rsqrt(var + EPS)
            gg = g[r]
            g_scr[r, h * E:(h + 1) * E] = (gg * jax.nn.sigmoid(gg) * on).astype(BF16)

    y_ref[0] = _dot(g_scr[...], wo_ref[...])


def _retention(x, mods3, mod_row, norm1, w_qkvg, dec8, s0f, s0b, w_o, *, emit_state):
    B, L, D = x.shape
    C = min(RET_CHUNK, L)
    has_init = s0f is not None
    H, E = RET_HEADS, HEAD_DIM
    in_specs = [pl.BlockSpec((1, L, D), lambda b: (b, 0, 0)),
                pl.BlockSpec((1, N_MOD, D), lambda b: (mod_row(b), 0, 0)),
                _const_spec((1, D)),
                _const_spec((D, N_QKVG)),
                _const_spec((8, C))]
    args = [x, mods3, norm1, w_qkvg, dec8[:, :C]]
    st_spec = pl.BlockSpec((1, H, E, E), lambda b: (b, 0, 0, 0))
    if has_init:
        in_specs += [st_spec, st_spec]
        args += [s0f, s0b]
    in_specs.append(_const_spec((RET_W, D)))
    args.append(w_o)
    out_specs = [pl.BlockSpec((1, L, D), lambda b: (b, 0, 0))]
    out_shape = [jax.ShapeDtypeStruct((B, L, D), F32)]
    if emit_state:
        out_specs += [st_spec, st_spec]
        out_shape += [jax.ShapeDtypeStruct((B, H, E, E), F32)] * 2
    return pl.pallas_call(
        functools.partial(_ret_kernel, L=L, C=C, has_init=has_init, emit_state=emit_state),
        grid=(B,),
        in_specs=in_specs,
        out_specs=out_specs,
        out_shape=out_shape,
        scratch_shapes=[pltpu.VMEM((H, C, C), F32),
                        pltpu.VMEM((H, 4, C, E), F32),
                        pltpu.VMEM((8, E), F32),
                        pltpu.VMEM((L, RET_W), BF16)],
        compiler_params=_params(1),
        name=f"retention{L}",
    )(*args)


def _hy_kernel(x_ref, mod_ref, n1_ref, w_ref, cw_ref, cb_ref, fw_ref, bw_ref, fa_ref, fb_ref,
               fd_ref, hb_ref, wo_ref, y_ref, *, L, W, b, nb):
    m = L // b
    CB = HY_CBLK
    nblk = HY_W // CB
    mod = mod_ref[0]
    pos = lax.broadcasted_iota(jnp.int32, (L, CB), 0) % W
    first = pos == 0
    last = pos == W - 1
    hn = [_modnorm(x_ref[s], n1_ref[...], mod[1:2], mod[0:1]).astype(BF16) for s in range(nb)]
    chains = [(s, blk) for s in range(nb) for blk in range(nblk)]

    def short_conv(s, base, blk):
        cs = slice(base + blk * CB, base + (blk + 1) * CB)
        ug = _dot(hn[s], w_ref[:, cs])
        prev = jnp.where(first, 0.0, pltpu.roll(ug, 1, axis=0))
        nxt = jnp.where(last, 0.0, pltpu.roll(ug, L - 1, axis=0))
        u = (prev * cw_ref[0:1, cs] + ug * cw_ref[1:2, cs] + nxt * cw_ref[2:3, cs]
             + cb_ref[:, cs])
        return [u[j * b:(j + 1) * b] for j in range(m)]

    def long_conv(sigs, o):
        spec = [[_dot(fw_ref[...], sj.astype(BF16)) for sj in sig] for sig in sigs]
        prods = []
        for (s, blk), sp in zip(chains, spec):
            cs = slice(blk * CB, (blk + 1) * CB)
            per_i = []
            for i in range(m):
                yre = yim = yim8 = None
                for j in range(m):
                    d = i - j + m - 1
                    sre, sim = sp[j][0:b], sp[j][b:2 * b]
                    ka, kb = fa_ref[o, d, :, cs], fb_ref[o, d, :, cs]
                    tre = sre * ka - sim * kb
                    tim = sre * kb + sim * ka
                    t8 = sre[0:8] * kb[0:8] + sim[0:8] * fd_ref[o, d, :, cs]
                    yre = tre if yre is None else yre + tre
                    yim = tim if yim is None else yim + tim
                    yim8 = t8 if yim8 is None else yim8 + t8
                yim = jnp.concatenate([yim8, yim[8:]], axis=0)
                per_i.append((yre.astype(BF16), yim.astype(BF16)))
            prods.append(per_i)
        return [[_dot(bw_ref[:, 0:b], yre) + _dot(bw_ref[:, b:2 * b], yim) for yre, yim in per_i]
                for per_i in prods]

    hv = [short_conv(s, 0, blk) for s, blk in chains]
    hx1 = [short_conv(s, HY_W, blk) for s, blk in chains]
    hx2 = [short_conv(s, 2 * HY_W, blk) for s, blk in chains]

    def gate(hx, conv, sig, o):
        out = []
        for (s, blk), hxc, cc, sc in zip(chains, hx, conv, sig):
            bias = hb_ref[o:o + 1, blk * CB:(blk + 1) * CB]
            out.append([hxc[i] * (cc[i] + sc[i] * bias) for i in range(m)])
        return out

    z = gate(hx1, long_conv(hv, 0), hv, 0)
    z = gate(hx2, long_conv(z, 1), z, 1)
    for s in range(nb):
        for i in range(m):
            acc = None
            for blk in range(nblk):
                zc = z[chains.index((s, blk))][i].astype(BF16)
                part = _dot(zc, wo_ref[blk * CB:(blk + 1) * CB, :])
                acc = part if acc is None else acc + part
            y_ref[s, i * b:(i + 1) * b, :] = acc


def _hyena(x, mods3, mod_row, norm1, w_hy, conv_w, conv_b, fw, bw, filt, hy_bias, w_o, *, W, b,
           nb):
    B, L, D = x.shape
    fa, fb, fd = filt
    nd = fa.shape[1]
    return pl.pallas_call(
        functools.partial(_hy_kernel, L=L, W=W, b=b, nb=nb),
        grid=(B // nb,),
        in_specs=[pl.BlockSpec((nb, L, D), lambda g: (g, 0, 0)),
                  pl.BlockSpec((1, N_MOD, D), lambda g: (mod_row(g * nb), 0, 0)),
                  _const_spec((1, D)),
                  _const_spec((D, N_HY)),
                  _const_spec((3, N_HY)),
                  _const_spec((1, N_HY)),
                  _const_spec((2 * b, b)),
                  _const_spec((b, 2 * b)),
                  _const_spec((HY_ORDER, nd, b, HY_W)),
                  _const_spec((HY_ORDER, nd, b, HY_W)),
                  _const_spec((HY_ORDER, nd, 8, HY_W)),
                  _const_spec((HY_ORDER, HY_W)),
                  _const_spec((HY_W, D))],
        out_specs=pl.BlockSpec((nb, L, D), lambda g: (g, 0, 0)),
        out_shape=jax.ShapeDtypeStruct((B, L, D), F32),
        compiler_params=_params(1),
        name=f"hyena{L}",
    )(x, mods3, norm1, w_hy, conv_w, conv_b, fw, bw, fa, fb, fd, hy_bias, w_o)


def _mlp_kernel(x_ref, yr_ref, yh_ref, mod_ref, n1_ref, n2_ref, fg_ref, wg_ref, wout_ref,
                wfi_ref, wfo_ref, y_ref):
    m = mod_ref[0]
    x = x_ref[0]
    hn = _modnorm(x, n1_ref[...], m[1:2], m[0:1]).astype(BF16)
    gates = _dot(hn, wg_ref[...])
    mix = (jax.nn.sigmoid(gates[:, 0:D_MODEL]) * yr_ref[0]
           + jax.nn.sigmoid(gates[:, D_MODEL:2 * D_MODEL]) * yh_ref[0])
    x1 = x + m[2:3] * _dot(mix.astype(BF16), wout_ref[...])
    h2 = _modnorm(x1, n2_ref[...], m[4:5], m[3:4]).astype(BF16)
    ab = _dot(h2, wfi_ref[...])
    a = ab[:, 0:D_FF]
    ff = (a * jax.nn.sigmoid(a) * ab[:, D_FF:2 * D_FF]).astype(BF16)
    x2 = x1 + m[5:6] * _dot(ff, wfo_ref[...])
    ms = jnp.mean(x2 * x2, axis=-1, keepdims=True)
    y_ref[0] = x2 * lax.rsqrt(ms + EPS) * fg_ref[...]


def _mlp(x, y_ret, y_hy, mods3, mod_row, norm1, norm2, final_g, w_gate, w_out, w_fi, w_fo):
    B, L, D = x.shape
    T = MLP_ROWS
    act = pl.BlockSpec((1, T, D), lambda b, i: (b, i, 0))
    return pl.pallas_call(
        _mlp_kernel,
        grid=(B, L // T),
        in_specs=[act, act, act,
                  pl.BlockSpec((1, N_MOD, D), lambda b, i: (mod_row(b), 0, 0)),
                  _const_spec((1, D)), _const_spec((1, D)), _const_spec((1, D)),
                  _const_spec((D, N_GATE)),
                  _const_spec((D, D)),
                  _const_spec((D, 2 * D_FF)),
                  _const_spec((D_FF, D))],
        out_specs=act,
        out_shape=jax.ShapeDtypeStruct((B, L, D), F32),
        compiler_params=_params(2),
        name=f"mlp{L}",
    )(x, y_ret, y_hy, mods3, norm1, norm2, final_g, w_gate, w_out, w_fi, w_fo)


def kernel(x_prompt, x_sample, state_ret_fwd, state_ret_bwd, c, c_ctx, norm1_g, norm2_g, w_ada,
           b_ada, w_in, ret_decay_fwd, ret_decay_bwd, hy_conv_w, hy_conv_b, hy_pos_w1, hy_pos_b1,
           hy_pos_w2, hy_pos_b2, hy_pos_w3, hy_sin_freq, hy_bias, w_ret_o, w_hy_o, w_out,
           w_ffn_in, w_ffn_out, final_g):
    assert w_in.shape[0] == 1, "single-layer configuration"
    nb_lat = x_sample.shape[0]
    l_ctx, l_lat = x_prompt.shape[1], x_sample.shape[1]

    cond8 = jnp.zeros((8, D_MODEL), F32).at[0].set(c_ctx).at[1:1 + nb_lat].set(c)
    mods3 = _ada(cond8, w_ada[0], b_ada).reshape(8, N_MOD, D_MODEL)

    w_in_b = w_in[0].astype(BF16)
    w_qkvg = w_in_b[:, 0:N_QKVG]
    w_hy = w_in_b[:, N_QKVG:N_QKVG + N_HY]
    w_gate = w_in_b[:, N_QKVG + N_HY:]
    w_ret_o_b = w_ret_o[0].astype(BF16)
    w_hy_o_b = w_hy_o[0].astype(BF16)
    w_out_b = w_out[0].astype(BF16)
    w_fi_b = w_ffn_in[0].astype(BF16)
    w_fo_b = w_ffn_out[0].astype(BF16)
    norm1 = norm1_g[0][None, :]
    norm2 = norm2_g[0][None, :]
    fg = final_g[None, :]
    dec8 = jnp.broadcast_to(jnp.concatenate([ret_decay_fwd[0], ret_decay_bwd[0]])[:, None],
                            (8, RET_CHUNK))
    w1 = jnp.pad(hy_pos_w1[0], ((0, HY_EMB_PAD - HY_EMB), (0, 0)))
    b1, b2 = hy_pos_b1[0][None, :], hy_pos_b2[0][None, :]
    freq = hy_sin_freq[0][None, :]
    conv_b = hy_conv_b[0][None, :]

    def group(x, mod_row, s0f, s0b, grid_w, emit_state, hy_nb):
        L = x.shape[1]
        blk = min(HY_TBLK, L)
        fw, bw, sgn = _dft_mats(blk)
        filt = _filters(L, blk, fw, sgn, w1, b1, hy_pos_w2[0], b2, hy_pos_w3[0], freq)
        ret = _retention(x, mods3, mod_row, norm1, w_qkvg, dec8, s0f, s0b, w_ret_o_b,
                         emit_state=emit_state)
        y_ret = ret[0]
        y_hy = _hyena(x, mods3, mod_row, norm1, w_hy, hy_conv_w[0], conv_b, fw, bw, filt,
                      hy_bias[0], w_hy_o_b, W=grid_w, b=blk, nb=hy_nb)
        y = _mlp(x, y_ret, y_hy, mods3, mod_row, norm1, norm2, fg, w_gate, w_out_b, w_fi_b,
                 w_fo_b)
        return y, ret[1:]

    y_prompt, (s_f, s_b) = group(x_prompt, lambda b: 0, None, None, l_ctx, True, HY_CTX_SEQS)
    y_sample, _ = group(x_sample, lambda b: b + 1, state_ret_fwd[:, 0], state_ret_bwd[:, 0],
                        GRID_W, False, 1)
    return (y_prompt, y_sample, s_f[:, None], s_b[:, None])
```

```python
import functools
import math

import numpy as np
import jax
import jax.numpy as jnp
from jax import lax
from jax.experimental import pallas as pl
from jax.experimental.pallas import tpu as pltpu

F32 = jnp.float32
BF16 = jnp.bfloat16

D_MODEL = 1024
RET_HEADS = 4
HEAD_DIM = 128
RET_W = RET_HEADS * HEAD_DIM
HY_W = 512
HY_ORDER = 2
HY_BANDS = 16
HY_EMB = 1 + 2 * HY_BANDS
HY_EMB_PAD = 40
HY_HIDDEN = 64
HY_FAST_DECAY = 0.3
HY_SLOW_DECAY = 1.5
HY_TARGET = 1e-2
D_FF = 2816
N_QKVG = 4 * RET_W
N_HY = 3 * HY_W
N_GATE = 2 * D_MODEL
N_MOD = 6
EPS = 1e-6
GRID_W = 64
RET_CHUNK = 256
RET_CTX_SEQS = 2
HY_CBLK = 256
HY_TBLK = 512
HY_CTX_SEQS = 2
MLP_ROWS = 512
FF_CHUNK = 256
ADA_COLS = 768
VMEM_LIMIT = 56 * 1024 * 1024


def _const_spec(shape):
    nd = len(shape)
    return pl.BlockSpec(shape, lambda *_: (0,) * nd, pipeline_mode=pl.Buffered(1))


def _params(n_axes):
    return pltpu.CompilerParams(dimension_semantics=("arbitrary",) * n_axes,
                                vmem_limit_bytes=VMEM_LIMIT)


def _modnorm(x, g, scale, shift):
    ms = jnp.mean(x * x, axis=-1, keepdims=True)
    return (x * lax.rsqrt(ms + EPS) * g) * (1.0 + scale) + shift


def _dot(a, b):
    return jnp.dot(a, b, preferred_element_type=F32)


def _ada_kernel(c_ref, w_ref, b_ref, o_ref):
    c = c_ref[...]
    s = (c * jax.nn.sigmoid(c)).astype(BF16)
    o_ref[...] = _dot(s, w_ref[...].astype(BF16)) + b_ref[...]


def _ada(cond8, w, b):
    n = w.shape[1]
    return pl.pallas_call(
        _ada_kernel,
        grid=(n // ADA_COLS,),
        in_specs=[pl.BlockSpec((8, D_MODEL), lambda j: (0, 0)),
                  pl.BlockSpec((D_MODEL, ADA_COLS), lambda j: (0, j)),
                  pl.BlockSpec((1, ADA_COLS), lambda j: (0, j))],
        out_specs=pl.BlockSpec((8, ADA_COLS), lambda j: (0, j)),
        out_shape=jax.ShapeDtypeStruct((8, n), F32),
        compiler_params=_params(1),
        name="ada",
    )(cond8, w, b)


@functools.lru_cache(maxsize=None)
def _dft_mats(L):
    n = 2 * L
    t = np.arange(L, dtype=np.int64)
    f = np.arange(L, dtype=np.int64)
    ang = 2.0 * np.pi * ((f[:, None] * t[None, :]) % n).astype(np.float64) / n
    cos = np.cos(ang)
    sin = np.sin(ang)
    nyq = np.where(t % 2 == 0, 1.0, -1.0)
    fwd = np.concatenate([cos, -sin], axis=0)
    fwd[L] = nyq
    wre = np.full((L,), 2.0 / n)
    wre[0] = 1.0 / n
    inv = np.concatenate([cos.T * wre[None, :], -sin.T * (2.0 / n)], axis=1)
    inv[:, L] = nyq / n
    sgn = np.broadcast_to(nyq[:, None], (L, HY_W))
    return (jnp.asarray(fwd, dtype=BF16), jnp.asarray(inv, dtype=BF16),
            np.asarray(sgn, dtype=np.float32))


@functools.lru_cache(maxsize=None)
def _filter_consts(L):
    t = np.linspace(0.0, 1.0, L)[:, None]
    ang = 2.0 * np.pi * np.arange(L, dtype=np.float64)[:, None] / L
    bands = np.linspace(1e-4, HY_BANDS - 1, HY_BANDS)[None]
    z = np.concatenate([t, np.cos(bands * ang), -np.sin(bands * ang)], axis=-1)
    z = np.pad(z, ((0, 0), (0, HY_EMB_PAD - HY_EMB)))
    max_decay = math.log(HY_TARGET) / HY_FAST_DECAY
    min_decay = math.log(HY_TARGET) / HY_SLOW_DECAY
    deltas = np.linspace(min_decay, max_decay, HY_W)
    tdel = t * np.abs(deltas)[None, :]
    return np.asarray(z, np.float32), np.asarray(tdel, np.float32)


def _filter_kernel(z_ref, tdel_ref, sgn_ref, w1_ref, b1_ref, w2_ref, b2_ref, w3_ref, fr_ref,
                   fw_ref, oa_ref, ob_ref, od_ref, *, L, b):
    m = L // b
    hi = lax.Precision.HIGHEST
    fr = fr_ref[...]
    h = jnp.sin(fr * (jnp.dot(z_ref[...], w1_ref[...], precision=hi,
                              preferred_element_type=F32) + b1_ref[...]))
    h = jnp.sin(fr * (jnp.dot(h, w2_ref[...], precision=hi,
                              preferred_element_type=F32) + b2_ref[...]))
    h = jnp.dot(h, w3_ref[...], precision=hi, preferred_element_type=F32)
    win = jnp.exp(-tdel_ref[...])
    sg = sgn_ref[...]
    row0_l = lax.broadcasted_iota(jnp.int32, (L, HY_W), 0) == 0
    row0_b = lax.broadcasted_iota(jnp.int32, (b, HY_W), 0) == 0
    row0_8 = lax.broadcasted_iota(jnp.int32, (8, HY_W), 0) == 0
    for o in range(HY_ORDER):
        base = o * 2 * HY_W
        fwd = h[:, base:base + HY_W] * win
        bwd = jnp.where(row0_l, 0.0, h[:, base + HY_W:base + 2 * HY_W] * win)
        nrm = (jnp.sum(jnp.abs(fwd), axis=0, keepdims=True)
               + jnp.sum(jnp.abs(bwd), axis=0, keepdims=True))
        inv = 1.0 / nrm
        fn = fwd * inv
        bn = bwd * inv
        xr, xn, xi, wr, wn, wi = [], [], [], [], [], []
        for r in range(m):
            p = _dot(fw_ref[...], fn[r * b:(r + 1) * b].astype(BF16))
            q = _dot(fw_ref[...], bn[r * b:(r + 1) * b].astype(BF16))
            xr.append(p[0:b])
            xn.append(p[b:b + 1])
            xi.append(jnp.where(row0_b, 0.0, p[b:2 * b]))
            wr.append(q[0:b])
            wn.append(q[b:b + 1])
            wi.append(jnp.where(row0_b, 0.0, -q[b:2 * b]))

        def emit(d, ka, kn, kb):
            oa_ref[o, d + m - 1] = ka
            ob_ref[o, d + m - 1] = kb
            od_ref[o, d + m - 1] = jnp.where(row0_8, kn, ka[0:8])

        emit(0, xr[0] + wr[0], xn[0] + wn[0], xi[0] + wi[0])
        for d in range(1, m):
            f0 = fn[(d - 1) * b:(d - 1) * b + 1]
            b0 = bn[(d - 1) * b:(d - 1) * b + 1]
            emit(d, xr[d] + sg * (xr[d - 1] - f0), xn[d] + (xn[d - 1] - f0),
                 xi[d] + sg * xi[d - 1])
            emit(-d, wr[d] + sg * (wr[d - 1] - b0), wn[d] + (wn[d - 1] - b0),
                 wi[d] + sg * wi[d - 1])


def _filters(L, b, fw, sgn, w1, b1, w2, b2, w3, freq):
    z, tdel = _filter_consts(L)
    nd = 2 * (L // b) - 1
    args = (jnp.asarray(z), jnp.asarray(tdel), jnp.asarray(sgn), w1, b1, w2, b2, w3, freq, fw)
    return pl.pallas_call(
        functools.partial(_filter_kernel, L=L, b=b),
        out_shape=[jax.ShapeDtypeStruct((HY_ORDER, nd, b, HY_W), F32),
                   jax.ShapeDtypeStruct((HY_ORDER, nd, b, HY_W), F32),
                   jax.ShapeDtypeStruct((HY_ORDER, nd, 8, HY_W), F32)],
        compiler_params=pltpu.CompilerParams(vmem_limit_bytes=VMEM_LIMIT),
        name=f"filters{L}",
    )(*args)


def _ret_kernel(*refs, L, C, nb, has_init, emit_state):
    it = iter(refs)
    x_ref, mod_ref, n1_ref, w_ref, dec_ref = (next(it) for _ in range(5))
    s0f_ref = s0b_ref = sf_ref = sb_ref = None
    if has_init:
        s0f_ref, s0b_ref = next(it), next(it)
    wo_ref, y_ref = next(it), next(it)
    if emit_state:
        sf_ref, sb_ref = next(it), next(it)
    mask_scr, vec_scr, cd_scr, g_scr = (next(it) for _ in range(4))
    n = L // C
    H, E = RET_HEADS, HEAD_DIM
    scale = float(E) ** -0.5

    @pl.when(pl.program_id(0) == 0)
    def _():
        lg = jnp.log(jax.nn.sigmoid(dec_ref[...]))
        cd_scr[...] = jnp.exp(float(C) * lg[:, 0:E])
        ii = lax.broadcasted_iota(jnp.int32, (C, C), 0)
        jj = lax.broadcasted_iota(jnp.int32, (C, C), 1)
        rel = (ii - jj).astype(F32)
        ri = lax.broadcasted_iota(jnp.int32, (C, E), 0).astype(F32)
        for h in range(H):
            lf = lg[h:h + 1, :]
            lb = lg[H + h:H + h + 1, :]
            mf = jnp.where(rel >= 0, jnp.exp(jnp.maximum(rel, 0.0) * lf), 0.0)
            mb = jnp.where(rel <= 0, jnp.exp(jnp.maximum(-rel, 0.0) * lb), 0.0)
            mask_scr[h] = scale * (mf + mb)
            lfe, lbe = lf[:, 0:E], lb[:, 0:E]
            vec_scr[h, 0] = jnp.exp((ri + 1.0) * lfe)
            vec_scr[h, 1] = jnp.exp((float(C) - ri) * lbe)
            vec_scr[h, 2] = scale * jnp.exp((float(C) - 1.0 - ri) * lfe)
            vec_scr[h, 3] = scale * jnp.exp(ri * lbe)

    mod = mod_ref[0]
    tdims = (((0,), (0,)), ((), ()))
    ndims = (((1,), (1,)), ((), ()))
    chains = [(s, h) for s in range(nb) for h in range(H)]
    rows = [slice(c * C, (c + 1) * C) for c in range(n)]
    qkvg = [_dot(_modnorm(x_ref[s], n1_ref[...], mod[1:2], mod[0:1]).astype(BF16), w_ref[...])
            for s in range(nb)]

    def cols(s, part, h):
        return qkvg[s][:, part * RET_W + h * E:part * RET_W + (h + 1) * E]

    qb = [cols(s, 0, h).astype(BF16) for s, h in chains]
    kf = [cols(s, 1, h) for s, h in chains]
    kb = [k.astype(BF16) for k in kf]
    vb = [cols(s, 2, h).astype(BF16) for s, h in chains]
    att = [[lax.dot_general(qb[i][r], kb[i][r], ndims, preferred_element_type=F32) for r in rows]
           for i in range(len(chains))]
    prob = [[(att[i][c] * mask_scr[h]).astype(BF16) for c in range(n)]
            for i, (s, h) in enumerate(chains)]
    out = [[_dot(prob[i][c], vb[i][rows[c]]) for c in range(n)] for i in range(len(chains))]
    kv = []
    for i, (s, h) in enumerate(chains):
        dk2 = jnp.concatenate([vec_scr[h, 2], vec_scr[h, 3]], axis=1)
        per_c = []
        for r in rows:
            k2 = (jnp.concatenate([kf[i][r], kf[i][r]], axis=1) * dk2).astype(BF16)
            per_c.append(lax.dot_general(k2, vb[i][r], tdims, preferred_element_type=F32))
        kv.append(per_c)
    for i, (s, h) in enumerate(chains):
        cdf = cd_scr[h:h + 1, :]
        cdb = cd_scr[H + h:H + h + 1, :]
        sf_in, sb_in = [None] * n, [None] * n
        st = s0f_ref[s, h] if has_init else None
        for c in range(n):
            sf_in[c] = st
            kvc = kv[i][c][0:E]
            st = kvc if st is None else st * cdf + kvc
        if emit_state:
            sf_ref[s, h] = st
        st = s0b_ref[s, h] if has_init else None
        for c in range(n - 1, -1, -1):
            sb_in[c] = st
            kvc = kv[i][c][E:2 * E]
            st = kvc if st is None else st * cdb + kvc
        if emit_state:
            sb_ref[s, h] = st
        for c in range(n):
            if sf_in[c] is not None and sb_in[c] is not None:
                s2 = jnp.concatenate([sf_in[c], sb_in[c]], axis=1).astype(BF16)
                inter = _dot(qb[i][rows[c]], s2)
                out[i][c] = (out[i][c] + inter[:, 0:E] * vec_scr[h, 0]
                             + inter[:, E:2 * E] * vec_scr[h, 1])
            elif sf_in[c] is not None:
                out[i][c] = (out[i][c]
                             + _dot(qb[i][rows[c]], sf_in[c].astype(BF16)) * vec_scr[h, 0])
            elif sb_in[c] is not None:
                out[i][c] = (out[i][c]
                             + _dot(qb[i][rows[c]], sb_in[c].astype(BF16)) * vec_scr[h, 1])
    for i, (s, h) in enumerate(chains):
        for c in range(n):
            o = out[i][c]
            mu = jnp.mean(o, axis=-1, keepdims=True)
            d = o - mu
            var = jnp.mean(d * d, axis=-1, keepdims=True)
            on = d * lax.rsqrt(var + EPS)
            gg = cols(s, 3, h)[rows[c]]
            g_scr[s, rows[c], h * E:(h + 1) * E] = (gg * jax.nn.sigmoid(gg) * on).astype(BF16)
    for s in range(nb):
        y_ref[s] = _dot(g_scr[s], wo_ref[...])


def _retention(x, mods3, mod_row, norm1, w_qkvg, dec8, s0f, s0b, w_o, *, emit_state, nb):
    B, L, D = x.shape
    C = min(RET_CHUNK, L)
    has_init = s0f is not None
    H, E = RET_HEADS, HEAD_DIM
    in_specs = [pl.BlockSpec((nb, L, D), lambda g: (g, 0, 0)),
                pl.BlockSpec((1, N_MOD, D), lambda g: (mod_row(g * nb), 0, 0)),
                _const_spec((1, D)),
                _const_spec((D, N_QKVG)),
                _const_spec((8, C))]
    args = [x, mods3, norm1, w_qkvg, dec8[:, :C]]
    st_spec = pl.BlockSpec((nb, H, E, E), lambda g: (g, 0, 0, 0))
    if has_init:
        in_specs += [st_spec, st_spec]
        args += [s0f, s0b]
    in_specs.append(_const_spec((RET_W, D)))
    args.append(w_o)
    out_specs = [pl.BlockSpec((nb, L, D), lambda g: (g, 0, 0))]
    out_shape = [jax.ShapeDtypeStruct((B, L, D), F32)]
    if emit_state:
        out_specs += [st_spec, st_spec]
        out_shape += [jax.ShapeDtypeStruct((B, H, E, E), F32)] * 2
    return pl.pallas_call(
        functools.partial(_ret_kernel, L=L, C=C, nb=nb, has_init=has_init,
                          emit_state=emit_state),
        grid=(B // nb,),
        in_specs=in_specs,
        out_specs=out_specs,
        out_shape=out_shape,
        scratch_shapes=[pltpu.VMEM((H, C, C), F32),
                        pltpu.VMEM((H, 4, C, E), F32),
                        pltpu.VMEM((8, E), F32),
                        pltpu.VMEM((nb, L, RET_W), BF16)],
        compiler_params=_params(1),
        name=f"retention{L}",
    )(*args)


def _hy_kernel(x_ref, mod_ref, n1_ref, w_ref, cw_ref, cb_ref, fw_ref, bw_ref, fa_ref, fb_ref,
               fd_ref, hb_ref, wo_ref, y_ref, *, L, W, b, nb):
    m = L // b
    CB = HY_CBLK
    nblk = HY_W // CB
    mod = mod_ref[0]
    pos = lax.broadcasted_iota(jnp.int32, (L, CB), 0) % W
    first = pos == 0
    last = pos == W - 1
    hn = [_modnorm(x_ref[s], n1_ref[...], mod[1:2], mod[0:1]).astype(BF16) for s in range(nb)]
    chains = [(s, blk) for s in range(nb) for blk in range(nblk)]

    def short_conv(s, base, blk):
        cs = slice(base + blk * CB, base + (blk + 1) * CB)
        ug = _dot(hn[s], w_ref[:, cs])
        prev = jnp.where(first, 0.0, pltpu.roll(ug, 1, axis=0))
        nxt = jnp.where(last, 0.0, pltpu.roll(ug, L - 1, axis=0))
        u = (prev * cw_ref[0:1, cs] + ug * cw_ref[1:2, cs] + nxt * cw_ref[2:3, cs]
             + cb_ref[:, cs])
        return [u[j * b:(j + 1) * b] for j in range(m)]

    def long_conv(sigs, o):
        spec = [[_dot(fw_ref[...], sj.astype(BF16)) for sj in sig] for sig in sigs]
        prods = []
        for (s, blk), sp in zip(chains, spec):
            cs = slice(blk * CB, (blk + 1) * CB)
            per_i = []
            for i in range(m):
                yre = yim = yim8 = None
                for j in range(m):
                    d = i - j + m - 1
                    sre, sim = sp[j][0:b], sp[j][b:2 * b]
                    ka, kb = fa_ref[o, d, :, cs], fb_ref[o, d, :, cs]
                    tre = sre * ka - sim * kb
                    tim = sre * kb + sim * ka
                    t8 = sre[0:8] * kb[0:8] + sim[0:8] * fd_ref[o, d, :, cs]
                    yre = tre if yre is None else yre + tre
                    yim = tim if yim is None else yim + tim
                    yim8 = t8 if yim8 is None else yim8 + t8
                yim = jnp.concatenate([yim8, yim[8:]], axis=0)
                per_i.append((yre.astype(BF16), yim.astype(BF16)))
            prods.append(per_i)
        return [[_dot(bw_ref[:, 0:b], yre) + _dot(bw_ref[:, b:2 * b], yim) for yre, yim in per_i]
                for per_i in prods]

    hv = [short_conv(s, 0, blk) for s, blk in chains]
    hx1 = [short_conv(s, HY_W, blk) for s, blk in chains]
    hx2 = [short_conv(s, 2 * HY_W, blk) for s, blk in chains]

    def gate(hx, conv, sig, o):
        out = []
        for (s, blk), hxc, cc, sc in zip(chains, hx, conv, sig):
            bias = hb_ref[o:o + 1, blk * CB:(blk + 1) * CB]
            out.append([hxc[i] * (cc[i] + sc[i] * bias) for i in range(m)])
        return out

    z = gate(hx1, long_conv(hv, 0), hv, 0)
    z = gate(hx2, long_conv(z, 1), z, 1)
    for s in range(nb):
        for i in range(m):
            acc = None
            for blk in range(nblk):
                zc = z[chains.index((s, blk))][i].astype(BF16)
                part = _dot(zc, wo_ref[blk * CB:(blk + 1) * CB, :])
                acc = part if acc is None else acc + part
            y_ref[s, i * b:(i + 1) * b, :] = acc


def _hyena(x, mods3, mod_row, norm1, w_hy, conv_w, conv_b, fw, bw, filt, hy_bias, w_o, *, W, b,
           nb):
    B, L, D = x.shape
    fa, fb, fd = filt
    nd = fa.shape[1]
    return pl.pallas_call(
        functools.partial(_hy_kernel, L=L, W=W, b=b, nb=nb),
        grid=(B // nb,),
        in_specs=[pl.BlockSpec((nb, L, D), lambda g: (g, 0, 0)),
                  pl.BlockSpec((1, N_MOD, D), lambda g: (mod_row(g * nb), 0, 0)),
                  _const_spec((1, D)),
                  _const_spec((D, N_HY)),
                  _const_spec((3, N_HY)),
                  _const_spec((1, N_HY)),
                  _const_spec((2 * b, b)),
                  _const_spec((b, 2 * b)),
                  _const_spec((HY_ORDER, nd, b, HY_W)),
                  _const_spec((HY_ORDER, nd, b, HY_W)),
                  _const_spec((HY_ORDER, nd, 8, HY_W)),
                  _const_spec((HY_ORDER, HY_W)),
                  _const_spec((HY_W, D))],
        out_specs=pl.BlockSpec((nb, L, D), lambda g: (g, 0, 0)),
        out_shape=jax.ShapeDtypeStruct((B, L, D), F32),
        compiler_params=_params(1),
        name=f"hyena{L}",
    )(x, mods3, norm1, w_hy, conv_w, conv_b, fw, bw, fa, fb, fd, hy_bias, w_o)


def _mlp_kernel(x_ref, yr_ref, yh_ref, mod_ref, n1_ref, n2_ref, fg_ref, wg_ref, wout_ref,
                wfi_ref, wfo_ref, y_ref):
    m = mod_ref[0]
    x = x_ref[...]
    hn = _modnorm(x, n1_ref[...], m[1:2], m[0:1]).astype(BF16)
    gates = _dot(hn, wg_ref[...])
    mix = (jax.nn.sigmoid(gates[:, 0:D_MODEL]) * yr_ref[...]
           + jax.nn.sigmoid(gates[:, D_MODEL:2 * D_MODEL]) * yh_ref[...])
    x1 = x + m[2:3] * _dot(mix.astype(BF16), wout_ref[...])
    h2 = _modnorm(x1, n2_ref[...], m[4:5], m[3:4]).astype(BF16)
    acc = None
    for j in range(D_FF // FF_CHUNK):
        cs = slice(j * FF_CHUNK, (j + 1) * FF_CHUNK)
        a = _dot(h2, wfi_ref[:, cs])
        b = _dot(h2, wfi_ref[:, D_FF + j * FF_CHUNK:D_FF + (j + 1) * FF_CHUNK])
        ff = (a * jax.nn.sigmoid(a) * b).astype(BF16)
        part = _dot(ff, wfo_ref[cs, :])
        acc = part if acc is None else acc + part
    x2 = x1 + m[5:6] * acc
    ms = jnp.mean(x2 * x2, axis=-1, keepdims=True)
    y_ref[...] = x2 * lax.rsqrt(ms + EPS) * fg_ref[...]


def _mlp(x, y_ret, y_hy, mods3, mod_row, norm1, norm2, final_g, w_gate, w_out, w_fi, w_fo):
    B, L, D = x.shape
    T = MLP_ROWS
    flat = lambda a: a.reshape(B * L, D)
    act = pl.BlockSpec((T, D), lambda i: (i, 0))
    y = pl.pallas_call(
        _mlp_kernel,
        grid=(B * L // T,),
        in_specs=[act, act, act,
                  pl.BlockSpec((1, N_MOD, D), lambda i: (mod_row((i * T) // L), 0, 0)),
                  _const_spec((1, D)), _const_spec((1, D)), _const_spec((1, D)),
                  _const_spec((D, N_GATE)),
                  _const_spec((D, D)),
                  _const_spec((D, 2 * D_FF)),
                  _const_spec((D_FF, D))],
        out_specs=act,
        out_shape=jax.ShapeDtypeStruct((B * L, D), F32),
        compiler_params=_params(1),
        name=f"mlp{L}",
    )(flat(x), flat(y_ret), flat(y_hy), mods3, norm1, norm2, final_g, w_gate, w_out, w_fi, w_fo)
    return y.reshape(B, L, D)


def kernel(x_prompt, x_sample, state_ret_fwd, state_ret_bwd, c, c_ctx, norm1_g, norm2_g, w_ada,
           b_ada, w_in, ret_decay_fwd, ret_decay_bwd, hy_conv_w, hy_conv_b, hy_pos_w1, hy_pos_b1,
           hy_pos_w2, hy_pos_b2, hy_pos_w3, hy_sin_freq, hy_bias, w_ret_o, w_hy_o, w_out,
           w_ffn_in, w_ffn_out, final_g):
    assert w_in.shape[0] == 1, "single-layer configuration"
    nb_lat = x_sample.shape[0]
    l_ctx, l_lat = x_prompt.shape[1], x_sample.shape[1]

    cond8 = jnp.zeros((8, D_MODEL), F32).at[0].set(c_ctx).at[1:1 + nb_lat].set(c)
    mods3 = _ada(cond8, w_ada[0], b_ada).reshape(8, N_MOD, D_MODEL)

    w_in_b = w_in[0].astype(BF16)
    w_qkvg = w_in_b[:, 0:N_QKVG]
    w_hy = w_in_b[:, N_QKVG:N_QKVG + N_HY]
    w_gate = w_in_b[:, N_QKVG + N_HY:]
    w_ret_o_b = w_ret_o[0].astype(BF16)
    w_hy_o_b = w_hy_o[0].astype(BF16)
    w_out_b = w_out[0].astype(BF16)
    w_fi_b = w_ffn_in[0].astype(BF16)
    w_fo_b = w_ffn_out[0].astype(BF16)
    norm1 = norm1_g[0][None, :]
    norm2 = norm2_g[0][None, :]
    fg = final_g[None, :]
    dec8 = jnp.broadcast_to(jnp.concatenate([ret_decay_fwd[0], ret_decay_bwd[0]])[:, None],
                            (8, RET_CHUNK))
    w1 = jnp.pad(hy_pos_w1[0], ((0, HY_EMB_PAD - HY_EMB), (0, 0)))
    b1, b2 = hy_pos_b1[0][None, :], hy_pos_b2[0][None, :]
    freq = hy_sin_freq[0][None, :]
    conv_b = hy_conv_b[0][None, :]

    def group(x, mod_row, s0f, s0b, grid_w, emit_state, hy_nb, ret_nb):
        L = x.shape[1]
        blk = min(HY_TBLK, L)
        fw, bw, sgn = _dft_mats(blk)
        filt = _filters(L, blk, fw, sgn, w1, b1, hy_pos_w2[0], b2, hy_pos_w3[0], freq)
        ret = _retention(x, mods3, mod_row, norm1, w_qkvg, dec8, s0f, s0b, w_ret_o_b,
                         emit_state=emit_state, nb=ret_nb)
        y_ret = ret[0]
        y_hy = _hyena(x, mods3, mod_row, norm1, w_hy, hy_conv_w[0], conv_b, fw, bw, filt,
                      hy_bias[0], w_hy_o_b, W=grid_w, b=blk, nb=hy_nb)
        y = _mlp(x, y_ret, y_hy, mods3, mod_row, norm1, norm2, fg, w_gate, w_out_b, w_fi_b,
                 w_fo_b)
        return y, ret[1:]

    y_prompt, (s_f, s_b) = group(x_prompt, lambda b: 0, None, None, l_ctx, True, HY_CTX_SEQS,
                                 RET_CTX_SEQS)
    y_sample, _ = group(x_sample, lambda b: b + 1, state_ret_fwd[:, 0], state_ret_bwd[:, 0],
                        GRID_W, False, 1, 1)
    return (y_prompt, y_sample, s_f[:, None], s_b[:, None])
```

```python
import functools
import math

import numpy as np
import jax
import jax.numpy as jnp
from jax import lax
from jax.experimental import pallas as pl
from jax.experimental.pallas import tpu as pltpu

F32 = jnp.float32
BF16 = jnp.bfloat16

D_MODEL = 1024
RET_HEADS = 4
HEAD_DIM = 128
RET_W = RET_HEADS * HEAD_DIM
HY_W = 512
HY_ORDER = 2
HY_BANDS = 16
HY_EMB = 1 + 2 * HY_BANDS
HY_EMB_PAD = 40
HY_HIDDEN = 64
HY_FAST_DECAY = 0.3
HY_SLOW_DECAY = 1.5
HY_TARGET = 1e-2
D_FF = 2816
N_QKVG = 4 * RET_W
N_HY = 3 * HY_W
N_GATE = 2 * D_MODEL
N_IN = N_QKVG + N_HY + N_GATE
N_MOD = 6
EPS = 1e-6
GRID_W = 64
RET_CHUNK = 256
HY_CBLK = 256
HY_TBLK = 512
CTX_SEQS = 2
MLP_ROWS = 512
FF_CHUNK = 256
ADA_COLS = 768
VMEM_LIMIT = 56 * 1024 * 1024


def _const_spec(shape):
    nd = len(shape)
    return pl.BlockSpec(shape, lambda *_: (0,) * nd, pipeline_mode=pl.Buffered(1))


def _params(n_axes):
    return pltpu.CompilerParams(dimension_semantics=("arbitrary",) * n_axes,
                                vmem_limit_bytes=VMEM_LIMIT)


def _modnorm(x, g, scale, shift):
    ms = jnp.mean(x * x, axis=-1, keepdims=True)
    return (x * lax.rsqrt(ms + EPS) * g) * (1.0 + scale) + shift


def _dot(a, b):
    return jnp.dot(a, b, preferred_element_type=F32)


def _cast_specs(casts, steps):
    in_specs, out_specs, out_shape, args = [], [], [], []
    for arr, cs in casts:
        rows, width = arr.shape
        rb = rows // steps
        assert rb * steps == rows and rb % 16 == 0
        cols = width if cs is None else cs.stop - cs.start
        in_specs.append(pl.BlockSpec((rb, width), lambda g: (g, 0)))
        out_specs.append(pl.BlockSpec((rb, cols), lambda g: (g, 0)))
        out_shape.append(jax.ShapeDtypeStruct((rows, cols), BF16))
        args.append(arr)
    return in_specs, out_specs, out_shape, args


def _do_casts(col_slices, srcs, dsts):
    for cs, src, dst in zip(col_slices, srcs, dsts):
        v = src[...] if cs is None else src[:, cs]
        dst[...] = v.astype(BF16)


def _ada_kernel(c_ref, w_ref, b_ref, o_ref):
    c = c_ref[...]
    s = (c * jax.nn.sigmoid(c)).astype(BF16)
    o_ref[...] = _dot(s, w_ref[...].astype(BF16)) + b_ref[...]


def _ada(cond8, w, b):
    n = w.shape[1]
    return pl.pallas_call(
        _ada_kernel,
        grid=(n // ADA_COLS,),
        in_specs=[pl.BlockSpec((8, D_MODEL), lambda j: (0, 0)),
                  pl.BlockSpec((D_MODEL, ADA_COLS), lambda j: (0, j)),
                  pl.BlockSpec((1, ADA_COLS), lambda j: (0, j))],
        out_specs=pl.BlockSpec((8, ADA_COLS), lambda j: (0, j)),
        out_shape=jax.ShapeDtypeStruct((8, n), F32),
        compiler_params=_params(1),
        name="ada",
    )(cond8, w, b)


@functools.lru_cache(maxsize=None)
def _dft_mats(L):
    n = 2 * L
    t = np.arange(L, dtype=np.int64)
    f = np.arange(L, dtype=np.int64)
    ang = 2.0 * np.pi * ((f[:, None] * t[None, :]) % n).astype(np.float64) / n
    cos = np.cos(ang)
    sin = np.sin(ang)
    nyq = np.where(t % 2 == 0, 1.0, -1.0)
    fwd = np.concatenate([cos, -sin], axis=0)
    fwd[L] = nyq
    wre = np.full((L,), 2.0 / n)
    wre[0] = 1.0 / n
    inv = np.concatenate([cos.T * wre[None, :], -sin.T * (2.0 / n)], axis=1)
    inv[:, L] = nyq / n
    sgn = np.broadcast_to(nyq[:, None], (L, HY_W))
    return (jnp.asarray(fwd, dtype=BF16), jnp.asarray(inv, dtype=BF16),
            np.asarray(sgn, dtype=np.float32))


@functools.lru_cache(maxsize=None)
def _filter_consts(L):
    t = np.linspace(0.0, 1.0, L)[:, None]
    ang = 2.0 * np.pi * np.arange(L, dtype=np.float64)[:, None] / L
    bands = np.linspace(1e-4, HY_BANDS - 1, HY_BANDS)[None]
    z = np.concatenate([t, np.cos(bands * ang), -np.sin(bands * ang)], axis=-1)
    z = np.pad(z, ((0, 0), (0, HY_EMB_PAD - HY_EMB)))
    max_decay = math.log(HY_TARGET) / HY_FAST_DECAY
    min_decay = math.log(HY_TARGET) / HY_SLOW_DECAY
    deltas = np.linspace(min_decay, max_decay, HY_W)
    tdel = t * np.abs(deltas)[None, :]
    return np.asarray(z, np.float32), np.asarray(tdel, np.float32)


def _filter_kernel(z_ref, tdel_ref, sgn_ref, w1_ref, b1_ref, w2_ref, b2_ref, w3_ref, fr_ref,
                   fw_ref, oa_ref, ob_ref, od_ref, *, L, b):
    m = L // b
    hi = lax.Precision.HIGHEST
    fr = fr_ref[...]
    h = jnp.sin(fr * (jnp.dot(z_ref[...], w1_ref[...], precision=hi,
                              preferred_element_type=F32) + b1_ref[...]))
    h = jnp.sin(fr * (jnp.dot(h, w2_ref[...], precision=hi,
                              preferred_element_type=F32) + b2_ref[...]))
    h = jnp.dot(h, w3_ref[...], precision=hi, preferred_element_type=F32)
    win = jnp.exp(-tdel_ref[...])
    sg = sgn_ref[...]
    row0_l = lax.broadcasted_iota(jnp.int32, (L, HY_W), 0) == 0
    row0_b = lax.broadcasted_iota(jnp.int32, (b, HY_W), 0) == 0
    row0_8 = lax.broadcasted_iota(jnp.int32, (8, HY_W), 0) == 0
    for o in range(HY_ORDER):
        base = o * 2 * HY_W
        fwd = h[:, base:base + HY_W] * win
        bwd = jnp.where(row0_l, 0.0, h[:, base + HY_W:base + 2 * HY_W] * win)
        nrm = (jnp.sum(jnp.abs(fwd), axis=0, keepdims=True)
               + jnp.sum(jnp.abs(bwd), axis=0, keepdims=True))
        inv = 1.0 / nrm
        fn = fwd * inv
        bn = bwd * inv
        xr, xn, xi, wr, wn, wi = [], [], [], [], [], []
        for r in range(m):
            p = _dot(fw_ref[...], fn[r * b:(r + 1) * b].astype(BF16))
            q = _dot(fw_ref[...], bn[r * b:(r + 1) * b].astype(BF16))
            xr.append(p[0:b])
            xn.append(p[b:b + 1])
            xi.append(jnp.where(row0_b, 0.0, p[b:2 * b]))
            wr.append(q[0:b])
            wn.append(q[b:b + 1])
            wi.append(jnp.where(row0_b, 0.0, -q[b:2 * b]))

        def emit(d, ka, kn, kb):
            oa_ref[o, d + m - 1] = ka
            ob_ref[o, d + m - 1] = kb
            od_ref[o, d + m - 1] = jnp.where(row0_8, kn, ka[0:8])

        emit(0, xr[0] + wr[0], xn[0] + wn[0], xi[0] + wi[0])
        for d in range(1, m):
            f0 = fn[(d - 1) * b:(d - 1) * b + 1]
            b0 = bn[(d - 1) * b:(d - 1) * b + 1]
            emit(d, xr[d] + sg * (xr[d - 1] - f0), xn[d] + (xn[d - 1] - f0),
                 xi[d] + sg * xi[d - 1])
            emit(-d, wr[d] + sg * (wr[d - 1] - b0), wn[d] + (wn[d - 1] - b0),
                 wi[d] + sg * wi[d - 1])


def _filters(L, b, fw, sgn, w1, b1, w2, b2, w3, freq):
    z, tdel = _filter_consts(L)
    nd = 2 * (L // b) - 1
    args = (jnp.asarray(z), jnp.asarray(tdel), jnp.asarray(sgn), w1, b1, w2, b2, w3, freq, fw)
    return pl.pallas_call(
        functools.partial(_filter_kernel, L=L, b=b),
        out_shape=[jax.ShapeDtypeStruct((HY_ORDER, nd, b, HY_W), F32),
                   jax.ShapeDtypeStruct((HY_ORDER, nd, b, HY_W), F32),
                   jax.ShapeDtypeStruct((HY_ORDER, nd, 8, HY_W), F32)],
        compiler_params=pltpu.CompilerParams(vmem_limit_bytes=VMEM_LIMIT),
        name=f"filters{L}",
    )(*args)


def _ret_kernel(*refs, L, C, nb, has_init, emit_state, casts):
    it = iter(refs)
    x_ref, mod_ref, n1_ref, w_ref, dec_ref = (next(it) for _ in range(5))
    s0f_ref = s0b_ref = sf_ref = sb_ref = None
    if has_init:
        s0f_ref, s0b_ref = next(it), next(it)
    wo_ref = next(it)
    cast_srcs = [next(it) for _ in casts]
    y_ref = next(it)
    if emit_state:
        sf_ref, sb_ref = next(it), next(it)
    cast_dsts = [next(it) for _ in casts]
    mask_scr, vec_scr, cd_scr, g_scr = (next(it) for _ in range(4))
    _do_casts(casts, cast_srcs, cast_dsts)
    n = L // C
    H, E = RET_HEADS, HEAD_DIM
    scale = float(E) ** -0.5

    @pl.when(pl.program_id(0) == 0)
    def _():
        lg = jnp.log(jax.nn.sigmoid(dec_ref[...]))
        cd_scr[...] = jnp.exp(float(C) * lg[:, 0:E])
        ii = lax.broadcasted_iota(jnp.int32, (C, C), 0)
        jj = lax.broadcasted_iota(jnp.int32, (C, C), 1)
        rel = (ii - jj).astype(F32)
        ri = lax.broadcasted_iota(jnp.int32, (C, E), 0).astype(F32)
        for h in range(H):
            lf = lg[h:h + 1, :]
            lb = lg[H + h:H + h + 1, :]
            mf = jnp.where(rel >= 0, jnp.exp(jnp.maximum(rel, 0.0) * lf), 0.0)
            mb = jnp.where(rel <= 0, jnp.exp(jnp.maximum(-rel, 0.0) * lb), 0.0)
            mask_scr[h] = scale * (mf + mb)
            lfe, lbe = lf[:, 0:E], lb[:, 0:E]
            vec_scr[h, 0] = jnp.exp((ri + 1.0) * lfe)
            vec_scr[h, 1] = jnp.exp((float(C) - ri) * lbe)
            vec_scr[h, 2] = scale * jnp.exp((float(C) - 1.0 - ri) * lfe)
            vec_scr[h, 3] = scale * jnp.exp(ri * lbe)

    mod = mod_ref[0]
    tdims = (((0,), (0,)), ((), ()))
    ndims = (((1,), (1,)), ((), ()))
    chains = [(s, h) for s in range(nb) for h in range(H)]
    rows = [slice(c * C, (c + 1) * C) for c in range(n)]
    qkvg = [_dot(_modnorm(x_ref[s], n1_ref[...], mod[1:2], mod[0:1]).astype(BF16), w_ref[...])
            for s in range(nb)]

    def cols(s, part, h):
        return qkvg[s][:, part * RET_W + h * E:part * RET_W + (h + 1) * E]

    qb = [cols(s, 0, h).astype(BF16) for s, h in chains]
    kf = [cols(s, 1, h) for s, h in chains]
    kb = [k.astype(BF16) for k in kf]
    vb = [cols(s, 2, h).astype(BF16) for s, h in chains]
    att = [[lax.dot_general(qb[i][r], kb[i][r], ndims, preferred_element_type=F32) for r in rows]
           for i in range(len(chains))]
    prob = [[(att[i][c] * mask_scr[h]).astype(BF16) for c in range(n)]
            for i, (s, h) in enumerate(chains)]
    out = [[_dot(prob[i][c], vb[i][rows[c]]) for c in range(n)] for i in range(len(chains))]
    kv = []
    for i, (s, h) in enumerate(chains):
        dk2 = jnp.concatenate([vec_scr[h, 2], vec_scr[h, 3]], axis=1)
        per_c = []
        for r in rows:
            k2 = (jnp.concatenate([kf[i][r], kf[i][r]], axis=1) * dk2).astype(BF16)
            per_c.append(lax.dot_general(k2, vb[i][r], tdims, preferred_element_type=F32))
        kv.append(per_c)
    for i, (s, h) in enumerate(chains):
        cdf = cd_scr[h:h + 1, :]
        cdb = cd_scr[H + h:H + h + 1, :]
        sf_in, sb_in = [None] * n, [None] * n
        st = s0f_ref[s, h] if has_init else None
        for c in range(n):
            sf_in[c] = st
            kvc = kv[i][c][0:E]
            st = kvc if st is None else st * cdf + kvc
        if emit_state:
            sf_ref[s, h] = st
        st = s0b_ref[s, h] if has_init else None
        for c in range(n - 1, -1, -1):
            sb_in[c] = st
            kvc = kv[i][c][E:2 * E]
            st = kvc if st is None else st * cdb + kvc
        if emit_state:
            sb_ref[s, h] = st
        for c in range(n):
            if sf_in[c] is not None and sb_in[c] is not None:
                s2 = jnp.concatenate([sf_in[c], sb_in[c]], axis=1).astype(BF16)
                inter = _dot(qb[i][rows[c]], s2)
                out[i][c] = (out[i][c] + inter[:, 0:E] * vec_scr[h, 0]
                             + inter[:, E:2 * E] * vec_scr[h, 1])
            elif sf_in[c] is not None:
                out[i][c] = (out[i][c]
                             + _dot(qb[i][rows[c]], sf_in[c].astype(BF16)) * vec_scr[h, 0])
            elif sb_in[c] is not None:
                out[i][c] = (out[i][c]
                             + _dot(qb[i][rows[c]], sb_in[c].astype(BF16)) * vec_scr[h, 1])
    for i, (s, h) in enumerate(chains):
        for c in range(n):
            o = out[i][c]
            mu = jnp.mean(o, axis=-1, keepdims=True)
            d = o - mu
            var = jnp.mean(d * d, axis=-1, keepdims=True)
            on = d * lax.rsqrt(var + EPS)
            gg = cols(s, 3, h)[rows[c]]
            g_scr[s, rows[c], h * E:(h + 1) * E] = (gg * jax.nn.sigmoid(gg) * on).astype(BF16)
    for s in range(nb):
        y_ref[s] = _dot(g_scr[s], wo_ref[...])


def _retention(x, mods3, mod_row, norm1, w_qkvg, dec8, s0f, s0b, w_o, *, emit_state, nb,
               casts=()):
    B, L, D = x.shape
    C = min(RET_CHUNK, L)
    has_init = s0f is not None
    H, E = RET_HEADS, HEAD_DIM
    in_specs = [pl.BlockSpec((nb, L, D), lambda g: (g, 0, 0)),
                pl.BlockSpec((1, N_MOD, D), lambda g: (mod_row(g * nb), 0, 0)),
                _const_spec((1, D)),
                _const_spec((D, N_QKVG)),
                _const_spec((8, C))]
    args = [x, mods3, norm1, w_qkvg, dec8[:, :C]]
    st_spec = pl.BlockSpec((nb, H, E, E), lambda g: (g, 0, 0, 0))
    if has_init:
        in_specs += [st_spec, st_spec]
        args += [s0f, s0b]
    in_specs.append(_const_spec((RET_W, D)))
    args.append(w_o)
    c_in, c_out, c_shape, c_args = _cast_specs(casts, B // nb)
    in_specs += c_in
    args += c_args
    out_specs = [pl.BlockSpec((nb, L, D), lambda g: (g, 0, 0))]
    out_shape = [jax.ShapeDtypeStruct((B, L, D), F32)]
    if emit_state:
        out_specs += [st_spec, st_spec]
        out_shape += [jax.ShapeDtypeStruct((B, H, E, E), F32)] * 2
    out_specs += c_out
    out_shape += c_shape
    return pl.pallas_call(
        functools.partial(_ret_kernel, L=L, C=C, nb=nb, has_init=has_init,
                          emit_state=emit_state, casts=tuple(cs for _, cs in casts)),
        grid=(B // nb,),
        in_specs=in_specs,
        out_specs=out_specs,
        out_shape=out_shape,
        scratch_shapes=[pltpu.VMEM((H, C, C), F32),
                        pltpu.VMEM((H, 4, C, E), F32),
                        pltpu.VMEM((8, E), F32),
                        pltpu.VMEM((nb, L, RET_W), BF16)],
        compiler_params=_params(1),
        name=f"retention{L}",
    )(*args)


def _hy_kernel(x_ref, mod_ref, n1_ref, w_ref, cw_ref, cb_ref, fw_ref, bw_ref, fa_ref, fb_ref,
               fd_ref, hb_ref, wo_ref, *rest, L, W, b, nb, casts):
    nc = len(casts)
    y_ref = rest[nc]
    _do_casts(casts, rest[0:nc], rest[nc + 1:])
    m = L // b
    CB = HY_CBLK
    nblk = HY_W // CB
    mod = mod_ref[0]
    pos = lax.broadcasted_iota(jnp.int32, (L, CB), 0) % W
    first = pos == 0
    last = pos == W - 1
    hn = [_modnorm(x_ref[s], n1_ref[...], mod[1:2], mod[0:1]).astype(BF16) for s in range(nb)]
    chains = [(s, blk) for s in range(nb) for blk in range(nblk)]

    def short_conv(s, base, blk):
        cs = slice(base + blk * CB, base + (blk + 1) * CB)
        ug = _dot(hn[s], w_ref[:, cs])
        prev = jnp.where(first, 0.0, pltpu.roll(ug, 1, axis=0))
        nxt = jnp.where(last, 0.0, pltpu.roll(ug, L - 1, axis=0))
        u = (prev * cw_ref[0:1, cs] + ug * cw_ref[1:2, cs] + nxt * cw_ref[2:3, cs]
             + cb_ref[:, cs])
        return [u[j * b:(j + 1) * b] for j in range(m)]

    def long_conv(sigs, o):
        spec = [[_dot(fw_ref[...], sj.astype(BF16)) for sj in sig] for sig in sigs]
        prods = []
        for (s, blk), sp in zip(chains, spec):
            cs = slice(blk * CB, (blk + 1) * CB)
            per_i = []
            for i in range(m):
                yre = yim = yim8 = None
                for j in range(m):
                    d = i - j + m - 1
                    sre, sim = sp[j][0:b], sp[j][b:2 * b]
                    ka, kb = fa_ref[o, d, :, cs], fb_ref[o, d, :, cs]
                    tre = sre * ka - sim * kb
                    tim = sre * kb + sim * ka
                    t8 = sre[0:8] * kb[0:8] + sim[0:8] * fd_ref[o, d, :, cs]
                    yre = tre if yre is None else yre + tre
                    yim = tim if yim is None else yim + tim
                    yim8 = t8 if yim8 is None else yim8 + t8
                yim = jnp.concatenate([yim8, yim[8:]], axis=0)
                per_i.append((yre.astype(BF16), yim.astype(BF16)))
            prods.append(per_i)
        return [[_dot(bw_ref[:, 0:b], yre) + _dot(bw_ref[:, b:2 * b], yim) for yre, yim in per_i]
                for per_i in prods]

    hv = [short_conv(s, 0, blk) for s, blk in chains]
    hx1 = [short_conv(s, HY_W, blk) for s, blk in chains]
    hx2 = [short_conv(s, 2 * HY_W, blk) for s, blk in chains]

    def gate(hx, conv, sig, o):
        out = []
        for (s, blk), hxc, cc, sc in zip(chains, hx, conv, sig):
            bias = hb_ref[o:o + 1, blk * CB:(blk + 1) * CB]
            out.append([hxc[i] * (cc[i] + sc[i] * bias) for i in range(m)])
        return out

    z = gate(hx1, long_conv(hv, 0), hv, 0)
    z = gate(hx2, long_conv(z, 1), z, 1)
    for s in range(nb):
        for i in range(m):
            acc = None
            for blk in range(nblk):
                zc = z[chains.index((s, blk))][i].astype(BF16)
                part = _dot(zc, wo_ref[blk * CB:(blk + 1) * CB, :])
                acc = part if acc is None else acc + part
            y_ref[s, i * b:(i + 1) * b, :] = acc


def _hyena(x, mods3, mod_row, norm1, w_hy, conv_w, conv_b, fw, bw, filt, hy_bias, w_o, *, W, b,
           nb, casts=()):
    B, L, D = x.shape
    fa, fb, fd = filt
    nd = fa.shape[1]
    c_in, c_out, c_shape, c_args = _cast_specs(casts, B // nb)
    return pl.pallas_call(
        functools.partial(_hy_kernel, L=L, W=W, b=b, nb=nb,
                          casts=tuple(cs for _, cs in casts)),
        grid=(B // nb,),
        in_specs=[pl.BlockSpec((nb, L, D), lambda g: (g, 0, 0)),
                  pl.BlockSpec((1, N_MOD, D), lambda g: (mod_row(g * nb), 0, 0)),
                  _const_spec((1, D)),
                  _const_spec((D, N_HY)),
                  _const_spec((3, N_HY)),
                  _const_spec((1, N_HY)),
                  _const_spec((2 * b, b)),
                  _const_spec((b, 2 * b)),
                  _const_spec((HY_ORDER, nd, b, HY_W)),
                  _const_spec((HY_ORDER, nd, b, HY_W)),
                  _const_spec((HY_ORDER, nd, 8, HY_W)),
                  _const_spec((HY_ORDER, HY_W)),
                  _const_spec((HY_W, D))] + c_in,
        out_specs=[pl.BlockSpec((nb, L, D), lambda g: (g, 0, 0))] + c_out,
        out_shape=[jax.ShapeDtypeStruct((B, L, D), F32)] + c_shape,
        compiler_params=_params(1),
        name=f"hyena{L}",
    )(x, mods3, norm1, w_hy, conv_w, conv_b, fw, bw, fa, fb, fd, hy_bias, w_o, *c_args)


def _mlp_kernel(x_ref, yr_ref, yh_ref, mod_ref, n1_ref, n2_ref, fg_ref, wg_ref, wout_ref,
                wfi_ref, wfo_ref, y_ref):
    m = mod_ref[0]
    x = x_ref[...]
    hn = _modnorm(x, n1_ref[...], m[1:2], m[0:1]).astype(BF16)
    gates = _dot(hn, wg_ref[...])
    mix = (jax.nn.sigmoid(gates[:, 0:D_MODEL]) * yr_ref[...]
           + jax.nn.sigmoid(gates[:, D_MODEL:2 * D_MODEL]) * yh_ref[...])
    x1 = x + m[2:3] * _dot(mix.astype(BF16), wout_ref[...])
    h2 = _modnorm(x1, n2_ref[...], m[4:5], m[3:4]).astype(BF16)
    acc = None
    for j in range(D_FF // FF_CHUNK):
        cs = slice(j * FF_CHUNK, (j + 1) * FF_CHUNK)
        a = _dot(h2, wfi_ref[:, cs])
        b = _dot(h2, wfi_ref[:, D_FF + j * FF_CHUNK:D_FF + (j + 1) * FF_CHUNK])
        ff = (a * jax.nn.sigmoid(a) * b).astype(BF16)
        part = _dot(ff, wfo_ref[cs, :])
        acc = part if acc is None else acc + part
    x2 = x1 + m[5:6] * acc
    ms = jnp.mean(x2 * x2, axis=-1, keepdims=True)
    y_ref[...] = x2 * lax.rsqrt(ms + EPS) * fg_ref[...]


def _mlp(x, y_ret, y_hy, mods3, mod_row, norm1, norm2, final_g, w_gate, w_out, w_fi, w_fo):
    B, L, D = x.shape
    T = MLP_ROWS
    flat = lambda a: a.reshape(B * L, D)
    act = pl.BlockSpec((T, D), lambda i: (i, 0))
    y = pl.pallas_call(
        _mlp_kernel,
        grid=(B * L // T,),
        in_specs=[act, act, act,
                  pl.BlockSpec((1, N_MOD, D), lambda i: (mod_row((i * T) // L), 0, 0)),
                  _const_spec((1, D)), _const_spec((1, D)), _const_spec((1, D)),
                  _const_spec((D, N_GATE)),
                  _const_spec((D, D)),
                  _const_spec((D, 2 * D_FF)),
                  _const_spec((D_FF, D))],
        out_specs=act,
        out_shape=jax.ShapeDtypeStruct((B * L, D), F32),
        compiler_params=_params(1),
        name=f"mlp{L}",
    )(flat(x), flat(y_ret), flat(y_hy), mods3, norm1, norm2, final_g, w_gate, w_out, w_fi, w_fo)
    return y.reshape(B, L, D)


def kernel(x_prompt, x_sample, state_ret_fwd, state_ret_bwd, c, c_ctx, norm1_g, norm2_g, w_ada,
           b_ada, w_in, ret_decay_fwd, ret_decay_bwd, hy_conv_w, hy_conv_b, hy_pos_w1, hy_pos_b1,
           hy_pos_w2, hy_pos_b2, hy_pos_w3, hy_sin_freq, hy_bias, w_ret_o, w_hy_o, w_out,
           w_ffn_in, w_ffn_out, final_g):
    assert w_in.shape[0] == 1, "single-layer configuration"
    nb_lat = x_sample.shape[0]
    l_ctx = x_prompt.shape[1]

    cond8 = jnp.zeros((8, D_MODEL), F32).at[0].set(c_ctx).at[1:1 + nb_lat].set(c)
    mods3 = _ada(cond8, w_ada[0], b_ada).reshape(8, N_MOD, D_MODEL)

    n_hy_end = N_QKVG + N_HY
    w_qkvg = w_in[0][:, 0:N_QKVG].astype(BF16)
    w_hy = w_in[0][:, N_QKVG:n_hy_end].astype(BF16)
    w_ret_o_b = w_ret_o[0].astype(BF16)
    w_hy_o_b = w_hy_o[0].astype(BF16)
    norm1 = norm1_g[0][None, :]
    norm2 = norm2_g[0][None, :]
    fg = final_g[None, :]
    dec8 = jnp.broadcast_to(jnp.concatenate([ret_decay_fwd[0], ret_decay_bwd[0]])[:, None],
                            (8, RET_CHUNK))
    w1 = jnp.pad(hy_pos_w1[0], ((0, HY_EMB_PAD - HY_EMB), (0, 0)))
    b1, b2 = hy_pos_b1[0][None, :], hy_pos_b2[0][None, :]
    freq = hy_sin_freq[0][None, :]
    conv_b = hy_conv_b[0][None, :]

    def mixers(x, mod_row, s0f, s0b, grid_w, emit_state, nb, ret_casts=(), hy_casts=()):
        L = x.shape[1]
        blk = min(HY_TBLK, L)
        fw, bw, sgn = _dft_mats(blk)
        filt = _filters(L, blk, fw, sgn, w1, b1, hy_pos_w2[0], b2, hy_pos_w3[0], freq)
        ret = _retention(x, mods3, mod_row, norm1, w_qkvg, dec8, s0f, s0b, w_ret_o_b,
                         emit_state=emit_state, nb=nb, casts=ret_casts)
        hy = _hyena(x, mods3, mod_row, norm1, w_hy, hy_conv_w[0], conv_b, fw, bw, filt,
                    hy_bias[0], w_hy_o_b, W=grid_w, b=blk, nb=nb, casts=hy_casts)
        return ret, hy

    ctx_row = lambda b: 0
    lat_row = lambda b: b + 1
    ret_c, hy_c = mixers(x_prompt, ctx_row, None, None, l_ctx, True, CTX_SEQS,
                         ret_casts=[(w_ffn_in[0], None)],
                         hy_casts=[(w_ffn_out[0], None), (w_out[0], None),
                                   (w_in[0], slice(n_hy_end, N_IN))])
    y_ret_c, s_f, s_b, w_fi_b = ret_c
    y_hy_c, w_fo_b, w_out_b, w_gate = hy_c
    y_prompt = _mlp(x_prompt, y_ret_c, y_hy_c, mods3, ctx_row, norm1, norm2, fg, w_gate, w_out_b,
                    w_fi_b, w_fo_b)
    ret_l, hy_l = mixers(x_sample, lat_row, state_ret_fwd[:, 0], state_ret_bwd[:, 0], GRID_W,
                         False, 1)
    y_sample = _mlp(x_sample, ret_l[0], hy_l[0], mods3, lat_row, norm1, norm2, fg, w_gate,
                    w_out_b, w_fi_b, w_fo_b)
    return (y_prompt, y_sample, s_f[:, None], s_b[:, None])
```

```python
import functools
import math

import numpy as np
import jax
import jax.numpy as jnp
from jax import lax
from jax.experimental import pallas as pl
from jax.experimental.pallas import tpu as pltpu

F32 = jnp.float32
BF16 = jnp.bfloat16

D_MODEL = 1024
RET_HEADS = 4
HEAD_DIM = 128
RET_W = RET_HEADS * HEAD_DIM
HY_W = 512
HY_ORDER = 2
HY_BANDS = 16
HY_EMB = 1 + 2 * HY_BANDS
HY_EMB_PAD = 40
HY_HIDDEN = 64
HY_FAST_DECAY = 0.3
HY_SLOW_DECAY = 1.5
HY_TARGET = 1e-2
D_FF = 2816
N_QKVG = 4 * RET_W
N_HY = 3 * HY_W
N_GATE = 2 * D_MODEL
N_IN = N_QKVG + N_HY + N_GATE
N_MOD = 6
EPS = 1e-6
GRID_W = 64
RET_CHUNK = 256
HY_CBLK = 256
HY_TBLK = 512
CTX_SEQS = 2
MLP_ROWS = 512
FF_CHUNK = 256
ADA_COLS = 768
VMEM_LIMIT = 56 * 1024 * 1024


def _const_spec(shape):
    nd = len(shape)
    return pl.BlockSpec(shape, lambda *_: (0,) * nd, pipeline_mode=pl.Buffered(1))


def _params(n_axes):
    return pltpu.CompilerParams(dimension_semantics=("arbitrary",) * n_axes,
                                vmem_limit_bytes=VMEM_LIMIT)


def _modnorm(x, g, scale, shift):
    ms = jnp.mean(x * x, axis=-1, keepdims=True)
    return (x * lax.rsqrt(ms + EPS) * g) * (1.0 + scale) + shift


def _dot(a, b):
    return jnp.dot(a, b, preferred_element_type=F32)


def _cast_specs(casts, steps):
    in_specs, out_specs, out_shape, args = [], [], [], []
    for arr, col_slices in casts:
        rows, width = arr.shape
        rb = rows // steps
        assert rb * steps == rows and rb % 16 == 0
        in_specs.append(pl.BlockSpec((rb, width), lambda g: (g, 0)))
        args.append(arr)
        for cs in col_slices or (slice(0, width),):
            cols = cs.stop - cs.start
            out_specs.append(pl.BlockSpec((rb, cols), lambda g: (g, 0)))
            out_shape.append(jax.ShapeDtypeStruct((rows, cols), BF16))
    return in_specs, out_specs, out_shape, args


def _n_cast_outputs(col_slices_per_src):
    return sum(1 if s is None else len(s) for s in col_slices_per_src)


def _do_casts(col_slices_per_src, srcs, dsts):
    dsts = iter(dsts)
    for col_slices, src in zip(col_slices_per_src, srcs):
        if col_slices is None:
            next(dsts)[...] = src[...].astype(BF16)
        else:
            for cs in col_slices:
                next(dsts)[...] = src[:, cs].astype(BF16)


def _ada_kernel(c_ref, w_ref, b_ref, *rest, casts):
    nc = len(casts)
    o_ref = rest[nc]
    _do_casts(casts, rest[0:nc], rest[nc + 1:])
    c = c_ref[...]
    s = (c * jax.nn.sigmoid(c)).astype(BF16)
    o_ref[...] = _dot(s, w_ref[...].astype(BF16)) + b_ref[...]


def _ada(cond8, w, b, casts=()):
    n = w.shape[1]
    steps = n // ADA_COLS
    c_in, c_out, c_shape, c_args = _cast_specs(casts, steps)
    return pl.pallas_call(
        functools.partial(_ada_kernel, casts=tuple(cs for _, cs in casts)),
        grid=(steps,),
        in_specs=[pl.BlockSpec((8, D_MODEL), lambda j: (0, 0)),
                  pl.BlockSpec((D_MODEL, ADA_COLS), lambda j: (0, j)),
                  pl.BlockSpec((1, ADA_COLS), lambda j: (0, j))] + c_in,
        out_specs=[pl.BlockSpec((8, ADA_COLS), lambda j: (0, j))] + c_out,
        out_shape=[jax.ShapeDtypeStruct((8, n), F32)] + c_shape,
        compiler_params=_params(1),
        name="ada",
    )(cond8, w, b, *c_args)


@functools.lru_cache(maxsize=None)
def _dft_mats(L):
    n = 2 * L
    t = np.arange(L, dtype=np.int64)
    f = np.arange(L, dtype=np.int64)
    ang = 2.0 * np.pi * ((f[:, None] * t[None, :]) % n).astype(np.float64) / n
    cos = np.cos(ang)
    sin = np.sin(ang)
    nyq = np.where(t % 2 == 0, 1.0, -1.0)
    fwd = np.concatenate([cos, -sin], axis=0)
    fwd[L] = nyq
    wre = np.full((L,), 2.0 / n)
    wre[0] = 1.0 / n
    inv = np.concatenate([cos.T * wre[None, :], -sin.T * (2.0 / n)], axis=1)
    inv[:, L] = nyq / n
    sgn = np.broadcast_to(nyq[:, None], (L, HY_W))
    return (jnp.asarray(fwd, dtype=BF16), jnp.asarray(inv, dtype=BF16),
            np.asarray(sgn, dtype=np.float32))


@functools.lru_cache(maxsize=None)
def _filter_consts(L):
    t = np.linspace(0.0, 1.0, L)[:, None]
    ang = 2.0 * np.pi * np.arange(L, dtype=np.float64)[:, None] / L
    bands = np.linspace(1e-4, HY_BANDS - 1, HY_BANDS)[None]
    z = np.concatenate([t, np.cos(bands * ang), -np.sin(bands * ang)], axis=-1)
    z = np.pad(z, ((0, 0), (0, HY_EMB_PAD - HY_EMB)))
    max_decay = math.log(HY_TARGET) / HY_FAST_DECAY
    min_decay = math.log(HY_TARGET) / HY_SLOW_DECAY
    deltas = np.linspace(min_decay, max_decay, HY_W)
    tdel = t * np.abs(deltas)[None, :]
    return np.asarray(z, np.float32), np.asarray(tdel, np.float32)


def _filter_kernel(z_ref, tdel_ref, sgn_ref, w1_ref, b1_ref, w2_ref, b2_ref, w3_ref, fr_ref,
                   fw_ref, oa_ref, ob_ref, od_ref, *, L, b):
    m = L // b
    hi = lax.Precision.HIGHEST
    fr = fr_ref[...]
    h = jnp.sin(fr * (jnp.dot(z_ref[...], w1_ref[...], precision=hi,
                              preferred_element_type=F32) + b1_ref[...]))
    h = jnp.sin(fr * (jnp.dot(h, w2_ref[...], precision=hi,
                              preferred_element_type=F32) + b2_ref[...]))
    h = jnp.dot(h, w3_ref[...], precision=hi, preferred_element_type=F32)
    win = jnp.exp(-tdel_ref[...])
    sg = sgn_ref[...]
    row0_l = lax.broadcasted_iota(jnp.int32, (L, HY_W), 0) == 0
    row0_b = lax.broadcasted_iota(jnp.int32, (b, HY_W), 0) == 0
    row0_8 = lax.broadcasted_iota(jnp.int32, (8, HY_W), 0) == 0
    for o in range(HY_ORDER):
        base = o * 2 * HY_W
        fwd = h[:, base:base + HY_W] * win
        bwd = jnp.where(row0_l, 0.0, h[:, base + HY_W:base + 2 * HY_W] * win)
        nrm = (jnp.sum(jnp.abs(fwd), axis=0, keepdims=True)
               + jnp.sum(jnp.abs(bwd), axis=0, keepdims=True))
        inv = 1.0 / nrm
        fn = fwd * inv
        bn = bwd * inv
        xr, xn, xi, wr, wn, wi = [], [], [], [], [], []
        for r in range(m):
            p = _dot(fw_ref[...], fn[r * b:(r + 1) * b].astype(BF16))
            q = _dot(fw_ref[...], bn[r * b:(r + 1) * b].astype(BF16))
            xr.append(p[0:b])
            xn.append(p[b:b + 1])
            xi.append(jnp.where(row0_b, 0.0, p[b:2 * b]))
            wr.append(q[0:b])
            wn.append(q[b:b + 1])
            wi.append(jnp.where(row0_b, 0.0, -q[b:2 * b]))

        def emit(d, ka, kn, kb):
            oa_ref[o, d + m - 1] = ka
            ob_ref[o, d + m - 1] = kb
            od_ref[o, d + m - 1] = jnp.where(row0_8, kn, ka[0:8])

        emit(0, xr[0] + wr[0], xn[0] + wn[0], xi[0] + wi[0])
        for d in range(1, m):
            f0 = fn[(d - 1) * b:(d - 1) * b + 1]
            b0 = bn[(d - 1) * b:(d - 1) * b + 1]
            emit(d, xr[d] + sg * (xr[d - 1] - f0), xn[d] + (xn[d - 1] - f0),
                 xi[d] + sg * xi[d - 1])
            emit(-d, wr[d] + sg * (wr[d - 1] - b0), wn[d] + (wn[d - 1] - b0),
                 wi[d] + sg * wi[d - 1])


def _filters(L, b, fw, sgn, w1, b1, w2, b2, w3, freq):
    z, tdel = _filter_consts(L)
    nd = 2 * (L // b) - 1
    args = (jnp.asarray(z), jnp.asarray(tdel), jnp.asarray(sgn), w1, b1, w2, b2, w3, freq, fw)
    return pl.pallas_call(
        functools.partial(_filter_kernel, L=L, b=b),
        out_shape=[jax.ShapeDtypeStruct((HY_ORDER, nd, b, HY_W), F32),
                   jax.ShapeDtypeStruct((HY_ORDER, nd, b, HY_W), F32),
                   jax.ShapeDtypeStruct((HY_ORDER, nd, 8, HY_W), F32)],
        compiler_params=pltpu.CompilerParams(vmem_limit_bytes=VMEM_LIMIT),
        name=f"filters{L}",
    )(*args)


def _ret_kernel(*refs, L, C, nb, has_init, emit_state, casts):
    it = iter(refs)
    x_ref, mod_ref, n1_ref, w_ref, dec_ref = (next(it) for _ in range(5))
    s0f_ref = s0b_ref = sf_ref = sb_ref = None
    if has_init:
        s0f_ref, s0b_ref = next(it), next(it)
    wo_ref = next(it)
    cast_srcs = [next(it) for _ in casts]
    y_ref = next(it)
    if emit_state:
        sf_ref, sb_ref = next(it), next(it)
    cast_dsts = [next(it) for _ in range(_n_cast_outputs(casts))]
    mask_scr, vec_scr, cd_scr, g_scr = (next(it) for _ in range(4))
    _do_casts(casts, cast_srcs, cast_dsts)
    n = L // C
    H, E = RET_HEADS, HEAD_DIM
    scale = float(E) ** -0.5

    @pl.when(pl.program_id(0) == 0)
    def _():
        lg = jnp.log(jax.nn.sigmoid(dec_ref[...]))
        cd_scr[...] = jnp.exp(float(C) * lg[:, 0:E])
        ii = lax.broadcasted_iota(jnp.int32, (C, C), 0)
        jj = lax.broadcasted_iota(jnp.int32, (C, C), 1)
        rel = (ii - jj).astype(F32)
        ri = lax.broadcasted_iota(jnp.int32, (C, E), 0).astype(F32)
        for h in range(H):
            lf = lg[h:h + 1, :]
            lb = lg[H + h:H + h + 1, :]
            mf = jnp.where(rel >= 0, jnp.exp(jnp.maximum(rel, 0.0) * lf), 0.0)
            mb = jnp.where(rel <= 0, jnp.exp(jnp.maximum(-rel, 0.0) * lb), 0.0)
            mask_scr[h] = scale * (mf + mb)
            lfe, lbe = lf[:, 0:E], lb[:, 0:E]
            vec_scr[h, 0] = jnp.exp((ri + 1.0) * lfe)
            vec_scr[h, 1] = jnp.exp((float(C) - ri) * lbe)
            vec_scr[h, 2] = scale * jnp.exp((float(C) - 1.0 - ri) * lfe)
            vec_scr[h, 3] = scale * jnp.exp(ri * lbe)

    mod = mod_ref[0]
    tdims = (((0,), (0,)), ((), ()))
    ndims = (((1,), (1,)), ((), ()))
    chains = [(s, h) for s in range(nb) for h in range(H)]
    rows = [slice(c * C, (c + 1) * C) for c in range(n)]
    qkvg = [_dot(_modnorm(x_ref[s], n1_ref[...], mod[1:2], mod[0:1]).astype(BF16), w_ref[...])
            for s in range(nb)]

    def cols(s, part, h):
        return qkvg[s][:, part * RET_W + h * E:part * RET_W + (h + 1) * E]

    qb = [cols(s, 0, h).astype(BF16) for s, h in chains]
    kf = [cols(s, 1, h) for s, h in chains]
    kb = [k.astype(BF16) for k in kf]
    vb = [cols(s, 2, h).astype(BF16) for s, h in chains]
    att = [[lax.dot_general(qb[i][r], kb[i][r], ndims, preferred_element_type=F32) for r in rows]
           for i in range(len(chains))]
    prob = [[(att[i][c] * mask_scr[h]).astype(BF16) for c in range(n)]
            for i, (s, h) in enumerate(chains)]
    out = [[_dot(prob[i][c], vb[i][rows[c]]) for c in range(n)] for i in range(len(chains))]
    kv = []
    for i, (s, h) in enumerate(chains):
        dk2 = jnp.concatenate([vec_scr[h, 2], vec_scr[h, 3]], axis=1)
        per_c = []
        for r in rows:
            k2 = (jnp.concatenate([kf[i][r], kf[i][r]], axis=1) * dk2).astype(BF16)
            per_c.append(lax.dot_general(k2, vb[i][r], tdims, preferred_element_type=F32))
        kv.append(per_c)
    for i, (s, h) in enumerate(chains):
        cdf = cd_scr[h:h + 1, :]
        cdb = cd_scr[H + h:H + h + 1, :]
        sf_in, sb_in = [None] * n, [None] * n
        st = s0f_ref[s, h] if has_init else None
        for c in range(n):
            sf_in[c] = st
            kvc = kv[i][c][0:E]
            st = kvc if st is None else st * cdf + kvc
        if emit_state:
            sf_ref[s, h] = st
        st = s0b_ref[s, h] if has_init else None
        for c in range(n - 1, -1, -1):
            sb_in[c] = st
            kvc = kv[i][c][E:2 * E]
            st = kvc if st is None else st * cdb + kvc
        if emit_state:
            sb_ref[s, h] = st
        for c in range(n):
            if sf_in[c] is not None and sb_in[c] is not None:
                s2 = jnp.concatenate([sf_in[c], sb_in[c]], axis=1).astype(BF16)
                inter = _dot(qb[i][rows[c]], s2)
                out[i][c] = (out[i][c] + inter[:, 0:E] * vec_scr[h, 0]
                             + inter[:, E:2 * E] * vec_scr[h, 1])
            elif sf_in[c] is not None:
                out[i][c] = (out[i][c]
                             + _dot(qb[i][rows[c]], sf_in[c].astype(BF16)) * vec_scr[h, 0])
            elif sb_in[c] is not None:
                out[i][c] = (out[i][c]
                             + _dot(qb[i][rows[c]], sb_in[c].astype(BF16)) * vec_scr[h, 1])
    for i, (s, h) in enumerate(chains):
        for c in range(n):
            o = out[i][c]
            mu = jnp.mean(o, axis=-1, keepdims=True)
            d = o - mu
            var = jnp.mean(d * d, axis=-1, keepdims=True)
            on = d * lax.rsqrt(var + EPS)
            gg = cols(s, 3, h)[rows[c]]
            g_scr[s, rows[c], h * E:(h + 1) * E] = (gg * jax.nn.sigmoid(gg) * on).astype(BF16)
    for s in range(nb):
        y_ref[s] = _dot(g_scr[s], wo_ref[...])


def _retention(x, mods3, mod_row, norm1, w_qkvg, dec8, s0f, s0b, w_o, *, emit_state, nb,
               casts=()):
    B, L, D = x.shape
    C = min(RET_CHUNK, L)
    has_init = s0f is not None
    H, E = RET_HEADS, HEAD_DIM
    in_specs = [pl.BlockSpec((nb, L, D), lambda g: (g, 0, 0)),
                pl.BlockSpec((1, N_MOD, D), lambda g: (mod_row(g * nb), 0, 0)),
                _const_spec((1, D)),
                _const_spec((D, N_QKVG)),
                _const_spec((8, C))]
    args = [x, mods3, norm1, w_qkvg, dec8[:, :C]]
    st_spec = pl.BlockSpec((nb, H, E, E), lambda g: (g, 0, 0, 0))
    if has_init:
        in_specs += [st_spec, st_spec]
        args += [s0f, s0b]
    in_specs.append(_const_spec((RET_W, D)))
    args.append(w_o)
    c_in, c_out, c_shape, c_args = _cast_specs(casts, B // nb)
    in_specs += c_in
    args += c_args
    out_specs = [pl.BlockSpec((nb, L, D), lambda g: (g, 0, 0))]
    out_shape = [jax.ShapeDtypeStruct((B, L, D), F32)]
    if emit_state:
        out_specs += [st_spec, st_spec]
        out_shape += [jax.ShapeDtypeStruct((B, H, E, E), F32)] * 2
    out_specs += c_out
    out_shape += c_shape
    return pl.pallas_call(
        functools.partial(_ret_kernel, L=L, C=C, nb=nb, has_init=has_init,
                          emit_state=emit_state, casts=tuple(cs for _, cs in casts)),
        grid=(B // nb,),
        in_specs=in_specs,
        out_specs=out_specs,
        out_shape=out_shape,
        scratch_shapes=[pltpu.VMEM((H, C, C), F32),
                        pltpu.VMEM((H, 4, C, E), F32),
                        pltpu.VMEM((8, E), F32),
                        pltpu.VMEM((nb, L, RET_W), BF16)],
        compiler_params=_params(1),
        name=f"retention{L}",
    )(*args)


def _hy_kernel(x_ref, mod_ref, n1_ref, w_ref, cw_ref, cb_ref, fw_ref, bw_ref, fa_ref, fb_ref,
               fd_ref, hb_ref, wo_ref, *rest, L, W, b, nb, casts):
    nc = len(casts)
    y_ref = rest[nc]
    _do_casts(casts, rest[0:nc], rest[nc + 1:])
    m = L // b
    CB = HY_CBLK
    nblk = HY_W // CB
    mod = mod_ref[0]
    pos = lax.broadcasted_iota(jnp.int32, (L, CB), 0) % W
    first = pos == 0
    last = pos == W - 1
    hn = [_modnorm(x_ref[s], n1_ref[...], mod[1:2], mod[0:1]).astype(BF16) for s in range(nb)]
    chains = [(s, blk) for s in range(nb) for blk in range(nblk)]

    def short_conv(s, base, blk):
        cs = slice(base + blk * CB, base + (blk + 1) * CB)
        ug = _dot(hn[s], w_ref[:, cs])
        prev = jnp.where(first, 0.0, pltpu.roll(ug, 1, axis=0))
        nxt = jnp.where(last, 0.0, pltpu.roll(ug, L - 1, axis=0))
        u = (prev * cw_ref[0:1, cs] + ug * cw_ref[1:2, cs] + nxt * cw_ref[2:3, cs]
             + cb_ref[:, cs])
        return [u[j * b:(j + 1) * b] for j in range(m)]

    def long_conv(sigs, o):
        spec = [[_dot(fw_ref[...], sj.astype(BF16)) for sj in sig] for sig in sigs]
        prods = []
        for (s, blk), sp in zip(chains, spec):
            cs = slice(blk * CB, (blk + 1) * CB)
            per_i = []
            for i in range(m):
                yre = yim = yim8 = None
                for j in range(m):
                    d = i - j + m - 1
                    sre, sim = sp[j][0:b], sp[j][b:2 * b]
                    ka, kb = fa_ref[o, d, :, cs], fb_ref[o, d, :, cs]
                    tre = sre * ka - sim * kb
                    tim = sre * kb + sim * ka
                    t8 = sre[0:8] * kb[0:8] + sim[0:8] * fd_ref[o, d, :, cs]
                    yre = tre if yre is None else yre + tre
                    yim = tim if yim is None else yim + tim
                    yim8 = t8 if yim8 is None else yim8 + t8
                yim = jnp.concatenate([yim8, yim[8:]], axis=0)
                per_i.append((yre.astype(BF16), yim.astype(BF16)))
            prods.append(per_i)
        return [[_dot(bw_ref[:, 0:b], yre) + _dot(bw_ref[:, b:2 * b], yim) for yre, yim in per_i]
                for per_i in prods]

    hv = [short_conv(s, 0, blk) for s, blk in chains]
    hx1 = [short_conv(s, HY_W, blk) for s, blk in chains]
    hx2 = [short_conv(s, 2 * HY_W, blk) for s, blk in chains]

    def gate(hx, conv, sig, o):
        out = []
        for (s, blk), hxc, cc, sc in zip(chains, hx, conv, sig):
            bias = hb_ref[o:o + 1, blk * CB:(blk + 1) * CB]
            out.append([hxc[i] * (cc[i] + sc[i] * bias) for i in range(m)])
        return out

    z = gate(hx1, long_conv(hv, 0), hv, 0)
    z = gate(hx2, long_conv(z, 1), z, 1)
    for s in range(nb):
        for i in range(m):
            acc = None
            for blk in range(nblk):
                zc = z[chains.index((s, blk))][i].astype(BF16)
                part = _dot(zc, wo_ref[blk * CB:(blk + 1) * CB, :])
                acc = part if acc is None else acc + part
            y_ref[s, i * b:(i + 1) * b, :] = acc


def _hyena(x, mods3, mod_row, norm1, w_hy, conv_w, conv_b, fw, bw, filt, hy_bias, w_o, *, W, b,
           nb, casts=()):
    B, L, D = x.shape
    fa, fb, fd = filt
    nd = fa.shape[1]
    c_in, c_out, c_shape, c_args = _cast_specs(casts, B // nb)
    return pl.pallas_call(
        functools.partial(_hy_kernel, L=L, W=W, b=b, nb=nb,
                          casts=tuple(cs for _, cs in casts)),
        grid=(B // nb,),
        in_specs=[pl.BlockSpec((nb, L, D), lambda g: (g, 0, 0)),
                  pl.BlockSpec((1, N_MOD, D), lambda g: (mod_row(g * nb), 0, 0)),
                  _const_spec((1, D)),
                  _const_spec((D, N_HY)),
                  _const_spec((3, N_HY)),
                  _const_spec((1, N_HY)),
                  _const_spec((2 * b, b)),
                  _const_spec((b, 2 * b)),
                  _const_spec((HY_ORDER, nd, b, HY_W)),
                  _const_spec((HY_ORDER, nd, b, HY_W)),
                  _const_spec((HY_ORDER, nd, 8, HY_W)),
                  _const_spec((HY_ORDER, HY_W)),
                  _const_spec((HY_W, D))] + c_in,
        out_specs=[pl.BlockSpec((nb, L, D), lambda g: (g, 0, 0))] + c_out,
        out_shape=[jax.ShapeDtypeStruct((B, L, D), F32)] + c_shape,
        compiler_params=_params(1),
        name=f"hyena{L}",
    )(x, mods3, norm1, w_hy, conv_w, conv_b, fw, bw, fa, fb, fd, hy_bias, w_o, *c_args)


def _mlp_kernel(x_ref, yr_ref, yh_ref, mod_ref, n1_ref, n2_ref, fg_ref, wg_ref, wout_ref,
                wfi_ref, wfo_ref, y_ref):
    m = mod_ref[0]
    x = x_ref[...]
    hn = _modnorm(x, n1_ref[...], m[1:2], m[0:1]).astype(BF16)
    gates = _dot(hn, wg_ref[...])
    mix = (jax.nn.sigmoid(gates[:, 0:D_MODEL]) * yr_ref[...]
           + jax.nn.sigmoid(gates[:, D_MODEL:2 * D_MODEL]) * yh_ref[...])
    x1 = x + m[2:3] * _dot(mix.astype(BF16), wout_ref[...])
    h2 = _modnorm(x1, n2_ref[...], m[4:5], m[3:4]).astype(BF16)
    acc = None
    for j in range(D_FF // FF_CHUNK):
        cs = slice(j * FF_CHUNK, (j + 1) * FF_CHUNK)
        a = _dot(h2, wfi_ref[:, cs])
        b = _dot(h2, wfi_ref[:, D_FF + j * FF_CHUNK:D_FF + (j + 1) * FF_CHUNK])
        ff = (a * jax.nn.sigmoid(a) * b).astype(BF16)
        part = _dot(ff, wfo_ref[cs, :])
        acc = part if acc is None else acc + part
    x2 = x1 + m[5:6] * acc
    ms = jnp.mean(x2 * x2, axis=-1, keepdims=True)
    y_ref[...] = x2 * lax.rsqrt(ms + EPS) * fg_ref[...]


def _mlp(x, y_ret, y_hy, mods3, mod_row, norm1, norm2, final_g, w_gate, w_out, w_fi, w_fo):
    B, L, D = x.shape
    T = MLP_ROWS
    flat = lambda a: a.reshape(B * L, D)
    act = pl.BlockSpec((T, D), lambda i: (i, 0))
    y = pl.pallas_call(
        _mlp_kernel,
        grid=(B * L // T,),
        in_specs=[act, act, act,
                  pl.BlockSpec((1, N_MOD, D), lambda i: (mod_row((i * T) // L), 0, 0)),
                  _const_spec((1, D)), _const_spec((1, D)), _const_spec((1, D)),
                  _const_spec((D, N_GATE)),
                  _const_spec((D, D)),
                  _const_spec((D, 2 * D_FF)),
                  _const_spec((D_FF, D))],
        out_specs=act,
        out_shape=jax.ShapeDtypeStruct((B * L, D), F32),
        compiler_params=_params(1),
        name=f"mlp{L}",
    )(flat(x), flat(y_ret), flat(y_hy), mods3, norm1, norm2, final_g, w_gate, w_out, w_fi, w_fo)
    return y.reshape(B, L, D)


def kernel(x_prompt, x_sample, state_ret_fwd, state_ret_bwd, c, c_ctx, norm1_g, norm2_g, w_ada,
           b_ada, w_in, ret_decay_fwd, ret_decay_bwd, hy_conv_w, hy_conv_b, hy_pos_w1, hy_pos_b1,
           hy_pos_w2, hy_pos_b2, hy_pos_w3, hy_sin_freq, hy_bias, w_ret_o, w_hy_o, w_out,
           w_ffn_in, w_ffn_out, final_g):
    assert w_in.shape[0] == 1, "single-layer configuration"
    nb_lat = x_sample.shape[0]
    l_ctx = x_prompt.shape[1]

    cond8 = jnp.zeros((8, D_MODEL), F32).at[0].set(c_ctx).at[1:1 + nb_lat].set(c)
    n_hy_end = N_QKVG + N_HY
    w_in_parts = (slice(0, N_QKVG), slice(N_QKVG, n_hy_end), slice(n_hy_end, N_IN))
    mods, w_qkvg, w_hy, w_gate, w_ret_o_b, w_hy_o_b = _ada(
        cond8, w_ada[0], b_ada,
        casts=[(w_in[0], w_in_parts), (w_ret_o[0], None), (w_hy_o[0], None)])
    mods3 = mods.reshape(8, N_MOD, D_MODEL)
    norm1 = norm1_g[0][None, :]
    norm2 = norm2_g[0][None, :]
    fg = final_g[None, :]
    dec8 = jnp.broadcast_to(jnp.concatenate([ret_decay_fwd[0], ret_decay_bwd[0]])[:, None],
                            (8, RET_CHUNK))
    w1 = jnp.pad(hy_pos_w1[0], ((0, HY_EMB_PAD - HY_EMB), (0, 0)))
    b1, b2 = hy_pos_b1[0][None, :], hy_pos_b2[0][None, :]
    freq = hy_sin_freq[0][None, :]
    conv_b = hy_conv_b[0][None, :]

    def mixers(x, mod_row, s0f, s0b, grid_w, emit_state, nb, ret_casts=(), hy_casts=()):
        L = x.shape[1]
        blk = min(HY_TBLK, L)
        fw, bw, sgn = _dft_mats(blk)
        filt = _filters(L, blk, fw, sgn, w1, b1, hy_pos_w2[0], b2, hy_pos_w3[0], freq)
        ret = _retention(x, mods3, mod_row, norm1, w_qkvg, dec8, s0f, s0b, w_ret_o_b,
                         emit_state=emit_state, nb=nb, casts=ret_casts)
        hy = _hyena(x, mods3, mod_row, norm1, w_hy, hy_conv_w[0], conv_b, fw, bw, filt,
                    hy_bias[0], w_hy_o_b, W=grid_w, b=blk, nb=nb, casts=hy_casts)
        return ret, hy

    ctx_row = lambda b: 0
    lat_row = lambda b: b + 1
    ret_c, hy_c = mixers(x_prompt, ctx_row, None, None, l_ctx, True, CTX_SEQS,
                         ret_casts=[(w_ffn_in[0], None)],
                         hy_casts=[(w_ffn_out[0], None), (w_out[0], None)])
    y_ret_c, s_f, s_b, w_fi_b = ret_c
    y_hy_c, w_fo_b, w_out_b = hy_c
    y_prompt = _mlp(x_prompt, y_ret_c, y_hy_c, mods3, ctx_row, norm1, norm2, fg, w_gate, w_out_b,
                    w_fi_b, w_fo_b)
    ret_l, hy_l = mixers(x_sample, lat_row, state_ret_fwd[:, 0], state_ret_bwd[:, 0], GRID_W,
                         False, 1)
    y_sample = _mlp(x_sample, ret_l[0], hy_l[0], mods3, lat_row, norm1, norm2, fg, w_gate,
                    w_out_b, w_fi_b, w_fo_b)
    return (y_prompt, y_sample, s_f[:, None], s_b[:, None])
```

```python
import functools
import math

import numpy as np
import jax
import jax.numpy as jnp
from jax import lax
from jax.experimental import pallas as pl
from jax.experimental.pallas import tpu as pltpu

F32 = jnp.float32
BF16 = jnp.bfloat16

D_MODEL = 1024
RET_HEADS = 4
HEAD_DIM = 128
RET_W = RET_HEADS * HEAD_DIM
HY_W = 512
HY_ORDER = 2
HY_BANDS = 16
HY_EMB = 1 + 2 * HY_BANDS
HY_EMB_PAD = 40
HY_HIDDEN = 64
HY_FAST_DECAY = 0.3
HY_SLOW_DECAY = 1.5
HY_TARGET = 1e-2
D_FF = 2816
N_QKVG = 4 * RET_W
N_HY = 3 * HY_W
N_GATE = 2 * D_MODEL
N_IN = N_QKVG + N_HY + N_GATE
N_MOD = 6
EPS = 1e-6
GRID_W = 64
RET_CHUNK = 256
HY_CBLK = 256
HY_TBLK = 512
CTX_SEQS = 2
MLP_ROWS = 512
FF_CHUNK = 256
ADA_COLS = 768
VMEM_LIMIT = 56 * 1024 * 1024


def _const_spec(shape):
    nd = len(shape)
    return pl.BlockSpec(shape, lambda *_: (0,) * nd, pipeline_mode=pl.Buffered(1))


def _params(n_axes):
    return pltpu.CompilerParams(dimension_semantics=("arbitrary",) * n_axes,
                                vmem_limit_bytes=VMEM_LIMIT)


def _modnorm(x, g, scale, shift):
    ms = jnp.mean(x * x, axis=-1, keepdims=True)
    return (x * lax.rsqrt(ms + EPS) * g) * (1.0 + scale) + shift


def _dot(a, b):
    return jnp.dot(a, b, preferred_element_type=F32)


def _cast_specs(casts, steps):
    in_specs, out_specs, out_shape, args = [], [], [], []
    for arr, col_slices in casts:
        rows, width = arr.shape
        rb = rows // steps
        assert rb * steps == rows and rb % 16 == 0
        in_specs.append(pl.BlockSpec((rb, width), lambda g: (g, 0)))
        args.append(arr)
        for cs in col_slices or (slice(0, width),):
            cols = cs.stop - cs.start
            out_specs.append(pl.BlockSpec((rb, cols), lambda g: (g, 0)))
            out_shape.append(jax.ShapeDtypeStruct((rows, cols), BF16))
    return in_specs, out_specs, out_shape, args


def _n_cast_outputs(col_slices_per_src):
    return sum(1 if s is None else len(s) for s in col_slices_per_src)


def _do_casts(col_slices_per_src, srcs, dsts):
    dsts = iter(dsts)
    for col_slices, src in zip(col_slices_per_src, srcs):
        if col_slices is None:
            next(dsts)[...] = src[...].astype(BF16)
        else:
            for cs in col_slices:
                next(dsts)[...] = src[:, cs].astype(BF16)


def _ada_kernel(c_ref, w_ref, b_ref, *rest, casts):
    nc = len(casts)
    o_ref = rest[nc]
    _do_casts(casts, rest[0:nc], rest[nc + 1:])
    c = c_ref[...]
    s = (c * jax.nn.sigmoid(c)).astype(BF16)
    o_ref[...] = _dot(s, w_ref[...].astype(BF16)) + b_ref[...]


def _ada(cond8, w, b, casts=()):
    n = w.shape[1]
    steps = n // ADA_COLS
    c_in, c_out, c_shape, c_args = _cast_specs(casts, steps)
    return pl.pallas_call(
        functools.partial(_ada_kernel, casts=tuple(cs for _, cs in casts)),
        grid=(steps,),
        in_specs=[pl.BlockSpec((8, D_MODEL), lambda j: (0, 0)),
                  pl.BlockSpec((D_MODEL, ADA_COLS), lambda j: (0, j)),
                  pl.BlockSpec((1, ADA_COLS), lambda j: (0, j))] + c_in,
        out_specs=[pl.BlockSpec((8, ADA_COLS), lambda j: (0, j))] + c_out,
        out_shape=[jax.ShapeDtypeStruct((8, n), F32)] + c_shape,
        compiler_params=_params(1),
        name="ada",
    )(cond8, w, b, *c_args)


@functools.lru_cache(maxsize=None)
def _dft_mats(L):
    n = 2 * L
    t = np.arange(L, dtype=np.int64)
    f = np.arange(L, dtype=np.int64)
    ang = 2.0 * np.pi * ((f[:, None] * t[None, :]) % n).astype(np.float64) / n
    cos = np.cos(ang)
    sin = np.sin(ang)
    nyq = np.where(t % 2 == 0, 1.0, -1.0)
    fwd = np.concatenate([cos, -sin], axis=0)
    fwd[L] = nyq
    wre = np.full((L,), 2.0 / n)
    wre[0] = 1.0 / n
    inv = np.concatenate([cos.T * wre[None, :], -sin.T * (2.0 / n)], axis=1)
    inv[:, L] = nyq / n
    sgn = np.broadcast_to(nyq[:, None], (L, HY_W))
    return (jnp.asarray(fwd, dtype=BF16), jnp.asarray(inv, dtype=BF16),
            np.asarray(sgn, dtype=np.float32))


@functools.lru_cache(maxsize=None)
def _filter_consts(L):
    t = np.linspace(0.0, 1.0, L)[:, None]
    ang = 2.0 * np.pi * np.arange(L, dtype=np.float64)[:, None] / L
    bands = np.linspace(1e-4, HY_BANDS - 1, HY_BANDS)[None]
    z = np.concatenate([t, np.cos(bands * ang), -np.sin(bands * ang)], axis=-1)
    z = np.pad(z, ((0, 0), (0, HY_EMB_PAD - HY_EMB)))
    max_decay = math.log(HY_TARGET) / HY_FAST_DECAY
    min_decay = math.log(HY_TARGET) / HY_SLOW_DECAY
    deltas = np.linspace(min_decay, max_decay, HY_W)
    tdel = t * np.abs(deltas)[None, :]
    return np.asarray(z, np.float32), np.asarray(tdel, np.float32)


def _filter_kernel(zt_ref, tdel_ref, sgn_ref, w1_ref, b1_ref, w2_ref, b2_ref, w3_ref, fr_ref,
                   fw_ref, oa_ref, ob_ref, od_ref, *, L, b):
    m = L // b
    hi = lax.Precision.HIGHEST
    tdims = (((0,), (0,)), ((), ()))
    fr = fr_ref[...]
    a1 = lax.dot_general(w1_ref[...], zt_ref[...], tdims, precision=hi,
                         preferred_element_type=F32)
    h1 = jnp.sin(fr * (a1 + b1_ref[...]))
    a2 = lax.dot_general(w2_ref[...], h1, tdims, precision=hi, preferred_element_type=F32)
    h2 = jnp.sin(fr * (a2 + b2_ref[...]))
    h2_hi = h2.astype(BF16)
    h2_lo = (h2 - h2_hi.astype(F32)).astype(BF16)
    w3 = w3_ref[...]
    w3_hi = w3.astype(BF16)
    w3_lo = (w3 - w3_hi.astype(F32)).astype(BF16)
    lhs = jnp.concatenate([h2_hi, h2_lo, h2_hi, jnp.zeros_like(h2_hi)], axis=0)
    rhs = jnp.concatenate([w3_hi, w3_hi, w3_lo, jnp.zeros_like(w3_hi)], axis=0)
    h = lax.dot_general(lhs, rhs, tdims, preferred_element_type=F32)
    win = jnp.exp(-tdel_ref[...])
    sg = sgn_ref[...]
    row0_l = lax.broadcasted_iota(jnp.int32, (L, HY_W), 0) == 0
    row0_b = lax.broadcasted_iota(jnp.int32, (b, HY_W), 0) == 0
    row0_8 = lax.broadcasted_iota(jnp.int32, (8, HY_W), 0) == 0
    for o in range(HY_ORDER):
        base = o * 2 * HY_W
        fwd = h[:, base:base + HY_W] * win
        bwd = jnp.where(row0_l, 0.0, h[:, base + HY_W:base + 2 * HY_W] * win)
        nrm = (jnp.sum(jnp.abs(fwd), axis=0, keepdims=True)
               + jnp.sum(jnp.abs(bwd), axis=0, keepdims=True))
        inv = 1.0 / nrm
        fn = fwd * inv
        bn = bwd * inv
        xr, xn, xi, wr, wn, wi = [], [], [], [], [], []
        for r in range(m):
            p = _dot(fw_ref[...], fn[r * b:(r + 1) * b].astype(BF16))
            q = _dot(fw_ref[...], bn[r * b:(r + 1) * b].astype(BF16))
            xr.append(p[0:b])
            xn.append(p[b:b + 1])
            xi.append(jnp.where(row0_b, 0.0, p[b:2 * b]))
            wr.append(q[0:b])
            wn.append(q[b:b + 1])
            wi.append(jnp.where(row0_b, 0.0, -q[b:2 * b]))

        def emit(d, ka, kn, kb):
            oa_ref[o, d + m - 1] = ka
            ob_ref[o, d + m - 1] = kb
            od_ref[o, d + m - 1] = jnp.where(row0_8, kn, ka[0:8])

        emit(0, xr[0] + wr[0], xn[0] + wn[0], xi[0] + wi[0])
        for d in range(1, m):
            f0 = fn[(d - 1) * b:(d - 1) * b + 1]
            b0 = bn[(d - 1) * b:(d - 1) * b + 1]
            emit(d, xr[d] + sg * (xr[d - 1] - f0), xn[d] + (xn[d - 1] - f0),
                 xi[d] + sg * xi[d - 1])
            emit(-d, wr[d] + sg * (wr[d - 1] - b0), wn[d] + (wn[d - 1] - b0),
                 wi[d] + sg * wi[d - 1])


def _filters(L, b, fw, sgn, w1, b1, w2, b2, w3, freq):
    z, tdel = _filter_consts(L)
    nd = 2 * (L // b) - 1
    args = (jnp.asarray(z.T), jnp.asarray(tdel), jnp.asarray(sgn), w1, b1, w2, b2, w3, freq, fw)
    return pl.pallas_call(
        functools.partial(_filter_kernel, L=L, b=b),
        out_shape=[jax.ShapeDtypeStruct((HY_ORDER, nd, b, HY_W), F32),
                   jax.ShapeDtypeStruct((HY_ORDER, nd, b, HY_W), F32),
                   jax.ShapeDtypeStruct((HY_ORDER, nd, 8, HY_W), F32)],
        compiler_params=pltpu.CompilerParams(vmem_limit_bytes=VMEM_LIMIT),
        name=f"filters{L}",
    )(*args)


def _ret_kernel(*refs, L, C, nb, has_init, emit_state, casts):
    it = iter(refs)
    x_ref, mod_ref, n1_ref, w_ref, dec_ref = (next(it) for _ in range(5))
    s0f_ref = s0b_ref = sf_ref = sb_ref = None
    if has_init:
        s0f_ref, s0b_ref = next(it), next(it)
    wo_ref = next(it)
    cast_srcs = [next(it) for _ in casts]
    y_ref = next(it)
    if emit_state:
        sf_ref, sb_ref = next(it), next(it)
    cast_dsts = [next(it) for _ in range(_n_cast_outputs(casts))]
    mask_scr, vec_scr, cd_scr, g_scr = (next(it) for _ in range(4))
    _do_casts(casts, cast_srcs, cast_dsts)
    n = L // C
    H, E = RET_HEADS, HEAD_DIM
    scale = float(E) ** -0.5

    @pl.when(pl.program_id(0) == 0)
    def _():
        lg = jnp.log(jax.nn.sigmoid(dec_ref[...]))
        cd_scr[...] = jnp.exp(float(C) * lg[:, 0:E])
        ii = lax.broadcasted_iota(jnp.int32, (C, C), 0)
        jj = lax.broadcasted_iota(jnp.int32, (C, C), 1)
        rel = (ii - jj).astype(F32)
        ri = lax.broadcasted_iota(jnp.int32, (C, E), 0).astype(F32)
        for h in range(H):
            lf = lg[h:h + 1, :]
            lb = lg[H + h:H + h + 1, :]
            mf = jnp.where(rel >= 0, jnp.exp(jnp.maximum(rel, 0.0) * lf), 0.0)
            mb = jnp.where(rel <= 0, jnp.exp(jnp.maximum(-rel, 0.0) * lb), 0.0)
            mask_scr[h] = scale * (mf + mb)
            lfe, lbe = lf[:, 0:E], lb[:, 0:E]
            vec_scr[h, 0] = jnp.exp((ri + 1.0) * lfe)
            vec_scr[h, 1] = jnp.exp((float(C) - ri) * lbe)
            vec_scr[h, 2] = scale * jnp.exp((float(C) - 1.0 - ri) * lfe)
            vec_scr[h, 3] = scale * jnp.exp(ri * lbe)

    mod = mod_ref[0]
    tdims = (((0,), (0,)), ((), ()))
    ndims = (((1,), (1,)), ((), ()))
    chains = [(s, h) for s in range(nb) for h in range(H)]
    rows = [slice(c * C, (c + 1) * C) for c in range(n)]
    qkvg = [_dot(_modnorm(x_ref[s], n1_ref[...], mod[1:2], mod[0:1]).astype(BF16), w_ref[...])
            for s in range(nb)]

    def cols(s, part, h):
        return qkvg[s][:, part * RET_W + h * E:part * RET_W + (h + 1) * E]

    qb = [cols(s, 0, h).astype(BF16) for s, h in chains]
    kf = [cols(s, 1, h) for s, h in chains]
    kb = [k.astype(BF16) for k in kf]
    vb = [cols(s, 2, h).astype(BF16) for s, h in chains]
    att = [[lax.dot_general(qb[i][r], kb[i][r], ndims, preferred_element_type=F32) for r in rows]
           for i in range(len(chains))]
    prob = [[(att[i][c] * mask_scr[h]).astype(BF16) for c in range(n)]
            for i, (s, h) in enumerate(chains)]
    out = [[_dot(prob[i][c], vb[i][rows[c]]) for c in range(n)] for i in range(len(chains))]
    kv = []
    for i, (s, h) in enumerate(chains):
        dk2 = jnp.concatenate([vec_scr[h, 2], vec_scr[h, 3]], axis=1)
        per_c = []
        for r in rows:
            k2 = (jnp.concatenate([kf[i][r], kf[i][r]], axis=1) * dk2).astype(BF16)
            per_c.append(lax.dot_general(k2, vb[i][r], tdims, preferred_element_type=F32))
        kv.append(per_c)
    for i, (s, h) in enumerate(chains):
        cdf = cd_scr[h:h + 1, :]
        cdb = cd_scr[H + h:H + h + 1, :]
        sf_in, sb_in = [None] * n, [None] * n
        st = s0f_ref[s, h] if has_init else None
        for c in range(n):
            sf_in[c] = st
            kvc = kv[i][c][0:E]
            st = kvc if st is None else st * cdf + kvc
        if emit_state:
            sf_ref[s, h] = st
        st = s0b_ref[s, h] if has_init else None
        for c in range(n - 1, -1, -1):
            sb_in[c] = st
            kvc = kv[i][c][E:2 * E]
            st = kvc if st is None else st * cdb + kvc
        if emit_state:
            sb_ref[s, h] = st
        for c in range(n):
            if sf_in[c] is not None and sb_in[c] is not None:
                s2 = jnp.concatenate([sf_in[c], sb_in[c]], axis=1).astype(BF16)
                inter = _dot(qb[i][rows[c]], s2)
                out[i][c] = (out[i][c] + inter[:, 0:E] * vec_scr[h, 0]
                             + inter[:, E:2 * E] * vec_scr[h, 1])
            elif sf_in[c] is not None:
                out[i][c] = (out[i][c]
                             + _dot(qb[i][rows[c]], sf_in[c].astype(BF16)) * vec_scr[h, 0])
            elif sb_in[c] is not None:
                out[i][c] = (out[i][c]
                             + _dot(qb[i][rows[c]], sb_in[c].astype(BF16)) * vec_scr[h, 1])
    for i, (s, h) in enumerate(chains):
        for c in range(n):
            o = out[i][c]
            mu = jnp.mean(o, axis=-1, keepdims=True)
            d = o - mu
            var = jnp.mean(d * d, axis=-1, keepdims=True)
            on = d * lax.rsqrt(var + EPS)
            gg = cols(s, 3, h)[rows[c]]
            g_scr[s, rows[c], h * E:(h + 1) * E] = (gg * jax.nn.sigmoid(gg) * on).astype(BF16)
    for s in range(nb):
        y_ref[s] = _dot(g_scr[s], wo_ref[...])


def _retention(x, mods3, mod_row, norm1, w_qkvg, dec8, s0f, s0b, w_o, *, emit_state, nb,
               casts=()):
    B, L, D = x.shape
    C = min(RET_CHUNK, L)
    has_init = s0f is not None
    H, E = RET_HEADS, HEAD_DIM
    in_specs = [pl.BlockSpec((nb, L, D), lambda g: (g, 0, 0)),
                pl.BlockSpec((1, N_MOD, D), lambda g: (mod_row(g * nb), 0, 0)),
                _const_spec((1, D)),
                _const_spec((D, N_QKVG)),
                _const_spec((8, C))]
    args = [x, mods3, norm1, w_qkvg, dec8[:, :C]]
    st_spec = pl.BlockSpec((nb, H, E, E), lambda g: (g, 0, 0, 0))
    if has_init:
        in_specs += [st_spec, st_spec]
        args += [s0f, s0b]
    in_specs.append(_const_spec((RET_W, D)))
    args.append(w_o)
    c_in, c_out, c_shape, c_args = _cast_specs(casts, B // nb)
    in_specs += c_in
    args += c_args
    out_specs = [pl.BlockSpec((nb, L, D), lambda g: (g, 0, 0))]
    out_shape = [jax.ShapeDtypeStruct((B, L, D), F32)]
    if emit_state:
        out_specs += [st_spec, st_spec]
        out_shape += [jax.ShapeDtypeStruct((B, H, E, E), F32)] * 2
    out_specs += c_out
    out_shape += c_shape
    return pl.pallas_call(
        functools.partial(_ret_kernel, L=L, C=C, nb=nb, has_init=has_init,
                          emit_state=emit_state, casts=tuple(cs for _, cs in casts)),
        grid=(B // nb,),
        in_specs=in_specs,
        out_specs=out_specs,
        out_shape=out_shape,
        scratch_shapes=[pltpu.VMEM((H, C, C), F32),
                        pltpu.VMEM((H, 4, C, E), F32),
                        pltpu.VMEM((8, E), F32),
                        pltpu.VMEM((nb, L, RET_W), BF16)],
        compiler_params=_params(1),
        name=f"retention{L}",
    )(*args)


def _hy_kernel(x_ref, mod_ref, n1_ref, w_ref, cw_ref, cb_ref, fw_ref, bw_ref, fa_ref, fb_ref,
               fd_ref, hb_ref, wo_ref, *rest, L, W, b, nb, casts):
    nc = len(casts)
    y_ref = rest[nc]
    _do_casts(casts, rest[0:nc], rest[nc + 1:])
    m = L // b
    CB = HY_CBLK
    nblk = HY_W // CB
    mod = mod_ref[0]
    pos = lax.broadcasted_iota(jnp.int32, (L, CB), 0) % W
    first = pos == 0
    last = pos == W - 1
    hn = [_modnorm(x_ref[s], n1_ref[...], mod[1:2], mod[0:1]).astype(BF16) for s in range(nb)]
    chains = [(s, blk) for s in range(nb) for blk in range(nblk)]

    def short_conv(s, base, blk):
        cs = slice(base + blk * CB, base + (blk + 1) * CB)
        ug = _dot(hn[s], w_ref[:, cs])
        prev = jnp.where(first, 0.0, pltpu.roll(ug, 1, axis=0))
        nxt = jnp.where(last, 0.0, pltpu.roll(ug, L - 1, axis=0))
        u = (prev * cw_ref[0:1, cs] + ug * cw_ref[1:2, cs] + nxt * cw_ref[2:3, cs]
             + cb_ref[:, cs])
        return [u[j * b:(j + 1) * b] for j in range(m)]

    def long_conv(sigs, o):
        spec = [[_dot(fw_ref[...], sj.astype(BF16)) for sj in sig] for sig in sigs]
        prods = []
        for (s, blk), sp in zip(chains, spec):
            cs = slice(blk * CB, (blk + 1) * CB)
            per_i = []
            for i in range(m):
                yre = yim = yim8 = None
                for j in range(m):
                    d = i - j + m - 1
                    sre, sim = sp[j][0:b], sp[j][b:2 * b]
                    ka, kb = fa_ref[o, d, :, cs], fb_ref[o, d, :, cs]
                    tre = sre * ka - sim * kb
                    tim = sre * kb + sim * ka
                    t8 = sre[0:8] * kb[0:8] + sim[0:8] * fd_ref[o, d, :, cs]
                    yre = tre if yre is None else yre + tre
                    yim = tim if yim is None else yim + tim
                    yim8 = t8 if yim8 is None else yim8 + t8
                yim = jnp.concatenate([yim8, yim[8:]], axis=0)
                per_i.append((yre.astype(BF16), yim.astype(BF16)))
            prods.append(per_i)
        return [[_dot(bw_ref[:, 0:b], yre) + _dot(bw_ref[:, b:2 * b], yim) for yre, yim in per_i]
                for per_i in prods]

    hv = [short_conv(s, 0, blk) for s, blk in chains]
    hx1 = [short_conv(s, HY_W, blk) for s, blk in chains]
    hx2 = [short_conv(s, 2 * HY_W, blk) for s, blk in chains]

    def gate(hx, conv, sig, o):
        out = []
        for (s, blk), hxc, cc, sc in zip(chains, hx, conv, sig):
            bias = hb_ref[o:o + 1, blk * CB:(blk + 1) * CB]
            out.append([hxc[i] * (cc[i] + sc[i] * bias) for i in range(m)])
        return out

    z = gate(hx1, long_conv(hv, 0), hv, 0)
    z = gate(hx2, long_conv(z, 1), z, 1)
    for s in range(nb):
        for i in range(m):
            acc = None
            for blk in range(nblk):
                zc = z[chains.index((s, blk))][i].astype(BF16)
                part = _dot(zc, wo_ref[blk * CB:(blk + 1) * CB, :])
                acc = part if acc is None else acc + part
            y_ref[s, i * b:(i + 1) * b, :] = acc


def _hyena(x, mods3, mod_row, norm1, w_hy, conv_w, conv_b, fw, bw, filt, hy_bias, w_o, *, W, b,
           nb, casts=()):
    B, L, D = x.shape
    fa, fb, fd = filt
    nd = fa.shape[1]
    c_in, c_out, c_shape, c_args = _cast_specs(casts, B // nb)
    return pl.pallas_call(
        functools.partial(_hy_kernel, L=L, W=W, b=b, nb=nb,
                          casts=tuple(cs for _, cs in casts)),
        grid=(B // nb,),
        in_specs=[pl.BlockSpec((nb, L, D), lambda g: (g, 0, 0)),
                  pl.BlockSpec((1, N_MOD, D), lambda g: (mod_row(g * nb), 0, 0)),
                  _const_spec((1, D)),
                  _const_spec((D, N_HY)),
                  _const_spec((3, N_HY)),
                  _const_spec((1, N_HY)),
                  _const_spec((2 * b, b)),
                  _const_spec((b, 2 * b)),
                  _const_spec((HY_ORDER, nd, b, HY_W)),
                  _const_spec((HY_ORDER, nd, b, HY_W)),
                  _const_spec((HY_ORDER, nd, 8, HY_W)),
                  _const_spec((HY_ORDER, HY_W)),
                  _const_spec((HY_W, D))] + c_in,
        out_specs=[pl.BlockSpec((nb, L, D), lambda g: (g, 0, 0))] + c_out,
        out_shape=[jax.ShapeDtypeStruct((B, L, D), F32)] + c_shape,
        compiler_params=_params(1),
        name=f"hyena{L}",
    )(x, mods3, norm1, w_hy, conv_w, conv_b, fw, bw, fa, fb, fd, hy_bias, w_o, *c_args)


def _mlp_kernel(x_ref, yr_ref, yh_ref, mod_ref, n1_ref, n2_ref, fg_ref, wg_ref, wout_ref,
                wfi_ref, wfo_ref, y_ref):
    m = mod_ref[0]
    x = x_ref[...]
    hn = _modnorm(x, n1_ref[...], m[1:2], m[0:1]).astype(BF16)
    gates = _dot(hn, wg_ref[...])
    mix = (jax.nn.sigmoid(gates[:, 0:D_MODEL]) * yr_ref[...]
           + jax.nn.sigmoid(gates[:, D_MODEL:2 * D_MODEL]) * yh_ref[...])
    x1 = x + m[2:3] * _dot(mix.astype(BF16), wout_ref[...])
    h2 = _modnorm(x1, n2_ref[...], m[4:5], m[3:4]).astype(BF16)
    acc = None
    for j in range(D_FF // FF_CHUNK):
        cs = slice(j * FF_CHUNK, (j + 1) * FF_CHUNK)
        a = _dot(h2, wfi_ref[:, cs])
        b = _dot(h2, wfi_ref[:, D_FF + j * FF_CHUNK:D_FF + (j + 1) * FF_CHUNK])
        ff = (a * jax.nn.sigmoid(a) * b).astype(BF16)
        part = _dot(ff, wfo_ref[cs, :])
        acc = part if acc is None else acc + part
    x2 = x1 + m[5:6] * acc
    ms = jnp.mean(x2 * x2, axis=-1, keepdims=True)
    y_ref[...] = x2 * lax.rsqrt(ms + EPS) * fg_ref[...]


def _mlp(x, y_ret, y_hy, mods3, mod_row, norm1, norm2, final_g, w_gate, w_out, w_fi, w_fo):
    B, L, D = x.shape
    T = MLP_ROWS
    flat = lambda a: a.reshape(B * L, D)
    act = pl.BlockSpec((T, D), lambda i: (i, 0))
    y = pl.pallas_call(
        _mlp_kernel,
        grid=(B * L // T,),
        in_specs=[act, act, act,
                  pl.BlockSpec((1, N_MOD, D), lambda i: (mod_row((i * T) // L), 0, 0)),
                  _const_spec((1, D)), _const_spec((1, D)), _const_spec((1, D)),
                  _const_spec((D, N_GATE)),
                  _const_spec((D, D)),
                  _const_spec((D, 2 * D_FF)),
                  _const_spec((D_FF, D))],
        out_specs=act,
        out_shape=jax.ShapeDtypeStruct((B * L, D), F32),
        compiler_params=_params(1),
        name=f"mlp{L}",
    )(flat(x), flat(y_ret), flat(y_hy), mods3, norm1, norm2, final_g, w_gate, w_out, w_fi, w_fo)
    return y.reshape(B, L, D)


def kernel(x_prompt, x_sample, state_ret_fwd, state_ret_bwd, c, c_ctx, norm1_g, norm2_g, w_ada,
           b_ada, w_in, ret_decay_fwd, ret_decay_bwd, hy_conv_w, hy_conv_b, hy_pos_w1, hy_pos_b1,
           hy_pos_w2, hy_pos_b2, hy_pos_w3, hy_sin_freq, hy_bias, w_ret_o, w_hy_o, w_out,
           w_ffn_in, w_ffn_out, final_g):
    assert w_in.shape[0] == 1, "single-layer configuration"
    nb_lat = x_sample.shape[0]
    l_ctx = x_prompt.shape[1]

    cond8 = jnp.concatenate([c_ctx[None, :], c, jnp.zeros((8 - 1 - nb_lat, D_MODEL), F32)])
    n_hy_end = N_QKVG + N_HY
    w_in_parts = (slice(0, N_QKVG), slice(N_QKVG, n_hy_end), slice(n_hy_end, N_IN))
    mods, w_qkvg, w_hy, w_gate, w_ret_o_b, w_hy_o_b = _ada(
        cond8, w_ada[0], b_ada,
        casts=[(w_in[0], w_in_parts), (w_ret_o[0], None), (w_hy_o[0], None)])
    mods3 = mods.reshape(8, N_MOD, D_MODEL)
    norm1 = norm1_g[0][None, :]
    norm2 = norm2_g[0][None, :]
    fg = final_g[None, :]
    dec8 = jnp.broadcast_to(jnp.concatenate([ret_decay_fwd[0], ret_decay_bwd[0]])[:, None],
                            (8, RET_CHUNK))
    w1 = jnp.pad(hy_pos_w1[0], ((0, HY_EMB_PAD - HY_EMB), (0, 0)))
    b1, b2 = hy_pos_b1[0][:, None], hy_pos_b2[0][:, None]
    freq = hy_sin_freq[0][:, None]
    conv_b = hy_conv_b[0][None, :]

    def mixers(x, mod_row, s0f, s0b, grid_w, emit_state, nb, ret_casts=(), hy_casts=()):
        L = x.shape[1]
        blk = min(HY_TBLK, L)
        fw, bw, sgn = _dft_mats(blk)
        filt = _filters(L, blk, fw, sgn, w1, b1, hy_pos_w2[0], b2, hy_pos_w3[0], freq)
        ret = _retention(x, mods3, mod_row, norm1, w_qkvg, dec8, s0f, s0b, w_ret_o_b,
                         emit_state=emit_state, nb=nb, casts=ret_casts)
        hy = _hyena(x, mods3, mod_row, norm1, w_hy, hy_conv_w[0], conv_b, fw, bw, filt,
                    hy_bias[0], w_hy_o_b, W=grid_w, b=blk, nb=nb, casts=hy_casts)
        return ret, hy

    ctx_row = lambda b: 0
    lat_row = lambda b: b + 1
    ret_c, hy_c = mixers(x_prompt, ctx_row, None, None, l_ctx, True, CTX_SEQS,
                         ret_casts=[(w_ffn_in[0], None)],
                         hy_casts=[(w_ffn_out[0], None), (w_out[0], None)])
    y_ret_c, s_f, s_b, w_fi_b = ret_c
    y_hy_c, w_fo_b, w_out_b = hy_c
    y_prompt = _mlp(x_prompt, y_ret_c, y_hy_c, mods3, ctx_row, norm1, norm2, fg, w_gate, w_out_b,
                    w_fi_b, w_fo_b)
    ret_l, hy_l = mixers(x_sample, lat_row, state_ret_fwd[:, 0], state_ret_bwd[:, 0], GRID_W,
                         False, 1)
    y_sample = _mlp(x_sample, ret_l[0], hy_l[0], mods3, lat_row, norm1, norm2, fg, w_gate,
                    w_out_b, w_fi_b, w_fo_b)
    return (y_prompt, y_sample, s_f[:, None], s_b[:, None])
```

```python
import functools
import math

import numpy as np
import jax
import jax.numpy as jnp
from jax import lax
from jax.experimental import pallas as pl
from jax.experimental.pallas import tpu as pltpu

F32 = jnp.float32
BF16 = jnp.bfloat16

D_MODEL = 1024
RET_HEADS = 4
HEAD_DIM = 128
RET_W = RET_HEADS * HEAD_DIM
HY_W = 512
HY_ORDER = 2
HY_BANDS = 16
HY_EMB = 1 + 2 * HY_BANDS
HY_EMB_PAD = 40
HY_HIDDEN = 64
HY_FAST_DECAY = 0.3
HY_SLOW_DECAY = 1.5
HY_TARGET = 1e-2
D_FF = 2816
N_QKVG = 4 * RET_W
N_HY = 3 * HY_W
N_GATE = 2 * D_MODEL
N_IN = N_QKVG + N_HY + N_GATE
N_MOD = 6
EPS = 1e-6
GRID_W = 64
RET_CHUNK = 256
HY_CBLK = 256
HY_TBLK = 512
CTX_SEQS = 2
MLP_ROWS = 512
FF_CHUNK = 256
ADA_COLS = 768
VMEM_LIMIT = 56 * 1024 * 1024


def _const_spec(shape):
    nd = len(shape)
    return pl.BlockSpec(shape, lambda *_: (0,) * nd, pipeline_mode=pl.Buffered(1))


def _params(n_axes):
    return pltpu.CompilerParams(dimension_semantics=("arbitrary",) * n_axes,
                                vmem_limit_bytes=VMEM_LIMIT)


def _modnorm(x, g, scale, shift):
    ms = jnp.mean(x * x, axis=-1, keepdims=True)
    return (x * lax.rsqrt(ms + EPS) * g) * (1.0 + scale) + shift


def _dot(a, b):
    return jnp.dot(a, b, preferred_element_type=F32)


def _cast_specs(casts, steps):
    in_specs, out_specs, out_shape, args = [], [], [], []
    for arr, col_slices in casts:
        rows, width = arr.shape
        rb = rows // steps
        assert rb * steps == rows and rb % 16 == 0
        in_specs.append(pl.BlockSpec((rb, width), lambda g: (g, 0)))
        args.append(arr)
        for cs in col_slices or (slice(0, width),):
            cols = cs.stop - cs.start
            out_specs.append(pl.BlockSpec((rb, cols), lambda g: (g, 0)))
            out_shape.append(jax.ShapeDtypeStruct((rows, cols), BF16))
    return in_specs, out_specs, out_shape, args


def _n_cast_outputs(col_slices_per_src):
    return sum(1 if s is None else len(s) for s in col_slices_per_src)


def _do_casts(col_slices_per_src, srcs, dsts):
    dsts = iter(dsts)
    for col_slices, src in zip(col_slices_per_src, srcs):
        if col_slices is None:
            next(dsts)[...] = src[...].astype(BF16)
        else:
            for cs in col_slices:
                next(dsts)[...] = src[:, cs].astype(BF16)


def _ada_kernel(c_ref, w_ref, b_ref, *rest, casts):
    nc = len(casts)
    o_ref = rest[nc]
    _do_casts(casts, rest[0:nc], rest[nc + 1:])
    c = c_ref[...]
    s = (c * jax.nn.sigmoid(c)).astype(BF16)
    o_ref[...] = _dot(s, w_ref[...].astype(BF16)) + b_ref[...]


def _ada(cond8, w, b, casts=()):
    n = w.shape[1]
    steps = n // ADA_COLS
    c_in, c_out, c_shape, c_args = _cast_specs(casts, steps)
    return pl.pallas_call(
        functools.partial(_ada_kernel, casts=tuple(cs for _, cs in casts)),
        grid=(steps,),
        in_specs=[pl.BlockSpec((8, D_MODEL), lambda j: (0, 0)),
                  pl.BlockSpec((D_MODEL, ADA_COLS), lambda j: (0, j)),
                  pl.BlockSpec((1, ADA_COLS), lambda j: (0, j))] + c_in,
        out_specs=[pl.BlockSpec((8, ADA_COLS), lambda j: (0, j))] + c_out,
        out_shape=[jax.ShapeDtypeStruct((8, n), F32)] + c_shape,
        compiler_params=_params(1),
        name="ada",
    )(cond8, w, b, *c_args)


@functools.lru_cache(maxsize=None)
def _dft_mats(L):
    n = 2 * L
    t = np.arange(L, dtype=np.int64)
    f = np.arange(L, dtype=np.int64)
    ang = 2.0 * np.pi * ((f[:, None] * t[None, :]) % n).astype(np.float64) / n
    cos = np.cos(ang)
    sin = np.sin(ang)
    nyq = np.where(t % 2 == 0, 1.0, -1.0)
    fwd = np.concatenate([cos, -sin], axis=0)
    fwd[L] = nyq
    wre = np.full((L,), 2.0 / n)
    wre[0] = 1.0 / n
    inv = np.concatenate([cos.T * wre[None, :], -sin.T * (2.0 / n)], axis=1)
    inv[:, L] = nyq / n
    sgn = np.broadcast_to(nyq[:, None], (L, HY_W))
    return (jnp.asarray(fwd, dtype=BF16), jnp.asarray(inv, dtype=BF16),
            np.asarray(sgn, dtype=np.float32))


@functools.lru_cache(maxsize=None)
def _filter_consts(L):
    t = np.linspace(0.0, 1.0, L)[:, None]
    ang = 2.0 * np.pi * np.arange(L, dtype=np.float64)[:, None] / L
    bands = np.linspace(1e-4, HY_BANDS - 1, HY_BANDS)[None]
    z = np.concatenate([t, np.cos(bands * ang), -np.sin(bands * ang)], axis=-1)
    z = np.pad(z, ((0, 0), (0, HY_EMB_PAD - HY_EMB)))
    max_decay = math.log(HY_TARGET) / HY_FAST_DECAY
    min_decay = math.log(HY_TARGET) / HY_SLOW_DECAY
    deltas = np.linspace(min_decay, max_decay, HY_W)
    tdel = t * np.abs(deltas)[None, :]
    return np.asarray(z, np.float32), np.asarray(tdel, np.float32)


def _filter_kernel(zt_ref, tdel_ref, sgn_ref, w1_ref, b1_ref, w2_ref, b2_ref, w3_ref, fr_ref,
                   fw_ref, oa_ref, ob_ref, od_ref, *, L, b):
    m = L // b
    hi = lax.Precision.HIGHEST
    tdims = (((0,), (0,)), ((), ()))
    fr = fr_ref[...]
    a1 = lax.dot_general(w1_ref[...], zt_ref[...], tdims, precision=hi,
                         preferred_element_type=F32)
    h1 = jnp.sin(fr * (a1 + b1_ref[...]))
    a2 = lax.dot_general(w2_ref[...], h1, tdims, precision=hi, preferred_element_type=F32)
    h2 = jnp.sin(fr * (a2 + b2_ref[...]))
    h2_hi = h2.astype(BF16)
    h2_lo = (h2 - h2_hi.astype(F32)).astype(BF16)
    w3 = w3_ref[...]
    w3_hi = w3.astype(BF16)
    w3_lo = (w3 - w3_hi.astype(F32)).astype(BF16)
    lhs = jnp.concatenate([h2_hi, h2_lo, h2_hi, jnp.zeros_like(h2_hi)], axis=0)
    rhs = jnp.concatenate([w3_hi, w3_hi, w3_lo, jnp.zeros_like(w3_hi)], axis=0)
    h = lax.dot_general(lhs, rhs, tdims, preferred_element_type=F32)
    win = jnp.exp(-tdel_ref[...])
    sg = sgn_ref[...]
    row0_l = lax.broadcasted_iota(jnp.int32, (L, HY_W), 0) == 0
    row0_b = lax.broadcasted_iota(jnp.int32, (b, HY_W), 0) == 0
    row0_8 = lax.broadcasted_iota(jnp.int32, (8, HY_W), 0) == 0
    for o in range(HY_ORDER):
        base = o * 2 * HY_W
        fwd = h[:, base:base + HY_W] * win
        bwd = jnp.where(row0_l, 0.0, h[:, base + HY_W:base + 2 * HY_W] * win)
        nrm = (jnp.sum(jnp.abs(fwd), axis=0, keepdims=True)
               + jnp.sum(jnp.abs(bwd), axis=0, keepdims=True))
        inv = 1.0 / nrm
        fn = fwd * inv
        bn = bwd * inv
        xr, xn, xi, wr, wn, wi = [], [], [], [], [], []
        for r in range(m):
            p = _dot(fw_ref[...], fn[r * b:(r + 1) * b].astype(BF16))
            q = _dot(fw_ref[...], bn[r * b:(r + 1) * b].astype(BF16))
            xr.append(p[0:b])
            xn.append(p[b:b + 1])
            xi.append(jnp.where(row0_b, 0.0, p[b:2 * b]))
            wr.append(q[0:b])
            wn.append(q[b:b + 1])
            wi.append(jnp.where(row0_b, 0.0, -q[b:2 * b]))

        def emit(d, ka, kn, kb):
            oa_ref[o, d + m - 1] = ka
            ob_ref[o, d + m - 1] = kb
            od_ref[o, d + m - 1] = jnp.where(row0_8, kn, ka[0:8])

        emit(0, xr[0] + wr[0], xn[0] + wn[0], xi[0] + wi[0])
        for d in range(1, m):
            f0 = fn[(d - 1) * b:(d - 1) * b + 1]
            b0 = bn[(d - 1) * b:(d - 1) * b + 1]
            emit(d, xr[d] + sg * (xr[d - 1] - f0), xn[d] + (xn[d - 1] - f0),
                 xi[d] + sg * xi[d - 1])
            emit(-d, wr[d] + sg * (wr[d - 1] - b0), wn[d] + (wn[d - 1] - b0),
                 wi[d] + sg * wi[d - 1])


def _filters(L, b, fw, sgn, w1, b1, w2, b2, w3, freq):
    z, tdel = _filter_consts(L)
    nd = 2 * (L // b) - 1
    args = (jnp.asarray(z.T), jnp.asarray(tdel), jnp.asarray(sgn), w1, b1, w2, b2, w3, freq, fw)
    return pl.pallas_call(
        functools.partial(_filter_kernel, L=L, b=b),
        out_shape=[jax.ShapeDtypeStruct((HY_ORDER, nd, b, HY_W), F32),
                   jax.ShapeDtypeStruct((HY_ORDER, nd, b, HY_W), F32),
                   jax.ShapeDtypeStruct((HY_ORDER, nd, 8, HY_W), F32)],
        compiler_params=pltpu.CompilerParams(vmem_limit_bytes=VMEM_LIMIT),
        name=f"filters{L}",
    )(*args)


def _ret_kernel(*refs, L, C, nb, has_init, emit_state, casts):
    it = iter(refs)
    x_ref, mod_ref, n1_ref, w_ref, dec_ref = (next(it) for _ in range(5))
    s0f_ref = s0b_ref = sf_ref = sb_ref = None
    if has_init:
        s0f_ref, s0b_ref = next(it), next(it)
    wo_ref = next(it)
    cast_srcs = [next(it) for _ in casts]
    y_ref = next(it)
    if emit_state:
        sf_ref, sb_ref = next(it), next(it)
    cast_dsts = [next(it) for _ in range(_n_cast_outputs(casts))]
    mask_scr, vec_scr, cd_scr, g_scr = (next(it) for _ in range(4))
    _do_casts(casts, cast_srcs, cast_dsts)
    n = L // C
    H, E = RET_HEADS, HEAD_DIM
    scale = float(E) ** -0.5

    @pl.when(pl.program_id(0) == 0)
    def _():
        lg = jnp.log(jax.nn.sigmoid(dec_ref[...]))
        cd_scr[...] = jnp.exp(float(C) * lg[:, 0:E])
        ii = lax.broadcasted_iota(jnp.int32, (C, C), 0)
        jj = lax.broadcasted_iota(jnp.int32, (C, C), 1)
        rel = (ii - jj).astype(F32)
        ri = lax.broadcasted_iota(jnp.int32, (C, E), 0).astype(F32)
        for h in range(H):
            lf = lg[h:h + 1, :]
            lb = lg[H + h:H + h + 1, :]
            mf = jnp.where(rel >= 0, jnp.exp(jnp.maximum(rel, 0.0) * lf), 0.0)
            mb = jnp.where(rel <= 0, jnp.exp(jnp.maximum(-rel, 0.0) * lb), 0.0)
            mask_scr[h] = scale * (mf + mb)
            lfe, lbe = lf[:, 0:E], lb[:, 0:E]
            vec_scr[h, 0] = jnp.exp((ri + 1.0) * lfe)
            vec_scr[h, 1] = jnp.exp((float(C) - ri) * lbe)
            vec_scr[h, 2] = scale * jnp.exp((float(C) - 1.0 - ri) * lfe)
            vec_scr[h, 3] = scale * jnp.exp(ri * lbe)

    mod = mod_ref[0]
    tdims = (((0,), (0,)), ((), ()))
    ndims = (((1,), (1,)), ((), ()))
    chains = [(s, h) for s in range(nb) for h in range(H)]
    rows = [slice(c * C, (c + 1) * C) for c in range(n)]
    qkvg = [_dot(_modnorm(x_ref[s], n1_ref[...], mod[1:2], mod[0:1]).astype(BF16), w_ref[...])
            for s in range(nb)]

    def cols(s, part, h):
        return qkvg[s][:, part * RET_W + h * E:part * RET_W + (h + 1) * E]

    qb = [cols(s, 0, h).astype(BF16) for s, h in chains]
    kf = [cols(s, 1, h) for s, h in chains]
    kb = [k.astype(BF16) for k in kf]
    vb = [cols(s, 2, h).astype(BF16) for s, h in chains]
    att = [[lax.dot_general(qb[i][r], kb[i][r], ndims, preferred_element_type=F32) for r in rows]
           for i in range(len(chains))]
    prob = [[(att[i][c] * mask_scr[h]).astype(BF16) for c in range(n)]
            for i, (s, h) in enumerate(chains)]
    out = [[_dot(prob[i][c], vb[i][rows[c]]) for c in range(n)] for i in range(len(chains))]
    kv = []
    for i, (s, h) in enumerate(chains):
        dk2 = jnp.concatenate([vec_scr[h, 2], vec_scr[h, 3]], axis=1)
        per_c = []
        for r in rows:
            k2 = (jnp.concatenate([kf[i][r], kf[i][r]], axis=1) * dk2).astype(BF16)
            per_c.append(lax.dot_general(k2, vb[i][r], tdims, preferred_element_type=F32))
        kv.append(per_c)
    for i, (s, h) in enumerate(chains):
        cdf = cd_scr[h:h + 1, :]
        cdb = cd_scr[H + h:H + h + 1, :]
        sf_in, sb_in = [None] * n, [None] * n
        st = s0f_ref[s, h] if has_init else None
        for c in range(n):
            sf_in[c] = st
            kvc = kv[i][c][0:E]
            st = kvc if st is None else st * cdf + kvc
        if emit_state:
            sf_ref[s, h] = st
        st = s0b_ref[s, h] if has_init else None
        for c in range(n - 1, -1, -1):
            sb_in[c] = st
            kvc = kv[i][c][E:2 * E]
            st = kvc if st is None else st * cdb + kvc
        if emit_state:
            sb_ref[s, h] = st
        for c in range(n):
            if sf_in[c] is not None and sb_in[c] is not None:
                s2 = jnp.concatenate([sf_in[c], sb_in[c]], axis=1).astype(BF16)
                inter = _dot(qb[i][rows[c]], s2)
                out[i][c] = (out[i][c] + inter[:, 0:E] * vec_scr[h, 0]
                             + inter[:, E:2 * E] * vec_scr[h, 1])
            elif sf_in[c] is not None:
                out[i][c] = (out[i][c]
                             + _dot(qb[i][rows[c]], sf_in[c].astype(BF16)) * vec_scr[h, 0])
            elif sb_in[c] is not None:
                out[i][c] = (out[i][c]
                             + _dot(qb[i][rows[c]], sb_in[c].astype(BF16)) * vec_scr[h, 1])
    for i, (s, h) in enumerate(chains):
        for c in range(n):
            o = out[i][c]
            mu = jnp.mean(o, axis=-1, keepdims=True)
            d = o - mu
            var = jnp.mean(d * d, axis=-1, keepdims=True)
            on = d * lax.rsqrt(var + EPS)
            gg = cols(s, 3, h)[rows[c]]
            g_scr[s, rows[c], h * E:(h + 1) * E] = (gg * jax.nn.sigmoid(gg) * on).astype(BF16)
    for s in range(nb):
        y_ref[s] = _dot(g_scr[s], wo_ref[...])


def _retention(x, mods3, mod_row, norm1, w_qkvg, dec8, s0f, s0b, w_o, *, emit_state, nb,
               casts=()):
    B, L, D = x.shape
    C = min(RET_CHUNK, L)
    has_init = s0f is not None
    H, E = RET_HEADS, HEAD_DIM
    in_specs = [pl.BlockSpec((nb, L, D), lambda g: (g, 0, 0)),
                pl.BlockSpec((1, N_MOD, D), lambda g: (mod_row(g * nb), 0, 0)),
                _const_spec((1, D)),
                _const_spec((D, N_QKVG)),
                _const_spec((8, C))]
    args = [x, mods3, norm1, w_qkvg, dec8[:, :C]]
    st_spec = pl.BlockSpec((nb, H, E, E), lambda g: (g, 0, 0, 0))
    if has_init:
        in_specs += [st_spec, st_spec]
        args += [s0f, s0b]
    in_specs.append(_const_spec((RET_W, D)))
    args.append(w_o)
    c_in, c_out, c_shape, c_args = _cast_specs(casts, B // nb)
    in_specs += c_in
    args += c_args
    out_specs = [pl.BlockSpec((nb, L, D), lambda g: (g, 0, 0))]
    out_shape = [jax.ShapeDtypeStruct((B, L, D), F32)]
    if emit_state:
        out_specs += [st_spec, st_spec]
        out_shape += [jax.ShapeDtypeStruct((B, H, E, E), F32)] * 2
    out_specs += c_out
    out_shape += c_shape
    return pl.pallas_call(
        functools.partial(_ret_kernel, L=L, C=C, nb=nb, has_init=has_init,
                          emit_state=emit_state, casts=tuple(cs for _, cs in casts)),
        grid=(B // nb,),
        in_specs=in_specs,
        out_specs=out_specs,
        out_shape=out_shape,
        scratch_shapes=[pltpu.VMEM((H, C, C), F32),
                        pltpu.VMEM((H, 4, C, E), F32),
                        pltpu.VMEM((8, E), F32),
                        pltpu.VMEM((nb, L, RET_W), BF16)],
        compiler_params=_params(1),
        name=f"retention{L}",
    )(*args)


def _hy_kernel(x_ref, mod_ref, n1_ref, w_ref, cw_ref, cb_ref, fw_ref, bw_ref, fa_ref, fb_ref,
               fd_ref, hb_ref, wo_ref, *rest, L, W, b, nb, casts):
    nc = len(casts)
    y_ref = rest[nc]
    _do_casts(casts, rest[0:nc], rest[nc + 1:])
    m = L // b
    CB = HY_CBLK
    nblk = HY_W // CB
    mod = mod_ref[0]
    pos = lax.broadcasted_iota(jnp.int32, (L, CB), 0) % W
    first = pos == 0
    last = pos == W - 1
    hn = [_modnorm(x_ref[s], n1_ref[...], mod[1:2], mod[0:1]).astype(BF16) for s in range(nb)]
    chains = [(s, blk) for s in range(nb) for blk in range(nblk)]

    def short_conv(s, base, blk):
        cs = slice(base + blk * CB, base + (blk + 1) * CB)
        ug = _dot(hn[s], w_ref[:, cs])
        prev = jnp.where(first, 0.0, pltpu.roll(ug, 1, axis=0))
        nxt = jnp.where(last, 0.0, pltpu.roll(ug, L - 1, axis=0))
        u = (prev * cw_ref[0:1, cs] + ug * cw_ref[1:2, cs] + nxt * cw_ref[2:3, cs]
             + cb_ref[:, cs])
        return [u[j * b:(j + 1) * b] for j in range(m)]

    def long_conv(sigs, o):
        spec = [[_dot(fw_ref[...], sj.astype(BF16)) for sj in sig] for sig in sigs]
        prods = []
        for (s, blk), sp in zip(chains, spec):
            cs = slice(blk * CB, (blk + 1) * CB)
            per_i = []
            for i in range(m):
                yre = yim = yim8 = None
                for j in range(m):
                    d = i - j + m - 1
                    sre, sim = sp[j][0:b], sp[j][b:2 * b]
                    ka, kb = fa_ref[o, d, :, cs], fb_ref[o, d, :, cs]
                    tre = sre * ka - sim * kb
                    tim = sre * kb + sim * ka
                    t8 = sre[0:8] * kb[0:8] + sim[0:8] * fd_ref[o, d, :, cs]
                    yre = tre if yre is None else yre + tre
                    yim = tim if yim is None else yim + tim
                    yim8 = t8 if yim8 is None else yim8 + t8
                yim = jnp.concatenate([yim8, yim[8:]], axis=0)
                per_i.append((yre.astype(BF16), yim.astype(BF16)))
            prods.append(per_i)
        return [[_dot(bw_ref[:, 0:b], yre) + _dot(bw_ref[:, b:2 * b], yim) for yre, yim in per_i]
                for per_i in prods]

    hv = [short_conv(s, 0, blk) for s, blk in chains]
    hx1 = [short_conv(s, HY_W, blk) for s, blk in chains]
    hx2 = [short_conv(s, 2 * HY_W, blk) for s, blk in chains]

    def gate(hx, conv, sig, o):
        out = []
        for (s, blk), hxc, cc, sc in zip(chains, hx, conv, sig):
            bias = hb_ref[o:o + 1, blk * CB:(blk + 1) * CB]
            out.append([hxc[i] * (cc[i] + sc[i] * bias) for i in range(m)])
        return out

    z = gate(hx1, long_conv(hv, 0), hv, 0)
    z = gate(hx2, long_conv(z, 1), z, 1)
    for s in range(nb):
        for i in range(m):
            acc = None
            for blk in range(nblk):
                zc = z[chains.index((s, blk))][i].astype(BF16)
                part = _dot(zc, wo_ref[blk * CB:(blk + 1) * CB, :])
                acc = part if acc is None else acc + part
            y_ref[s, i * b:(i + 1) * b, :] = acc


def _hyena(x, mods3, mod_row, norm1, w_hy, conv_w, conv_b, fw, bw, filt, hy_bias, w_o, *, W, b,
           nb, casts=()):
    B, L, D = x.shape
    fa, fb, fd = filt
    nd = fa.shape[1]
    c_in, c_out, c_shape, c_args = _cast_specs(casts, B // nb)
    return pl.pallas_call(
        functools.partial(_hy_kernel, L=L, W=W, b=b, nb=nb,
                          casts=tuple(cs for _, cs in casts)),
        grid=(B // nb,),
        in_specs=[pl.BlockSpec((nb, L, D), lambda g: (g, 0, 0)),
                  pl.BlockSpec((1, N_MOD, D), lambda g: (mod_row(g * nb), 0, 0)),
                  _const_spec((1, D)),
                  _const_spec((D, N_HY)),
                  _const_spec((3, N_HY)),
                  _const_spec((1, N_HY)),
                  _const_spec((2 * b, b)),
                  _const_spec((b, 2 * b)),
                  _const_spec((HY_ORDER, nd, b, HY_W)),
                  _const_spec((HY_ORDER, nd, b, HY_W)),
                  _const_spec((HY_ORDER, nd, 8, HY_W)),
                  _const_spec((HY_ORDER, HY_W)),
                  _const_spec((HY_W, D))] + c_in,
        out_specs=[pl.BlockSpec((nb, L, D), lambda g: (g, 0, 0))] + c_out,
        out_shape=[jax.ShapeDtypeStruct((B, L, D), F32)] + c_shape,
        compiler_params=_params(1),
        name=f"hyena{L}",
    )(x, mods3, norm1, w_hy, conv_w, conv_b, fw, bw, fa, fb, fd, hy_bias, w_o, *c_args)


def _mlp_kernel(x_ref, yr_ref, yh_ref, modp_ref, modq_ref, n1_ref, n2_ref, fg_ref, wg_ref,
                wout_ref, wfi_ref, wfo_ref, y_ref, x1_scr, h2_scr, *, n_tiles):
    i = pl.program_id(0)
    wr = i % 2
    rd = 1 - wr
    nq = N_GATE // 4

    def pre_stages():
        mp = modp_ref[0]
        st = {}

        def p1():
            st["x"] = x_ref[...]
            st["hn"] = _modnorm(st["x"], n1_ref[...], mp[1:2], mp[0:1]).astype(BF16)

        def p2(q):
            def f():
                st["g%d" % q] = _dot(st["hn"], wg_ref[:, q * nq:(q + 1) * nq])
            return f

        def p3(h):
            def f():
                cs = slice(h * nq, (h + 1) * nq)
                st["mix%d" % h] = (jax.nn.sigmoid(st["g%d" % h]) * yr_ref[:, cs]
                                   + jax.nn.sigmoid(st["g%d" % (2 + h)]) * yh_ref[:, cs]
                                   ).astype(BF16)
            return f

        def p4():
            upd = (_dot(st["mix0"], wout_ref[0:nq, :]) + _dot(st["mix1"], wout_ref[nq:2 * nq, :]))
            st["x1"] = st["x"] + mp[2:3] * upd

        def p5():
            x1_scr[wr] = st["x1"]
            h2_scr[wr] = _modnorm(st["x1"], n2_ref[...], mp[4:5], mp[3:4]).astype(BF16)

        return [p1, p2(0), p2(1), p2(2), p2(3), p3(0), p3(1), p4, p5]

    def ffn_stages():
        mq = modq_ref[0]
        st = {"acc": None}

        def f(j):
            def g():
                cs = slice(j * FF_CHUNK, (j + 1) * FF_CHUNK)
                h2 = h2_scr[rd]
                a = _dot(h2, wfi_ref[:, cs])
                b = _dot(h2, wfi_ref[:, D_FF + j * FF_CHUNK:D_FF + (j + 1) * FF_CHUNK])
                ff = (a * jax.nn.sigmoid(a) * b).astype(BF16)
                part = _dot(ff, wfo_ref[cs, :])
                st["acc"] = part if st["acc"] is None else st["acc"] + part
            return g

        def e():
            x2 = x1_scr[rd] + mq[5:6] * st["acc"]
            ms = jnp.mean(x2 * x2, axis=-1, keepdims=True)
            y_ref[...] = x2 * lax.rsqrt(ms + EPS) * fg_ref[...]

        return [f(j) for j in range(D_FF // FF_CHUNK)] + [e]

    @pl.when(i == 0)
    def _():
        for stage in pre_stages():
            stage()

    @pl.when(jnp.logical_and(i > 0, i < n_tiles))
    def _():
        pre, ffn = pre_stages(), ffn_stages()
        order = []
        while pre or ffn:
            if ffn:
                order.append(ffn.pop(0))
            if pre:
                order.append(pre.pop(0))
        for stage in order:
            stage()

    @pl.when(i == n_tiles)
    def _():
        for stage in ffn_stages():
            stage()


def _mlp(x, y_ret, y_hy, mods3, mod_row, norm1, norm2, final_g, w_gate, w_out, w_fi, w_fo):
    B, L, D = x.shape
    T = MLP_ROWS
    n_tiles = B * L // T
    flat = lambda a: a.reshape(B * L, D)
    pre_tile = lambda i: jnp.minimum(i, n_tiles - 1)
    post_tile = lambda i: jnp.maximum(i - 1, 0)
    act = pl.BlockSpec((T, D), lambda i: (pre_tile(i), 0))
    y = pl.pallas_call(
        functools.partial(_mlp_kernel, n_tiles=n_tiles),
        grid=(n_tiles + 1,),
        in_specs=[act, act, act,
                  pl.BlockSpec((1, N_MOD, D), lambda i: (mod_row((pre_tile(i) * T) // L), 0, 0)),
                  pl.BlockSpec((1, N_MOD, D), lambda i: (mod_row((post_tile(i) * T) // L), 0, 0)),
                  _const_spec((1, D)), _const_spec((1, D)), _const_spec((1, D)),
                  _const_spec((D, N_GATE)),
                  _const_spec((D, D)),
                  _const_spec((D, 2 * D_FF)),
                  _const_spec((D_FF, D))],
        out_specs=pl.BlockSpec((T, D), lambda i: (post_tile(i), 0)),
        out_shape=jax.ShapeDtypeStruct((B * L, D), F32),
        scratch_shapes=[pltpu.VMEM((2, T, D), F32), pltpu.VMEM((2, T, D), BF16)],
        compiler_params=_params(1),
        name=f"mlp{L}",
    )(flat(x), flat(y_ret), flat(y_hy), mods3, mods3, norm1, norm2, final_g, w_gate, w_out, w_fi,
      w_fo)
    return y.reshape(B, L, D)


def kernel(x_prompt, x_sample, state_ret_fwd, state_ret_bwd, c, c_ctx, norm1_g, norm2_g, w_ada,
           b_ada, w_in, ret_decay_fwd, ret_decay_bwd, hy_conv_w, hy_conv_b, hy_pos_w1, hy_pos_b1,
           hy_pos_w2, hy_pos_b2, hy_pos_w3, hy_sin_freq, hy_bias, w_ret_o, w_hy_o, w_out,
           w_ffn_in, w_ffn_out, final_g):
    assert w_in.shape[0] == 1, "single-layer configuration"
    nb_lat = x_sample.shape[0]
    l_ctx = x_prompt.shape[1]

    cond8 = jnp.concatenate([c_ctx[None, :], c, jnp.zeros((8 - 1 - nb_lat, D_MODEL), F32)])
    n_hy_end = N_QKVG + N_HY
    w_in_parts = (slice(0, N_QKVG), slice(N_QKVG, n_hy_end), slice(n_hy_end, N_IN))
    mods, w_qkvg, w_hy, w_gate, w_ret_o_b, w_hy_o_b = _ada(
        cond8, w_ada[0], b_ada,
        casts=[(w_in[0], w_in_parts), (w_ret_o[0], None), (w_hy_o[0], None)])
    mods3 = mods.reshape(8, N_MOD, D_MODEL)
    norm1 = norm1_g[0][None, :]
    norm2 = norm2_g[0][None, :]
    fg = final_g[None, :]
    dec8 = jnp.broadcast_to(jnp.concatenate([ret_decay_fwd[0], ret_decay_bwd[0]])[:, None],
                            (8, RET_CHUNK))
    w1 = jnp.pad(hy_pos_w1[0], ((0, HY_EMB_PAD - HY_EMB), (0, 0)))
    b1, b2 = hy_pos_b1[0][:, None], hy_pos_b2[0][:, None]
    freq = hy_sin_freq[0][:, None]
    conv_b = hy_conv_b[0][None, :]

    def mixers(x, mod_row, s0f, s0b, grid_w, emit_state, nb, ret_casts=(), hy_casts=()):
        L = x.shape[1]
        blk = min(HY_TBLK, L)
        fw, bw, sgn = _dft_mats(blk)
        filt = _filters(L, blk, fw, sgn, w1, b1, hy_pos_w2[0], b2, hy_pos_w3[0], freq)
        ret = _retention(x, mods3, mod_row, norm1, w_qkvg, dec8, s0f, s0b, w_ret_o_b,
                         emit_state=emit_state, nb=nb, casts=ret_casts)
        hy = _hyena(x, mods3, mod_row, norm1, w_hy, hy_conv_w[0], conv_b, fw, bw, filt,
                    hy_bias[0], w_hy_o_b, W=grid_w, b=blk, nb=nb, casts=hy_casts)
        return ret, hy

    ctx_row = lambda b: 0
    lat_row = lambda b: b + 1
    ret_c, hy_c = mixers(x_prompt, ctx_row, None, None, l_ctx, True, CTX_SEQS,
                         ret_casts=[(w_ffn_in[0], None)],
                         hy_casts=[(w_ffn_out[0], None), (w_out[0], None)])
    y_ret_c, s_f, s_b, w_fi_b = ret_c
    y_hy_c, w_fo_b, w_out_b = hy_c
    y_prompt = _mlp(x_prompt, y_ret_c, y_hy_c, mods3, ctx_row, norm1, norm2, fg, w_gate, w_out_b,
                    w_fi_b, w_fo_b)
    ret_l, hy_l = mixers(x_sample, lat_row, state_ret_fwd[:, 0], state_ret_bwd[:, 0], GRID_W,
                         False, 1)
    y_sample = _mlp(x_sample, ret_l[0], hy_l[0], mods3, lat_row, norm1, norm2, fg, w_gate,
                    w_out_b, w_fi_b, w_fo_b)
    return (y_prompt, y_sample, s_f[:, None], s_b[:, None])
```

```python
import functools
import math

import numpy as np
import jax
import jax.numpy as jnp
from jax import lax
from jax.experimental import pallas as pl
from jax.experimental.pallas import tpu as pltpu

F32 = jnp.float32
BF16 = jnp.bfloat16

D_MODEL = 1024
RET_HEADS = 4
HEAD_DIM = 128
RET_W = RET_HEADS * HEAD_DIM
HY_W = 512
HY_ORDER = 2
HY_BANDS = 16
HY_EMB = 1 + 2 * HY_BANDS
HY_EMB_PAD = 40
HY_HIDDEN = 64
HY_FAST_DECAY = 0.3
HY_SLOW_DECAY = 1.5
HY_TARGET = 1e-2
D_FF = 2816
N_QKVG = 4 * RET_W
N_HY = 3 * HY_W
N_GATE = 2 * D_MODEL
N_IN = N_QKVG + N_HY + N_GATE
N_MOD = 6
EPS = 1e-6
GRID_W = 64
RET_CHUNK = 256
HY_CBLK = 256
HY_TBLK = 512
CTX_SEQS = 2
MLP_ROWS = 512
FF_CHUNK = 256
ADA_COLS = 768
VMEM_LIMIT = 56 * 1024 * 1024


def _const_spec(shape):
    nd = len(shape)
    return pl.BlockSpec(shape, lambda *_: (0,) * nd, pipeline_mode=pl.Buffered(1))


def _params(n_axes):
    return pltpu.CompilerParams(dimension_semantics=("arbitrary",) * n_axes,
                                vmem_limit_bytes=VMEM_LIMIT)


def _modnorm(x, g, scale, shift):
    ms = jnp.mean(x * x, axis=-1, keepdims=True)
    return (x * lax.rsqrt(ms + EPS) * g) * (1.0 + scale) + shift


def _dot(a, b):
    return jnp.dot(a, b, preferred_element_type=F32)


def _cast_specs(casts, steps):
    in_specs, out_specs, out_shape, args = [], [], [], []
    for arr, col_slices in casts:
        rows, width = arr.shape
        rb = rows // steps
        assert rb * steps == rows and rb % 16 == 0
        in_specs.append(pl.BlockSpec((rb, width), lambda g: (g, 0)))
        args.append(arr)
        for cs in col_slices or (slice(0, width),):
            cols = cs.stop - cs.start
            out_specs.append(pl.BlockSpec((rb, cols), lambda g: (g, 0)))
            out_shape.append(jax.ShapeDtypeStruct((rows, cols), BF16))
    return in_specs, out_specs, out_shape, args


def _n_cast_outputs(col_slices_per_src):
    return sum(1 if s is None else len(s) for s in col_slices_per_src)


def _do_casts(col_slices_per_src, srcs, dsts):
    dsts = iter(dsts)
    for col_slices, src in zip(col_slices_per_src, srcs):
        if col_slices is None:
            next(dsts)[...] = src[...].astype(BF16)
        else:
            for cs in col_slices:
                next(dsts)[...] = src[:, cs].astype(BF16)


def _ada_kernel(c_ref, w_ref, b_ref, *rest, casts):
    nc = len(casts)
    o_ref = rest[nc]
    _do_casts(casts, rest[0:nc], rest[nc + 1:])
    c = c_ref[...]
    s = (c * jax.nn.sigmoid(c)).astype(BF16)
    o_ref[...] = _dot(s, w_ref[...].astype(BF16)) + b_ref[...]


def _ada(cond8, w, b, casts=()):
    n = w.shape[1]
    steps = n // ADA_COLS
    c_in, c_out, c_shape, c_args = _cast_specs(casts, steps)
    return pl.pallas_call(
        functools.partial(_ada_kernel, casts=tuple(cs for _, cs in casts)),
        grid=(steps,),
        in_specs=[pl.BlockSpec((8, D_MODEL), lambda j: (0, 0)),
                  pl.BlockSpec((D_MODEL, ADA_COLS), lambda j: (0, j)),
                  pl.BlockSpec((1, ADA_COLS), lambda j: (0, j))] + c_in,
        out_specs=[pl.BlockSpec((8, ADA_COLS), lambda j: (0, j))] + c_out,
        out_shape=[jax.ShapeDtypeStruct((8, n), F32)] + c_shape,
        compiler_params=_params(1),
        name="ada",
    )(cond8, w, b, *c_args)


@functools.lru_cache(maxsize=None)
def _dft_mats(L):
    n = 2 * L
    t = np.arange(L, dtype=np.int64)
    f = np.arange(L, dtype=np.int64)
    ang = 2.0 * np.pi * ((f[:, None] * t[None, :]) % n).astype(np.float64) / n
    cos = np.cos(ang)
    sin = np.sin(ang)
    nyq = np.where(t % 2 == 0, 1.0, -1.0)
    fwd = np.concatenate([cos, -sin], axis=0)
    fwd[L] = nyq
    wre = np.full((L,), 2.0 / n)
    wre[0] = 1.0 / n
    inv = np.concatenate([cos.T * wre[None, :], -sin.T * (2.0 / n)], axis=1)
    inv[:, L] = nyq / n
    sgn = np.broadcast_to(nyq[:, None], (L, HY_W))
    return (jnp.asarray(fwd, dtype=BF16), jnp.asarray(inv, dtype=BF16),
            np.asarray(sgn, dtype=np.float32))


@functools.lru_cache(maxsize=None)
def _filter_consts(L):
    t = np.linspace(0.0, 1.0, L)[:, None]
    ang = 2.0 * np.pi * np.arange(L, dtype=np.float64)[:, None] / L
    bands = np.linspace(1e-4, HY_BANDS - 1, HY_BANDS)[None]
    z = np.concatenate([t, np.cos(bands * ang), -np.sin(bands * ang)], axis=-1)
    z = np.pad(z, ((0, 0), (0, HY_EMB_PAD - HY_EMB)))
    max_decay = math.log(HY_TARGET) / HY_FAST_DECAY
    min_decay = math.log(HY_TARGET) / HY_SLOW_DECAY
    deltas = np.linspace(min_decay, max_decay, HY_W)
    tdel = t * np.abs(deltas)[None, :]
    return np.asarray(z, np.float32), np.asarray(tdel, np.float32)


def _filter_kernel(zt_ref, tdel_ref, sgn_ref, w1_ref, b1_ref, w2_ref, b2_ref, w3_ref, fr_ref,
                   fw_ref, oa_ref, ob_ref, od_ref, *, L, b):
    m = L // b
    hi = lax.Precision.HIGHEST
    tdims = (((0,), (0,)), ((), ()))
    fr = fr_ref[...]
    a1 = lax.dot_general(w1_ref[...], zt_ref[...], tdims, precision=hi,
                         preferred_element_type=F32)
    h1 = jnp.sin(fr * (a1 + b1_ref[...]))
    a2 = lax.dot_general(w2_ref[...], h1, tdims, precision=hi, preferred_element_type=F32)
    h2 = jnp.sin(fr * (a2 + b2_ref[...]))
    h2_hi = h2.astype(BF16)
    h2_lo = (h2 - h2_hi.astype(F32)).astype(BF16)
    w3 = w3_ref[...]
    w3_hi = w3.astype(BF16)
    w3_lo = (w3 - w3_hi.astype(F32)).astype(BF16)
    lhs = jnp.concatenate([h2_hi, h2_lo, h2_hi, jnp.zeros_like(h2_hi)], axis=0)
    rhs = jnp.concatenate([w3_hi, w3_hi, w3_lo, jnp.zeros_like(w3_hi)], axis=0)
    h = lax.dot_general(lhs, rhs, tdims, preferred_element_type=F32)
    win = jnp.exp(-tdel_ref[...])
    sg = sgn_ref[...]
    row0_l = lax.broadcasted_iota(jnp.int32, (L, HY_W), 0) == 0
    row0_b = lax.broadcasted_iota(jnp.int32, (b, HY_W), 0) == 0
    row0_8 = lax.broadcasted_iota(jnp.int32, (8, HY_W), 0) == 0
    for o in range(HY_ORDER):
        base = o * 2 * HY_W
        fwd = h[:, base:base + HY_W] * win
        bwd = jnp.where(row0_l, 0.0, h[:, base + HY_W:base + 2 * HY_W] * win)
        nrm = (jnp.sum(jnp.abs(fwd), axis=0, keepdims=True)
               + jnp.sum(jnp.abs(bwd), axis=0, keepdims=True))
        inv = 1.0 / nrm
        fn = fwd * inv
        bn = bwd * inv
        xr, xn, xi, wr, wn, wi = [], [], [], [], [], []
        for r in range(m):
            p = _dot(fw_ref[...], fn[r * b:(r + 1) * b].astype(BF16))
            q = _dot(fw_ref[...], bn[r * b:(r + 1) * b].astype(BF16))
            xr.append(p[0:b])
            xn.append(p[b:b + 1])
            xi.append(jnp.where(row0_b, 0.0, p[b:2 * b]))
            wr.append(q[0:b])
            wn.append(q[b:b + 1])
            wi.append(jnp.where(row0_b, 0.0, -q[b:2 * b]))

        def emit(d, ka, kn, kb):
            oa_ref[o, d + m - 1] = ka
            ob_ref[o, d + m - 1] = kb
            od_ref[o, d + m - 1] = jnp.where(row0_8, kn, ka[0:8])

        emit(0, xr[0] + wr[0], xn[0] + wn[0], xi[0] + wi[0])
        for d in range(1, m):
            f0 = fn[(d - 1) * b:(d - 1) * b + 1]
            b0 = bn[(d - 1) * b:(d - 1) * b + 1]
            emit(d, xr[d] + sg * (xr[d - 1] - f0), xn[d] + (xn[d - 1] - f0),
                 xi[d] + sg * xi[d - 1])
            emit(-d, wr[d] + sg * (wr[d - 1] - b0), wn[d] + (wn[d - 1] - b0),
                 wi[d] + sg * wi[d - 1])


def _filters(L, b, fw, sgn, w1, b1, w2, b2, w3, freq):
    z, tdel = _filter_consts(L)
    nd = 2 * (L // b) - 1
    args = (jnp.asarray(z.T), jnp.asarray(tdel), jnp.asarray(sgn), w1, b1, w2, b2, w3, freq, fw)
    return pl.pallas_call(
        functools.partial(_filter_kernel, L=L, b=b),
        out_shape=[jax.ShapeDtypeStruct((HY_ORDER, nd, b, HY_W), F32),
                   jax.ShapeDtypeStruct((HY_ORDER, nd, b, HY_W), F32),
                   jax.ShapeDtypeStruct((HY_ORDER, nd, 8, HY_W), F32)],
        compiler_params=pltpu.CompilerParams(vmem_limit_bytes=VMEM_LIMIT),
        name=f"filters{L}",
    )(*args)


def _ret_kernel(*refs, L, C, nb, has_init, emit_state, casts):
    it = iter(refs)
    x_ref, mod_ref, n1_ref, w_ref, dec_ref = (next(it) for _ in range(5))
    s0f_ref = s0b_ref = sf_ref = sb_ref = None
    if has_init:
        s0f_ref, s0b_ref = next(it), next(it)
    wo_ref = next(it)
    cast_srcs = [next(it) for _ in casts]
    y_ref = next(it)
    if emit_state:
        sf_ref, sb_ref = next(it), next(it)
    cast_dsts = [next(it) for _ in range(_n_cast_outputs(casts))]
    mask_scr, vec_scr, cd_scr, g_scr = (next(it) for _ in range(4))
    _do_casts(casts, cast_srcs, cast_dsts)
    n = L // C
    H, E = RET_HEADS, HEAD_DIM
    scale = float(E) ** -0.5

    @pl.when(pl.program_id(0) == 0)
    def _():
        lg = jnp.log(jax.nn.sigmoid(dec_ref[...]))
        cd_scr[...] = jnp.exp(float(C) * lg[:, 0:E])
        ii = lax.broadcasted_iota(jnp.int32, (C, C), 0)
        jj = lax.broadcasted_iota(jnp.int32, (C, C), 1)
        rel = (ii - jj).astype(F32)
        ri = lax.broadcasted_iota(jnp.int32, (C, E), 0).astype(F32)
        for h in range(H):
            lf = lg[h:h + 1, :]
            lb = lg[H + h:H + h + 1, :]
            mf = jnp.where(rel >= 0, jnp.exp(jnp.maximum(rel, 0.0) * lf), 0.0)
            mb = jnp.where(rel <= 0, jnp.exp(jnp.maximum(-rel, 0.0) * lb), 0.0)
            mask_scr[h] = scale * (mf + mb)
            lfe, lbe = lf[:, 0:E], lb[:, 0:E]
            vec_scr[h, 0] = jnp.exp((ri + 1.0) * lfe)
            vec_scr[h, 1] = jnp.exp((float(C) - ri) * lbe)
            vec_scr[h, 2] = scale * jnp.exp((float(C) - 1.0 - ri) * lfe)
            vec_scr[h, 3] = scale * jnp.exp(ri * lbe)

    mod = mod_ref[0]
    tdims = (((0,), (0,)), ((), ()))
    ndims = (((1,), (1,)), ((), ()))
    chains = [(s, h) for s in range(nb) for h in range(H)]
    rows = [slice(c * C, (c + 1) * C) for c in range(n)]
    qkvg = [_dot(_modnorm(x_ref[s], n1_ref[...], mod[1:2], mod[0:1]).astype(BF16), w_ref[...])
            for s in range(nb)]

    def cols(s, part, h):
        return qkvg[s][:, part * RET_W + h * E:part * RET_W + (h + 1) * E]

    qb = [cols(s, 0, h).astype(BF16) for s, h in chains]
    kf = [cols(s, 1, h) for s, h in chains]
    kb = [k.astype(BF16) for k in kf]
    vb = [cols(s, 2, h).astype(BF16) for s, h in chains]
    att = [[lax.dot_general(qb[i][r], kb[i][r], ndims, preferred_element_type=F32) for r in rows]
           for i in range(len(chains))]
    prob = [[(att[i][c] * mask_scr[h]).astype(BF16) for c in range(n)]
            for i, (s, h) in enumerate(chains)]
    out = [[_dot(prob[i][c], vb[i][rows[c]]) for c in range(n)] for i in range(len(chains))]
    kv = []
    for i, (s, h) in enumerate(chains):
        dk2 = jnp.concatenate([vec_scr[h, 2], vec_scr[h, 3]], axis=1)
        per_c = []
        for r in rows:
            k2 = (jnp.concatenate([kf[i][r], kf[i][r]], axis=1) * dk2).astype(BF16)
            per_c.append(lax.dot_general(k2, vb[i][r], tdims, preferred_element_type=F32))
        kv.append(per_c)
    for i, (s, h) in enumerate(chains):
        cdf = cd_scr[h:h + 1, :]
        cdb = cd_scr[H + h:H + h + 1, :]
        sf_in, sb_in = [None] * n, [None] * n
        st = s0f_ref[s, h] if has_init else None
        for c in range(n):
            sf_in[c] = st
            kvc = kv[i][c][0:E]
            st = kvc if st is None else st * cdf + kvc
        if emit_state:
            sf_ref[s, h] = st
        st = s0b_ref[s, h] if has_init else None
        for c in range(n - 1, -1, -1):
            sb_in[c] = st
            kvc = kv[i][c][E:2 * E]
            st = kvc if st is None else st * cdb + kvc
        if emit_state:
            sb_ref[s, h] = st
        for c in range(n):
            if sf_in[c] is not None and sb_in[c] is not None:
                s2 = jnp.concatenate([sf_in[c], sb_in[c]], axis=1).astype(BF16)
                inter = _dot(qb[i][rows[c]], s2)
                out[i][c] = (out[i][c] + inter[:, 0:E] * vec_scr[h, 0]
                             + inter[:, E:2 * E] * vec_scr[h, 1])
            elif sf_in[c] is not None:
                out[i][c] = (out[i][c]
                             + _dot(qb[i][rows[c]], sf_in[c].astype(BF16)) * vec_scr[h, 0])
            elif sb_in[c] is not None:
                out[i][c] = (out[i][c]
                             + _dot(qb[i][rows[c]], sb_in[c].astype(BF16)) * vec_scr[h, 1])
    for i, (s, h) in enumerate(chains):
        for c in range(n):
            o = out[i][c]
            mu = jnp.mean(o, axis=-1, keepdims=True)
            d = o - mu
            var = jnp.mean(d * d, axis=-1, keepdims=True)
            on = d * lax.rsqrt(var + EPS)
            gg = cols(s, 3, h)[rows[c]]
            g_scr[s, rows[c], h * E:(h + 1) * E] = (gg * jax.nn.sigmoid(gg) * on).astype(BF16)
    for s in range(nb):
        y_ref[s] = _dot(g_scr[s], wo_ref[...])


def _retention(x, mods3, mod_row, norm1, w_qkvg, dec8, s0f, s0b, w_o, *, emit_state, nb,
               casts=()):
    B, L, D = x.shape
    C = min(RET_CHUNK, L)
    has_init = s0f is not None
    H, E = RET_HEADS, HEAD_DIM
    in_specs = [pl.BlockSpec((nb, L, D), lambda g: (g, 0, 0)),
                pl.BlockSpec((1, N_MOD, D), lambda g: (mod_row(g * nb), 0, 0)),
                _const_spec((1, D)),
                _const_spec((D, N_QKVG)),
                _const_spec((8, C))]
    args = [x, mods3, norm1, w_qkvg, dec8[:, :C]]
    st_spec = pl.BlockSpec((nb, H, E, E), lambda g: (g, 0, 0, 0))
    if has_init:
        in_specs += [st_spec, st_spec]
        args += [s0f, s0b]
    in_specs.append(_const_spec((RET_W, D)))
    args.append(w_o)
    c_in, c_out, c_shape, c_args = _cast_specs(casts, B // nb)
    in_specs += c_in
    args += c_args
    out_specs = [pl.BlockSpec((nb, L, D), lambda g: (g, 0, 0))]
    out_shape = [jax.ShapeDtypeStruct((B, L, D), F32)]
    if emit_state:
        out_specs += [st_spec, st_spec]
        out_shape += [jax.ShapeDtypeStruct((B, H, E, E), F32)] * 2
    out_specs += c_out
    out_shape += c_shape
    return pl.pallas_call(
        functools.partial(_ret_kernel, L=L, C=C, nb=nb, has_init=has_init,
                          emit_state=emit_state, casts=tuple(cs for _, cs in casts)),
        grid=(B // nb,),
        in_specs=in_specs,
        out_specs=out_specs,
        out_shape=out_shape,
        scratch_shapes=[pltpu.VMEM((H, C, C), F32),
                        pltpu.VMEM((H, 4, C, E), F32),
                        pltpu.VMEM((8, E), F32),
                        pltpu.VMEM((nb, L, RET_W), BF16)],
        compiler_params=_params(1),
        name=f"retention{L}",
    )(*args)


def _hy_kernel(x_ref, mod_ref, n1_ref, w_ref, cw_ref, cb_ref, fw_ref, bw_ref, fa_ref, fb_ref,
               fd_ref, hb_ref, wo_ref, *rest, L, W, b, nb, casts):
    nc = len(casts)
    y_ref = rest[nc]
    _do_casts(casts, rest[0:nc], rest[nc + 1:])
    m = L // b
    CB = HY_CBLK
    nblk = HY_W // CB
    mod = mod_ref[0]
    pos = lax.broadcasted_iota(jnp.int32, (L, CB), 0) % W
    first = pos == 0
    last = pos == W - 1
    hn = [_modnorm(x_ref[s], n1_ref[...], mod[1:2], mod[0:1]).astype(BF16) for s in range(nb)]
    chains = [(s, blk) for s in range(nb) for blk in range(nblk)]

    def short_conv(s, base, blk):
        cs = slice(base + blk * CB, base + (blk + 1) * CB)
        ug = _dot(hn[s], w_ref[:, cs])
        prev = jnp.where(first, 0.0, pltpu.roll(ug, 1, axis=0))
        nxt = jnp.where(last, 0.0, pltpu.roll(ug, L - 1, axis=0))
        u = (prev * cw_ref[0:1, cs] + ug * cw_ref[1:2, cs] + nxt * cw_ref[2:3, cs]
             + cb_ref[:, cs])
        return [u[j * b:(j + 1) * b] for j in range(m)]

    def long_conv(sigs, o):
        spec = [[_dot(fw_ref[...], sj.astype(BF16)) for sj in sig] for sig in sigs]
        prods = []
        for (s, blk), sp in zip(chains, spec):
            cs = slice(blk * CB, (blk + 1) * CB)
            per_i = []
            for i in range(m):
                yre = yim = yim8 = None
                for j in range(m):
                    d = i - j + m - 1
                    sre, sim = sp[j][0:b], sp[j][b:2 * b]
                    ka, kb = fa_ref[o, d, :, cs], fb_ref[o, d, :, cs]
                    tre = sre * ka - sim * kb
                    tim = sre * kb + sim * ka
                    t8 = sre[0:8] * kb[0:8] + sim[0:8] * fd_ref[o, d, :, cs]
                    yre = tre if yre is None else yre + tre
                    yim = tim if yim is None else yim + tim
                    yim8 = t8 if yim8 is None else yim8 + t8
                yim = jnp.concatenate([yim8, yim[8:]], axis=0)
                per_i.append((yre.astype(BF16), yim.astype(BF16)))
            prods.append(per_i)
        return [[_dot(bw_ref[:, 0:b], yre) + _dot(bw_ref[:, b:2 * b], yim) for yre, yim in per_i]
                for per_i in prods]

    hv = [short_conv(s, 0, blk) for s, blk in chains]
    hx1 = [short_conv(s, HY_W, blk) for s, blk in chains]
    hx2 = [short_conv(s, 2 * HY_W, blk) for s, blk in chains]

    def gate(hx, conv, sig, o):
        out = []
        for (s, blk), hxc, cc, sc in zip(chains, hx, conv, sig):
            bias = hb_ref[o:o + 1, blk * CB:(blk + 1) * CB]
            out.append([hxc[i] * (cc[i] + sc[i] * bias) for i in range(m)])
        return out

    z = gate(hx1, long_conv(hv, 0), hv, 0)
    z = gate(hx2, long_conv(z, 1), z, 1)
    for s in range(nb):
        for i in range(m):
            acc = None
            for blk in range(nblk):
                zc = z[chains.index((s, blk))][i].astype(BF16)
                part = _dot(zc, wo_ref[blk * CB:(blk + 1) * CB, :])
                acc = part if acc is None else acc + part
            y_ref[s, i * b:(i + 1) * b, :] = acc


def _hyena(x, mods3, mod_row, norm1, w_hy, conv_w, conv_b, fw, bw, filt, hy_bias, w_o, *, W, b,
           nb, casts=()):
    B, L, D = x.shape
    fa, fb, fd = filt
    nd = fa.shape[1]
    c_in, c_out, c_shape, c_args = _cast_specs(casts, B // nb)
    return pl.pallas_call(
        functools.partial(_hy_kernel, L=L, W=W, b=b, nb=nb,
                          casts=tuple(cs for _, cs in casts)),
        grid=(B // nb,),
        in_specs=[pl.BlockSpec((nb, L, D), lambda g: (g, 0, 0)),
                  pl.BlockSpec((1, N_MOD, D), lambda g: (mod_row(g * nb), 0, 0)),
                  _const_spec((1, D)),
                  _const_spec((D, N_HY)),
                  _const_spec((3, N_HY)),
                  _const_spec((1, N_HY)),
                  _const_spec((2 * b, b)),
                  _const_spec((b, 2 * b)),
                  _const_spec((HY_ORDER, nd, b, HY_W)),
                  _const_spec((HY_ORDER, nd, b, HY_W)),
                  _const_spec((HY_ORDER, nd, 8, HY_W)),
                  _const_spec((HY_ORDER, HY_W)),
                  _const_spec((HY_W, D))] + c_in,
        out_specs=[pl.BlockSpec((nb, L, D), lambda g: (g, 0, 0))] + c_out,
        out_shape=[jax.ShapeDtypeStruct((B, L, D), F32)] + c_shape,
        compiler_params=_params(1),
        name=f"hyena{L}",
    )(x, mods3, norm1, w_hy, conv_w, conv_b, fw, bw, fa, fb, fd, hy_bias, w_o, *c_args)


def _mlp_kernel(x_ref, yr_ref, yh_ref, modp_ref, modq_ref, n1_ref, n2_ref, fg_ref, wg_ref,
                wout_ref, wfi_hbm, wfo_hbm, y_ref, x1_scr, h2_scr, wfi_ref, wfo_ref, sem, *,
                n_tiles):
    i = pl.program_id(0)
    wr = i % 2
    rd = 1 - wr
    nq = N_GATE // 4

    def pre_stages():
        mp = modp_ref[0]
        st = {}

        def p1():
            st["x"] = x_ref[...]
            st["hn"] = _modnorm(st["x"], n1_ref[...], mp[1:2], mp[0:1]).astype(BF16)

        def p2(q):
            def f():
                st["g%d" % q] = _dot(st["hn"], wg_ref[:, q * nq:(q + 1) * nq])
            return f

        def p3(h):
            def f():
                cs = slice(h * nq, (h + 1) * nq)
                st["mix%d" % h] = (jax.nn.sigmoid(st["g%d" % h]) * yr_ref[:, cs]
                                   + jax.nn.sigmoid(st["g%d" % (2 + h)]) * yh_ref[:, cs]
                                   ).astype(BF16)
            return f

        def p4():
            upd = (_dot(st["mix0"], wout_ref[0:nq, :]) + _dot(st["mix1"], wout_ref[nq:2 * nq, :]))
            st["x1"] = st["x"] + mp[2:3] * upd

        def p5():
            x1_scr[wr] = st["x1"]
            h2_scr[wr] = _modnorm(st["x1"], n2_ref[...], mp[4:5], mp[3:4]).astype(BF16)

        return [p1, p2(0), p2(1), p2(2), p2(3), p3(0), p3(1), p4, p5]

    def ffn_stages():
        mq = modq_ref[0]
        st = {"acc": None}

        def f(j):
            def g():
                cs = slice(j * FF_CHUNK, (j + 1) * FF_CHUNK)
                h2 = h2_scr[rd]
                a = _dot(h2, wfi_ref[:, cs])
                b = _dot(h2, wfi_ref[:, D_FF + j * FF_CHUNK:D_FF + (j + 1) * FF_CHUNK])
                ff = (a * jax.nn.sigmoid(a) * b).astype(BF16)
                part = _dot(ff, wfo_ref[cs, :])
                st["acc"] = part if st["acc"] is None else st["acc"] + part
            return g

        def e():
            x2 = x1_scr[rd] + mq[5:6] * st["acc"]
            ms = jnp.mean(x2 * x2, axis=-1, keepdims=True)
            y_ref[...] = x2 * lax.rsqrt(ms + EPS) * fg_ref[...]

        return [f(j) for j in range(D_FF // FF_CHUNK)] + [e]

    @pl.when(i == 0)
    def _():
        copies = [pltpu.make_async_copy(wfi_hbm, wfi_ref, sem.at[0]),
                  pltpu.make_async_copy(wfo_hbm, wfo_ref, sem.at[1])]
        for cp in copies:
            cp.start()
        for stage in pre_stages():
            stage()
        for cp in copies:
            cp.wait()

    @pl.when(jnp.logical_and(i > 0, i < n_tiles))
    def _():
        pre, ffn = pre_stages(), ffn_stages()
        order = []
        while pre or ffn:
            if ffn:
                order.append(ffn.pop(0))
            if pre:
                order.append(pre.pop(0))
        for stage in order:
            stage()

    @pl.when(i == n_tiles)
    def _():
        for stage in ffn_stages():
            stage()


def _mlp(x, y_ret, y_hy, mods3, mod_row, norm1, norm2, final_g, w_gate, w_out, w_fi, w_fo):
    B, L, D = x.shape
    T = MLP_ROWS
    n_tiles = B * L // T
    flat = lambda a: a.reshape(B * L, D)
    pre_tile = lambda i: jnp.minimum(i, n_tiles - 1)
    post_tile = lambda i: jnp.maximum(i - 1, 0)
    act = pl.BlockSpec((T, D), lambda i: (pre_tile(i), 0))
    y = pl.pallas_call(
        functools.partial(_mlp_kernel, n_tiles=n_tiles),
        grid=(n_tiles + 1,),
        in_specs=[act, act, act,
                  pl.BlockSpec((1, N_MOD, D), lambda i: (mod_row((pre_tile(i) * T) // L), 0, 0)),
                  pl.BlockSpec((1, N_MOD, D), lambda i: (mod_row((post_tile(i) * T) // L), 0, 0)),
                  _const_spec((1, D)), _const_spec((1, D)), _const_spec((1, D)),
                  _const_spec((D, N_GATE)),
                  _const_spec((D, D)),
                  pl.BlockSpec(memory_space=pl.ANY),
                  pl.BlockSpec(memory_space=pl.ANY)],
        out_specs=pl.BlockSpec((T, D), lambda i: (post_tile(i), 0)),
        out_shape=jax.ShapeDtypeStruct((B * L, D), F32),
        scratch_shapes=[pltpu.VMEM((2, T, D), F32), pltpu.VMEM((2, T, D), BF16),
                        pltpu.VMEM((D, 2 * D_FF), BF16), pltpu.VMEM((D_FF, D), BF16),
                        pltpu.SemaphoreType.DMA((2,))],
        compiler_params=_params(1),
        name=f"mlp{L}",
    )(flat(x), flat(y_ret), flat(y_hy), mods3, mods3, norm1, norm2, final_g, w_gate, w_out, w_fi,
      w_fo)
    return y.reshape(B, L, D)


def kernel(x_prompt, x_sample, state_ret_fwd, state_ret_bwd, c, c_ctx, norm1_g, norm2_g, w_ada,
           b_ada, w_in, ret_decay_fwd, ret_decay_bwd, hy_conv_w, hy_conv_b, hy_pos_w1, hy_pos_b1,
           hy_pos_w2, hy_pos_b2, hy_pos_w3, hy_sin_freq, hy_bias, w_ret_o, w_hy_o, w_out,
           w_ffn_in, w_ffn_out, final_g):
    assert w_in.shape[0] == 1, "single-layer configuration"
    nb_lat = x_sample.shape[0]
    l_ctx = x_prompt.shape[1]

    cond8 = jnp.concatenate([c_ctx[None, :], c, jnp.zeros((8 - 1 - nb_lat, D_MODEL), F32)])
    n_hy_end = N_QKVG + N_HY
    w_in_parts = (slice(0, N_QKVG), slice(N_QKVG, n_hy_end), slice(n_hy_end, N_IN))
    mods, w_qkvg, w_hy, w_gate, w_ret_o_b, w_hy_o_b = _ada(
        cond8, w_ada[0], b_ada,
        casts=[(w_in[0], w_in_parts), (w_ret_o[0], None), (w_hy_o[0], None)])
    mods3 = mods.reshape(8, N_MOD, D_MODEL)
    norm1 = norm1_g[0][None, :]
    norm2 = norm2_g[0][None, :]
    fg = final_g[None, :]
    dec8 = jnp.broadcast_to(jnp.concatenate([ret_decay_fwd[0], ret_decay_bwd[0]])[:, None],
                            (8, RET_CHUNK))
    w1 = jnp.pad(hy_pos_w1[0], ((0, HY_EMB_PAD - HY_EMB), (0, 0)))
    b1, b2 = hy_pos_b1[0][:, None], hy_pos_b2[0][:, None]
    freq = hy_sin_freq[0][:, None]
    conv_b = hy_conv_b[0][None, :]

    def mixers(x, mod_row, s0f, s0b, grid_w, emit_state, nb, ret_casts=(), hy_casts=()):
        L = x.shape[1]
        blk = min(HY_TBLK, L)
        fw, bw, sgn = _dft_mats(blk)
        filt = _filters(L, blk, fw, sgn, w1, b1, hy_pos_w2[0], b2, hy_pos_w3[0], freq)
        ret = _retention(x, mods3, mod_row, norm1, w_qkvg, dec8, s0f, s0b, w_ret_o_b,
                         emit_state=emit_state, nb=nb, casts=ret_casts)
        hy = _hyena(x, mods3, mod_row, norm1, w_hy, hy_conv_w[0], conv_b, fw, bw, filt,
                    hy_bias[0], w_hy_o_b, W=grid_w, b=blk, nb=nb, casts=hy_casts)
        return ret, hy

    ctx_row = lambda b: 0
    lat_row = lambda b: b + 1
    ret_c, hy_c = mixers(x_prompt, ctx_row, None, None, l_ctx, True, CTX_SEQS,
                         ret_casts=[(w_ffn_in[0], None)],
                         hy_casts=[(w_ffn_out[0], None), (w_out[0], None)])
    y_ret_c, s_f, s_b, w_fi_b = ret_c
    y_hy_c, w_fo_b, w_out_b = hy_c
    y_prompt = _mlp(x_prompt, y_ret_c, y_hy_c, mods3, ctx_row, norm1, norm2, fg, w_gate, w_out_b,
                    w_fi_b, w_fo_b)
    ret_l, hy_l = mixers(x_sample, lat_row, state_ret_fwd[:, 0], state_ret_bwd[:, 0], GRID_W,
                         False, 1)
    y_sample = _mlp(x_sample, ret_l[0], hy_l[0], mods3, lat_row, norm1, norm2, fg, w_gate,
                    w_out_b, w_fi_b, w_fo_b)
    return (y_prompt, y_sample, s_f[:, None], s_b[:, None])
```

```python
import functools
import math

import numpy as np
import jax
import jax.numpy as jnp
from jax import lax
from jax.experimental import pallas as pl
from jax.experimental.pallas import tpu as pltpu

F32 = jnp.float32
BF16 = jnp.bfloat16

D_MODEL = 1024
RET_HEADS = 4
HEAD_DIM = 128
RET_W = RET_HEADS * HEAD_DIM
HY_W = 512
HY_ORDER = 2
HY_BANDS = 16
HY_EMB = 1 + 2 * HY_BANDS
HY_EMB_PAD = 40
HY_HIDDEN = 64
HY_FAST_DECAY = 0.3
HY_SLOW_DECAY = 1.5
HY_TARGET = 1e-2
D_FF = 2816
N_QKVG = 4 * RET_W
N_HY = 3 * HY_W
N_GATE = 2 * D_MODEL
N_IN = N_QKVG + N_HY + N_GATE
N_MOD = 6
EPS = 1e-6
GRID_W = 64
RET_CHUNK = 256
HY_CBLK = 256
HY_TBLK = 512
CTX_SEQS = 2
MLP_ROWS = 512
FF_CHUNK = 256
ADA_COLS = 768
VMEM_LIMIT = 56 * 1024 * 1024


def _const_spec(shape):
    nd = len(shape)
    return pl.BlockSpec(shape, lambda *_: (0,) * nd, pipeline_mode=pl.Buffered(1))


def _params(n_axes):
    return pltpu.CompilerParams(dimension_semantics=("arbitrary",) * n_axes,
                                vmem_limit_bytes=VMEM_LIMIT)


def _modnorm(x, g, scale, shift):
    ms = jnp.mean(x * x, axis=-1, keepdims=True)
    return (x * lax.rsqrt(ms + EPS) * g) * (1.0 + scale) + shift


def _dot(a, b):
    return jnp.dot(a, b, preferred_element_type=F32)


def _cast_specs(casts, steps):
    in_specs, out_specs, out_shape, args = [], [], [], []
    for arr, col_slices in casts:
        rows, width = arr.shape
        rb = rows // steps
        assert rb * steps == rows and rb % 16 == 0
        in_specs.append(pl.BlockSpec((rb, width), lambda g: (g, 0)))
        args.append(arr)
        for cs in col_slices or (slice(0, width),):
            cols = cs.stop - cs.start
            out_specs.append(pl.BlockSpec((rb, cols), lambda g: (g, 0)))
            out_shape.append(jax.ShapeDtypeStruct((rows, cols), BF16))
    return in_specs, out_specs, out_shape, args


def _n_cast_outputs(col_slices_per_src):
    return sum(1 if s is None else len(s) for s in col_slices_per_src)


def _do_casts(col_slices_per_src, srcs, dsts):
    dsts = iter(dsts)
    for col_slices, src in zip(col_slices_per_src, srcs):
        if col_slices is None:
            next(dsts)[...] = src[...].astype(BF16)
        else:
            for cs in col_slices:
                next(dsts)[...] = src[:, cs].astype(BF16)


def _ada_kernel(c_ref, w_ref, b_ref, *rest, casts):
    nc = len(casts)
    o_ref = rest[nc]
    _do_casts(casts, rest[0:nc], rest[nc + 1:])
    c = c_ref[...]
    s = (c * jax.nn.sigmoid(c)).astype(BF16)
    o_ref[...] = _dot(s, w_ref[...].astype(BF16)) + b_ref[...]


def _ada(cond8, w, b, casts=()):
    n = w.shape[1]
    steps = n // ADA_COLS
    c_in, c_out, c_shape, c_args = _cast_specs(casts, steps)
    return pl.pallas_call(
        functools.partial(_ada_kernel, casts=tuple(cs for _, cs in casts)),
        grid=(steps,),
        in_specs=[pl.BlockSpec((8, D_MODEL), lambda j: (0, 0)),
                  pl.BlockSpec((D_MODEL, ADA_COLS), lambda j: (0, j)),
                  pl.BlockSpec((1, ADA_COLS), lambda j: (0, j))] + c_in,
        out_specs=[pl.BlockSpec((8, ADA_COLS), lambda j: (0, j))] + c_out,
        out_shape=[jax.ShapeDtypeStruct((8, n), F32)] + c_shape,
        compiler_params=_params(1),
        name="ada",
    )(cond8, w, b, *c_args)


@functools.lru_cache(maxsize=None)
def _dft_mats(L):
    n = 2 * L
    t = np.arange(L, dtype=np.int64)
    f = np.arange(L, dtype=np.int64)
    ang = 2.0 * np.pi * ((f[:, None] * t[None, :]) % n).astype(np.float64) / n
    cos = np.cos(ang)
    sin = np.sin(ang)
    nyq = np.where(t % 2 == 0, 1.0, -1.0)
    fwd = np.concatenate([cos, -sin], axis=0)
    fwd[L] = nyq
    wre = np.full((L,), 2.0 / n)
    wre[0] = 1.0 / n
    inv = np.concatenate([cos.T * wre[None, :], -sin.T * (2.0 / n)], axis=1)
    inv[:, L] = nyq / n
    sgn = np.broadcast_to(nyq[:, None], (L, HY_W))
    return (jnp.asarray(fwd, dtype=BF16), jnp.asarray(inv, dtype=BF16),
            np.asarray(sgn, dtype=np.float32))


@functools.lru_cache(maxsize=None)
def _filter_consts(L):
    t = np.linspace(0.0, 1.0, L)[:, None]
    ang = 2.0 * np.pi * np.arange(L, dtype=np.float64)[:, None] / L
    bands = np.linspace(1e-4, HY_BANDS - 1, HY_BANDS)[None]
    z = np.concatenate([t, np.cos(bands * ang), -np.sin(bands * ang)], axis=-1)
    z = np.pad(z, ((0, 0), (0, HY_EMB_PAD - HY_EMB)))
    max_decay = math.log(HY_TARGET) / HY_FAST_DECAY
    min_decay = math.log(HY_TARGET) / HY_SLOW_DECAY
    deltas = np.linspace(min_decay, max_decay, HY_W)
    tdel = t * np.abs(deltas)[None, :]
    return np.asarray(z, np.float32), np.asarray(tdel, np.float32)


def _filter_kernel(zt_ref, tdel_ref, sgn_ref, w1_ref, b1_ref, w2_ref, b2_ref, w3_ref, fr_ref,
                   fw_ref, oa_ref, ob_ref, od_ref, *, L, b):
    m = L // b
    hi = lax.Precision.HIGHEST
    tdims = (((0,), (0,)), ((), ()))
    fr = fr_ref[...]
    a1 = lax.dot_general(w1_ref[...], zt_ref[...], tdims, precision=hi,
                         preferred_element_type=F32)
    h1 = jnp.sin(fr * (a1 + b1_ref[...]))
    a2 = lax.dot_general(w2_ref[...], h1, tdims, precision=hi, preferred_element_type=F32)
    h2 = jnp.sin(fr * (a2 + b2_ref[...]))
    h2_hi = h2.astype(BF16)
    h2_lo = (h2 - h2_hi.astype(F32)).astype(BF16)
    w3 = w3_ref[...]
    w3_hi = w3.astype(BF16)
    w3_lo = (w3 - w3_hi.astype(F32)).astype(BF16)
    lhs = jnp.concatenate([h2_hi, h2_lo, h2_hi, jnp.zeros_like(h2_hi)], axis=0)
    rhs = jnp.concatenate([w3_hi, w3_hi, w3_lo, jnp.zeros_like(w3_hi)], axis=0)
    h = lax.dot_general(lhs, rhs, tdims, preferred_element_type=F32)
    win = jnp.exp(-tdel_ref[...])
    sg = sgn_ref[...]
    row0_l = lax.broadcasted_iota(jnp.int32, (L, HY_W), 0) == 0
    row0_b = lax.broadcasted_iota(jnp.int32, (b, HY_W), 0) == 0
    row0_8 = lax.broadcasted_iota(jnp.int32, (8, HY_W), 0) == 0
    for o in range(HY_ORDER):
        base = o * 2 * HY_W
        fwd = h[:, base:base + HY_W] * win
        bwd = jnp.where(row0_l, 0.0, h[:, base + HY_W:base + 2 * HY_W] * win)
        nrm = (jnp.sum(jnp.abs(fwd), axis=0, keepdims=True)
               + jnp.sum(jnp.abs(bwd), axis=0, keepdims=True))
        inv = 1.0 / nrm
        fn = fwd * inv
        bn = bwd * inv
        xr, xn, xi, wr, wn, wi = [], [], [], [], [], []
        for r in range(m):
            p = _dot(fw_ref[...], fn[r * b:(r + 1) * b].astype(BF16))
            q = _dot(fw_ref[...], bn[r * b:(r + 1) * b].astype(BF16))
            xr.append(p[0:b])
            xn.append(p[b:b + 1])
            xi.append(jnp.where(row0_b, 0.0, p[b:2 * b]))
            wr.append(q[0:b])
            wn.append(q[b:b + 1])
            wi.append(jnp.where(row0_b, 0.0, -q[b:2 * b]))

        def emit(d, ka, kn, kb):
            oa_ref[o, d + m - 1] = ka
            ob_ref[o, d + m - 1] = kb
            od_ref[o, d + m - 1] = jnp.where(row0_8, kn, ka[0:8])

        emit(0, xr[0] + wr[0], xn[0] + wn[0], xi[0] + wi[0])
        for d in range(1, m):
            f0 = fn[(d - 1) * b:(d - 1) * b + 1]
            b0 = bn[(d - 1) * b:(d - 1) * b + 1]
            emit(d, xr[d] + sg * (xr[d - 1] - f0), xn[d] + (xn[d - 1] - f0),
                 xi[d] + sg * xi[d - 1])
            emit(-d, wr[d] + sg * (wr[d - 1] - b0), wn[d] + (wn[d - 1] - b0),
                 wi[d] + sg * wi[d - 1])


def _filters(L, b, fw, sgn, w1, b1, w2, b2, w3, freq):
    z, tdel = _filter_consts(L)
    nd = 2 * (L // b) - 1
    args = (jnp.asarray(z.T), jnp.asarray(tdel), jnp.asarray(sgn), w1, b1, w2, b2, w3, freq, fw)
    return pl.pallas_call(
        functools.partial(_filter_kernel, L=L, b=b),
        out_shape=[jax.ShapeDtypeStruct((HY_ORDER, nd, b, HY_W), F32),
                   jax.ShapeDtypeStruct((HY_ORDER, nd, b, HY_W), F32),
                   jax.ShapeDtypeStruct((HY_ORDER, nd, 8, HY_W), F32)],
        compiler_params=pltpu.CompilerParams(vmem_limit_bytes=VMEM_LIMIT),
        name=f"filters{L}",
    )(*args)


def _ret_init(dec_ref, mask_scr, vec_scr, cd_scr, C):
    H, E = RET_HEADS, HEAD_DIM
    scale = float(E) ** -0.5

    @pl.when(pl.program_id(0) == 0)
    def _():
        lg = jnp.log(jax.nn.sigmoid(dec_ref[...]))
        cd_scr[...] = jnp.exp(float(C) * lg[:, 0:E])
        ii = lax.broadcasted_iota(jnp.int32, (C, C), 0)
        jj = lax.broadcasted_iota(jnp.int32, (C, C), 1)
        rel = (ii - jj).astype(F32)
        ri = lax.broadcasted_iota(jnp.int32, (C, E), 0).astype(F32)
        for h in range(H):
            lf = lg[h:h + 1, :]
            lb = lg[H + h:H + h + 1, :]
            mf = jnp.where(rel >= 0, jnp.exp(jnp.maximum(rel, 0.0) * lf), 0.0)
            mb = jnp.where(rel <= 0, jnp.exp(jnp.maximum(-rel, 0.0) * lb), 0.0)
            mask_scr[h] = scale * (mf + mb)
            lfe, lbe = lf[:, 0:E], lb[:, 0:E]
            vec_scr[h, 0] = jnp.exp((ri + 1.0) * lfe)
            vec_scr[h, 1] = jnp.exp((float(C) - ri) * lbe)
            vec_scr[h, 2] = scale * jnp.exp((float(C) - 1.0 - ri) * lfe)
            vec_scr[h, 3] = scale * jnp.exp(ri * lbe)


def _ret_core(hn, w_ref, s0f_ref, s0b_ref, wo_ref, y_ref, sf_ref, sb_ref, mask_scr, vec_scr,
              cd_scr, g_scr, *, L, C, nb, has_init, emit_state):
    n = L // C
    H, E = RET_HEADS, HEAD_DIM
    tdims = (((0,), (0,)), ((), ()))
    ndims = (((1,), (1,)), ((), ()))
    chains = [(s, h) for s in range(nb) for h in range(H)]
    rows = [slice(c * C, (c + 1) * C) for c in range(n)]
    qkvg = [_dot(hn[s], w_ref[...]) for s in range(nb)]

    def cols(s, part, h):
        return qkvg[s][:, part * RET_W + h * E:part * RET_W + (h + 1) * E]

    qb = [cols(s, 0, h).astype(BF16) for s, h in chains]
    kf = [cols(s, 1, h) for s, h in chains]
    kb = [k.astype(BF16) for k in kf]
    vb = [cols(s, 2, h).astype(BF16) for s, h in chains]
    att = [[lax.dot_general(qb[i][r], kb[i][r], ndims, preferred_element_type=F32) for r in rows]
           for i in range(len(chains))]
    prob = [[(att[i][c] * mask_scr[h]).astype(BF16) for c in range(n)]
            for i, (s, h) in enumerate(chains)]
    out = [[_dot(prob[i][c], vb[i][rows[c]]) for c in range(n)] for i in range(len(chains))]
    kv = []
    for i, (s, h) in enumerate(chains):
        dk2 = jnp.concatenate([vec_scr[h, 2], vec_scr[h, 3]], axis=1)
        per_c = []
        for r in rows:
            k2 = (jnp.concatenate([kf[i][r], kf[i][r]], axis=1) * dk2).astype(BF16)
            per_c.append(lax.dot_general(k2, vb[i][r], tdims, preferred_element_type=F32))
        kv.append(per_c)
    for i, (s, h) in enumerate(chains):
        cdf = cd_scr[h:h + 1, :]
        cdb = cd_scr[H + h:H + h + 1, :]
        sf_in, sb_in = [None] * n, [None] * n
        st = s0f_ref[s, h] if has_init else None
        for c in range(n):
            sf_in[c] = st
            kvc = kv[i][c][0:E]
            st = kvc if st is None else st * cdf + kvc
        if emit_state:
            sf_ref[s, h] = st
        st = s0b_ref[s, h] if has_init else None
        for c in range(n - 1, -1, -1):
            sb_in[c] = st
            kvc = kv[i][c][E:2 * E]
            st = kvc if st is None else st * cdb + kvc
        if emit_state:
            sb_ref[s, h] = st
        for c in range(n):
            if sf_in[c] is not None and sb_in[c] is not None:
                s2 = jnp.concatenate([sf_in[c], sb_in[c]], axis=1).astype(BF16)
                inter = _dot(qb[i][rows[c]], s2)
                out[i][c] = (out[i][c] + inter[:, 0:E] * vec_scr[h, 0]
                             + inter[:, E:2 * E] * vec_scr[h, 1])
            elif sf_in[c] is not None:
                out[i][c] = (out[i][c]
                             + _dot(qb[i][rows[c]], sf_in[c].astype(BF16)) * vec_scr[h, 0])
            elif sb_in[c] is not None:
                out[i][c] = (out[i][c]
                             + _dot(qb[i][rows[c]], sb_in[c].astype(BF16)) * vec_scr[h, 1])
    for i, (s, h) in enumerate(chains):
        for c in range(n):
            o = out[i][c]
            mu = jnp.mean(o, axis=-1, keepdims=True)
            d = o - mu
            var = jnp.mean(d * d, axis=-1, keepdims=True)
            on = d * lax.rsqrt(var + EPS)
            gg = cols(s, 3, h)[rows[c]]
            g_scr[s, rows[c], h * E:(h + 1) * E] = (gg * jax.nn.sigmoid(gg) * on).astype(BF16)
    for s in range(nb):
        y_ref[s] = _dot(g_scr[s], wo_ref[...])


def _hy_core(hn, w_ref, cw_ref, cb_ref, fw_ref, bw_ref, fa_ref, fb_ref, fd_ref, hb_ref, wo_ref,
             y_ref, *, L, W, b, nb):
    m = L // b
    CB = HY_CBLK
    nblk = HY_W // CB
    pos = lax.broadcasted_iota(jnp.int32, (L, CB), 0) % W
    first = pos == 0
    last = pos == W - 1
    chains = [(s, blk) for s in range(nb) for blk in range(nblk)]

    def short_conv(s, base, blk):
        cs = slice(base + blk * CB, base + (blk + 1) * CB)
        ug = _dot(hn[s], w_ref[:, cs])
        prev = jnp.where(first, 0.0, pltpu.roll(ug, 1, axis=0))
        nxt = jnp.where(last, 0.0, pltpu.roll(ug, L - 1, axis=0))
        u = (prev * cw_ref[0:1, cs] + ug * cw_ref[1:2, cs] + nxt * cw_ref[2:3, cs]
             + cb_ref[:, cs])
        return [u[j * b:(j + 1) * b] for j in range(m)]

    def long_conv(sigs, o):
        spec = [[_dot(fw_ref[...], sj.astype(BF16)) for sj in sig] for sig in sigs]
        prods = []
        for (s, blk), sp in zip(chains, spec):
            cs = slice(blk * CB, (blk + 1) * CB)
            per_i = []
            for i in range(m):
                yre = yim = yim8 = None
                for j in range(m):
                    d = i - j + m - 1
                    sre, sim = sp[j][0:b], sp[j][b:2 * b]
                    ka, kb = fa_ref[o, d, :, cs], fb_ref[o, d, :, cs]
                    tre = sre * ka - sim * kb
                    tim = sre * kb + sim * ka
                    t8 = sre[0:8] * kb[0:8] + sim[0:8] * fd_ref[o, d, :, cs]
                    yre = tre if yre is None else yre + tre
                    yim = tim if yim is None else yim + tim
                    yim8 = t8 if yim8 is None else yim8 + t8
                yim = jnp.concatenate([yim8, yim[8:]], axis=0)
                per_i.append((yre.astype(BF16), yim.astype(BF16)))
            prods.append(per_i)
        return [[_dot(bw_ref[:, 0:b], yre) + _dot(bw_ref[:, b:2 * b], yim) for yre, yim in per_i]
                for per_i in prods]

    hv = [short_conv(s, 0, blk) for s, blk in chains]
    hx1 = [short_conv(s, HY_W, blk) for s, blk in chains]
    hx2 = [short_conv(s, 2 * HY_W, blk) for s, blk in chains]

    def gate(hx, conv, sig, o):
        out = []
        for (s, blk), hxc, cc, sc in zip(chains, hx, conv, sig):
            bias = hb_ref[o:o + 1, blk * CB:(blk + 1) * CB]
            out.append([hxc[i] * (cc[i] + sc[i] * bias) for i in range(m)])
        return out

    z = gate(hx1, long_conv(hv, 0), hv, 0)
    z = gate(hx2, long_conv(z, 1), z, 1)
    for s in range(nb):
        for i in range(m):
            acc = None
            for blk in range(nblk):
                zc = z[chains.index((s, blk))][i].astype(BF16)
                part = _dot(zc, wo_ref[blk * CB:(blk + 1) * CB, :])
                acc = part if acc is None else acc + part
            y_ref[s, i * b:(i + 1) * b, :] = acc


def _mix_kernel(*refs, L, C, W, b, nb, do_ret, do_hy, has_init, emit_state, casts):
    it = iter(refs)
    x_ref, mod_ref, n1_ref = next(it), next(it), next(it)
    s0f_ref = s0b_ref = sf_ref = sb_ref = None
    if do_ret:
        wq_ref, dec_ref = next(it), next(it)
        if has_init:
            s0f_ref, s0b_ref = next(it), next(it)
        wo_ret_ref = next(it)
    if do_hy:
        hy_in = [next(it) for _ in range(10)]
    cast_srcs = [next(it) for _ in casts]
    if do_ret:
        y_ret_ref = next(it)
        if emit_state:
            sf_ref, sb_ref = next(it), next(it)
    if do_hy:
        y_hy_ref = next(it)
    cast_dsts = [next(it) for _ in range(_n_cast_outputs(casts))]
    if do_ret:
        ret_scr = [next(it) for _ in range(4)]
        _ret_init(dec_ref, ret_scr[0], ret_scr[1], ret_scr[2], C)
    _do_casts(casts, cast_srcs, cast_dsts)
    mod = mod_ref[0]
    hn = [_modnorm(x_ref[s], n1_ref[...], mod[1:2], mod[0:1]).astype(BF16) for s in range(nb)]
    if do_hy:
        _hy_core(hn, *hy_in, y_hy_ref, L=L, W=W, b=b, nb=nb)
    if do_ret:
        _ret_core(hn, wq_ref, s0f_ref, s0b_ref, wo_ret_ref, y_ret_ref, sf_ref, sb_ref, *ret_scr,
                  L=L, C=C, nb=nb, has_init=has_init, emit_state=emit_state)


def _mixer(x, mods3, mod_row, norm1, *, nb, ret=None, hy=None, casts=()):
    B, L, D = x.shape
    H, E = RET_HEADS, HEAD_DIM
    C = min(RET_CHUNK, L)
    seq_spec = pl.BlockSpec((nb, L, D), lambda g: (g, 0, 0))
    in_specs = [seq_spec,
                pl.BlockSpec((1, N_MOD, D), lambda g: (mod_row(g * nb), 0, 0)),
                _const_spec((1, D))]
    args = [x, mods3, norm1]
    out_specs, out_shape, scratch = [], [], []
    has_init = emit_state = False
    W = b = None
    if ret is not None:
        w_qkvg, dec8, s0f, s0b, w_o, emit_state = ret
        has_init = s0f is not None
        st_spec = pl.BlockSpec((nb, H, E, E), lambda g: (g, 0, 0, 0))
        in_specs += [_const_spec((D, N_QKVG)), _const_spec((8, C))]
        args += [w_qkvg, dec8[:, :C]]
        if has_init:
            in_specs += [st_spec, st_spec]
            args += [s0f, s0b]
        in_specs.append(_const_spec((RET_W, D)))
        args.append(w_o)
        out_specs.append(seq_spec)
        out_shape.append(jax.ShapeDtypeStruct((B, L, D), F32))
        if emit_state:
            out_specs += [st_spec, st_spec]
            out_shape += [jax.ShapeDtypeStruct((B, H, E, E), F32)] * 2
        scratch = [pltpu.VMEM((H, C, C), F32), pltpu.VMEM((H, 4, C, E), F32),
                   pltpu.VMEM((8, E), F32), pltpu.VMEM((nb, L, RET_W), BF16)]
    if hy is not None:
        w_hy, conv_w, conv_b, fw, bw, (fa, fb, fd), hy_bias, w_o, W, b = hy
        nd = fa.shape[1]
        in_specs += [_const_spec((D, N_HY)), _const_spec((3, N_HY)), _const_spec((1, N_HY)),
                     _const_spec((2 * b, b)), _const_spec((b, 2 * b)),
                     _const_spec((HY_ORDER, nd, b, HY_W)), _const_spec((HY_ORDER, nd, b, HY_W)),
                     _const_spec((HY_ORDER, nd, 8, HY_W)), _const_spec((HY_ORDER, HY_W)),
                     _const_spec((HY_W, D))]
        args += [w_hy, conv_w, conv_b, fw, bw, fa, fb, fd, hy_bias, w_o]
        out_specs.append(seq_spec)
        out_shape.append(jax.ShapeDtypeStruct((B, L, D), F32))
    c_in, c_out, c_shape, c_args = _cast_specs(casts, B // nb)
    name = ("ret" if ret is not None else "") + ("hy" if hy is not None else "")
    return pl.pallas_call(
        functools.partial(_mix_kernel, L=L, C=C, W=W, b=b, nb=nb, do_ret=ret is not None,
                          do_hy=hy is not None, has_init=has_init, emit_state=emit_state,
                          casts=tuple(cs for _, cs in casts)),
        grid=(B // nb,),
        in_specs=in_specs + c_in,
        out_specs=out_specs + c_out,
        out_shape=out_shape + c_shape,
        scratch_shapes=scratch,
        compiler_params=_params(1),
        name=f"{name}{L}",
    )(*args, *c_args)


def _mlp_kernel(x_ref, yr_ref, yh_ref, modp_ref, modq_ref, n1_ref, n2_ref, fg_ref, wg_ref,
                wout_ref, wfi_hbm, wfo_hbm, y_ref, x1_scr, h2_scr, wfi_ref, wfo_ref, sem, *,
                n_tiles):
    i = pl.program_id(0)
    wr = i % 2
    rd = 1 - wr
    nq = N_GATE // 4

    def pre_stages():
        mp = modp_ref[0]
        st = {}

        def p1():
            st["x"] = x_ref[...]
            st["hn"] = _modnorm(st["x"], n1_ref[...], mp[1:2], mp[0:1]).astype(BF16)

        def p2(q):
            def f():
                st["g%d" % q] = _dot(st["hn"], wg_ref[:, q * nq:(q + 1) * nq])
            return f

        def p3(h):
            def f():
                cs = slice(h * nq, (h + 1) * nq)
                st["mix%d" % h] = (jax.nn.sigmoid(st["g%d" % h]) * yr_ref[:, cs]
                                   + jax.nn.sigmoid(st["g%d" % (2 + h)]) * yh_ref[:, cs]
                                   ).astype(BF16)
            return f

        def p4():
            upd = (_dot(st["mix0"], wout_ref[0:nq, :]) + _dot(st["mix1"], wout_ref[nq:2 * nq, :]))
            st["x1"] = st["x"] + mp[2:3] * upd

        def p5():
            x1_scr[wr] = st["x1"]
            h2_scr[wr] = _modnorm(st["x1"], n2_ref[...], mp[4:5], mp[3:4]).astype(BF16)

        return [p1, p2(0), p2(1), p2(2), p2(3), p3(0), p3(1), p4, p5]

    def ffn_stages():
        mq = modq_ref[0]
        st = {"acc": None}

        def f(j):
            def g():
                cs = slice(j * FF_CHUNK, (j + 1) * FF_CHUNK)
                h2 = h2_scr[rd]
                a = _dot(h2, wfi_ref[:, cs])
                b = _dot(h2, wfi_ref[:, D_FF + j * FF_CHUNK:D_FF + (j + 1) * FF_CHUNK])
                ff = (a * jax.nn.sigmoid(a) * b).astype(BF16)
                part = _dot(ff, wfo_ref[cs, :])
                st["acc"] = part if st["acc"] is None else st["acc"] + part
            return g

        def e():
            x2 = x1_scr[rd] + mq[5:6] * st["acc"]
            ms = jnp.mean(x2 * x2, axis=-1, keepdims=True)
            y_ref[...] = x2 * lax.rsqrt(ms + EPS) * fg_ref[...]

        return [f(j) for j in range(D_FF // FF_CHUNK)] + [e]

    @pl.when(i == 0)
    def _():
        copies = [pltpu.make_async_copy(wfi_hbm, wfi_ref, sem.at[0]),
                  pltpu.make_async_copy(wfo_hbm, wfo_ref, sem.at[1])]
        for cp in copies:
            cp.start()
        for stage in pre_stages():
            stage()
        for cp in copies:
            cp.wait()

    @pl.when(jnp.logical_and(i > 0, i < n_tiles))
    def _():
        pre, ffn = pre_stages(), ffn_stages()
        order = []
        while pre or ffn:
            if ffn:
                order.append(ffn.pop(0))
            if pre:
                order.append(pre.pop(0))
        for stage in order:
            stage()

    @pl.when(i == n_tiles)
    def _():
        for stage in ffn_stages():
            stage()


def _mlp(x, y_ret, y_hy, mods3, mod_row, norm1, norm2, final_g, w_gate, w_out, w_fi, w_fo):
    B, L, D = x.shape
    T = MLP_ROWS
    n_tiles = B * L // T
    flat = lambda a: a.reshape(B * L, D)
    pre_tile = lambda i: jnp.minimum(i, n_tiles - 1)
    post_tile = lambda i: jnp.maximum(i - 1, 0)
    act = pl.BlockSpec((T, D), lambda i: (pre_tile(i), 0))
    y = pl.pallas_call(
        functools.partial(_mlp_kernel, n_tiles=n_tiles),
        grid=(n_tiles + 1,),
        in_specs=[act, act, act,
                  pl.BlockSpec((1, N_MOD, D), lambda i: (mod_row((pre_tile(i) * T) // L), 0, 0)),
                  pl.BlockSpec((1, N_MOD, D), lambda i: (mod_row((post_tile(i) * T) // L), 0, 0)),
                  _const_spec((1, D)), _const_spec((1, D)), _const_spec((1, D)),
                  _const_spec((D, N_GATE)),
                  _const_spec((D, D)),
                  pl.BlockSpec(memory_space=pl.ANY),
                  pl.BlockSpec(memory_space=pl.ANY)],
        out_specs=pl.BlockSpec((T, D), lambda i: (post_tile(i), 0)),
        out_shape=jax.ShapeDtypeStruct((B * L, D), F32),
        scratch_shapes=[pltpu.VMEM((2, T, D), F32), pltpu.VMEM((2, T, D), BF16),
                        pltpu.VMEM((D, 2 * D_FF), BF16), pltpu.VMEM((D_FF, D), BF16),
                        pltpu.SemaphoreType.DMA((2,))],
        compiler_params=_params(1),
        name=f"mlp{L}",
    )(flat(x), flat(y_ret), flat(y_hy), mods3, mods3, norm1, norm2, final_g, w_gate, w_out, w_fi,
      w_fo)
    return y.reshape(B, L, D)


def kernel(x_prompt, x_sample, state_ret_fwd, state_ret_bwd, c, c_ctx, norm1_g, norm2_g, w_ada,
           b_ada, w_in, ret_decay_fwd, ret_decay_bwd, hy_conv_w, hy_conv_b, hy_pos_w1, hy_pos_b1,
           hy_pos_w2, hy_pos_b2, hy_pos_w3, hy_sin_freq, hy_bias, w_ret_o, w_hy_o, w_out,
           w_ffn_in, w_ffn_out, final_g):
    assert w_in.shape[0] == 1, "single-layer configuration"
    nb_lat = x_sample.shape[0]
    l_ctx = x_prompt.shape[1]

    cond8 = jnp.concatenate([c_ctx[None, :], c, jnp.zeros((8 - 1 - nb_lat, D_MODEL), F32)])
    n_hy_end = N_QKVG + N_HY
    w_in_parts = (slice(0, N_QKVG), slice(N_QKVG, n_hy_end), slice(n_hy_end, N_IN))
    mods, w_qkvg, w_hy, w_gate, w_ret_o_b, w_hy_o_b = _ada(
        cond8, w_ada[0], b_ada,
        casts=[(w_in[0], w_in_parts), (w_ret_o[0], None), (w_hy_o[0], None)])
    mods3 = mods.reshape(8, N_MOD, D_MODEL)
    norm1 = norm1_g[0][None, :]
    norm2 = norm2_g[0][None, :]
    fg = final_g[None, :]
    dec8 = jnp.broadcast_to(jnp.concatenate([ret_decay_fwd[0], ret_decay_bwd[0]])[:, None],
                            (8, RET_CHUNK))
    w1 = jnp.pad(hy_pos_w1[0], ((0, HY_EMB_PAD - HY_EMB), (0, 0)))
    b1, b2 = hy_pos_b1[0][:, None], hy_pos_b2[0][:, None]
    freq = hy_sin_freq[0][:, None]
    conv_b = hy_conv_b[0][None, :]

    def branches(x, s0f, s0b, grid_w, emit_state):
        L = x.shape[1]
        blk = min(HY_TBLK, L)
        fw, bw, sgn = _dft_mats(blk)
        filt = _filters(L, blk, fw, sgn, w1, b1, hy_pos_w2[0], b2, hy_pos_w3[0], freq)
        ret = (w_qkvg, dec8, s0f, s0b, w_ret_o_b, emit_state)
        hy = (w_hy, hy_conv_w[0], conv_b, fw, bw, filt, hy_bias[0], w_hy_o_b, grid_w, blk)
        return ret, hy

    ctx_row = lambda b: 0
    lat_row = lambda b: b + 1
    ret, hy = branches(x_prompt, None, None, l_ctx, True)
    y_ret_c, s_f, s_b, y_hy_c, w_fi_b, w_fo_b, w_out_b = _mixer(
        x_prompt, mods3, ctx_row, norm1, nb=CTX_SEQS, ret=ret, hy=hy,
        casts=[(w_ffn_in[0], None), (w_ffn_out[0], None), (w_out[0], None)])
    y_prompt = _mlp(x_prompt, y_ret_c, y_hy_c, mods3, ctx_row, norm1, norm2, fg, w_gate, w_out_b,
                    w_fi_b, w_fo_b)
    ret, hy = branches(x_sample, state_ret_fwd[:, 0], state_ret_bwd[:, 0], GRID_W, False)
    y_ret_l, = _mixer(x_sample, mods3, lat_row, norm1, nb=1, ret=ret)
    y_hy_l, = _mixer(x_sample, mods3, lat_row, norm1, nb=1, hy=hy)
    y_sample = _mlp(x_sample, y_ret_l, y_hy_l, mods3, lat_row, norm1, norm2, fg, w_gate,
                    w_out_b, w_fi_b, w_fo_b)
    return (y_prompt, y_sample, s_f[:, None], s_b[:, None])
```

```python
import functools
import math

import numpy as np
import jax
import jax.numpy as jnp
from jax import lax
from jax.experimental import pallas as pl
from jax.experimental.pallas import tpu as pltpu

F32 = jnp.float32
BF16 = jnp.bfloat16

D_MODEL = 1024
RET_HEADS = 4
HEAD_DIM = 128
RET_W = RET_HEADS * HEAD_DIM
HY_W = 512
HY_ORDER = 2
HY_BANDS = 16
HY_EMB = 1 + 2 * HY_BANDS
HY_EMB_PAD = 40
HY_HIDDEN = 64
HY_FAST_DECAY = 0.3
HY_SLOW_DECAY = 1.5
HY_TARGET = 1e-2
D_FF = 2816
N_QKVG = 4 * RET_W
N_HY = 3 * HY_W
N_GATE = 2 * D_MODEL
N_IN = N_QKVG + N_HY + N_GATE
N_MOD = 6
EPS = 1e-6
GRID_W = 64
RET_CHUNK = 256
HY_CBLK = 256
HY_TBLK = 512
CTX_SEQS = 2
MLP_ROWS = 512
FF_CHUNK = 256
FILTER_ONE_STEP_LEN = 256
ADA_COLS = 768
VMEM_LIMIT = 56 * 1024 * 1024


def _const_spec(shape):
    nd = len(shape)
    return pl.BlockSpec(shape, lambda *_: (0,) * nd, pipeline_mode=pl.Buffered(1))


def _params(n_axes):
    return pltpu.CompilerParams(dimension_semantics=("arbitrary",) * n_axes,
                                vmem_limit_bytes=VMEM_LIMIT)


def _modnorm(x, g, scale, shift):
    ms = jnp.mean(x * x, axis=-1, keepdims=True)
    return (x * lax.rsqrt(ms + EPS) * g) * (1.0 + scale) + shift


def _dot(a, b):
    return jnp.dot(a, b, preferred_element_type=F32)


def _cast_specs(casts, steps):
    in_specs, out_specs, out_shape, args = [], [], [], []
    for arr, col_slices in casts:
        rows, width = arr.shape
        rb = rows // steps
        assert rb * steps == rows and rb % 16 == 0
        in_specs.append(pl.BlockSpec((rb, width), lambda g: (g, 0)))
        args.append(arr)
        for cs in col_slices or (slice(0, width),):
            cols = cs.stop - cs.start
            out_specs.append(pl.BlockSpec((rb, cols), lambda g: (g, 0)))
            out_shape.append(jax.ShapeDtypeStruct((rows, cols), BF16))
    return in_specs, out_specs, out_shape, args


def _n_cast_outputs(col_slices_per_src):
    return sum(1 if s is None else len(s) for s in col_slices_per_src)


def _do_casts(col_slices_per_src, srcs, dsts):
    dsts = iter(dsts)
    for col_slices, src in zip(col_slices_per_src, srcs):
        if col_slices is None:
            next(dsts)[...] = src[...].astype(BF16)
        else:
            for cs in col_slices:
                next(dsts)[...] = src[:, cs].astype(BF16)


def _ada_kernel(*refs, casts, groups):
    it = iter(refs)
    c_ref, w_ref, b_ref = next(it), next(it), next(it)
    mlp_refs = [next(it) for _ in range(6)]
    g_in = [[next(it) for _ in range(4)] for _ in groups]
    cast_srcs = [next(it) for _ in casts]
    o_ref = next(it)
    g_out = [[next(it) for _ in range(3)] for _ in groups]
    cast_dsts = [next(it) for _ in range(_n_cast_outputs(casts))]
    h_scr = next(it)
    _do_casts(casts, cast_srcs, cast_dsts)
    c = c_ref[...]
    s = (c * jax.nn.sigmoid(c)).astype(BF16)
    o_ref[...] = _dot(s, w_ref[...].astype(BF16)) + b_ref[...]

    step = 0
    for (L, blk), (zt_ref, tdel_ref, sgn_ref, fw_ref), outs in zip(groups, g_in, g_out):
        def mlp_job(zt_ref=zt_ref, L=L):
            h_scr[0:L, :] = _filter_mlp(zt_ref, *mlp_refs)

        def order_job(o, tdel_ref=tdel_ref, sgn_ref=sgn_ref, fw_ref=fw_ref, outs=outs, L=L,
                      blk=blk):
            _filter_spectra(h_scr, o, tdel_ref, sgn_ref, fw_ref, *outs, L=L, b=blk)

        def all_job(mlp_job=mlp_job, order_job=order_job):
            mlp_job()
            for o in range(HY_ORDER):
                order_job(o)

        if L <= FILTER_ONE_STEP_LEN:
            jobs = [all_job]
        else:
            jobs = [mlp_job] + [functools.partial(order_job, o) for o in range(HY_ORDER)]
        for job in jobs:
            pl.when(pl.program_id(0) == step)(job)
            step += 1


def _ada(cond8, w, b, filt_params, groups, casts=()):
    n = w.shape[1]
    steps = n // ADA_COLS
    c_in, c_out, c_shape, c_args = _cast_specs(casts, steps)
    in_specs = [pl.BlockSpec((8, D_MODEL), lambda j: (0, 0)),
                pl.BlockSpec((D_MODEL, ADA_COLS), lambda j: (0, j)),
                pl.BlockSpec((1, ADA_COLS), lambda j: (0, j))]
    args = [cond8, w, b]
    for p in filt_params:
        in_specs.append(_const_spec(p.shape))
        args.append(p)
    out_specs = [pl.BlockSpec((8, ADA_COLS), lambda j: (0, j))]
    out_shape = [jax.ShapeDtypeStruct((8, n), F32)]
    max_len = 8
    for L, blk, *consts in groups:
        nd = 2 * (L // blk) - 1
        max_len = max(max_len, L)
        for cst in consts:
            in_specs.append(_const_spec(cst.shape))
            args.append(cst)
        for shp in ((HY_ORDER, nd, blk, HY_W), (HY_ORDER, nd, blk, HY_W), (HY_ORDER, nd, 8, HY_W)):
            out_specs.append(pl.BlockSpec(shp, lambda j: (0, 0, 0, 0)))
            out_shape.append(jax.ShapeDtypeStruct(shp, F32))
    n_jobs = sum(1 if L <= FILTER_ONE_STEP_LEN else 1 + HY_ORDER for L, *_ in groups)
    assert n_jobs <= steps
    return pl.pallas_call(
        functools.partial(_ada_kernel, casts=tuple(cs for _, cs in casts),
                          groups=tuple((L, blk) for L, blk, *_ in groups)),
        grid=(steps,),
        in_specs=in_specs + c_in,
        out_specs=out_specs + c_out,
        out_shape=out_shape + c_shape,
        scratch_shapes=[pltpu.VMEM((max_len, HY_ORDER * 2 * HY_W), F32)],
        compiler_params=_params(1),
        name="ada",
    )(*args, *c_args)


@functools.lru_cache(maxsize=None)
def _dft_mats(L):
    n = 2 * L
    t = np.arange(L, dtype=np.int64)
    f = np.arange(L, dtype=np.int64)
    ang = 2.0 * np.pi * ((f[:, None] * t[None, :]) % n).astype(np.float64) / n
    cos = np.cos(ang)
    sin = np.sin(ang)
    nyq = np.where(t % 2 == 0, 1.0, -1.0)
    fwd = np.concatenate([cos, -sin], axis=0)
    fwd[L] = nyq
    wre = np.full((L,), 2.0 / n)
    wre[0] = 1.0 / n
    inv = np.concatenate([cos.T * wre[None, :], -sin.T * (2.0 / n)], axis=1)
    inv[:, L] = nyq / n
    sgn = np.broadcast_to(nyq[:, None], (L, HY_W))
    return (jnp.asarray(fwd, dtype=BF16), jnp.asarray(inv, dtype=BF16),
            np.asarray(sgn, dtype=np.float32))


@functools.lru_cache(maxsize=None)
def _filter_consts(L):
    t = np.linspace(0.0, 1.0, L)[:, None]
    ang = 2.0 * np.pi * np.arange(L, dtype=np.float64)[:, None] / L
    bands = np.linspace(1e-4, HY_BANDS - 1, HY_BANDS)[None]
    z = np.concatenate([t, np.cos(bands * ang), -np.sin(bands * ang)], axis=-1)
    z = np.pad(z, ((0, 0), (0, HY_EMB_PAD - HY_EMB)))
    max_decay = math.log(HY_TARGET) / HY_FAST_DECAY
    min_decay = math.log(HY_TARGET) / HY_SLOW_DECAY
    deltas = np.linspace(min_decay, max_decay, HY_W)
    tdel = t * np.abs(deltas)[None, :]
    return np.asarray(z, np.float32), np.asarray(tdel, np.float32)


def _filter_mlp(zt_ref, w1_ref, b1_ref, w2_ref, b2_ref, w3_ref, fr_ref):
    hi = lax.Precision.HIGHEST
    tdims = (((0,), (0,)), ((), ()))
    fr = fr_ref[...]
    a1 = lax.dot_general(w1_ref[...], zt_ref[...], tdims, precision=hi,
                         preferred_element_type=F32)
    h1 = jnp.sin(fr * (a1 + b1_ref[...]))
    a2 = lax.dot_general(w2_ref[...], h1, tdims, precision=hi, preferred_element_type=F32)
    h2 = jnp.sin(fr * (a2 + b2_ref[...]))
    h2_hi = h2.astype(BF16)
    h2_lo = (h2 - h2_hi.astype(F32)).astype(BF16)
    w3 = w3_ref[...]
    w3_hi = w3.astype(BF16)
    w3_lo = (w3 - w3_hi.astype(F32)).astype(BF16)
    lhs = jnp.concatenate([h2_hi, h2_lo, h2_hi, jnp.zeros_like(h2_hi)], axis=0)
    rhs = jnp.concatenate([w3_hi, w3_hi, w3_lo, jnp.zeros_like(w3_hi)], axis=0)
    h = lax.dot_general(lhs, rhs, tdims, preferred_element_type=F32)
    return h


def _filter_spectra(h_ref, o, tdel_ref, sgn_ref, fw_ref, oa_ref, ob_ref, od_ref, *, L, b):
    m = L // b
    win = jnp.exp(-tdel_ref[...])
    sg = sgn_ref[...]
    row0_l = lax.broadcasted_iota(jnp.int32, (L, HY_W), 0) == 0
    row0_b = lax.broadcasted_iota(jnp.int32, (b, HY_W), 0) == 0
    row0_8 = lax.broadcasted_iota(jnp.int32, (8, HY_W), 0) == 0
    base = o * 2 * HY_W
    fwd = h_ref[0:L, base:base + HY_W] * win
    bwd = jnp.where(row0_l, 0.0, h_ref[0:L, base + HY_W:base + 2 * HY_W] * win)
    nrm = (jnp.sum(jnp.abs(fwd), axis=0, keepdims=True)
           + jnp.sum(jnp.abs(bwd), axis=0, keepdims=True))
    inv = 1.0 / nrm
    fn = fwd * inv
    bn = bwd * inv
    xr, xn, xi, wr, wn, wi = [], [], [], [], [], []
    for r in range(m):
        p = _dot(fw_ref[...], fn[r * b:(r + 1) * b].astype(BF16))
        q = _dot(fw_ref[...], bn[r * b:(r + 1) * b].astype(BF16))
        xr.append(p[0:b])
        xn.append(p[b:b + 1])
        xi.append(jnp.where(row0_b, 0.0, p[b:2 * b]))
        wr.append(q[0:b])
        wn.append(q[b:b + 1])
        wi.append(jnp.where(row0_b, 0.0, -q[b:2 * b]))

    def emit(d, ka, kn, kb):
        oa_ref[o, d + m - 1] = ka
        ob_ref[o, d + m - 1] = kb
        od_ref[o, d + m - 1] = jnp.where(row0_8, kn, ka[0:8])

    emit(0, xr[0] + wr[0], xn[0] + wn[0], xi[0] + wi[0])
    for d in range(1, m):
        f0 = fn[(d - 1) * b:(d - 1) * b + 1]
        b0 = bn[(d - 1) * b:(d - 1) * b + 1]
        emit(d, xr[d] + sg * (xr[d - 1] - f0), xn[d] + (xn[d - 1] - f0),
             xi[d] + sg * xi[d - 1])
        emit(-d, wr[d] + sg * (wr[d - 1] - b0), wn[d] + (wn[d - 1] - b0),
             wi[d] + sg * wi[d - 1])


def _ret_init(dec_ref, mask_scr, vec_scr, cd_scr, C):
    H, E = RET_HEADS, HEAD_DIM
    scale = float(E) ** -0.5

    @pl.when(pl.program_id(0) == 0)
    def _():
        lg = jnp.log(jax.nn.sigmoid(dec_ref[...]))
        cd_scr[...] = jnp.exp(float(C) * lg[:, 0:E])
        ii = lax.broadcasted_iota(jnp.int32, (C, C), 0)
        jj = lax.broadcasted_iota(jnp.int32, (C, C), 1)
        rel = (ii - jj).astype(F32)
        ri = lax.broadcasted_iota(jnp.int32, (C, E), 0).astype(F32)
        for h in range(H):
            lf = lg[h:h + 1, :]
            lb = lg[H + h:H + h + 1, :]
            mf = jnp.where(rel >= 0, jnp.exp(jnp.maximum(rel, 0.0) * lf), 0.0)
            mb = jnp.where(rel <= 0, jnp.exp(jnp.maximum(-rel, 0.0) * lb), 0.0)
            mask_scr[h] = scale * (mf + mb)
            lfe, lbe = lf[:, 0:E], lb[:, 0:E]
            vec_scr[h, 0] = jnp.exp((ri + 1.0) * lfe)
            vec_scr[h, 1] = jnp.exp((float(C) - ri) * lbe)
            vec_scr[h, 2] = scale * jnp.exp((float(C) - 1.0 - ri) * lfe)
            vec_scr[h, 3] = scale * jnp.exp(ri * lbe)


def _ret_core(hn, w_ref, s0f_ref, s0b_ref, wo_ref, y_ref, sf_ref, sb_ref, mask_scr, vec_scr,
              cd_scr, g_scr, *, L, C, nb, has_init, emit_state):
    n = L // C
    H, E = RET_HEADS, HEAD_DIM
    tdims = (((0,), (0,)), ((), ()))
    ndims = (((1,), (1,)), ((), ()))
    chains = [(s, h) for s in range(nb) for h in range(H)]
    rows = [slice(c * C, (c + 1) * C) for c in range(n)]
    qkvg = [_dot(hn[s], w_ref[...]) for s in range(nb)]

    def cols(s, part, h):
        return qkvg[s][:, part * RET_W + h * E:part * RET_W + (h + 1) * E]

    qb = [cols(s, 0, h).astype(BF16) for s, h in chains]
    kf = [cols(s, 1, h) for s, h in chains]
    kb = [k.astype(BF16) for k in kf]
    vb = [cols(s, 2, h).astype(BF16) for s, h in chains]
    att = [[lax.dot_general(qb[i][r], kb[i][r], ndims, preferred_element_type=F32) for r in rows]
           for i in range(len(chains))]
    prob = [[(att[i][c] * mask_scr[h]).astype(BF16) for c in range(n)]
            for i, (s, h) in enumerate(chains)]
    out = [[_dot(prob[i][c], vb[i][rows[c]]) for c in range(n)] for i in range(len(chains))]
    kv = []
    for i, (s, h) in enumerate(chains):
        dk2 = jnp.concatenate([vec_scr[h, 2], vec_scr[h, 3]], axis=1)
        per_c = []
        for r in rows:
            k2 = (jnp.concatenate([kf[i][r], kf[i][r]], axis=1) * dk2).astype(BF16)
            per_c.append(lax.dot_general(k2, vb[i][r], tdims, preferred_element_type=F32))
        kv.append(per_c)
    for i, (s, h) in enumerate(chains):
        cdf = cd_scr[h:h + 1, :]
        cdb = cd_scr[H + h:H + h + 1, :]
        sf_in, sb_in = [None] * n, [None] * n
        st = s0f_ref[s, h] if has_init else None
        for c in range(n):
            sf_in[c] = st
            kvc = kv[i][c][0:E]
            st = kvc if st is None else st * cdf + kvc
        if emit_state:
            sf_ref[s, h] = st
        st = s0b_ref[s, h] if has_init else None
        for c in range(n - 1, -1, -1):
            sb_in[c] = st
            kvc = kv[i][c][E:2 * E]
            st = kvc if st is None else st * cdb + kvc
        if emit_state:
            sb_ref[s, h] = st
        for c in range(n):
            if sf_in[c] is not None and sb_in[c] is not None:
                s2 = jnp.concatenate([sf_in[c], sb_in[c]], axis=1).astype(BF16)
                inter = _dot(qb[i][rows[c]], s2)
                out[i][c] = (out[i][c] + inter[:, 0:E] * vec_scr[h, 0]
                             + inter[:, E:2 * E] * vec_scr[h, 1])
            elif sf_in[c] is not None:
                out[i][c] = (out[i][c]
                             + _dot(qb[i][rows[c]], sf_in[c].astype(BF16)) * vec_scr[h, 0])
            elif sb_in[c] is not None:
                out[i][c] = (out[i][c]
                             + _dot(qb[i][rows[c]], sb_in[c].astype(BF16)) * vec_scr[h, 1])
    for i, (s, h) in enumerate(chains):
        for c in range(n):
            o = out[i][c]
            mu = jnp.mean(o, axis=-1, keepdims=True)
            d = o - mu
            var = jnp.mean(d * d, axis=-1, keepdims=True)
            on = d * lax.rsqrt(var + EPS)
            gg = cols(s, 3, h)[rows[c]]
            g_scr[s, rows[c], h * E:(h + 1) * E] = (gg * jax.nn.sigmoid(gg) * on).astype(BF16)
    for s in range(nb):
        y_ref[s] = _dot(g_scr[s], wo_ref[...])


def _hy_core(hn, w_ref, cw_ref, cb_ref, fw_ref, bw_ref, fa_ref, fb_ref, fd_ref, hb_ref, wo_ref,
             y_ref, *, L, W, b, nb):
    m = L // b
    CB = HY_CBLK
    nblk = HY_W // CB
    pos = lax.broadcasted_iota(jnp.int32, (L, CB), 0) % W
    first = pos == 0
    last = pos == W - 1
    chains = [(s, blk) for s in range(nb) for blk in range(nblk)]

    def short_conv(s, base, blk):
        cs = slice(base + blk * CB, base + (blk + 1) * CB)
        ug = _dot(hn[s], w_ref[:, cs])
        prev = jnp.where(first, 0.0, pltpu.roll(ug, 1, axis=0))
        nxt = jnp.where(last, 0.0, pltpu.roll(ug, L - 1, axis=0))
        u = (prev * cw_ref[0:1, cs] + ug * cw_ref[1:2, cs] + nxt * cw_ref[2:3, cs]
             + cb_ref[:, cs])
        return [u[j * b:(j + 1) * b] for j in range(m)]

    def long_conv(sigs, o):
        spec = [[_dot(fw_ref[...], sj.astype(BF16)) for sj in sig] for sig in sigs]
        prods = []
        for (s, blk), sp in zip(chains, spec):
            cs = slice(blk * CB, (blk + 1) * CB)
            per_i = []
            for i in range(m):
                yre = yim = yim8 = None
                for j in range(m):
                    d = i - j + m - 1
                    sre, sim = sp[j][0:b], sp[j][b:2 * b]
                    ka, kb = fa_ref[o, d, :, cs], fb_ref[o, d, :, cs]
                    tre = sre * ka - sim * kb
                    tim = sre * kb + sim * ka
                    t8 = sre[0:8] * kb[0:8] + sim[0:8] * fd_ref[o, d, :, cs]
                    yre = tre if yre is None else yre + tre
                    yim = tim if yim is None else yim + tim
                    yim8 = t8 if yim8 is None else yim8 + t8
                yim = jnp.concatenate([yim8, yim[8:]], axis=0)
                per_i.append((yre.astype(BF16), yim.astype(BF16)))
            prods.append(per_i)
        return [[_dot(bw_ref[:, 0:b], yre) + _dot(bw_ref[:, b:2 * b], yim) for yre, yim in per_i]
                for per_i in prods]

    hv = [short_conv(s, 0, blk) for s, blk in chains]
    hx1 = [short_conv(s, HY_W, blk) for s, blk in chains]
    hx2 = [short_conv(s, 2 * HY_W, blk) for s, blk in chains]

    def gate(hx, conv, sig, o):
        out = []
        for (s, blk), hxc, cc, sc in zip(chains, hx, conv, sig):
            bias = hb_ref[o:o + 1, blk * CB:(blk + 1) * CB]
            out.append([hxc[i] * (cc[i] + sc[i] * bias) for i in range(m)])
        return out

    z = gate(hx1, long_conv(hv, 0), hv, 0)
    z = gate(hx2, long_conv(z, 1), z, 1)
    for s in range(nb):
        for i in range(m):
            acc = None
            for blk in range(nblk):
                zc = z[chains.index((s, blk))][i].astype(BF16)
                part = _dot(zc, wo_ref[blk * CB:(blk + 1) * CB, :])
                acc = part if acc is None else acc + part
            y_ref[s, i * b:(i + 1) * b, :] = acc


def _mix_kernel(*refs, L, C, W, b, nb, do_ret, do_hy, has_init, emit_state, casts):
    it = iter(refs)
    x_ref, mod_ref, n1_ref = next(it), next(it), next(it)
    s0f_ref = s0b_ref = sf_ref = sb_ref = None
    if do_ret:
        wq_ref, dec_ref = next(it), next(it)
        if has_init:
            s0f_ref, s0b_ref = next(it), next(it)
        wo_ret_ref = next(it)
    if do_hy:
        hy_in = [next(it) for _ in range(10)]
    cast_srcs = [next(it) for _ in casts]
    if do_ret:
        y_ret_ref = next(it)
        if emit_state:
            sf_ref, sb_ref = next(it), next(it)
    if do_hy:
        y_hy_ref = next(it)
    cast_dsts = [next(it) for _ in range(_n_cast_outputs(casts))]
    if do_ret:
        ret_scr = [next(it) for _ in range(4)]
        _ret_init(dec_ref, ret_scr[0], ret_scr[1], ret_scr[2], C)
    _do_casts(casts, cast_srcs, cast_dsts)
    mod = mod_ref[0]
    hn = [_modnorm(x_ref[s], n1_ref[...], mod[1:2], mod[0:1]).astype(BF16) for s in range(nb)]
    if do_hy:
        _hy_core(hn, *hy_in, y_hy_ref, L=L, W=W, b=b, nb=nb)
    if do_ret:
        _ret_core(hn, wq_ref, s0f_ref, s0b_ref, wo_ret_ref, y_ret_ref, sf_ref, sb_ref, *ret_scr,
                  L=L, C=C, nb=nb, has_init=has_init, emit_state=emit_state)


def _mixer(x, mods3, mod_row, norm1, *, nb, ret=None, hy=None, casts=()):
    B, L, D = x.shape
    H, E = RET_HEADS, HEAD_DIM
    C = min(RET_CHUNK, L)
    seq_spec = pl.BlockSpec((nb, L, D), lambda g: (g, 0, 0))
    in_specs = [seq_spec,
                pl.BlockSpec((1, N_MOD, D), lambda g: (mod_row(g * nb), 0, 0)),
                _const_spec((1, D))]
    args = [x, mods3, norm1]
    out_specs, out_shape, scratch = [], [], []
    has_init = emit_state = False
    W = b = None
    if ret is not None:
        w_qkvg, dec8, s0f, s0b, w_o, emit_state = ret
        has_init = s0f is not None
        st_spec = pl.BlockSpec((nb, H, E, E), lambda g: (g, 0, 0, 0))
        in_specs += [_const_spec((D, N_QKVG)), _const_spec((8, C))]
        args += [w_qkvg, dec8[:, :C]]
        if has_init:
            in_specs += [st_spec, st_spec]
            args += [s0f, s0b]
        in_specs.append(_const_spec((RET_W, D)))
        args.append(w_o)
        out_specs.append(seq_spec)
        out_shape.append(jax.ShapeDtypeStruct((B, L, D), F32))
        if emit_state:
            out_specs += [st_spec, st_spec]
            out_shape += [jax.ShapeDtypeStruct((B, H, E, E), F32)] * 2
        scratch = [pltpu.VMEM((H, C, C), F32), pltpu.VMEM((H, 4, C, E), F32),
                   pltpu.VMEM((8, E), F32), pltpu.VMEM((nb, L, RET_W), BF16)]
    if hy is not None:
        w_hy, conv_w, conv_b, fw, bw, (fa, fb, fd), hy_bias, w_o, W, b = hy
        nd = fa.shape[1]
        in_specs += [_const_spec((D, N_HY)), _const_spec((3, N_HY)), _const_spec((1, N_HY)),
                     _const_spec((2 * b, b)), _const_spec((b, 2 * b)),
                     _const_spec((HY_ORDER, nd, b, HY_W)), _const_spec((HY_ORDER, nd, b, HY_W)),
                     _const_spec((HY_ORDER, nd, 8, HY_W)), _const_spec((HY_ORDER, HY_W)),
                     _const_spec((HY_W, D))]
        args += [w_hy, conv_w, conv_b, fw, bw, fa, fb, fd, hy_bias, w_o]
        out_specs.append(seq_spec)
        out_shape.append(jax.ShapeDtypeStruct((B, L, D), F32))
    c_in, c_out, c_shape, c_args = _cast_specs(casts, B // nb)
    name = ("ret" if ret is not None else "") + ("hy" if hy is not None else "")
    return pl.pallas_call(
        functools.partial(_mix_kernel, L=L, C=C, W=W, b=b, nb=nb, do_ret=ret is not None,
                          do_hy=hy is not None, has_init=has_init, emit_state=emit_state,
                          casts=tuple(cs for _, cs in casts)),
        grid=(B // nb,),
        in_specs=in_specs + c_in,
        out_specs=out_specs + c_out,
        out_shape=out_shape + c_shape,
        scratch_shapes=scratch,
        compiler_params=_params(1),
        name=f"{name}{L}",
    )(*args, *c_args)


def _mlp_kernel(x_ref, yr_ref, yh_ref, modp_ref, modq_ref, n1_ref, n2_ref, fg_ref, wg_ref,
                wout_ref, wfi_hbm, wfo_hbm, y_ref, x1_scr, h2_scr, wfi_ref, wfo_ref, sem, *,
                n_tiles):
    i = pl.program_id(0)
    wr = i % 2
    rd = 1 - wr
    nq = N_GATE // 4

    def pre_stages():
        mp = modp_ref[0]
        st = {}

        def p1():
            st["x"] = x_ref[...]
            st["hn"] = _modnorm(st["x"], n1_ref[...], mp[1:2], mp[0:1]).astype(BF16)

        def p2(q):
            def f():
                st["g%d" % q] = _dot(st["hn"], wg_ref[:, q * nq:(q + 1) * nq])
            return f

        def p3(h):
            def f():
                cs = slice(h * nq, (h + 1) * nq)
                st["mix%d" % h] = (jax.nn.sigmoid(st["g%d" % h]) * yr_ref[:, cs]
                                   + jax.nn.sigmoid(st["g%d" % (2 + h)]) * yh_ref[:, cs]
                                   ).astype(BF16)
            return f

        def p4():
            upd = (_dot(st["mix0"], wout_ref[0:nq, :]) + _dot(st["mix1"], wout_ref[nq:2 * nq, :]))
            st["x1"] = st["x"] + mp[2:3] * upd

        def p5():
            x1_scr[wr] = st["x1"]
            h2_scr[wr] = _modnorm(st["x1"], n2_ref[...], mp[4:5], mp[3:4]).astype(BF16)

        return [p1, p2(0), p2(1), p2(2), p2(3), p3(0), p3(1), p4, p5]

    def ffn_stages():
        mq = modq_ref[0]
        st = {"acc": None}

        def f(j):
            def g():
                cs = slice(j * FF_CHUNK, (j + 1) * FF_CHUNK)
                h2 = h2_scr[rd]
                a = _dot(h2, wfi_ref[:, cs])
                b = _dot(h2, wfi_ref[:, D_FF + j * FF_CHUNK:D_FF + (j + 1) * FF_CHUNK])
                ff = (a * jax.nn.sigmoid(a) * b).astype(BF16)
                part = _dot(ff, wfo_ref[cs, :])
                st["acc"] = part if st["acc"] is None else st["acc"] + part
            return g

        def e():
            x2 = x1_scr[rd] + mq[5:6] * st["acc"]
            ms = jnp.mean(x2 * x2, axis=-1, keepdims=True)
            y_ref[...] = x2 * lax.rsqrt(ms + EPS) * fg_ref[...]

        return [f(j) for j in range(D_FF // FF_CHUNK)] + [e]

    @pl.when(i == 0)
    def _():
        copies = [pltpu.make_async_copy(wfi_hbm, wfi_ref, sem.at[0]),
                  pltpu.make_async_copy(wfo_hbm, wfo_ref, sem.at[1])]
        for cp in copies:
            cp.start()
        for stage in pre_stages():
            stage()
        for cp in copies:
            cp.wait()

    @pl.when(jnp.logical_and(i > 0, i < n_tiles))
    def _():
        pre, ffn = pre_stages(), ffn_stages()
        order = []
        while pre or ffn:
            if ffn:
                order.append(ffn.pop(0))
            if pre:
                order.append(pre.pop(0))
        for stage in order:
            stage()

    @pl.when(i == n_tiles)
    def _():
        for stage in ffn_stages():
            stage()


def _mlp(x, y_ret, y_hy, mods3, mod_row, norm1, norm2, final_g, w_gate, w_out, w_fi, w_fo):
    B, L, D = x.shape
    T = MLP_ROWS
    n_tiles = B * L // T
    flat = lambda a: a.reshape(B * L, D)
    pre_tile = lambda i: jnp.minimum(i, n_tiles - 1)
    post_tile = lambda i: jnp.maximum(i - 1, 0)
    act = pl.BlockSpec((T, D), lambda i: (pre_tile(i), 0))
    y = pl.pallas_call(
        functools.partial(_mlp_kernel, n_tiles=n_tiles),
        grid=(n_tiles + 1,),
        in_specs=[act, act, act,
                  pl.BlockSpec((1, N_MOD, D), lambda i: (mod_row((pre_tile(i) * T) // L), 0, 0)),
                  pl.BlockSpec((1, N_MOD, D), lambda i: (mod_row((post_tile(i) * T) // L), 0, 0)),
                  _const_spec((1, D)), _const_spec((1, D)), _const_spec((1, D)),
                  _const_spec((D, N_GATE)),
                  _const_spec((D, D)),
                  pl.BlockSpec(memory_space=pl.ANY),
                  pl.BlockSpec(memory_space=pl.ANY)],
        out_specs=pl.BlockSpec((T, D), lambda i: (post_tile(i), 0)),
        out_shape=jax.ShapeDtypeStruct((B * L, D), F32),
        scratch_shapes=[pltpu.VMEM((2, T, D), F32), pltpu.VMEM((2, T, D), BF16),
                        pltpu.VMEM((D, 2 * D_FF), BF16), pltpu.VMEM((D_FF, D), BF16),
                        pltpu.SemaphoreType.DMA((2,))],
        compiler_params=_params(1),
        name=f"mlp{L}",
    )(flat(x), flat(y_ret), flat(y_hy), mods3, mods3, norm1, norm2, final_g, w_gate, w_out, w_fi,
      w_fo)
    return y.reshape(B, L, D)


def kernel(x_prompt, x_sample, state_ret_fwd, state_ret_bwd, c, c_ctx, norm1_g, norm2_g, w_ada,
           b_ada, w_in, ret_decay_fwd, ret_decay_bwd, hy_conv_w, hy_conv_b, hy_pos_w1, hy_pos_b1,
           hy_pos_w2, hy_pos_b2, hy_pos_w3, hy_sin_freq, hy_bias, w_ret_o, w_hy_o, w_out,
           w_ffn_in, w_ffn_out, final_g):
    assert w_in.shape[0] == 1, "single-layer configuration"
    nb_lat = x_sample.shape[0]
    l_ctx = x_prompt.shape[1]

    cond8 = jnp.concatenate([c_ctx[None, :], c, jnp.zeros((8 - 1 - nb_lat, D_MODEL), F32)])
    norm1 = norm1_g[0][None, :]
    norm2 = norm2_g[0][None, :]
    fg = final_g[None, :]
    dec8 = jnp.broadcast_to(jnp.concatenate([ret_decay_fwd[0], ret_decay_bwd[0]])[:, None],
                            (8, RET_CHUNK))
    w1 = jnp.pad(hy_pos_w1[0], ((0, HY_EMB_PAD - HY_EMB), (0, 0)))
    b1, b2 = hy_pos_b1[0][:, None], hy_pos_b2[0][:, None]
    freq = hy_sin_freq[0][:, None]
    conv_b = hy_conv_b[0][None, :]

    n_hy_end = N_QKVG + N_HY
    w_in_parts = (slice(0, N_QKVG), slice(N_QKVG, n_hy_end), slice(n_hy_end, N_IN))
    groups = []
    for L in (l_ctx, x_sample.shape[1]):
        blk = min(HY_TBLK, L)
        fw, bw, sgn = _dft_mats(blk)
        z, tdel = _filter_consts(L)
        groups.append((L, blk, jnp.asarray(z.T), jnp.asarray(tdel), jnp.asarray(sgn), fw))
    (mods, fa_c, fb_c, fd_c, fa_l, fb_l, fd_l, w_qkvg, w_hy, w_gate, w_ret_o_b,
     w_hy_o_b) = _ada(cond8, w_ada[0], b_ada, (w1, b1, hy_pos_w2[0], b2, hy_pos_w3[0], freq),
                      groups,
                      casts=[(w_in[0], w_in_parts), (w_ret_o[0], None), (w_hy_o[0], None)])
    mods3 = mods.reshape(8, N_MOD, D_MODEL)

    def branches(x, filt, s0f, s0b, grid_w, emit_state):
        blk = min(HY_TBLK, x.shape[1])
        fw, bw, _ = _dft_mats(blk)
        ret = (w_qkvg, dec8, s0f, s0b, w_ret_o_b, emit_state)
        hy = (w_hy, hy_conv_w[0], conv_b, fw, bw, filt, hy_bias[0], w_hy_o_b, grid_w, blk)
        return ret, hy

    ctx_row = lambda b: 0
    lat_row = lambda b: b + 1
    ret, hy = branches(x_prompt, (fa_c, fb_c, fd_c), None, None, l_ctx, True)
    y_ret_c, s_f, s_b, y_hy_c, w_fi_b, w_fo_b, w_out_b = _mixer(
        x_prompt, mods3, ctx_row, norm1, nb=CTX_SEQS, ret=ret, hy=hy,
        casts=[(w_ffn_in[0], None), (w_ffn_out[0], None), (w_out[0], None)])
    y_prompt = _mlp(x_prompt, y_ret_c, y_hy_c, mods3, ctx_row, norm1, norm2, fg, w_gate, w_out_b,
                    w_fi_b, w_fo_b)
    ret, hy = branches(x_sample, (fa_l, fb_l, fd_l), state_ret_fwd[:, 0], state_ret_bwd[:, 0],
                       GRID_W, False)
    y_ret_l, = _mixer(x_sample, mods3, lat_row, norm1, nb=1, ret=ret)
    y_hy_l, = _mixer(x_sample, mods3, lat_row, norm1, nb=1, hy=hy)
    y_sample = _mlp(x_sample, y_ret_l, y_hy_l, mods3, lat_row, norm1, norm2, fg, w_gate,
                    w_out_b, w_fi_b, w_fo_b)
    return (y_prompt, y_sample, s_f[:, None], s_b[:, None])
```

```python
import functools
import math

import numpy as np
import jax
import jax.numpy as jnp
from jax import lax
from jax.experimental import pallas as pl
from jax.experimental.pallas import tpu as pltpu

F32 = jnp.float32
BF16 = jnp.bfloat16

D_MODEL = 1024
RET_HEADS = 4
HEAD_DIM = 128
RET_W = RET_HEADS * HEAD_DIM
HY_W = 512
HY_ORDER = 2
HY_BANDS = 16
HY_EMB = 1 + 2 * HY_BANDS
HY_EMB_PAD = 40
HY_HIDDEN = 64
HY_FAST_DECAY = 0.3
HY_SLOW_DECAY = 1.5
HY_TARGET = 1e-2
D_FF = 2816
N_QKVG = 4 * RET_W
N_HY = 3 * HY_W
N_GATE = 2 * D_MODEL
N_IN = N_QKVG + N_HY + N_GATE
N_MOD = 6
EPS = 1e-6
GRID_W = 64
RET_CHUNK = 256
HY_CBLK = 256
HY_TBLK = 512
CTX_SEQS = 2
MLP_ROWS = 512
FF_CHUNK = 256
FILTER_ONE_STEP_LEN = 256
ADA_COLS = 768
VMEM_LIMIT = 56 * 1024 * 1024


def _const_spec(shape):
    nd = len(shape)
    return pl.BlockSpec(shape, lambda *_: (0,) * nd, pipeline_mode=pl.Buffered(1))


def _params(n_axes):
    return pltpu.CompilerParams(dimension_semantics=("arbitrary",) * n_axes,
                                vmem_limit_bytes=VMEM_LIMIT)


def _modnorm(x, g, scale, shift):
    ms = jnp.mean(x * x, axis=-1, keepdims=True)
    return (x * lax.rsqrt(ms + EPS) * g) * (1.0 + scale) + shift


def _dot(a, b):
    return jnp.dot(a, b, preferred_element_type=F32)


def _cast_specs(casts, steps):
    in_specs, out_specs, out_shape, args = [], [], [], []
    for arr, col_slices in casts:
        rows, width = arr.shape
        rb = rows // steps
        assert rb * steps == rows and rb % 16 == 0
        in_specs.append(pl.BlockSpec((rb, width), lambda g: (g, 0)))
        args.append(arr)
        for cs in col_slices or (slice(0, width),):
            cols = cs.stop - cs.start
            out_specs.append(pl.BlockSpec((rb, cols), lambda g: (g, 0)))
            out_shape.append(jax.ShapeDtypeStruct((rows, cols), BF16))
    return in_specs, out_specs, out_shape, args


def _n_cast_outputs(col_slices_per_src):
    return sum(1 if s is None else len(s) for s in col_slices_per_src)


def _do_casts(col_slices_per_src, srcs, dsts):
    dsts = iter(dsts)
    for col_slices, src in zip(col_slices_per_src, srcs):
        if col_slices is None:
            next(dsts)[...] = src[...].astype(BF16)
        else:
            for cs in col_slices:
                next(dsts)[...] = src[:, cs].astype(BF16)


def _ada_kernel(*refs, casts, groups):
    it = iter(refs)
    c_ref, w_ref, b_ref = next(it), next(it), next(it)
    mlp_refs = [next(it) for _ in range(4)]
    g_in = [[next(it) for _ in range(4)] for _ in groups]
    cast_srcs = [next(it) for _ in casts]
    o_ref = next(it)
    g_out = [[next(it) for _ in range(3)] for _ in groups]
    cast_dsts = [next(it) for _ in range(_n_cast_outputs(casts))]
    h_scr = next(it)
    _do_casts(casts, cast_srcs, cast_dsts)
    c = c_ref[...]
    s = (c * jax.nn.sigmoid(c)).astype(BF16)
    o_ref[...] = _dot(s, w_ref[...].astype(BF16)) + b_ref[...]

    step = 0
    for (L, blk), (zt_ref, tdel_ref, sgn_ref, fw_ref), outs in zip(groups, g_in, g_out):
        def mlp_job(zt_ref=zt_ref, L=L):
            h_scr[0:L, :] = _filter_mlp(zt_ref, *mlp_refs)

        def order_job(o, tdel_ref=tdel_ref, sgn_ref=sgn_ref, fw_ref=fw_ref, outs=outs, L=L,
                      blk=blk):
            _filter_spectra(h_scr, o, tdel_ref, sgn_ref, fw_ref, *outs, L=L, b=blk)

        def all_job(mlp_job=mlp_job, order_job=order_job):
            mlp_job()
            for o in range(HY_ORDER):
                order_job(o)

        if L <= FILTER_ONE_STEP_LEN:
            jobs = [all_job]
        else:
            jobs = [mlp_job] + [functools.partial(order_job, o) for o in range(HY_ORDER)]
        for job in jobs:
            pl.when(pl.program_id(0) == step)(job)
            step += 1


def _ada(cond8, w, b, filt_params, groups, casts=()):
    n = w.shape[1]
    steps = n // ADA_COLS
    c_in, c_out, c_shape, c_args = _cast_specs(casts, steps)
    in_specs = [pl.BlockSpec((8, D_MODEL), lambda j: (0, 0)),
                pl.BlockSpec((D_MODEL, ADA_COLS), lambda j: (0, j)),
                pl.BlockSpec((1, ADA_COLS), lambda j: (0, j))]
    args = [cond8, w, b]
    for p in filt_params:
        in_specs.append(_const_spec(p.shape))
        args.append(p)
    out_specs = [pl.BlockSpec((8, ADA_COLS), lambda j: (0, j))]
    out_shape = [jax.ShapeDtypeStruct((8, n), F32)]
    max_len = 8
    for L, blk, *consts in groups:
        nd = 2 * (L // blk) - 1
        max_len = max(max_len, L)
        for cst in consts:
            in_specs.append(_const_spec(cst.shape))
            args.append(cst)
        for shp in ((HY_ORDER, nd, blk, HY_W), (HY_ORDER, nd, blk, HY_W), (HY_ORDER, nd, 8, HY_W)):
            out_specs.append(pl.BlockSpec(shp, lambda j: (0, 0, 0, 0)))
            out_shape.append(jax.ShapeDtypeStruct(shp, F32))
    n_jobs = sum(1 if L <= FILTER_ONE_STEP_LEN else 1 + HY_ORDER for L, *_ in groups)
    assert n_jobs <= steps
    return pl.pallas_call(
        functools.partial(_ada_kernel, casts=tuple(cs for _, cs in casts),
                          groups=tuple((L, blk) for L, blk, *_ in groups)),
        grid=(steps,),
        in_specs=in_specs + c_in,
        out_specs=out_specs + c_out,
        out_shape=out_shape + c_shape,
        scratch_shapes=[pltpu.VMEM((max_len, HY_ORDER * 2 * HY_W), F32)],
        compiler_params=_params(1),
        name="ada",
    )(*args, *c_args)


@functools.lru_cache(maxsize=None)
def _dft_mats(L):
    n = 2 * L
    t = np.arange(L, dtype=np.int64)
    f = np.arange(L, dtype=np.int64)
    ang = 2.0 * np.pi * ((f[:, None] * t[None, :]) % n).astype(np.float64) / n
    cos = np.cos(ang)
    sin = np.sin(ang)
    nyq = np.where(t % 2 == 0, 1.0, -1.0)
    fwd = np.concatenate([cos, -sin], axis=0)
    fwd[L] = nyq
    wre = np.full((L,), 2.0 / n)
    wre[0] = 1.0 / n
    inv = np.concatenate([cos.T * wre[None, :], -sin.T * (2.0 / n)], axis=1)
    inv[:, L] = nyq / n
    sgn = np.broadcast_to(nyq[:, None], (L, HY_W))
    return (jnp.asarray(fwd, dtype=BF16), jnp.asarray(inv, dtype=BF16),
            np.asarray(sgn, dtype=np.float32))


@functools.lru_cache(maxsize=None)
def _filter_consts(L):
    t = np.linspace(0.0, 1.0, L)[:, None]
    ang = 2.0 * np.pi * np.arange(L, dtype=np.float64)[:, None] / L
    bands = np.linspace(1e-4, HY_BANDS - 1, HY_BANDS)[None]
    z = np.concatenate([t, np.cos(bands * ang), -np.sin(bands * ang)], axis=-1)
    z = np.pad(z, ((0, 0), (0, HY_EMB_PAD - HY_EMB)))
    z[:, HY_EMB] = 1.0
    max_decay = math.log(HY_TARGET) / HY_FAST_DECAY
    min_decay = math.log(HY_TARGET) / HY_SLOW_DECAY
    deltas = np.linspace(min_decay, max_decay, HY_W)
    tdel = t * np.abs(deltas)[None, :]
    return np.asarray(z, np.float32), np.asarray(tdel, np.float32)


def _filter_mlp(zt_ref, w1_ref, w2_ref, w3_ref, fr_ref):
    hi = lax.Precision.HIGHEST
    tdims = (((0,), (0,)), ((), ()))
    fr = fr_ref[...]
    zt = zt_ref[...]
    h1 = jnp.sin(lax.dot_general(w1_ref[...] * fr, zt, tdims, precision=hi,
                                 preferred_element_type=F32))
    h1 = jnp.concatenate([h1, jnp.ones((8, zt.shape[1]), F32)], axis=0)
    h2 = jnp.sin(lax.dot_general(w2_ref[...] * fr, h1, tdims, precision=hi,
                                 preferred_element_type=F32))
    h2_hi = h2.astype(BF16)
    h2_lo = (h2 - h2_hi.astype(F32)).astype(BF16)
    w3 = w3_ref[...]
    w3_hi = w3.astype(BF16)
    w3_lo = (w3 - w3_hi.astype(F32)).astype(BF16)
    lhs = jnp.concatenate([h2_hi, h2_lo, h2_hi, jnp.zeros_like(h2_hi)], axis=0)
    rhs = jnp.concatenate([w3_hi, w3_hi, w3_lo, jnp.zeros_like(w3_hi)], axis=0)
    h = lax.dot_general(lhs, rhs, tdims, preferred_element_type=F32)
    return h


def _filter_spectra(h_ref, o, tdel_ref, sgn_ref, fw_ref, oa_ref, ob_ref, od_ref, *, L, b):
    m = L // b
    win = jnp.exp(-tdel_ref[...])
    sg = sgn_ref[...]
    row0_l = lax.broadcasted_iota(jnp.int32, (L, HY_W), 0) == 0
    row0_b = lax.broadcasted_iota(jnp.int32, (b, HY_W), 0) == 0
    row0_8 = lax.broadcasted_iota(jnp.int32, (8, HY_W), 0) == 0
    base = o * 2 * HY_W
    fwd = h_ref[0:L, base:base + HY_W] * win
    bwd = jnp.where(row0_l, 0.0, h_ref[0:L, base + HY_W:base + 2 * HY_W] * win)
    nrm = (jnp.sum(jnp.abs(fwd), axis=0, keepdims=True)
           + jnp.sum(jnp.abs(bwd), axis=0, keepdims=True))
    inv = 1.0 / nrm
    fn = fwd * inv
    bn = bwd * inv
    xr, xn, xi, wr, wn, wi = [], [], [], [], [], []
    for r in range(m):
        p = _dot(fw_ref[...], fn[r * b:(r + 1) * b].astype(BF16))
        q = _dot(fw_ref[...], bn[r * b:(r + 1) * b].astype(BF16))
        xr.append(p[0:b])
        xn.append(p[b:b + 1])
        xi.append(jnp.where(row0_b, 0.0, p[b:2 * b]))
        wr.append(q[0:b])
        wn.append(q[b:b + 1])
        wi.append(jnp.where(row0_b, 0.0, -q[b:2 * b]))

    def emit(d, ka, kn, kb):
        oa_ref[o, d + m - 1] = ka
        ob_ref[o, d + m - 1] = kb
        od_ref[o, d + m - 1] = jnp.where(row0_8, kn, ka[0:8])

    emit(0, xr[0] + wr[0], xn[0] + wn[0], xi[0] + wi[0])
    for d in range(1, m):
        f0 = fn[(d - 1) * b:(d - 1) * b + 1]
        b0 = bn[(d - 1) * b:(d - 1) * b + 1]
        emit(d, xr[d] + sg * (xr[d - 1] - f0), xn[d] + (xn[d - 1] - f0),
             xi[d] + sg * xi[d - 1])
        emit(-d, wr[d] + sg * (wr[d - 1] - b0), wn[d] + (wn[d - 1] - b0),
             wi[d] + sg * wi[d - 1])


def _ret_init(dec_ref, mask_scr, vec_scr, cd_scr, C):
    H, E = RET_HEADS, HEAD_DIM
    scale = float(E) ** -0.5

    @pl.when(pl.program_id(0) == 0)
    def _():
        lg = jnp.log(jax.nn.sigmoid(dec_ref[...]))
        cd_scr[...] = jnp.exp(float(C) * lg[:, 0:E])
        ii = lax.broadcasted_iota(jnp.int32, (C, C), 0)
        jj = lax.broadcasted_iota(jnp.int32, (C, C), 1)
        rel = (ii - jj).astype(F32)
        ri = lax.broadcasted_iota(jnp.int32, (C, E), 0).astype(F32)
        for h in range(H):
            lf = lg[h:h + 1, :]
            lb = lg[H + h:H + h + 1, :]
            mf = jnp.where(rel >= 0, jnp.exp(jnp.maximum(rel, 0.0) * lf), 0.0)
            mb = jnp.where(rel <= 0, jnp.exp(jnp.maximum(-rel, 0.0) * lb), 0.0)
            mask_scr[h] = scale * (mf + mb)
            lfe, lbe = lf[:, 0:E], lb[:, 0:E]
            vec_scr[h, 0] = jnp.exp((ri + 1.0) * lfe)
            vec_scr[h, 1] = jnp.exp((float(C) - ri) * lbe)
            vec_scr[h, 2] = scale * jnp.exp((float(C) - 1.0 - ri) * lfe)
            vec_scr[h, 3] = scale * jnp.exp(ri * lbe)


def _ret_core(hn, w_ref, s0f_ref, s0b_ref, wo_ref, y_ref, sf_ref, sb_ref, mask_scr, vec_scr,
              cd_scr, g_scr, *, L, C, nb, has_init, emit_state):
    n = L // C
    H, E = RET_HEADS, HEAD_DIM
    tdims = (((0,), (0,)), ((), ()))
    ndims = (((1,), (1,)), ((), ()))
    chains = [(s, h) for s in range(nb) for h in range(H)]
    rows = [slice(c * C, (c + 1) * C) for c in range(n)]
    qkvg = [_dot(hn[s], w_ref[...]) for s in range(nb)]

    def cols(s, part, h):
        return qkvg[s][:, part * RET_W + h * E:part * RET_W + (h + 1) * E]

    qb = [cols(s, 0, h).astype(BF16) for s, h in chains]
    kf = [cols(s, 1, h) for s, h in chains]
    kb = [k.astype(BF16) for k in kf]
    vb = [cols(s, 2, h).astype(BF16) for s, h in chains]
    att = [[lax.dot_general(qb[i][r], kb[i][r], ndims, preferred_element_type=F32) for r in rows]
           for i in range(len(chains))]
    prob = [[(att[i][c] * mask_scr[h]).astype(BF16) for c in range(n)]
            for i, (s, h) in enumerate(chains)]
    out = [[_dot(prob[i][c], vb[i][rows[c]]) for c in range(n)] for i in range(len(chains))]
    kv = []
    for i, (s, h) in enumerate(chains):
        dk2 = jnp.concatenate([vec_scr[h, 2], vec_scr[h, 3]], axis=1)
        per_c = []
        for r in rows:
            k2 = (jnp.concatenate([kf[i][r], kf[i][r]], axis=1) * dk2).astype(BF16)
            per_c.append(lax.dot_general(k2, vb[i][r], tdims, preferred_element_type=F32))
        kv.append(per_c)
    for i, (s, h) in enumerate(chains):
        cdf = cd_scr[h:h + 1, :]
        cdb = cd_scr[H + h:H + h + 1, :]
        sf_in, sb_in = [None] * n, [None] * n
        st = s0f_ref[s, h] if has_init else None
        for c in range(n):
            sf_in[c] = st
            kvc = kv[i][c][0:E]
            st = kvc if st is None else st * cdf + kvc
        if emit_state:
            sf_ref[s, h] = st
        st = s0b_ref[s, h] if has_init else None
        for c in range(n - 1, -1, -1):
            sb_in[c] = st
            kvc = kv[i][c][E:2 * E]
            st = kvc if st is None else st * cdb + kvc
        if emit_state:
            sb_ref[s, h] = st
        for c in range(n):
            if sf_in[c] is not None and sb_in[c] is not None:
                s2 = jnp.concatenate([sf_in[c], sb_in[c]], axis=1).astype(BF16)
                inter = _dot(qb[i][rows[c]], s2)
                out[i][c] = (out[i][c] + inter[:, 0:E] * vec_scr[h, 0]
                             + inter[:, E:2 * E] * vec_scr[h, 1])
            elif sf_in[c] is not None:
                out[i][c] = (out[i][c]
                             + _dot(qb[i][rows[c]], sf_in[c].astype(BF16)) * vec_scr[h, 0])
            elif sb_in[c] is not None:
                out[i][c] = (out[i][c]
                             + _dot(qb[i][rows[c]], sb_in[c].astype(BF16)) * vec_scr[h, 1])
    for i, (s, h) in enumerate(chains):
        for c in range(n):
            o = out[i][c]
            mu = jnp.mean(o, axis=-1, keepdims=True)
            d = o - mu
            var = jnp.mean(d * d, axis=-1, keepdims=True)
            on = d * lax.rsqrt(var + EPS)
            gg = cols(s, 3, h)[rows[c]]
            g_scr[s, rows[c], h * E:(h + 1) * E] = (gg * jax.nn.sigmoid(gg) * on).astype(BF16)
    for s in range(nb):
        y_ref[s] = _dot(g_scr[s], wo_ref[...])


def _hy_core(hn, w_ref, cw_ref, cb_ref, fw_ref, bw_ref, fa_ref, fb_ref, fd_ref, hb_ref, wo_ref,
             y_ref, *, L, W, b, nb):
    m = L // b
    CB = HY_CBLK
    nblk = HY_W // CB
    pos = lax.broadcasted_iota(jnp.int32, (L, CB), 0) % W
    first = pos == 0
    last = pos == W - 1
    chains = [(s, blk) for s in range(nb) for blk in range(nblk)]

    def short_conv(s, base, blk):
        cs = slice(base + blk * CB, base + (blk + 1) * CB)
        ug = _dot(hn[s], w_ref[:, cs])
        prev = jnp.where(first, 0.0, pltpu.roll(ug, 1, axis=0))
        nxt = jnp.where(last, 0.0, pltpu.roll(ug, L - 1, axis=0))
        u = (prev * cw_ref[0:1, cs] + ug * cw_ref[1:2, cs] + nxt * cw_ref[2:3, cs]
             + cb_ref[:, cs])
        return [u[j * b:(j + 1) * b] for j in range(m)]

    def long_conv(sigs, o):
        spec = [[_dot(fw_ref[...], sj.astype(BF16)) for sj in sig] for sig in sigs]
        prods = []
        for (s, blk), sp in zip(chains, spec):
            cs = slice(blk * CB, (blk + 1) * CB)
            per_i = []
            for i in range(m):
                yre = yim = yim8 = None
                for j in range(m):
                    d = i - j + m - 1
                    sre, sim = sp[j][0:b], sp[j][b:2 * b]
                    ka, kb = fa_ref[o, d, :, cs], fb_ref[o, d, :, cs]
                    tre = sre * ka - sim * kb
                    tim = sre * kb + sim * ka
                    t8 = sre[0:8] * kb[0:8] + sim[0:8] * fd_ref[o, d, :, cs]
                    yre = tre if yre is None else yre + tre
                    yim = tim if yim is None else yim + tim
                    yim8 = t8 if yim8 is None else yim8 + t8
                yim = jnp.concatenate([yim8, yim[8:]], axis=0)
                per_i.append((yre.astype(BF16), yim.astype(BF16)))
            prods.append(per_i)
        return [[_dot(bw_ref[:, 0:b], yre) + _dot(bw_ref[:, b:2 * b], yim) for yre, yim in per_i]
                for per_i in prods]

    hv = [short_conv(s, 0, blk) for s, blk in chains]
    hx1 = [short_conv(s, HY_W, blk) for s, blk in chains]
    hx2 = [short_conv(s, 2 * HY_W, blk) for s, blk in chains]

    def gate(hx, conv, sig, o):
        out = []
        for (s, blk), hxc, cc, sc in zip(chains, hx, conv, sig):
            bias = hb_ref[o:o + 1, blk * CB:(blk + 1) * CB]
            out.append([hxc[i] * (cc[i] + sc[i] * bias) for i in range(m)])
        return out

    z = gate(hx1, long_conv(hv, 0), hv, 0)
    z = gate(hx2, long_conv(z, 1), z, 1)
    for s in range(nb):
        for i in range(m):
            acc = None
            for blk in range(nblk):
                zc = z[chains.index((s, blk))][i].astype(BF16)
                part = _dot(zc, wo_ref[blk * CB:(blk + 1) * CB, :])
                acc = part if acc is None else acc + part
            y_ref[s, i * b:(i + 1) * b, :] = acc


def _mix_kernel(*refs, L, C, W, b, nb, do_ret, do_hy, has_init, emit_state, casts):
    it = iter(refs)
    x_ref, mod_ref, n1_ref = next(it), next(it), next(it)
    s0f_ref = s0b_ref = sf_ref = sb_ref = None
    if do_ret:
        wq_ref, dec_ref = next(it), next(it)
        if has_init:
            s0f_ref, s0b_ref = next(it), next(it)
        wo_ret_ref = next(it)
    if do_hy:
        hy_in = [next(it) for _ in range(10)]
    cast_srcs = [next(it) for _ in casts]
    if do_ret:
        y_ret_ref = next(it)
        if emit_state:
            sf_ref, sb_ref = next(it), next(it)
    if do_hy:
        y_hy_ref = next(it)
    cast_dsts = [next(it) for _ in range(_n_cast_outputs(casts))]
    if do_ret:
        ret_scr = [next(it) for _ in range(4)]
        _ret_init(dec_ref, ret_scr[0], ret_scr[1], ret_scr[2], C)
    _do_casts(casts, cast_srcs, cast_dsts)
    mod = mod_ref[0]
    hn = [_modnorm(x_ref[s], n1_ref[...], mod[1:2], mod[0:1]).astype(BF16) for s in range(nb)]
    if do_hy:
        _hy_core(hn, *hy_in, y_hy_ref, L=L, W=W, b=b, nb=nb)
    if do_ret:
        _ret_core(hn, wq_ref, s0f_ref, s0b_ref, wo_ret_ref, y_ret_ref, sf_ref, sb_ref, *ret_scr,
                  L=L, C=C, nb=nb, has_init=has_init, emit_state=emit_state)


def _mixer(x, mods3, mod_row, norm1, *, nb, ret=None, hy=None, casts=()):
    B, L, D = x.shape
    H, E = RET_HEADS, HEAD_DIM
    C = min(RET_CHUNK, L)
    seq_spec = pl.BlockSpec((nb, L, D), lambda g: (g, 0, 0))
    in_specs = [seq_spec,
                pl.BlockSpec((1, N_MOD, D), lambda g: (mod_row(g * nb), 0, 0)),
                _const_spec((1, D))]
    args = [x, mods3, norm1]
    out_specs, out_shape, scratch = [], [], []
    has_init = emit_state = False
    W = b = None
    if ret is not None:
        w_qkvg, dec8, s0f, s0b, w_o, emit_state = ret
        has_init = s0f is not None
        st_spec = pl.BlockSpec((nb, H, E, E), lambda g: (g, 0, 0, 0))
        in_specs += [_const_spec((D, N_QKVG)), _const_spec((8, C))]
        args += [w_qkvg, dec8[:, :C]]
        if has_init:
            in_specs += [st_spec, st_spec]
            args += [s0f, s0b]
        in_specs.append(_const_spec((RET_W, D)))
        args.append(w_o)
        out_specs.append(seq_spec)
        out_shape.append(jax.ShapeDtypeStruct((B, L, D), F32))
        if emit_state:
            out_specs += [st_spec, st_spec]
            out_shape += [jax.ShapeDtypeStruct((B, H, E, E), F32)] * 2
        scratch = [pltpu.VMEM((H, C, C), F32), pltpu.VMEM((H, 4, C, E), F32),
                   pltpu.VMEM((8, E), F32), pltpu.VMEM((nb, L, RET_W), BF16)]
    if hy is not None:
        w_hy, conv_w, conv_b, fw, bw, (fa, fb, fd), hy_bias, w_o, W, b = hy
        nd = fa.shape[1]
        in_specs += [_const_spec((D, N_HY)), _const_spec((3, N_HY)), _const_spec((1, N_HY)),
                     _const_spec((2 * b, b)), _const_spec((b, 2 * b)),
                     _const_spec((HY_ORDER, nd, b, HY_W)), _const_spec((HY_ORDER, nd, b, HY_W)),
                     _const_spec((HY_ORDER, nd, 8, HY_W)), _const_spec((HY_ORDER, HY_W)),
                     _const_spec((HY_W, D))]
        args += [w_hy, conv_w, conv_b, fw, bw, fa, fb, fd, hy_bias, w_o]
        out_specs.append(seq_spec)
        out_shape.append(jax.ShapeDtypeStruct((B, L, D), F32))
    c_in, c_out, c_shape, c_args = _cast_specs(casts, B // nb)
    name = ("ret" if ret is not None else "") + ("hy" if hy is not None else "")
    return pl.pallas_call(
        functools.partial(_mix_kernel, L=L, C=C, W=W, b=b, nb=nb, do_ret=ret is not None,
                          do_hy=hy is not None, has_init=has_init, emit_state=emit_state,
                          casts=tuple(cs for _, cs in casts)),
        grid=(B // nb,),
        in_specs=in_specs + c_in,
        out_specs=out_specs + c_out,
        out_shape=out_shape + c_shape,
        scratch_shapes=scratch,
        compiler_params=_params(1),
        name=f"{name}{L}",
    )(*args, *c_args)


def _mlp_kernel(x_ref, yr_ref, yh_ref, modp_ref, modq_ref, n1_ref, n2_ref, fg_ref, wg_ref,
                wout_ref, wfi_hbm, wfo_hbm, y_ref, x1_scr, h2_scr, wfi_ref, wfo_ref, sem, *,
                n_tiles):
    i = pl.program_id(0)
    wr = i % 2
    rd = 1 - wr
    nq = N_GATE // 4

    def pre_stages():
        mp = modp_ref[0]
        st = {}

        def p1():
            st["x"] = x_ref[...]
            st["hn"] = _modnorm(st["x"], n1_ref[...], mp[1:2], mp[0:1]).astype(BF16)

        def p2(q):
            def f():
                st["g%d" % q] = _dot(st["hn"], wg_ref[:, q * nq:(q + 1) * nq])
            return f

        def p3(h):
            def f():
                cs = slice(h * nq, (h + 1) * nq)
                st["mix%d" % h] = (jax.nn.sigmoid(st["g%d" % h]) * yr_ref[:, cs]
                                   + jax.nn.sigmoid(st["g%d" % (2 + h)]) * yh_ref[:, cs]
                                   ).astype(BF16)
            return f

        def p4():
            upd = (_dot(st["mix0"], wout_ref[0:nq, :]) + _dot(st["mix1"], wout_ref[nq:2 * nq, :]))
            st["x1"] = st["x"] + mp[2:3] * upd

        def p5():
            x1_scr[wr] = st["x1"]
            h2_scr[wr] = _modnorm(st["x1"], n2_ref[...], mp[4:5], mp[3:4]).astype(BF16)

        return [p1, p2(0), p2(1), p2(2), p2(3), p3(0), p3(1), p4, p5]

    def ffn_stages():
        mq = modq_ref[0]
        st = {"acc": None}

        def f(j):
            def g():
                cs = slice(j * FF_CHUNK, (j + 1) * FF_CHUNK)
                h2 = h2_scr[rd]
                a = _dot(h2, wfi_ref[:, cs])
                b = _dot(h2, wfi_ref[:, D_FF + j * FF_CHUNK:D_FF + (j + 1) * FF_CHUNK])
                ff = (a * jax.nn.sigmoid(a) * b).astype(BF16)
                part = _dot(ff, wfo_ref[cs, :])
                st["acc"] = part if st["acc"] is None else st["acc"] + part
            return g

        def e():
            x2 = x1_scr[rd] + mq[5:6] * st["acc"]
            ms = jnp.mean(x2 * x2, axis=-1, keepdims=True)
            y_ref[...] = x2 * lax.rsqrt(ms + EPS) * fg_ref[...]

        return [f(j) for j in range(D_FF // FF_CHUNK)] + [e]

    @pl.when(i == 0)
    def _():
        copies = [pltpu.make_async_copy(wfi_hbm, wfi_ref, sem.at[0]),
                  pltpu.make_async_copy(wfo_hbm, wfo_ref, sem.at[1])]
        for cp in copies:
            cp.start()
        for stage in pre_stages():
            stage()
        for cp in copies:
            cp.wait()

    @pl.when(jnp.logical_and(i > 0, i < n_tiles))
    def _():
        pre, ffn = pre_stages(), ffn_stages()
        order = []
        while pre or ffn:
            if ffn:
                order.append(ffn.pop(0))
            if pre:
                order.append(pre.pop(0))
        for stage in order:
            stage()

    @pl.when(i == n_tiles)
    def _():
        for stage in ffn_stages():
            stage()


def _mlp(x, y_ret, y_hy, mods3, mod_row, norm1, norm2, final_g, w_gate, w_out, w_fi, w_fo):
    B, L, D = x.shape
    T = MLP_ROWS
    n_tiles = B * L // T
    flat = lambda a: a.reshape(B * L, D)
    pre_tile = lambda i: jnp.minimum(i, n_tiles - 1)
    post_tile = lambda i: jnp.maximum(i - 1, 0)
    act = pl.BlockSpec((T, D), lambda i: (pre_tile(i), 0))
    y = pl.pallas_call(
        functools.partial(_mlp_kernel, n_tiles=n_tiles),
        grid=(n_tiles + 1,),
        in_specs=[act, act, act,
                  pl.BlockSpec((1, N_MOD, D), lambda i: (mod_row((pre_tile(i) * T) // L), 0, 0)),
                  pl.BlockSpec((1, N_MOD, D), lambda i: (mod_row((post_tile(i) * T) // L), 0, 0)),
                  _const_spec((1, D)), _const_spec((1, D)), _const_spec((1, D)),
                  _const_spec((D, N_GATE)),
                  _const_spec((D, D)),
                  pl.BlockSpec(memory_space=pl.ANY),
                  pl.BlockSpec(memory_space=pl.ANY)],
        out_specs=pl.BlockSpec((T, D), lambda i: (post_tile(i), 0)),
        out_shape=jax.ShapeDtypeStruct((B * L, D), F32),
        scratch_shapes=[pltpu.VMEM((2, T, D), F32), pltpu.VMEM((2, T, D), BF16),
                        pltpu.VMEM((D, 2 * D_FF), BF16), pltpu.VMEM((D_FF, D), BF16),
                        pltpu.SemaphoreType.DMA((2,))],
        compiler_params=_params(1),
        name=f"mlp{L}",
    )(flat(x), flat(y_ret), flat(y_hy), mods3, mods3, norm1, norm2, final_g, w_gate, w_out, w_fi,
      w_fo)
    return y.reshape(B, L, D)


def kernel(x_prompt, x_sample, state_ret_fwd, state_ret_bwd, c, c_ctx, norm1_g, norm2_g, w_ada,
           b_ada, w_in, ret_decay_fwd, ret_decay_bwd, hy_conv_w, hy_conv_b, hy_pos_w1, hy_pos_b1,
           hy_pos_w2, hy_pos_b2, hy_pos_w3, hy_sin_freq, hy_bias, w_ret_o, w_hy_o, w_out,
           w_ffn_in, w_ffn_out, final_g):
    assert w_in.shape[0] == 1, "single-layer configuration"
    nb_lat = x_sample.shape[0]
    l_ctx = x_prompt.shape[1]

    cond8 = jnp.concatenate([c_ctx[None, :], c, jnp.zeros((8 - 1 - nb_lat, D_MODEL), F32)])
    norm1 = norm1_g[0][None, :]
    norm2 = norm2_g[0][None, :]
    fg = final_g[None, :]
    dec8 = jnp.broadcast_to(jnp.concatenate([ret_decay_fwd[0], ret_decay_bwd[0]])[:, None],
                            (8, RET_CHUNK))
    w1 = jnp.concatenate([hy_pos_w1[0], hy_pos_b1,
                          jnp.zeros((HY_EMB_PAD - HY_EMB - 1, HY_HIDDEN), F32)])
    w2 = jnp.concatenate([hy_pos_w2[0], hy_pos_b2, jnp.zeros((7, HY_HIDDEN), F32)])
    freq = hy_sin_freq
    conv_b = hy_conv_b[0][None, :]

    n_hy_end = N_QKVG + N_HY
    w_in_parts = (slice(0, N_QKVG), slice(N_QKVG, n_hy_end), slice(n_hy_end, N_IN))
    groups = []
    for L in (l_ctx, x_sample.shape[1]):
        blk = min(HY_TBLK, L)
        fw, bw, sgn = _dft_mats(blk)
        z, tdel = _filter_consts(L)
        groups.append((L, blk, jnp.asarray(z.T), jnp.asarray(tdel), jnp.asarray(sgn), fw))
    (mods, fa_c, fb_c, fd_c, fa_l, fb_l, fd_l, w_qkvg, w_hy, w_gate, w_ret_o_b,
     w_hy_o_b) = _ada(cond8, w_ada[0], b_ada, (w1, w2, hy_pos_w3[0], freq), groups,
                      casts=[(w_in[0], w_in_parts), (w_ret_o[0], None), (w_hy_o[0], None)])
    mods3 = mods.reshape(8, N_MOD, D_MODEL)

    def branches(x, filt, s0f, s0b, grid_w, emit_state):
        blk = min(HY_TBLK, x.shape[1])
        fw, bw, _ = _dft_mats(blk)
        ret = (w_qkvg, dec8, s0f, s0b, w_ret_o_b, emit_state)
        hy = (w_hy, hy_conv_w[0], conv_b, fw, bw, filt, hy_bias[0], w_hy_o_b, grid_w, blk)
        return ret, hy

    ctx_row = lambda b: 0
    lat_row = lambda b: b + 1
    ret, hy = branches(x_prompt, (fa_c, fb_c, fd_c), None, None, l_ctx, True)
    y_ret_c, s_f, s_b, y_hy_c, w_fi_b, w_fo_b, w_out_b = _mixer(
        x_prompt, mods3, ctx_row, norm1, nb=CTX_SEQS, ret=ret, hy=hy,
        casts=[(w_ffn_in[0], None), (w_ffn_out[0], None), (w_out[0], None)])
    y_prompt = _mlp(x_prompt, y_ret_c, y_hy_c, mods3, ctx_row, norm1, norm2, fg, w_gate, w_out_b,
                    w_fi_b, w_fo_b)
    ret, hy = branches(x_sample, (fa_l, fb_l, fd_l), state_ret_fwd[:, 0], state_ret_bwd[:, 0],
                       GRID_W, False)
    y_ret_l, = _mixer(x_sample, mods3, lat_row, norm1, nb=1, ret=ret)
    y_hy_l, = _mixer(x_sample, mods3, lat_row, norm1, nb=1, hy=hy)
    y_sample = _mlp(x_sample, y_ret_l, y_hy_l, mods3, lat_row, norm1, norm2, fg, w_gate,
                    w_out_b, w_fi_b, w_fo_b)
    return (y_prompt, y_sample, s_f[:, None], s_b[:, None])
```

```python
import functools
import math

import numpy as np
import jax
import jax.numpy as jnp
from jax import lax
from jax.experimental import pallas as pl
from jax.experimental.pallas import tpu as pltpu

F32 = jnp.float32
BF16 = jnp.bfloat16

D_MODEL = 1024
RET_HEADS = 4
HEAD_DIM = 128
RET_W = RET_HEADS * HEAD_DIM
HY_W = 512
HY_ORDER = 2
HY_BANDS = 16
HY_EMB = 1 + 2 * HY_BANDS
HY_EMB_PAD = 40
HY_HIDDEN = 64
HY_FAST_DECAY = 0.3
HY_SLOW_DECAY = 1.5
HY_TARGET = 1e-2
D_FF = 2816
N_QKVG = 4 * RET_W
N_HY = 3 * HY_W
N_GATE = 2 * D_MODEL
N_IN = N_QKVG + N_HY + N_GATE
N_MOD = 6
MOD_ROWS = 8
EPS = 1e-6
GRID_W = 64
RET_CHUNK = 256
HY_CBLK = 256
HY_TBLK = 512
CTX_SEQS = 2
MLP_ROWS = 512
FF_CHUNK = 256
FILTER_ONE_STEP_LEN = 256
ADA_COLS = 768
VMEM_LIMIT = 56 * 1024 * 1024


def _const_spec(shape):
    nd = len(shape)
    return pl.BlockSpec(shape, lambda *_: (0,) * nd, pipeline_mode=pl.Buffered(1))


def _params(n_axes):
    return pltpu.CompilerParams(dimension_semantics=("arbitrary",) * n_axes,
                                vmem_limit_bytes=VMEM_LIMIT)


def _modnorm(x, g, scale, shift):
    ms = jnp.mean(x * x, axis=-1, keepdims=True)
    return (x * lax.rsqrt(ms + EPS) * g) * (1.0 + scale) + shift


def _mod(mod, k):
    return mod[:, k * D_MODEL:(k + 1) * D_MODEL]


def _dot(a, b):
    return jnp.dot(a, b, preferred_element_type=F32)


def _cast_specs(casts, steps):
    in_specs, out_specs, out_shape, args = [], [], [], []
    for arr, col_slices in casts:
        rows, width = arr.shape
        rb = rows // steps
        assert rb * steps == rows and rb % 16 == 0
        in_specs.append(pl.BlockSpec((rb, width), lambda g: (g, 0)))
        args.append(arr)
        for cs in col_slices or (slice(0, width),):
            cols = cs.stop - cs.start
            out_specs.append(pl.BlockSpec((rb, cols), lambda g: (g, 0)))
            out_shape.append(jax.ShapeDtypeStruct((rows, cols), BF16))
    return in_specs, out_specs, out_shape, args


def _n_cast_outputs(col_slices_per_src):
    return sum(1 if s is None else len(s) for s in col_slices_per_src)


def _do_casts(col_slices_per_src, srcs, dsts):
    dsts = iter(dsts)
    for col_slices, src in zip(col_slices_per_src, srcs):
        if col_slices is None:
            next(dsts)[...] = src[...].astype(BF16)
        else:
            for cs in col_slices:
                next(dsts)[...] = src[:, cs].astype(BF16)


def _ada_kernel(*refs, casts, groups):
    it = iter(refs)
    cctx_ref, c_ref, w_ref, b_ref = next(it), next(it), next(it), next(it)
    mlp_refs = [next(it) for _ in range(6)]
    g_in = [[next(it) for _ in range(4)] for _ in groups]
    cast_srcs = [next(it) for _ in casts]
    o_ref = next(it)
    g_out = [[next(it) for _ in range(3)] for _ in groups]
    cast_dsts = [next(it) for _ in range(_n_cast_outputs(casts))]
    h_scr, cond_scr, w1_scr, w2_scr = next(it), next(it), next(it), next(it)
    _do_casts(casts, cast_srcs, cast_dsts)
    nlat = c_ref.shape[0]
    cond_scr[...] = jnp.zeros_like(cond_scr)
    cond_scr[0:1, :] = cctx_ref[...]
    cond_scr[1:1 + nlat, :] = c_ref[...]
    c = cond_scr[...]
    s = (c * jax.nn.sigmoid(c)).astype(BF16)
    res = _dot(s, w_ref[...].astype(BF16)) + b_ref[...]
    for r in range(res.shape[0]):
        o_ref[r] = res[r:r + 1, :]

    step = 0
    for (L, blk), (zt_ref, tdel_ref, sgn_ref, fw_ref), outs in zip(groups, g_in, g_out):
        def mlp_job(zt_ref=zt_ref, L=L):
            h_scr[0:L, :] = _filter_mlp(zt_ref, *mlp_refs, w1_scr, w2_scr)

        def order_job(o, tdel_ref=tdel_ref, sgn_ref=sgn_ref, fw_ref=fw_ref, outs=outs, L=L,
                      blk=blk):
            _filter_spectra(h_scr, o, tdel_ref, sgn_ref, fw_ref, *outs, L=L, b=blk)

        def all_job(mlp_job=mlp_job, order_job=order_job):
            mlp_job()
            for o in range(HY_ORDER):
                order_job(o)

        if L <= FILTER_ONE_STEP_LEN:
            jobs = [all_job]
        else:
            jobs = [mlp_job] + [functools.partial(order_job, o) for o in range(HY_ORDER)]
        for job in jobs:
            pl.when(pl.program_id(0) == step)(job)
            step += 1


def _ada(c_ctx, c, w, b, filt_params, groups, casts=()):
    n = w.shape[1]
    steps = n // ADA_COLS
    c_in, c_out, c_shape, c_args = _cast_specs(casts, steps)
    in_specs = [_const_spec(c_ctx.shape), _const_spec(c.shape),
                pl.BlockSpec((D_MODEL, ADA_COLS), lambda j: (0, j)),
                pl.BlockSpec((1, ADA_COLS), lambda j: (0, j))]
    args = [c_ctx, c, w, b]
    for p in filt_params:
        in_specs.append(_const_spec(p.shape))
        args.append(p)
    out_specs = [pl.BlockSpec((MOD_ROWS, 1, ADA_COLS), lambda j: (0, 0, j))]
    out_shape = [jax.ShapeDtypeStruct((MOD_ROWS, 1, n), F32)]
    max_len = 8
    for L, blk, *consts in groups:
        nd = 2 * (L // blk) - 1
        max_len = max(max_len, L)
        for cst in consts:
            in_specs.append(_const_spec(cst.shape))
            args.append(cst)
        for shp in ((HY_ORDER, nd, blk, HY_W), (HY_ORDER, nd, blk, HY_W), (HY_ORDER, nd, 8, HY_W)):
            out_specs.append(pl.BlockSpec(shp, lambda j: (0, 0, 0, 0)))
            out_shape.append(jax.ShapeDtypeStruct(shp, F32))
    n_jobs = sum(1 if L <= FILTER_ONE_STEP_LEN else 1 + HY_ORDER for L, *_ in groups)
    assert n_jobs <= steps
    return pl.pallas_call(
        functools.partial(_ada_kernel, casts=tuple(cs for _, cs in casts),
                          groups=tuple((L, blk) for L, blk, *_ in groups)),
        grid=(steps,),
        in_specs=in_specs + c_in,
        out_specs=out_specs + c_out,
        out_shape=out_shape + c_shape,
        scratch_shapes=[pltpu.VMEM((max_len, HY_ORDER * 2 * HY_W), F32),
                        pltpu.VMEM((MOD_ROWS, D_MODEL), F32),
                        pltpu.VMEM((HY_EMB_PAD, HY_HIDDEN), F32),
                        pltpu.VMEM((HY_HIDDEN + 8, HY_HIDDEN), F32)],
        compiler_params=_params(1),
        name="ada",
    )(*args, *c_args)


@functools.lru_cache(maxsize=None)
def _dft_mats(L):
    n = 2 * L
    t = np.arange(L, dtype=np.int64)
    f = np.arange(L, dtype=np.int64)
    ang = 2.0 * np.pi * ((f[:, None] * t[None, :]) % n).astype(np.float64) / n
    cos = np.cos(ang)
    sin = np.sin(ang)
    nyq = np.where(t % 2 == 0, 1.0, -1.0)
    fwd = np.concatenate([cos, -sin], axis=0)
    fwd[L] = nyq
    wre = np.full((L,), 2.0 / n)
    wre[0] = 1.0 / n
    inv = np.concatenate([cos.T * wre[None, :], -sin.T * (2.0 / n)], axis=1)
    inv[:, L] = nyq / n
    sgn = np.broadcast_to(nyq[:, None], (L, HY_W))
    return (jnp.asarray(fwd, dtype=BF16), jnp.asarray(inv, dtype=BF16),
            np.asarray(sgn, dtype=np.float32))


@functools.lru_cache(maxsize=None)
def _filter_consts(L):
    t = np.linspace(0.0, 1.0, L)[:, None]
    ang = 2.0 * np.pi * np.arange(L, dtype=np.float64)[:, None] / L
    bands = np.linspace(1e-4, HY_BANDS - 1, HY_BANDS)[None]
    z = np.concatenate([t, np.cos(bands * ang), -np.sin(bands * ang)], axis=-1)
    z = np.pad(z, ((0, 0), (0, HY_EMB_PAD - HY_EMB)))
    z[:, HY_EMB] = 1.0
    max_decay = math.log(HY_TARGET) / HY_FAST_DECAY
    min_decay = math.log(HY_TARGET) / HY_SLOW_DECAY
    deltas = np.linspace(min_decay, max_decay, HY_W)
    tdel = t * np.abs(deltas)[None, :]
    return np.asarray(z, np.float32), np.asarray(tdel, np.float32)


def _filter_mlp(zt_ref, w1_ref, b1_ref, w2_ref, b2_ref, w3_ref, fr_ref, w1_scr, w2_scr):
    hi = lax.Precision.HIGHEST
    tdims = (((0,), (0,)), ((), ()))
    w1_scr[...] = jnp.zeros_like(w1_scr)
    w1_scr[0:HY_EMB, :] = w1_ref[...]
    w1_scr[HY_EMB:HY_EMB + 1, :] = b1_ref[...]
    w2_scr[...] = jnp.zeros_like(w2_scr)
    w2_scr[0:HY_HIDDEN, :] = w2_ref[...]
    w2_scr[HY_HIDDEN:HY_HIDDEN + 1, :] = b2_ref[...]
    fr = fr_ref[...]
    zt = zt_ref[...]
    h1 = jnp.sin(lax.dot_general(w1_scr[...] * fr, zt, tdims, precision=hi,
                                 preferred_element_type=F32))
    h1 = jnp.concatenate([h1, jnp.ones((8, zt.shape[1]), F32)], axis=0)
    h2 = jnp.sin(lax.dot_general(w2_scr[...] * fr, h1, tdims, precision=hi,
                                 preferred_element_type=F32))
    h2_hi = h2.astype(BF16)
    h2_lo = (h2 - h2_hi.astype(F32)).astype(BF16)
    w3 = w3_ref[...]
    w3_hi = w3.astype(BF16)
    w3_lo = (w3 - w3_hi.astype(F32)).astype(BF16)
    lhs = jnp.concatenate([h2_hi, h2_lo, h2_hi, jnp.zeros_like(h2_hi)], axis=0)
    rhs = jnp.concatenate([w3_hi, w3_hi, w3_lo, jnp.zeros_like(w3_hi)], axis=0)
    h = lax.dot_general(lhs, rhs, tdims, preferred_element_type=F32)
    return h


def _filter_spectra(h_ref, o, tdel_ref, sgn_ref, fw_ref, oa_ref, ob_ref, od_ref, *, L, b):
    m = L // b
    win = jnp.exp(-tdel_ref[...])
    sg = sgn_ref[...]
    row0_l = lax.broadcasted_iota(jnp.int32, (L, HY_W), 0) == 0
    row0_b = lax.broadcasted_iota(jnp.int32, (b, HY_W), 0) == 0
    row0_8 = lax.broadcasted_iota(jnp.int32, (8, HY_W), 0) == 0
    base = o * 2 * HY_W
    fwd = h_ref[0:L, base:base + HY_W] * win
    bwd = jnp.where(row0_l, 0.0, h_ref[0:L, base + HY_W:base + 2 * HY_W] * win)
    nrm = (jnp.sum(jnp.abs(fwd), axis=0, keepdims=True)
           + jnp.sum(jnp.abs(bwd), axis=0, keepdims=True))
    inv = 1.0 / nrm
    fn = fwd * inv
    bn = bwd * inv
    xr, xn, xi, wr, wn, wi = [], [], [], [], [], []
    for r in range(m):
        p = _dot(fw_ref[...], fn[r * b:(r + 1) * b].astype(BF16))
        q = _dot(fw_ref[...], bn[r * b:(r + 1) * b].astype(BF16))
        xr.append(p[0:b])
        xn.append(p[b:b + 1])
        xi.append(jnp.where(row0_b, 0.0, p[b:2 * b]))
        wr.append(q[0:b])
        wn.append(q[b:b + 1])
        wi.append(jnp.where(row0_b, 0.0, -q[b:2 * b]))

    def emit(d, ka, kn, kb):
        oa_ref[o, d + m - 1] = ka
        ob_ref[o, d + m - 1] = kb
        od_ref[o, d + m - 1] = jnp.where(row0_8, kn, ka[0:8])

    emit(0, xr[0] + wr[0], xn[0] + wn[0], xi[0] + wi[0])
    for d in range(1, m):
        f0 = fn[(d - 1) * b:(d - 1) * b + 1]
        b0 = bn[(d - 1) * b:(d - 1) * b + 1]
        emit(d, xr[d] + sg * (xr[d - 1] - f0), xn[d] + (xn[d - 1] - f0),
             xi[d] + sg * xi[d - 1])
        emit(-d, wr[d] + sg * (wr[d - 1] - b0), wn[d] + (wn[d - 1] - b0),
             wi[d] + sg * wi[d - 1])


def _ret_init(decf_ref, decb_ref, mask_scr, vec_scr, cd_scr, C):
    H, E = RET_HEADS, HEAD_DIM
    scale = float(E) ** -0.5

    @pl.when(pl.program_id(0) == 0)
    def _():
        dec = jnp.concatenate([jnp.full((1, C), ref[h], F32)
                               for ref in (decf_ref, decb_ref) for h in range(H)], axis=0)
        lg = jnp.log(jax.nn.sigmoid(dec))
        cd_scr[...] = jnp.exp(float(C) * lg[:, 0:E])
        ii = lax.broadcasted_iota(jnp.int32, (C, C), 0)
        jj = lax.broadcasted_iota(jnp.int32, (C, C), 1)
        rel = (ii - jj).astype(F32)
        ri = lax.broadcasted_iota(jnp.int32, (C, E), 0).astype(F32)
        for h in range(H):
            lf = lg[h:h + 1, :]
            lb = lg[H + h:H + h + 1, :]
            mf = jnp.where(rel >= 0, jnp.exp(jnp.maximum(rel, 0.0) * lf), 0.0)
            mb = jnp.where(rel <= 0, jnp.exp(jnp.maximum(-rel, 0.0) * lb), 0.0)
            mask_scr[h] = scale * (mf + mb)
            lfe, lbe = lf[:, 0:E], lb[:, 0:E]
            vec_scr[h, 0] = jnp.exp((ri + 1.0) * lfe)
            vec_scr[h, 1] = jnp.exp((float(C) - ri) * lbe)
            vec_scr[h, 2] = scale * jnp.exp((float(C) - 1.0 - ri) * lfe)
            vec_scr[h, 3] = scale * jnp.exp(ri * lbe)


def _ret_core(hn, w_ref, s0f_ref, s0b_ref, wo_ref, y_ref, sf_ref, sb_ref, mask_scr, vec_scr,
              cd_scr, g_scr, *, L, C, nb, has_init, emit_state):
    n = L // C
    H, E = RET_HEADS, HEAD_DIM
    tdims = (((0,), (0,)), ((), ()))
    ndims = (((1,), (1,)), ((), ()))
    chains = [(s, h) for s in range(nb) for h in range(H)]
    rows = [slice(c * C, (c + 1) * C) for c in range(n)]
    qkvg = [_dot(hn[s], w_ref[...]) for s in range(nb)]

    def cols(s, part, h):
        return qkvg[s][:, part * RET_W + h * E:part * RET_W + (h + 1) * E]

    qb = [cols(s, 0, h).astype(BF16) for s, h in chains]
    kf = [cols(s, 1, h) for s, h in chains]
    kb = [k.astype(BF16) for k in kf]
    vb = [cols(s, 2, h).astype(BF16) for s, h in chains]
    att = [[lax.dot_general(qb[i][r], kb[i][r], ndims, preferred_element_type=F32) for r in rows]
           for i in range(len(chains))]
    prob = [[(att[i][c] * mask_scr[h]).astype(BF16) for c in range(n)]
            for i, (s, h) in enumerate(chains)]
    out = [[_dot(prob[i][c], vb[i][rows[c]]) for c in range(n)] for i in range(len(chains))]
    kv = []
    for i, (s, h) in enumerate(chains):
        dk2 = jnp.concatenate([vec_scr[h, 2], vec_scr[h, 3]], axis=1)
        per_c = []
        for r in rows:
            k2 = (jnp.concatenate([kf[i][r], kf[i][r]], axis=1) * dk2).astype(BF16)
            per_c.append(lax.dot_general(k2, vb[i][r], tdims, preferred_element_type=F32))
        kv.append(per_c)
    for i, (s, h) in enumerate(chains):
        cdf = cd_scr[h:h + 1, :]
        cdb = cd_scr[H + h:H + h + 1, :]
        sf_in, sb_in = [None] * n, [None] * n
        st = s0f_ref[s, h] if has_init else None
        for c in range(n):
            sf_in[c] = st
            kvc = kv[i][c][0:E]
            st = kvc if st is None else st * cdf + kvc
        if emit_state:
            sf_ref[s, h] = st
        st = s0b_ref[s, h] if has_init else None
        for c in range(n - 1, -1, -1):
            sb_in[c] = st
            kvc = kv[i][c][E:2 * E]
            st = kvc if st is None else st * cdb + kvc
        if emit_state:
            sb_ref[s, h] = st
        for c in range(n):
            if sf_in[c] is not None and sb_in[c] is not None:
                s2 = jnp.concatenate([sf_in[c], sb_in[c]], axis=1).astype(BF16)
                inter = _dot(qb[i][rows[c]], s2)
                out[i][c] = (out[i][c] + inter[:, 0:E] * vec_scr[h, 0]
                             + inter[:, E:2 * E] * vec_scr[h, 1])
            elif sf_in[c] is not None:
                out[i][c] = (out[i][c]
                             + _dot(qb[i][rows[c]], sf_in[c].astype(BF16)) * vec_scr[h, 0])
            elif sb_in[c] is not None:
                out[i][c] = (out[i][c]
                             + _dot(qb[i][rows[c]], sb_in[c].astype(BF16)) * vec_scr[h, 1])
    for i, (s, h) in enumerate(chains):
        for c in range(n):
            o = out[i][c]
            mu = jnp.mean(o, axis=-1, keepdims=True)
            d = o - mu
            var = jnp.mean(d * d, axis=-1, keepdims=True)
            on = d * lax.rsqrt(var + EPS)
            gg = cols(s, 3, h)[rows[c]]
            g_scr[s, rows[c], h * E:(h + 1) * E] = (gg * jax.nn.sigmoid(gg) * on).astype(BF16)
    for s in range(nb):
        y_ref[s] = _dot(g_scr[s], wo_ref[...])


def _hy_core(hn, w_ref, cw_ref, cb_ref, fw_ref, bw_ref, fa_ref, fb_ref, fd_ref, hb_ref, wo_ref,
             y_ref, *, L, W, b, nb):
    m = L // b
    CB = HY_CBLK
    nblk = HY_W // CB
    pos = lax.broadcasted_iota(jnp.int32, (L, CB), 0) % W
    first = pos == 0
    last = pos == W - 1
    chains = [(s, blk) for s in range(nb) for blk in range(nblk)]

    def short_conv(s, base, blk):
        cs = slice(base + blk * CB, base + (blk + 1) * CB)
        ug = _dot(hn[s], w_ref[:, cs])
        prev = jnp.where(first, 0.0, pltpu.roll(ug, 1, axis=0))
        nxt = jnp.where(last, 0.0, pltpu.roll(ug, L - 1, axis=0))
        taps = [cw_ref[:, t * N_HY + cs.start:t * N_HY + cs.stop] for t in range(3)]
        u = prev * taps[0] + ug * taps[1] + nxt * taps[2] + cb_ref[:, cs]
        return [u[j * b:(j + 1) * b] for j in range(m)]

    def long_conv(sigs, o):
        spec = [[_dot(fw_ref[...], sj.astype(BF16)) for sj in sig] for sig in sigs]
        prods = []
        for (s, blk), sp in zip(chains, spec):
            cs = slice(blk * CB, (blk + 1) * CB)
            per_i = []
            for i in range(m):
                yre = yim = yim8 = None
                for j in range(m):
                    d = i - j + m - 1
                    sre, sim = sp[j][0:b], sp[j][b:2 * b]
                    ka, kb = fa_ref[o, d, :, cs], fb_ref[o, d, :, cs]
                    tre = sre * ka - sim * kb
                    tim = sre * kb + sim * ka
                    t8 = sre[0:8] * kb[0:8] + sim[0:8] * fd_ref[o, d, :, cs]
                    yre = tre if yre is None else yre + tre
                    yim = tim if yim is None else yim + tim
                    yim8 = t8 if yim8 is None else yim8 + t8
                yim = jnp.concatenate([yim8, yim[8:]], axis=0)
                per_i.append((yre.astype(BF16), yim.astype(BF16)))
            prods.append(per_i)
        return [[_dot(bw_ref[:, 0:b], yre) + _dot(bw_ref[:, b:2 * b], yim) for yre, yim in per_i]
                for per_i in prods]

    hv = [short_conv(s, 0, blk) for s, blk in chains]
    hx1 = [short_conv(s, HY_W, blk) for s, blk in chains]
    hx2 = [short_conv(s, 2 * HY_W, blk) for s, blk in chains]

    def gate(hx, conv, sig, o):
        out = []
        for (s, blk), hxc, cc, sc in zip(chains, hx, conv, sig):
            bias = hb_ref[o:o + 1, blk * CB:(blk + 1) * CB]
            out.append([hxc[i] * (cc[i] + sc[i] * bias) for i in range(m)])
        return out

    z = gate(hx1, long_conv(hv, 0), hv, 0)
    z = gate(hx2, long_conv(z, 1), z, 1)
    for s in range(nb):
        for i in range(m):
            acc = None
            for blk in range(nblk):
                zc = z[chains.index((s, blk))][i].astype(BF16)
                part = _dot(zc, wo_ref[blk * CB:(blk + 1) * CB, :])
                acc = part if acc is None else acc + part
            y_ref[s, i * b:(i + 1) * b, :] = acc


def _mix_kernel(*refs, L, C, W, b, nb, do_ret, do_hy, has_init, emit_state, casts):
    it = iter(refs)
    x_ref, mod_ref, n1_ref = next(it), next(it), next(it)
    s0f_ref = s0b_ref = sf_ref = sb_ref = None
    if do_ret:
        wq_ref, decf_ref, decb_ref = next(it), next(it), next(it)
        if has_init:
            s0f_ref, s0b_ref = next(it), next(it)
        wo_ret_ref = next(it)
    if do_hy:
        hy_in = [next(it) for _ in range(10)]
    cast_srcs = [next(it) for _ in casts]
    if do_ret:
        y_ret_ref = next(it)
        if emit_state:
            sf_ref, sb_ref = next(it), next(it)
    if do_hy:
        y_hy_ref = next(it)
    cast_dsts = [next(it) for _ in range(_n_cast_outputs(casts))]
    if do_ret:
        ret_scr = [next(it) for _ in range(4)]
        _ret_init(decf_ref, decb_ref, ret_scr[0], ret_scr[1], ret_scr[2], C)
    _do_casts(casts, cast_srcs, cast_dsts)
    mod = mod_ref[0]
    hn = [_modnorm(x_ref[s], n1_ref[...], _mod(mod, 1), _mod(mod, 0)).astype(BF16)
          for s in range(nb)]
    if do_hy:
        _hy_core(hn, *hy_in, y_hy_ref, L=L, W=W, b=b, nb=nb)
    if do_ret:
        _ret_core(hn, wq_ref, s0f_ref, s0b_ref, wo_ret_ref, y_ret_ref, sf_ref, sb_ref, *ret_scr,
                  L=L, C=C, nb=nb, has_init=has_init, emit_state=emit_state)


def _mixer(x, mods, mod_row, norm1, *, nb, ret=None, hy=None, casts=()):
    B, L, D = x.shape
    H, E = RET_HEADS, HEAD_DIM
    C = min(RET_CHUNK, L)
    seq_spec = pl.BlockSpec((nb, L, D), lambda g: (g, 0, 0))
    in_specs = [seq_spec,
                pl.BlockSpec((1, 1, N_MOD * D), lambda g: (mod_row(g * nb), 0, 0)),
                _const_spec((1, D))]
    args = [x, mods, norm1]
    out_specs, out_shape, scratch = [], [], []
    has_init = emit_state = False
    W = b = None
    if ret is not None:
        w_qkvg, dec_f, dec_b, s0f, s0b, w_o, emit_state = ret
        has_init = s0f is not None
        st_spec = pl.BlockSpec((nb, H, E, E), lambda g: (g, 0, 0, 0))
        smem = pl.BlockSpec(memory_space=pltpu.SMEM)
        in_specs += [_const_spec((D, N_QKVG)), smem, smem]
        args += [w_qkvg, dec_f, dec_b]
        if has_init:
            in_specs += [st_spec, st_spec]
            args += [s0f, s0b]
        in_specs.append(_const_spec((RET_W, D)))
        args.append(w_o)
        out_specs.append(seq_spec)
        out_shape.append(jax.ShapeDtypeStruct((B, L, D), F32))
        if emit_state:
            out_specs += [st_spec, st_spec]
            out_shape += [jax.ShapeDtypeStruct((B, H, E, E), F32)] * 2
        scratch = [pltpu.VMEM((H, C, C), F32), pltpu.VMEM((H, 4, C, E), F32),
                   pltpu.VMEM((8, E), F32), pltpu.VMEM((nb, L, RET_W), BF16)]
    if hy is not None:
        w_hy, conv_w, conv_b, fw, bw, (fa, fb, fd), hy_bias, w_o, W, b = hy
        nd = fa.shape[1]
        in_specs += [_const_spec((D, N_HY)), _const_spec((1, 3 * N_HY)), _const_spec((1, N_HY)),
                     _const_spec((2 * b, b)), _const_spec((b, 2 * b)),
                     _const_spec((HY_ORDER, nd, b, HY_W)), _const_spec((HY_ORDER, nd, b, HY_W)),
                     _const_spec((HY_ORDER, nd, 8, HY_W)), _const_spec((HY_ORDER, HY_W)),
                     _const_spec((HY_W, D))]
        args += [w_hy, conv_w, conv_b, fw, bw, fa, fb, fd, hy_bias, w_o]
        out_specs.append(seq_spec)
        out_shape.append(jax.ShapeDtypeStruct((B, L, D), F32))
    c_in, c_out, c_shape, c_args = _cast_specs(casts, B // nb)
    name = ("ret" if ret is not None else "") + ("hy" if hy is not None else "")
    return pl.pallas_call(
        functools.partial(_mix_kernel, L=L, C=C, W=W, b=b, nb=nb, do_ret=ret is not None,
                          do_hy=hy is not None, has_init=has_init, emit_state=emit_state,
                          casts=tuple(cs for _, cs in casts)),
        grid=(B // nb,),
        in_specs=in_specs + c_in,
        out_specs=out_specs + c_out,
        out_shape=out_shape + c_shape,
        scratch_shapes=scratch,
        compiler_params=_params(1),
        name=f"{name}{L}",
    )(*args, *c_args)


def _mlp_kernel(x_ref, yr_ref, yh_ref, modp_ref, modq_ref, n1_ref, n2_ref, fg_ref, wg_ref,
                wout_ref, wfi_hbm, wfo_hbm, *rest, n_tiles, n_anchor):
    y_ref, x1_scr, h2_scr, wfi_ref, wfo_ref, sem = rest[n_anchor:]
    i = pl.program_id(0)
    wr = i % 2
    rd = 1 - wr
    nq = N_GATE // 4

    def pre_stages():
        mp = modp_ref[0]
        st = {}

        def p1():
            st["x"] = x_ref[...]
            st["hn"] = _modnorm(st["x"], n1_ref[...], _mod(mp, 1), _mod(mp, 0)).astype(BF16)

        def p2(q):
            def f():
                st["g%d" % q] = _dot(st["hn"], wg_ref[:, q * nq:(q + 1) * nq])
            return f

        def p3(h):
            def f():
                cs = slice(h * nq, (h + 1) * nq)
                st["mix%d" % h] = (jax.nn.sigmoid(st["g%d" % h]) * yr_ref[:, cs]
                                   + jax.nn.sigmoid(st["g%d" % (2 + h)]) * yh_ref[:, cs]
                                   ).astype(BF16)
            return f

        def p4():
            upd = (_dot(st["mix0"], wout_ref[0:nq, :]) + _dot(st["mix1"], wout_ref[nq:2 * nq, :]))
            st["x1"] = st["x"] + _mod(mp, 2) * upd

        def p5():
            x1_scr[wr] = st["x1"]
            h2_scr[wr] = _modnorm(st["x1"], n2_ref[...], _mod(mp, 4), _mod(mp, 3)).astype(BF16)

        return [p1, p2(0), p2(1), p2(2), p2(3), p3(0), p3(1), p4, p5]

    def ffn_stages():
        mq = modq_ref[0]
        st = {"acc": None}

        def f(j):
            def g():
                cs = slice(j * FF_CHUNK, (j + 1) * FF_CHUNK)
                h2 = h2_scr[rd]
                a = _dot(h2, wfi_ref[:, cs])
                b = _dot(h2, wfi_ref[:, D_FF + j * FF_CHUNK:D_FF + (j + 1) * FF_CHUNK])
                ff = (a * jax.nn.sigmoid(a) * b).astype(BF16)
                part = _dot(ff, wfo_ref[cs, :])
                st["acc"] = part if st["acc"] is None else st["acc"] + part
            return g

        def e():
            x2 = x1_scr[rd] + _mod(mq, 5) * st["acc"]
            ms = jnp.mean(x2 * x2, axis=-1, keepdims=True)
            y_ref[...] = x2 * lax.rsqrt(ms + EPS) * fg_ref[...]

        return [f(j) for j in range(D_FF // FF_CHUNK)] + [e]

    @pl.when(i == 0)
    def _():
        copies = [pltpu.make_async_copy(wfi_hbm, wfi_ref, sem.at[0]),
                  pltpu.make_async_copy(wfo_hbm, wfo_ref, sem.at[1])]
        for cp in copies:
            cp.start()
        for stage in pre_stages():
            stage()
        for cp in copies:
            cp.wait()

    @pl.when(jnp.logical_and(i > 0, i < n_tiles))
    def _():
        pre, ffn = pre_stages(), ffn_stages()
        order = []
        while pre or ffn:
            if ffn:
                order.append(ffn.pop(0))
            if pre:
                order.append(pre.pop(0))
        for stage in order:
            stage()

    @pl.when(i == n_tiles)
    def _():
        for stage in ffn_stages():
            stage()


def _mlp(x, y_ret, y_hy, mods, mod_row, norm1, norm2, final_g, w_gate, w_out, w_fi, w_fo,
         after=None):
    B, L, D = x.shape
    T = MLP_ROWS
    n_tiles = B * L // T
    flat = lambda a: a.reshape(B * L, D)
    pre_tile = lambda i: jnp.minimum(i, n_tiles - 1)
    post_tile = lambda i: jnp.maximum(i - 1, 0)
    act = pl.BlockSpec((T, D), lambda i: (pre_tile(i), 0))
    anchor = [] if after is None else [after]
    y = pl.pallas_call(
        functools.partial(_mlp_kernel, n_tiles=n_tiles, n_anchor=len(anchor)),
        grid=(n_tiles + 1,),
        in_specs=[act, act, act,
                  pl.BlockSpec((1, 1, N_MOD * D),
                               lambda i: (mod_row((pre_tile(i) * T) // L), 0, 0)),
                  pl.BlockSpec((1, 1, N_MOD * D),
                               lambda i: (mod_row((post_tile(i) * T) // L), 0, 0)),
                  _const_spec((1, D)), _const_spec((1, D)), _const_spec((1, D)),
                  _const_spec((D, N_GATE)),
                  _const_spec((D, D)),
                  pl.BlockSpec(memory_space=pl.ANY),
                  pl.BlockSpec(memory_space=pl.ANY)]
        + [pl.BlockSpec(memory_space=pl.ANY)] * len(anchor),
        out_specs=pl.BlockSpec((T, D), lambda i: (post_tile(i), 0)),
        out_shape=jax.ShapeDtypeStruct((B * L, D), F32),
        scratch_shapes=[pltpu.VMEM((2, T, D), F32), pltpu.VMEM((2, T, D), BF16),
                        pltpu.VMEM((D, 2 * D_FF), BF16), pltpu.VMEM((D_FF, D), BF16),
                        pltpu.SemaphoreType.DMA((2,))],
        compiler_params=_params(1),
        name=f"mlp{L}",
    )(flat(x), flat(y_ret), flat(y_hy), mods, mods, norm1, norm2, final_g, w_gate, w_out, w_fi,
      w_fo, *anchor)
    return y.reshape(B, L, D)


def kernel(x_prompt, x_sample, state_ret_fwd, state_ret_bwd, c, c_ctx, norm1_g, norm2_g, w_ada,
           b_ada, w_in, ret_decay_fwd, ret_decay_bwd, hy_conv_w, hy_conv_b, hy_pos_w1, hy_pos_b1,
           hy_pos_w2, hy_pos_b2, hy_pos_w3, hy_sin_freq, hy_bias, w_ret_o, w_hy_o, w_out,
           w_ffn_in, w_ffn_out, final_g):
    assert w_in.shape[0] == 1, "single-layer configuration"
    nb_lat = x_sample.shape[0]
    l_ctx = x_prompt.shape[1]

    assert 1 + nb_lat <= MOD_ROWS
    norm1 = norm1_g[0][None, :]
    norm2 = norm2_g[0][None, :]
    fg = final_g[None, :]
    conv_w = hy_conv_w[0].reshape(1, 3 * N_HY)
    conv_b = hy_conv_b[0][None, :]
    filt_params = (hy_pos_w1[0], hy_pos_b1, hy_pos_w2[0], hy_pos_b2, hy_pos_w3[0], hy_sin_freq)

    n_hy_end = N_QKVG + N_HY
    w_in_parts = (slice(0, N_QKVG), slice(N_QKVG, n_hy_end), slice(n_hy_end, N_IN))
    groups = []
    for L in (l_ctx, x_sample.shape[1]):
        blk = min(HY_TBLK, L)
        fw, bw, sgn = _dft_mats(blk)
        z, tdel = _filter_consts(L)
        groups.append((L, blk, jnp.asarray(z.T), jnp.asarray(tdel), jnp.asarray(sgn), fw))
    (mods, fa_c, fb_c, fd_c, fa_l, fb_l, fd_l, w_qkvg, w_hy, w_gate, w_ret_o_b,
     w_hy_o_b) = _ada(c_ctx[None, :], c, w_ada[0], b_ada, filt_params, groups,
                      casts=[(w_in[0], w_in_parts), (w_ret_o[0], None), (w_hy_o[0], None)])

    def branches(x, filt, s0f, s0b, grid_w, emit_state):
        blk = min(HY_TBLK, x.shape[1])
        fw, bw, _ = _dft_mats(blk)
        ret = (w_qkvg, ret_decay_fwd[0], ret_decay_bwd[0], s0f, s0b, w_ret_o_b, emit_state)
        hy = (w_hy, conv_w, conv_b, fw, bw, filt, hy_bias[0], w_hy_o_b, grid_w, blk)
        return ret, hy

    ctx_row = lambda b: 0
    lat_row = lambda b: b + 1
    ret, hy = branches(x_prompt, (fa_c, fb_c, fd_c), None, None, l_ctx, True)
    y_ret_c, s_f, s_b, y_hy_c, w_fi_b, w_fo_b, w_out_b = _mixer(
        x_prompt, mods, ctx_row, norm1, nb=CTX_SEQS, ret=ret, hy=hy,
        casts=[(w_ffn_in[0], None), (w_ffn_out[0], None), (w_out[0], None)])
    y_prompt = _mlp(x_prompt, y_ret_c, y_hy_c, mods, ctx_row, norm1, norm2, fg, w_gate, w_out_b,
                    w_fi_b, w_fo_b)
    ret, hy = branches(x_sample, (fa_l, fb_l, fd_l), state_ret_fwd[:, 0], state_ret_bwd[:, 0],
                       GRID_W, False)
    y_ret_l, = _mixer(x_sample, mods, lat_row, norm1, nb=1, ret=ret)
    y_hy_l, = _mixer(x_sample, mods, lat_row, norm1, nb=1, hy=hy)
    y_sample = _mlp(x_sample, y_ret_l, y_hy_l, mods, lat_row, norm1, norm2, fg, w_gate,
                    w_out_b, w_fi_b, w_fo_b, after=y_prompt)
    return (y_prompt, y_sample, s_f[:, None], s_b[:, None])
```

```python
import functools
import math

import numpy as np
import jax
import jax.numpy as jnp
from jax import lax
from jax.experimental import pallas as pl
from jax.experimental.pallas import tpu as pltpu

F32 = jnp.float32
BF16 = jnp.bfloat16

D_MODEL = 1024
RET_HEADS = 4
HEAD_DIM = 128
RET_W = RET_HEADS * HEAD_DIM
HY_W = 512
HY_ORDER = 2
HY_BANDS = 16
HY_EMB = 1 + 2 * HY_BANDS
HY_EMB_PAD = 40
HY_HIDDEN = 64
HY_FAST_DECAY = 0.3
HY_SLOW_DECAY = 1.5
HY_TARGET = 1e-2
D_FF = 2816
N_QKVG = 4 * RET_W
N_HY = 3 * HY_W
N_GATE = 2 * D_MODEL
N_IN = N_QKVG + N_HY + N_GATE
N_MOD = 6
MOD_ROWS = 8
EPS = 1e-6
GRID_W = 64
RET_CHUNK = 256
HY_CBLK = 256
HY_TBLK = 512
CTX_SEQS = 2
MLP_ROWS = 512
MLP_LAG = 2
FF_CHUNK = 256
FILTER_ONE_STEP_LEN = 256
ADA_COLS = 768
VMEM_LIMIT = 56 * 1024 * 1024


def _const_spec(shape):
    nd = len(shape)
    return pl.BlockSpec(shape, lambda *_: (0,) * nd, pipeline_mode=pl.Buffered(1))


def _params(n_axes):
    return pltpu.CompilerParams(dimension_semantics=("arbitrary",) * n_axes,
                                vmem_limit_bytes=VMEM_LIMIT)


def _modnorm(x, g, scale, shift):
    ms = jnp.mean(x * x, axis=-1, keepdims=True)
    return (x * lax.rsqrt(ms + EPS) * g) * (1.0 + scale) + shift


def _mod(mod, k):
    return mod[:, k * D_MODEL:(k + 1) * D_MODEL]


def _dot(a, b):
    return jnp.dot(a, b, preferred_element_type=F32)


def _cast_specs(casts, steps):
    in_specs, out_specs, out_shape, args = [], [], [], []
    for arr, col_slices in casts:
        rows, width = arr.shape
        rb = rows // steps
        assert rb * steps == rows and rb % 16 == 0
        in_specs.append(pl.BlockSpec((rb, width), lambda g: (g, 0)))
        args.append(arr)
        for cs in col_slices or (slice(0, width),):
            cols = cs.stop - cs.start
            out_specs.append(pl.BlockSpec((rb, cols), lambda g: (g, 0)))
            out_shape.append(jax.ShapeDtypeStruct((rows, cols), BF16))
    return in_specs, out_specs, out_shape, args


def _n_cast_outputs(col_slices_per_src):
    return sum(1 if s is None else len(s) for s in col_slices_per_src)


def _do_casts(col_slices_per_src, srcs, dsts):
    dsts = iter(dsts)
    for col_slices, src in zip(col_slices_per_src, srcs):
        if col_slices is None:
            next(dsts)[...] = src[...].astype(BF16)
        else:
            for cs in col_slices:
                next(dsts)[...] = src[:, cs].astype(BF16)


def _ada_kernel(*refs, casts, groups):
    it = iter(refs)
    cctx_ref, c_ref, w_ref, b_ref = next(it), next(it), next(it), next(it)
    mlp_refs = [next(it) for _ in range(6)]
    g_in = [[next(it) for _ in range(4)] for _ in groups]
    cast_srcs = [next(it) for _ in casts]
    o_ref = next(it)
    g_out = [[next(it) for _ in range(3)] for _ in groups]
    cast_dsts = [next(it) for _ in range(_n_cast_outputs(casts))]
    h_scr, cond_scr, w1_scr, w2_scr = next(it), next(it), next(it), next(it)
    _do_casts(casts, cast_srcs, cast_dsts)
    nlat = c_ref.shape[0]
    cond_scr[...] = jnp.zeros_like(cond_scr)
    cond_scr[0:1, :] = cctx_ref[...]
    cond_scr[1:1 + nlat, :] = c_ref[...]
    c = cond_scr[...]
    s = (c * jax.nn.sigmoid(c)).astype(BF16)
    res = _dot(s, w_ref[...].astype(BF16)) + b_ref[...]
    for r in range(res.shape[0]):
        o_ref[r] = res[r:r + 1, :]

    step = 0
    for (L, blk), (zt_ref, tdel_ref, sgn_ref, fw_ref), outs in zip(groups, g_in, g_out):
        def mlp_job(zt_ref=zt_ref, L=L):
            h_scr[0:L, :] = _filter_mlp(zt_ref, *mlp_refs, w1_scr, w2_scr)

        def order_job(o, tdel_ref=tdel_ref, sgn_ref=sgn_ref, fw_ref=fw_ref, outs=outs, L=L,
                      blk=blk):
            _filter_spectra(h_scr, o, tdel_ref, sgn_ref, fw_ref, *outs, L=L, b=blk)

        def all_job(mlp_job=mlp_job, order_job=order_job):
            mlp_job()
            for o in range(HY_ORDER):
                order_job(o)

        if L <= FILTER_ONE_STEP_LEN:
            jobs = [all_job]
        else:
            jobs = [mlp_job] + [functools.partial(order_job, o) for o in range(HY_ORDER)]
        for job in jobs:
            pl.when(pl.program_id(0) == step)(job)
            step += 1


def _ada(c_ctx, c, w, b, filt_params, groups, casts=()):
    n = w.shape[1]
    steps = n // ADA_COLS
    c_in, c_out, c_shape, c_args = _cast_specs(casts, steps)
    in_specs = [_const_spec(c_ctx.shape), _const_spec(c.shape),
                pl.BlockSpec((D_MODEL, ADA_COLS), lambda j: (0, j)),
                pl.BlockSpec((1, ADA_COLS), lambda j: (0, j))]
    args = [c_ctx, c, w, b]
    for p in filt_params:
        in_specs.append(_const_spec(p.shape))
        args.append(p)
    out_specs = [pl.BlockSpec((MOD_ROWS, 1, ADA_COLS), lambda j: (0, 0, j))]
    out_shape = [jax.ShapeDtypeStruct((MOD_ROWS, 1, n), F32)]
    max_len = 8
    for L, blk, *consts in groups:
        nd = 2 * (L // blk) - 1
        max_len = max(max_len, L)
        for cst in consts:
            in_specs.append(_const_spec(cst.shape))
            args.append(cst)
        for shp in ((HY_ORDER, nd, blk, HY_W), (HY_ORDER, nd, blk, HY_W), (HY_ORDER, nd, 8, HY_W)):
            out_specs.append(pl.BlockSpec(shp, lambda j: (0, 0, 0, 0)))
            out_shape.append(jax.ShapeDtypeStruct(shp, F32))
    n_jobs = sum(1 if L <= FILTER_ONE_STEP_LEN else 1 + HY_ORDER for L, *_ in groups)
    assert n_jobs <= steps
    return pl.pallas_call(
        functools.partial(_ada_kernel, casts=tuple(cs for _, cs in casts),
                          groups=tuple((L, blk) for L, blk, *_ in groups)),
        grid=(steps,),
        in_specs=in_specs + c_in,
        out_specs=out_specs + c_out,
        out_shape=out_shape + c_shape,
        scratch_shapes=[pltpu.VMEM((max_len, HY_ORDER * 2 * HY_W), F32),
                        pltpu.VMEM((MOD_ROWS, D_MODEL), F32),
                        pltpu.VMEM((HY_EMB_PAD, HY_HIDDEN), F32),
                        pltpu.VMEM((HY_HIDDEN + 8, HY_HIDDEN), F32)],
        compiler_params=_params(1),
        name="ada",
    )(*args, *c_args)


@functools.lru_cache(maxsize=None)
def _dft_mats(L):
    n = 2 * L
    t = np.arange(L, dtype=np.int64)
    f = np.arange(L, dtype=np.int64)
    ang = 2.0 * np.pi * ((f[:, None] * t[None, :]) % n).astype(np.float64) / n
    cos = np.cos(ang)
    sin = np.sin(ang)
    nyq = np.where(t % 2 == 0, 1.0, -1.0)
    fwd = np.concatenate([cos, -sin], axis=0)
    fwd[L] = nyq
    wre = np.full((L,), 2.0 / n)
    wre[0] = 1.0 / n
    inv = np.concatenate([cos.T * wre[None, :], -sin.T * (2.0 / n)], axis=1)
    inv[:, L] = nyq / n
    sgn = np.broadcast_to(nyq[:, None], (L, HY_W))
    return (jnp.asarray(fwd, dtype=BF16), jnp.asarray(inv, dtype=BF16),
            np.asarray(sgn, dtype=np.float32))


@functools.lru_cache(maxsize=None)
def _filter_consts(L):
    t = np.linspace(0.0, 1.0, L)[:, None]
    ang = 2.0 * np.pi * np.arange(L, dtype=np.float64)[:, None] / L
    bands = np.linspace(1e-4, HY_BANDS - 1, HY_BANDS)[None]
    z = np.concatenate([t, np.cos(bands * ang), -np.sin(bands * ang)], axis=-1)
    z = np.pad(z, ((0, 0), (0, HY_EMB_PAD - HY_EMB)))
    z[:, HY_EMB] = 1.0
    max_decay = math.log(HY_TARGET) / HY_FAST_DECAY
    min_decay = math.log(HY_TARGET) / HY_SLOW_DECAY
    deltas = np.linspace(min_decay, max_decay, HY_W)
    tdel = t * np.abs(deltas)[None, :]
    return np.asarray(z, np.float32), np.asarray(tdel, np.float32)


def _filter_mlp(zt_ref, w1_ref, b1_ref, w2_ref, b2_ref, w3_ref, fr_ref, w1_scr, w2_scr):
    hi = lax.Precision.HIGHEST
    tdims = (((0,), (0,)), ((), ()))
    w1_scr[...] = jnp.zeros_like(w1_scr)
    w1_scr[0:HY_EMB, :] = w1_ref[...]
    w1_scr[HY_EMB:HY_EMB + 1, :] = b1_ref[...]
    w2_scr[...] = jnp.zeros_like(w2_scr)
    w2_scr[0:HY_HIDDEN, :] = w2_ref[...]
    w2_scr[HY_HIDDEN:HY_HIDDEN + 1, :] = b2_ref[...]
    fr = fr_ref[...]
    zt = zt_ref[...]
    h1 = jnp.sin(lax.dot_general(w1_scr[...] * fr, zt, tdims, precision=hi,
                                 preferred_element_type=F32))
    h1 = jnp.concatenate([h1, jnp.ones((8, zt.shape[1]), F32)], axis=0)
    h2 = jnp.sin(lax.dot_general(w2_scr[...] * fr, h1, tdims, precision=hi,
                                 preferred_element_type=F32))
    h2_hi = h2.astype(BF16)
    h2_lo = (h2 - h2_hi.astype(F32)).astype(BF16)
    w3 = w3_ref[...]
    w3_hi = w3.astype(BF16)
    w3_lo = (w3 - w3_hi.astype(F32)).astype(BF16)
    lhs = jnp.concatenate([h2_hi, h2_lo, h2_hi, jnp.zeros_like(h2_hi)], axis=0)
    rhs = jnp.concatenate([w3_hi, w3_hi, w3_lo, jnp.zeros_like(w3_hi)], axis=0)
    h = lax.dot_general(lhs, rhs, tdims, preferred_element_type=F32)
    return h


def _filter_spectra(h_ref, o, tdel_ref, sgn_ref, fw_ref, oa_ref, ob_ref, od_ref, *, L, b):
    m = L // b
    win = jnp.exp(-tdel_ref[...])
    sg = sgn_ref[...]
    row0_l = lax.broadcasted_iota(jnp.int32, (L, HY_W), 0) == 0
    row0_b = lax.broadcasted_iota(jnp.int32, (b, HY_W), 0) == 0
    row0_8 = lax.broadcasted_iota(jnp.int32, (8, HY_W), 0) == 0
    base = o * 2 * HY_W
    fwd = h_ref[0:L, base:base + HY_W] * win
    bwd = jnp.where(row0_l, 0.0, h_ref[0:L, base + HY_W:base + 2 * HY_W] * win)
    nrm = (jnp.sum(jnp.abs(fwd), axis=0, keepdims=True)
           + jnp.sum(jnp.abs(bwd), axis=0, keepdims=True))
    inv = 1.0 / nrm
    fn = fwd * inv
    bn = bwd * inv
    xr, xn, xi, wr, wn, wi = [], [], [], [], [], []
    for r in range(m):
        p = _dot(fw_ref[...], fn[r * b:(r + 1) * b].astype(BF16))
        q = _dot(fw_ref[...], bn[r * b:(r + 1) * b].astype(BF16))
        xr.append(p[0:b])
        xn.append(p[b:b + 1])
        xi.append(jnp.where(row0_b, 0.0, p[b:2 * b]))
        wr.append(q[0:b])
        wn.append(q[b:b + 1])
        wi.append(jnp.where(row0_b, 0.0, -q[b:2 * b]))

    def emit(d, ka, kn, kb):
        oa_ref[o, d + m - 1] = ka
        ob_ref[o, d + m - 1] = kb
        od_ref[o, d + m - 1] = jnp.where(row0_8, kn, ka[0:8])

    emit(0, xr[0] + wr[0], xn[0] + wn[0], xi[0] + wi[0])
    for d in range(1, m):
        f0 = fn[(d - 1) * b:(d - 1) * b + 1]
        b0 = bn[(d - 1) * b:(d - 1) * b + 1]
        emit(d, xr[d] + sg * (xr[d - 1] - f0), xn[d] + (xn[d - 1] - f0),
             xi[d] + sg * xi[d - 1])
        emit(-d, wr[d] + sg * (wr[d - 1] - b0), wn[d] + (wn[d - 1] - b0),
             wi[d] + sg * wi[d - 1])


def _ret_init(decf_ref, decb_ref, mask_scr, vec_scr, cd_scr, C):
    H, E = RET_HEADS, HEAD_DIM
    scale = float(E) ** -0.5

    @pl.when(pl.program_id(0) == 0)
    def _():
        dec = jnp.concatenate([jnp.full((1, C), ref[h], F32)
                               for ref in (decf_ref, decb_ref) for h in range(H)], axis=0)
        lg = jnp.log(jax.nn.sigmoid(dec))
        cd_scr[...] = jnp.exp(float(C) * lg[:, 0:E])
        ii = lax.broadcasted_iota(jnp.int32, (C, C), 0)
        jj = lax.broadcasted_iota(jnp.int32, (C, C), 1)
        rel = (ii - jj).astype(F32)
        ri = lax.broadcasted_iota(jnp.int32, (C, E), 0).astype(F32)
        for h in range(H):
            lf = lg[h:h + 1, :]
            lb = lg[H + h:H + h + 1, :]
            mf = jnp.where(rel >= 0, jnp.exp(jnp.maximum(rel, 0.0) * lf), 0.0)
            mb = jnp.where(rel <= 0, jnp.exp(jnp.maximum(-rel, 0.0) * lb), 0.0)
            mask_scr[h] = scale * (mf + mb)
            lfe, lbe = lf[:, 0:E], lb[:, 0:E]
            vec_scr[h, 0] = jnp.exp((ri + 1.0) * lfe)
            vec_scr[h, 1] = jnp.exp((float(C) - ri) * lbe)
            vec_scr[h, 2] = scale * jnp.exp((float(C) - 1.0 - ri) * lfe)
            vec_scr[h, 3] = scale * jnp.exp(ri * lbe)


def _ret_core(hn, w_ref, s0f_ref, s0b_ref, wo_ref, y_ref, sf_ref, sb_ref, mask_scr, vec_scr,
              cd_scr, g_scr, *, L, C, nb, has_init, emit_state):
    n = L // C
    H, E = RET_HEADS, HEAD_DIM
    tdims = (((0,), (0,)), ((), ()))
    ndims = (((1,), (1,)), ((), ()))
    chains = [(s, h) for s in range(nb) for h in range(H)]
    rows = [slice(c * C, (c + 1) * C) for c in range(n)]
    qkvg = [_dot(hn[s], w_ref[...]) for s in range(nb)]

    def cols(s, part, h):
        return qkvg[s][:, part * RET_W + h * E:part * RET_W + (h + 1) * E]

    qb = [cols(s, 0, h).astype(BF16) for s, h in chains]
    kf = [cols(s, 1, h) for s, h in chains]
    kb = [k.astype(BF16) for k in kf]
    vb = [cols(s, 2, h).astype(BF16) for s, h in chains]
    att = [[lax.dot_general(qb[i][r], kb[i][r], ndims, preferred_element_type=F32) for r in rows]
           for i in range(len(chains))]
    prob = [[(att[i][c] * mask_scr[h]).astype(BF16) for c in range(n)]
            for i, (s, h) in enumerate(chains)]
    out = [[_dot(prob[i][c], vb[i][rows[c]]) for c in range(n)] for i in range(len(chains))]
    kv = []
    for i, (s, h) in enumerate(chains):
        dk2 = jnp.concatenate([vec_scr[h, 2], vec_scr[h, 3]], axis=1)
        per_c = []
        for r in rows:
            k2 = (jnp.concatenate([kf[i][r], kf[i][r]], axis=1) * dk2).astype(BF16)
            per_c.append(lax.dot_general(k2, vb[i][r], tdims, preferred_element_type=F32))
        kv.append(per_c)
    for i, (s, h) in enumerate(chains):
        cdf = cd_scr[h:h + 1, :]
        cdb = cd_scr[H + h:H + h + 1, :]
        sf_in, sb_in = [None] * n, [None] * n
        st = s0f_ref[s, h] if has_init else None
        for c in range(n):
            sf_in[c] = st
            kvc = kv[i][c][0:E]
            st = kvc if st is None else st * cdf + kvc
        if emit_state:
            sf_ref[s, h] = st
        st = s0b_ref[s, h] if has_init else None
        for c in range(n - 1, -1, -1):
            sb_in[c] = st
            kvc = kv[i][c][E:2 * E]
            st = kvc if st is None else st * cdb + kvc
        if emit_state:
            sb_ref[s, h] = st
        for c in range(n):
            if sf_in[c] is not None and sb_in[c] is not None:
                s2 = jnp.concatenate([sf_in[c], sb_in[c]], axis=1).astype(BF16)
                inter = _dot(qb[i][rows[c]], s2)
                out[i][c] = (out[i][c] + inter[:, 0:E] * vec_scr[h, 0]
                             + inter[:, E:2 * E] * vec_scr[h, 1])
            elif sf_in[c] is not None:
                out[i][c] = (out[i][c]
                             + _dot(qb[i][rows[c]], sf_in[c].astype(BF16)) * vec_scr[h, 0])
            elif sb_in[c] is not None:
                out[i][c] = (out[i][c]
                             + _dot(qb[i][rows[c]], sb_in[c].astype(BF16)) * vec_scr[h, 1])
    for i, (s, h) in enumerate(chains):
        for c in range(n):
            o = out[i][c]
            mu = jnp.mean(o, axis=-1, keepdims=True)
            d = o - mu
            var = jnp.mean(d * d, axis=-1, keepdims=True)
            on = d * lax.rsqrt(var + EPS)
            gg = cols(s, 3, h)[rows[c]]
            g_scr[s, rows[c], h * E:(h + 1) * E] = (gg * jax.nn.sigmoid(gg) * on).astype(BF16)
    for s in range(nb):
        y_ref[s] = _dot(g_scr[s], wo_ref[...])


def _hy_core(hn, w_ref, cw_ref, cb_ref, fw_ref, bw_ref, fa_ref, fb_ref, fd_ref, hb_ref, wo_ref,
             y_ref, *, L, W, b, nb):
    m = L // b
    CB = HY_CBLK
    nblk = HY_W // CB
    pos = lax.broadcasted_iota(jnp.int32, (L, CB), 0) % W
    first = pos == 0
    last = pos == W - 1
    chains = [(s, blk) for s in range(nb) for blk in range(nblk)]

    def short_conv(s, base, blk):
        cs = slice(base + blk * CB, base + (blk + 1) * CB)
        ug = _dot(hn[s], w_ref[:, cs])
        prev = jnp.where(first, 0.0, pltpu.roll(ug, 1, axis=0))
        nxt = jnp.where(last, 0.0, pltpu.roll(ug, L - 1, axis=0))
        taps = [cw_ref[:, t * N_HY + cs.start:t * N_HY + cs.stop] for t in range(3)]
        u = prev * taps[0] + ug * taps[1] + nxt * taps[2] + cb_ref[:, cs]
        return [u[j * b:(j + 1) * b] for j in range(m)]

    def long_conv(sigs, o):
        spec = [[_dot(fw_ref[...], sj.astype(BF16)) for sj in sig] for sig in sigs]
        prods = []
        for (s, blk), sp in zip(chains, spec):
            cs = slice(blk * CB, (blk + 1) * CB)
            per_i = []
            for i in range(m):
                yre = yim = yim8 = None
                for j in range(m):
                    d = i - j + m - 1
                    sre, sim = sp[j][0:b], sp[j][b:2 * b]
                    ka, kb = fa_ref[o, d, :, cs], fb_ref[o, d, :, cs]
                    tre = sre * ka - sim * kb
                    tim = sre * kb + sim * ka
                    t8 = sre[0:8] * kb[0:8] + sim[0:8] * fd_ref[o, d, :, cs]
                    yre = tre if yre is None else yre + tre
                    yim = tim if yim is None else yim + tim
                    yim8 = t8 if yim8 is None else yim8 + t8
                yim = jnp.concatenate([yim8, yim[8:]], axis=0)
                per_i.append((yre.astype(BF16), yim.astype(BF16)))
            prods.append(per_i)
        return [[_dot(bw_ref[:, 0:b], yre) + _dot(bw_ref[:, b:2 * b], yim) for yre, yim in per_i]
                for per_i in prods]

    hv = [short_conv(s, 0, blk) for s, blk in chains]
    hx1 = [short_conv(s, HY_W, blk) for s, blk in chains]
    hx2 = [short_conv(s, 2 * HY_W, blk) for s, blk in chains]

    def gate(hx, conv, sig, o):
        out = []
        for (s, blk), hxc, cc, sc in zip(chains, hx, conv, sig):
            bias = hb_ref[o:o + 1, blk * CB:(blk + 1) * CB]
            out.append([hxc[i] * (cc[i] + sc[i] * bias) for i in range(m)])
        return out

    z = gate(hx1, long_conv(hv, 0), hv, 0)
    z = gate(hx2, long_conv(z, 1), z, 1)
    for s in range(nb):
        for i in range(m):
            acc = None
            for blk in range(nblk):
                zc = z[chains.index((s, blk))][i].astype(BF16)
                part = _dot(zc, wo_ref[blk * CB:(blk + 1) * CB, :])
                acc = part if acc is None else acc + part
            y_ref[s, i * b:(i + 1) * b, :] = acc


def _mix_kernel(*refs, L, C, W, b, nb, do_ret, do_hy, has_init, emit_state, casts):
    it = iter(refs)
    x_ref, mod_ref, n1_ref = next(it), next(it), next(it)
    s0f_ref = s0b_ref = sf_ref = sb_ref = None
    if do_ret:
        wq_ref, decf_ref, decb_ref = next(it), next(it), next(it)
        if has_init:
            s0f_ref, s0b_ref = next(it), next(it)
        wo_ret_ref = next(it)
    if do_hy:
        hy_in = [next(it) for _ in range(10)]
    cast_srcs = [next(it) for _ in casts]
    if do_ret:
        y_ret_ref = next(it)
        if emit_state:
            sf_ref, sb_ref = next(it), next(it)
    if do_hy:
        y_hy_ref = next(it)
    cast_dsts = [next(it) for _ in range(_n_cast_outputs(casts))]
    if do_ret:
        ret_scr = [next(it) for _ in range(4)]
        _ret_init(decf_ref, decb_ref, ret_scr[0], ret_scr[1], ret_scr[2], C)
    _do_casts(casts, cast_srcs, cast_dsts)
    mod = mod_ref[0]
    hn = [_modnorm(x_ref[s], n1_ref[...], _mod(mod, 1), _mod(mod, 0)).astype(BF16)
          for s in range(nb)]
    if do_hy:
        _hy_core(hn, *hy_in, y_hy_ref, L=L, W=W, b=b, nb=nb)
    if do_ret:
        _ret_core(hn, wq_ref, s0f_ref, s0b_ref, wo_ret_ref, y_ret_ref, sf_ref, sb_ref, *ret_scr,
                  L=L, C=C, nb=nb, has_init=has_init, emit_state=emit_state)


def _mixer(x, mods, mod_row, norm1, *, nb, ret=None, hy=None, casts=()):
    B, L, D = x.shape
    H, E = RET_HEADS, HEAD_DIM
    C = min(RET_CHUNK, L)
    seq_spec = pl.BlockSpec((nb, L, D), lambda g: (g, 0, 0))
    in_specs = [seq_spec,
                pl.BlockSpec((1, 1, N_MOD * D), lambda g: (mod_row(g * nb), 0, 0)),
                _const_spec((1, D))]
    args = [x, mods, norm1]
    out_specs, out_shape, scratch = [], [], []
    has_init = emit_state = False
    W = b = None
    if ret is not None:
        w_qkvg, dec_f, dec_b, s0f, s0b, w_o, emit_state = ret
        has_init = s0f is not None
        st_spec = pl.BlockSpec((nb, H, E, E), lambda g: (g, 0, 0, 0))
        smem = pl.BlockSpec(memory_space=pltpu.SMEM)
        in_specs += [_const_spec((D, N_QKVG)), smem, smem]
        args += [w_qkvg, dec_f, dec_b]
        if has_init:
            in_specs += [st_spec, st_spec]
            args += [s0f, s0b]
        in_specs.append(_const_spec((RET_W, D)))
        args.append(w_o)
        out_specs.append(seq_spec)
        out_shape.append(jax.ShapeDtypeStruct((B, L, D), F32))
        if emit_state:
            out_specs += [st_spec, st_spec]
            out_shape += [jax.ShapeDtypeStruct((B, H, E, E), F32)] * 2
        scratch = [pltpu.VMEM((H, C, C), F32), pltpu.VMEM((H, 4, C, E), F32),
                   pltpu.VMEM((8, E), F32), pltpu.VMEM((nb, L, RET_W), BF16)]
    if hy is not None:
        w_hy, conv_w, conv_b, fw, bw, (fa, fb, fd), hy_bias, w_o, W, b = hy
        nd = fa.shape[1]
        in_specs += [_const_spec((D, N_HY)), _const_spec((1, 3 * N_HY)), _const_spec((1, N_HY)),
                     _const_spec((2 * b, b)), _const_spec((b, 2 * b)),
                     _const_spec((HY_ORDER, nd, b, HY_W)), _const_spec((HY_ORDER, nd, b, HY_W)),
                     _const_spec((HY_ORDER, nd, 8, HY_W)), _const_spec((HY_ORDER, HY_W)),
                     _const_spec((HY_W, D))]
        args += [w_hy, conv_w, conv_b, fw, bw, fa, fb, fd, hy_bias, w_o]
        out_specs.append(seq_spec)
        out_shape.append(jax.ShapeDtypeStruct((B, L, D), F32))
    c_in, c_out, c_shape, c_args = _cast_specs(casts, B // nb)
    name = ("ret" if ret is not None else "") + ("hy" if hy is not None else "")
    return pl.pallas_call(
        functools.partial(_mix_kernel, L=L, C=C, W=W, b=b, nb=nb, do_ret=ret is not None,
                          do_hy=hy is not None, has_init=has_init, emit_state=emit_state,
                          casts=tuple(cs for _, cs in casts)),
        grid=(B // nb,),
        in_specs=in_specs + c_in,
        out_specs=out_specs + c_out,
        out_shape=out_shape + c_shape,
        scratch_shapes=scratch,
        compiler_params=_params(1),
        name=f"{name}{L}",
    )(*args, *c_args)


def _mlp_kernel(x_ref, yr_ref, yh_ref, modp_ref, modq_ref, n1_ref, n2_ref, fg_ref, wg_ref,
                wout_ref, wfi_hbm, wfo_hbm, y_ref, x1_scr, h2_scr, wfi_ref, wfo_ref, sem, *,
                n_tiles):
    i = pl.program_id(0)
    slots = MLP_LAG + 1
    wr = i % slots
    rd = (i + 1) % slots
    nq = N_GATE // 4

    def pre_stages():
        mp = modp_ref[0]
        st = {}

        def p1():
            st["x"] = x_ref[...]
            st["hn"] = _modnorm(st["x"], n1_ref[...], _mod(mp, 1), _mod(mp, 0)).astype(BF16)

        def p2(q):
            def f():
                st["g%d" % q] = _dot(st["hn"], wg_ref[:, q * nq:(q + 1) * nq])
            return f

        def p3(h):
            def f():
                cs = slice(h * nq, (h + 1) * nq)
                st["mix%d" % h] = (jax.nn.sigmoid(st["g%d" % h]) * yr_ref[:, cs]
                                   + jax.nn.sigmoid(st["g%d" % (2 + h)]) * yh_ref[:, cs]
                                   ).astype(BF16)
            return f

        def p4():
            upd = (_dot(st["mix0"], wout_ref[0:nq, :]) + _dot(st["mix1"], wout_ref[nq:2 * nq, :]))
            st["x1"] = st["x"] + _mod(mp, 2) * upd

        def p5():
            x1_scr[wr] = st["x1"]
            h2_scr[wr] = _modnorm(st["x1"], n2_ref[...], _mod(mp, 4), _mod(mp, 3)).astype(BF16)

        return [p1, p2(0), p2(1), p2(2), p2(3), p3(0), p3(1), p4, p5]

    def ffn_stages():
        mq = modq_ref[0]
        st = {"acc": None}

        def f(j):
            def g():
                cs = slice(j * FF_CHUNK, (j + 1) * FF_CHUNK)
                h2 = h2_scr[rd]
                a = _dot(h2, wfi_ref[:, cs])
                b = _dot(h2, wfi_ref[:, D_FF + j * FF_CHUNK:D_FF + (j + 1) * FF_CHUNK])
                ff = (a * jax.nn.sigmoid(a) * b).astype(BF16)
                part = _dot(ff, wfo_ref[cs, :])
                st["acc"] = part if st["acc"] is None else st["acc"] + part
            return g

        def e():
            x2 = x1_scr[rd] + _mod(mq, 5) * st["acc"]
            ms = jnp.mean(x2 * x2, axis=-1, keepdims=True)
            y_ref[...] = x2 * lax.rsqrt(ms + EPS) * fg_ref[...]

        return [f(j) for j in range(D_FF // FF_CHUNK)] + [e]

    def ffn_weight_copies():
        return [pltpu.make_async_copy(wfi_hbm, wfi_ref, sem.at[0]),
                pltpu.make_async_copy(wfo_hbm, wfo_ref, sem.at[1])]

    @pl.when(i < MLP_LAG)
    def _():
        @pl.when(i == 0)
        def _():
            for cp in ffn_weight_copies():
                cp.start()

        for stage in pre_stages():
            stage()

        @pl.when(i == MLP_LAG - 1)
        def _():
            for cp in ffn_weight_copies():
                cp.wait()

    @pl.when(jnp.logical_and(i >= MLP_LAG, i < n_tiles))
    def _():
        pre, ffn = pre_stages(), ffn_stages()
        order = []
        while pre or ffn:
            if ffn:
                order.append(ffn.pop(0))
            if pre:
                order.append(pre.pop(0))
        for stage in order:
            stage()

    @pl.when(i >= n_tiles)
    def _():
        for stage in ffn_stages():
            stage()


def _mlp(x, y_ret, y_hy, mods, mod_row, norm1, norm2, final_g, w_gate, w_out, w_fi, w_fo):
    B, L, D = x.shape
    T = MLP_ROWS
    n_tiles = B * L // T
    flat = lambda a: a.reshape(B * L, D)
    pre_tile = lambda i: jnp.minimum(i, n_tiles - 1)
    post_tile = lambda i: jnp.clip(i - MLP_LAG, 0, n_tiles - 1)
    act = pl.BlockSpec((T, D), lambda i: (pre_tile(i), 0))
    y = pl.pallas_call(
        functools.partial(_mlp_kernel, n_tiles=n_tiles),
        grid=(n_tiles + MLP_LAG,),
        in_specs=[act, act, act,
                  pl.BlockSpec((1, 1, N_MOD * D),
                               lambda i: (mod_row((pre_tile(i) * T) // L), 0, 0)),
                  pl.BlockSpec((1, 1, N_MOD * D),
                               lambda i: (mod_row((post_tile(i) * T) // L), 0, 0)),
                  _const_spec((1, D)), _const_spec((1, D)), _const_spec((1, D)),
                  _const_spec((D, N_GATE)),
                  _const_spec((D, D)),
                  pl.BlockSpec(memory_space=pl.ANY),
                  pl.BlockSpec(memory_space=pl.ANY)],
        out_specs=pl.BlockSpec((T, D), lambda i: (post_tile(i), 0)),
        out_shape=jax.ShapeDtypeStruct((B * L, D), F32),
        scratch_shapes=[pltpu.VMEM((MLP_LAG + 1, T, D), F32),
                        pltpu.VMEM((MLP_LAG + 1, T, D), BF16),
                        pltpu.VMEM((D, 2 * D_FF), BF16), pltpu.VMEM((D_FF, D), BF16),
                        pltpu.SemaphoreType.DMA((2,))],
        compiler_params=_params(1),
        name=f"mlp{L}",
    )(flat(x), flat(y_ret), flat(y_hy), mods, mods, norm1, norm2, final_g, w_gate, w_out, w_fi,
      w_fo)
    return y.reshape(B, L, D)


def kernel(x_prompt, x_sample, state_ret_fwd, state_ret_bwd, c, c_ctx, norm1_g, norm2_g, w_ada,
           b_ada, w_in, ret_decay_fwd, ret_decay_bwd, hy_conv_w, hy_conv_b, hy_pos_w1, hy_pos_b1,
           hy_pos_w2, hy_pos_b2, hy_pos_w3, hy_sin_freq, hy_bias, w_ret_o, w_hy_o, w_out,
           w_ffn_in, w_ffn_out, final_g):
    assert w_in.shape[0] == 1, "single-layer configuration"
    nb_lat = x_sample.shape[0]
    l_ctx = x_prompt.shape[1]

    assert 1 + nb_lat <= MOD_ROWS
    norm1 = norm1_g[0][None, :]
    norm2 = norm2_g[0][None, :]
    fg = final_g[None, :]
    conv_w = hy_conv_w[0].reshape(1, 3 * N_HY)
    conv_b = hy_conv_b[0][None, :]
    filt_params = (hy_pos_w1[0], hy_pos_b1, hy_pos_w2[0], hy_pos_b2, hy_pos_w3[0], hy_sin_freq)

    n_hy_end = N_QKVG + N_HY
    w_in_parts = (slice(0, N_QKVG), slice(N_QKVG, n_hy_end), slice(n_hy_end, N_IN))
    groups = []
    for L in (l_ctx, x_sample.shape[1]):
        blk = min(HY_TBLK, L)
        fw, bw, sgn = _dft_mats(blk)
        z, tdel = _filter_consts(L)
        groups.append((L, blk, jnp.asarray(z.T), jnp.asarray(tdel), jnp.asarray(sgn), fw))
    (mods, fa_c, fb_c, fd_c, fa_l, fb_l, fd_l, w_qkvg, w_hy, w_gate, w_ret_o_b,
     w_hy_o_b) = _ada(c_ctx[None, :], c, w_ada[0], b_ada, filt_params, groups,
                      casts=[(w_in[0], w_in_parts), (w_ret_o[0], None), (w_hy_o[0], None)])

    def branches(x, filt, s0f, s0b, grid_w, emit_state):
        blk = min(HY_TBLK, x.shape[1])
        fw, bw, _ = _dft_mats(blk)
        ret = (w_qkvg, ret_decay_fwd[0], ret_decay_bwd[0], s0f, s0b, w_ret_o_b, emit_state)
        hy = (w_hy, conv_w, conv_b, fw, bw, filt, hy_bias[0], w_hy_o_b, grid_w, blk)
        return ret, hy

    ctx_row = lambda b: 0
    lat_row = lambda b: b + 1
    ret, hy = branches(x_prompt, (fa_c, fb_c, fd_c), None, None, l_ctx, True)
    y_ret_c, s_f, s_b, y_hy_c, w_fi_b, w_fo_b, w_out_b = _mixer(
        x_prompt, mods, ctx_row, norm1, nb=CTX_SEQS, ret=ret, hy=hy,
        casts=[(w_ffn_in[0], None), (w_ffn_out[0], None), (w_out[0], None)])
    y_prompt = _mlp(x_prompt, y_ret_c, y_hy_c, mods, ctx_row, norm1, norm2, fg, w_gate, w_out_b,
                    w_fi_b, w_fo_b)
    ret, hy = branches(x_sample, (fa_l, fb_l, fd_l), state_ret_fwd[:, 0], state_ret_bwd[:, 0],
                       GRID_W, False)
    y_ret_l, = _mixer(x_sample, mods, lat_row, norm1, nb=1, ret=ret)
    y_hy_l, = _mixer(x_sample, mods, lat_row, norm1, nb=1, hy=hy)
    y_sample = _mlp(x_sample, y_ret_l, y_hy_l, mods, lat_row, norm1, norm2, fg, w_gate,
                    w_out_b, w_fi_b, w_fo_b)
    return (y_prompt, y_sample, s_f[:, None], s_b[:, None])
```

```python
import functools
import math

import numpy as np
import jax
import jax.numpy as jnp
from jax import lax
from jax.experimental import pallas as pl
from jax.experimental.pallas import tpu as pltpu

F32 = jnp.float32
BF16 = jnp.bfloat16

D_MODEL = 1024
RET_HEADS = 4
HEAD_DIM = 128
RET_W = RET_HEADS * HEAD_DIM
HY_W = 512
HY_ORDER = 2
HY_BANDS = 16
HY_EMB = 1 + 2 * HY_BANDS
HY_EMB_PAD = 40
HY_HIDDEN = 64
HY_FAST_DECAY = 0.3
HY_SLOW_DECAY = 1.5
HY_TARGET = 1e-2
D_FF = 2816
N_QKVG = 4 * RET_W
N_HY = 3 * HY_W
N_GATE = 2 * D_MODEL
N_IN = N_QKVG + N_HY + N_GATE
N_MOD = 6
MOD_ROWS = 8
EPS = 1e-6
GRID_W = 64
RET_CHUNK = 256
HY_CBLK = 256
HY_TBLK = 512
CTX_SEQS = 2
MLP_ROWS = 512
FF_CHUNK = 256
FILTER_ONE_STEP_LEN = 256
ADA_COLS = 768
VMEM_LIMIT = 56 * 1024 * 1024
VMEM_LIMIT_ALT = 60 * 1024 * 1024


def _const_spec(shape):
    nd = len(shape)
    return pl.BlockSpec(shape, lambda *_: (0,) * nd, pipeline_mode=pl.Buffered(1))


def _params(n_axes, vmem_limit=VMEM_LIMIT):
    return pltpu.CompilerParams(dimension_semantics=("arbitrary",) * n_axes,
                                vmem_limit_bytes=vmem_limit)


def _modnorm(x, g, scale, shift):
    ms = jnp.mean(x * x, axis=-1, keepdims=True)
    return (x * lax.rsqrt(ms + EPS) * g) * (1.0 + scale) + shift


def _mod(mod, k):
    return mod[:, k * D_MODEL:(k + 1) * D_MODEL]


def _dot(a, b):
    return jnp.dot(a, b, preferred_element_type=F32)


def _cast_specs(casts, steps):
    in_specs, out_specs, out_shape, args = [], [], [], []
    for arr, col_slices in casts:
        rows, width = arr.shape
        rb = rows // steps
        assert rb * steps == rows and rb % 16 == 0
        in_specs.append(pl.BlockSpec((rb, width), lambda g: (g, 0)))
        args.append(arr)
        for cs in col_slices or (slice(0, width),):
            cols = cs.stop - cs.start
            out_specs.append(pl.BlockSpec((rb, cols), lambda g: (g, 0)))
            out_shape.append(jax.ShapeDtypeStruct((rows, cols), BF16))
    return in_specs, out_specs, out_shape, args


def _n_cast_outputs(col_slices_per_src):
    return sum(1 if s is None else len(s) for s in col_slices_per_src)


def _do_casts(col_slices_per_src, srcs, dsts):
    dsts = iter(dsts)
    for col_slices, src in zip(col_slices_per_src, srcs):
        if col_slices is None:
            next(dsts)[...] = src[...].astype(BF16)
        else:
            for cs in col_slices:
                next(dsts)[...] = src[:, cs].astype(BF16)


def _ada_kernel(*refs, casts, groups):
    it = iter(refs)
    cctx_ref, c_ref, w_ref, b_ref = next(it), next(it), next(it), next(it)
    mlp_refs = [next(it) for _ in range(6)]
    g_in = [[next(it) for _ in range(4)] for _ in groups]
    cast_srcs = [next(it) for _ in casts]
    o_ref = next(it)
    g_out = [[next(it) for _ in range(3)] for _ in groups]
    cast_dsts = [next(it) for _ in range(_n_cast_outputs(casts))]
    h_scr, cond_scr, w1_scr, w2_scr = next(it), next(it), next(it), next(it)
    _do_casts(casts, cast_srcs, cast_dsts)
    nlat = c_ref.shape[0]
    cond_scr[...] = jnp.zeros_like(cond_scr)
    cond_scr[0:1, :] = cctx_ref[...]
    cond_scr[1:1 + nlat, :] = c_ref[...]
    c = cond_scr[...]
    s = (c * jax.nn.sigmoid(c)).astype(BF16)
    res = _dot(s, w_ref[...].astype(BF16)) + b_ref[...]
    for r in range(res.shape[0]):
        o_ref[r] = res[r:r + 1, :]

    step = 0
    for (L, blk), (zt_ref, tdel_ref, sgn_ref, fw_ref), outs in zip(groups, g_in, g_out):
        def mlp_job(zt_ref=zt_ref, L=L):
            h_scr[0:L, :] = _filter_mlp(zt_ref, *mlp_refs, w1_scr, w2_scr)

        def order_job(o, tdel_ref=tdel_ref, sgn_ref=sgn_ref, fw_ref=fw_ref, outs=outs, L=L,
                      blk=blk):
            _filter_spectra(h_scr, o, tdel_ref, sgn_ref, fw_ref, *outs, L=L, b=blk)

        def all_job(mlp_job=mlp_job, order_job=order_job):
            mlp_job()
            for o in range(HY_ORDER):
                order_job(o)

        if L <= FILTER_ONE_STEP_LEN:
            jobs = [all_job]
        else:
            jobs = [mlp_job] + [functools.partial(order_job, o) for o in range(HY_ORDER)]
        for job in jobs:
            pl.when(pl.program_id(0) == step)(job)
            step += 1


def _ada(c_ctx, c, w, b, filt_params, groups, casts=()):
    n = w.shape[1]
    steps = n // ADA_COLS
    c_in, c_out, c_shape, c_args = _cast_specs(casts, steps)
    in_specs = [_const_spec(c_ctx.shape), _const_spec(c.shape),
                pl.BlockSpec((D_MODEL, ADA_COLS), lambda j: (0, j)),
                pl.BlockSpec((1, ADA_COLS), lambda j: (0, j))]
    args = [c_ctx, c, w, b]
    for p in filt_params:
        in_specs.append(_const_spec(p.shape))
        args.append(p)
    out_specs = [pl.BlockSpec((MOD_ROWS, 1, ADA_COLS), lambda j: (0, 0, j))]
    out_shape = [jax.ShapeDtypeStruct((MOD_ROWS, 1, n), F32)]
    max_len = 8
    for L, blk, *consts in groups:
        nd = 2 * (L // blk) - 1
        max_len = max(max_len, L)
        for cst in consts:
            in_specs.append(_const_spec(cst.shape))
            args.append(cst)
        for shp in ((HY_ORDER, nd, blk, HY_W), (HY_ORDER, nd, blk, HY_W), (HY_ORDER, nd, 8, HY_W)):
            out_specs.append(pl.BlockSpec(shp, lambda j: (0, 0, 0, 0)))
            out_shape.append(jax.ShapeDtypeStruct(shp, F32))
    n_jobs = sum(1 if L <= FILTER_ONE_STEP_LEN else 1 + HY_ORDER for L, *_ in groups)
    assert n_jobs <= steps
    return pl.pallas_call(
        functools.partial(_ada_kernel, casts=tuple(cs for _, cs in casts),
                          groups=tuple((L, blk) for L, blk, *_ in groups)),
        grid=(steps,),
        in_specs=in_specs + c_in,
        out_specs=out_specs + c_out,
        out_shape=out_shape + c_shape,
        scratch_shapes=[pltpu.VMEM((max_len, HY_ORDER * 2 * HY_W), F32),
                        pltpu.VMEM((MOD_ROWS, D_MODEL), F32),
                        pltpu.VMEM((HY_EMB_PAD, HY_HIDDEN), F32),
                        pltpu.VMEM((HY_HIDDEN + 8, HY_HIDDEN), F32)],
        compiler_params=_params(1),
        name="ada",
    )(*args, *c_args)


@functools.lru_cache(maxsize=None)
def _dft_mats(L):
    n = 2 * L
    t = np.arange(L, dtype=np.int64)
    f = np.arange(L, dtype=np.int64)
    ang = 2.0 * np.pi * ((f[:, None] * t[None, :]) % n).astype(np.float64) / n
    cos = np.cos(ang)
    sin = np.sin(ang)
    nyq = np.where(t % 2 == 0, 1.0, -1.0)
    fwd = np.concatenate([cos, -sin], axis=0)
    fwd[L] = nyq
    wre = np.full((L,), 2.0 / n)
    wre[0] = 1.0 / n
    inv = np.concatenate([cos.T * wre[None, :], -sin.T * (2.0 / n)], axis=1)
    inv[:, L] = nyq / n
    sgn = np.broadcast_to(nyq[:, None], (L, HY_W))
    return (jnp.asarray(fwd, dtype=BF16), jnp.asarray(inv, dtype=BF16),
            np.asarray(sgn, dtype=np.float32))


@functools.lru_cache(maxsize=None)
def _filter_consts(L):
    t = np.linspace(0.0, 1.0, L)[:, None]
    ang = 2.0 * np.pi * np.arange(L, dtype=np.float64)[:, None] / L
    bands = np.linspace(1e-4, HY_BANDS - 1, HY_BANDS)[None]
    z = np.concatenate([t, np.cos(bands * ang), -np.sin(bands * ang)], axis=-1)
    z = np.pad(z, ((0, 0), (0, HY_EMB_PAD - HY_EMB)))
    z[:, HY_EMB] = 1.0
    max_decay = math.log(HY_TARGET) / HY_FAST_DECAY
    min_decay = math.log(HY_TARGET) / HY_SLOW_DECAY
    deltas = np.linspace(min_decay, max_decay, HY_W)
    tdel = t * np.abs(deltas)[None, :]
    return np.asarray(z, np.float32), np.asarray(tdel, np.float32)


def _filter_mlp(zt_ref, w1_ref, b1_ref, w2_ref, b2_ref, w3_ref, fr_ref, w1_scr, w2_scr):
    hi = lax.Precision.HIGHEST
    tdims = (((0,), (0,)), ((), ()))
    w1_scr[...] = jnp.zeros_like(w1_scr)
    w1_scr[0:HY_EMB, :] = w1_ref[...]
    w1_scr[HY_EMB:HY_EMB + 1, :] = b1_ref[...]
    w2_scr[...] = jnp.zeros_like(w2_scr)
    w2_scr[0:HY_HIDDEN, :] = w2_ref[...]
    w2_scr[HY_HIDDEN:HY_HIDDEN + 1, :] = b2_ref[...]
    fr = fr_ref[...]
    zt = zt_ref[...]
    h1 = jnp.sin(lax.dot_general(w1_scr[...] * fr, zt, tdims, precision=hi,
                                 preferred_element_type=F32))
    h1 = jnp.concatenate([h1, jnp.ones((8, zt.shape[1]), F32)], axis=0)
    h2 = jnp.sin(lax.dot_general(w2_scr[...] * fr, h1, tdims, precision=hi,
                                 preferred_element_type=F32))
    h2_hi = h2.astype(BF16)
    h2_lo = (h2 - h2_hi.astype(F32)).astype(BF16)
    w3 = w3_ref[...]
    w3_hi = w3.astype(BF16)
    w3_lo = (w3 - w3_hi.astype(F32)).astype(BF16)
    lhs = jnp.concatenate([h2_hi, h2_lo, h2_hi, jnp.zeros_like(h2_hi)], axis=0)
    rhs = jnp.concatenate([w3_hi, w3_hi, w3_lo, jnp.zeros_like(w3_hi)], axis=0)
    h = lax.dot_general(lhs, rhs, tdims, preferred_element_type=F32)
    return h


def _filter_spectra(h_ref, o, tdel_ref, sgn_ref, fw_ref, oa_ref, ob_ref, od_ref, *, L, b):
    m = L // b
    win = jnp.exp(-tdel_ref[...])
    sg = sgn_ref[...]
    row0_l = lax.broadcasted_iota(jnp.int32, (L, HY_W), 0) == 0
    row0_b = lax.broadcasted_iota(jnp.int32, (b, HY_W), 0) == 0
    row0_8 = lax.broadcasted_iota(jnp.int32, (8, HY_W), 0) == 0
    base = o * 2 * HY_W
    fwd = h_ref[0:L, base:base + HY_W] * win
    bwd = jnp.where(row0_l, 0.0, h_ref[0:L, base + HY_W:base + 2 * HY_W] * win)
    nrm = (jnp.sum(jnp.abs(fwd), axis=0, keepdims=True)
           + jnp.sum(jnp.abs(bwd), axis=0, keepdims=True))
    inv = 1.0 / nrm
    fn = fwd * inv
    bn = bwd * inv
    xr, xn, xi, wr, wn, wi = [], [], [], [], [], []
    for r in range(m):
        p = _dot(fw_ref[...], fn[r * b:(r + 1) * b].astype(BF16))
        q = _dot(fw_ref[...], bn[r * b:(r + 1) * b].astype(BF16))
        xr.append(p[0:b])
        xn.append(p[b:b + 1])
        xi.append(jnp.where(row0_b, 0.0, p[b:2 * b]))
        wr.append(q[0:b])
        wn.append(q[b:b + 1])
        wi.append(jnp.where(row0_b, 0.0, -q[b:2 * b]))

    def emit(d, ka, kn, kb):
        oa_ref[o, d + m - 1] = ka
        ob_ref[o, d + m - 1] = kb
        od_ref[o, d + m - 1] = jnp.where(row0_8, kn, ka[0:8])

    emit(0, xr[0] + wr[0], xn[0] + wn[0], xi[0] + wi[0])
    for d in range(1, m):
        f0 = fn[(d - 1) * b:(d - 1) * b + 1]
        b0 = bn[(d - 1) * b:(d - 1) * b + 1]
        emit(d, xr[d] + sg * (xr[d - 1] - f0), xn[d] + (xn[d - 1] - f0),
             xi[d] + sg * xi[d - 1])
        emit(-d, wr[d] + sg * (wr[d - 1] - b0), wn[d] + (wn[d - 1] - b0),
             wi[d] + sg * wi[d - 1])


def _ret_init(decf_ref, decb_ref, mask_scr, vec_scr, cd_scr, C):
    H, E = RET_HEADS, HEAD_DIM
    scale = float(E) ** -0.5

    @pl.when(pl.program_id(0) == 0)
    def _():
        dec = jnp.concatenate([jnp.full((1, C), ref[h], F32)
                               for ref in (decf_ref, decb_ref) for h in range(H)], axis=0)
        lg = jnp.log(jax.nn.sigmoid(dec))
        cd_scr[...] = jnp.exp(float(C) * lg[:, 0:E])
        ii = lax.broadcasted_iota(jnp.int32, (C, C), 0)
        jj = lax.broadcasted_iota(jnp.int32, (C, C), 1)
        rel = (ii - jj).astype(F32)
        ri = lax.broadcasted_iota(jnp.int32, (C, E), 0).astype(F32)
        for h in range(H):
            lf = lg[h:h + 1, :]
            lb = lg[H + h:H + h + 1, :]
            mf = jnp.where(rel >= 0, jnp.exp(jnp.maximum(rel, 0.0) * lf), 0.0)
            mb = jnp.where(rel <= 0, jnp.exp(jnp.maximum(-rel, 0.0) * lb), 0.0)
            mask_scr[h] = scale * (mf + mb)
            lfe, lbe = lf[:, 0:E], lb[:, 0:E]
            vec_scr[h, 0] = jnp.exp((ri + 1.0) * lfe)
            vec_scr[h, 1] = jnp.exp((float(C) - ri) * lbe)
            vec_scr[h, 2] = scale * jnp.exp((float(C) - 1.0 - ri) * lfe)
            vec_scr[h, 3] = scale * jnp.exp(ri * lbe)


def _ret_core(hn, w_ref, s0f_ref, s0b_ref, wo_ref, y_ref, sf_ref, sb_ref, mask_scr, vec_scr,
              cd_scr, g_scr, *, L, C, nb, has_init, emit_state):
    n = L // C
    H, E = RET_HEADS, HEAD_DIM
    tdims = (((0,), (0,)), ((), ()))
    ndims = (((1,), (1,)), ((), ()))
    chains = [(s, h) for s in range(nb) for h in range(H)]
    rows = [slice(c * C, (c + 1) * C) for c in range(n)]
    qkvg = [_dot(hn[s], w_ref[...]) for s in range(nb)]

    def cols(s, part, h):
        return qkvg[s][:, part * RET_W + h * E:part * RET_W + (h + 1) * E]

    qb = [cols(s, 0, h).astype(BF16) for s, h in chains]
    kf = [cols(s, 1, h) for s, h in chains]
    kb = [k.astype(BF16) for k in kf]
    vb = [cols(s, 2, h).astype(BF16) for s, h in chains]
    att = [[lax.dot_general(qb[i][r], kb[i][r], ndims, preferred_element_type=F32) for r in rows]
           for i in range(len(chains))]
    prob = [[(att[i][c] * mask_scr[h]).astype(BF16) for c in range(n)]
            for i, (s, h) in enumerate(chains)]
    out = [[_dot(prob[i][c], vb[i][rows[c]]) for c in range(n)] for i in range(len(chains))]
    kv = []
    for i, (s, h) in enumerate(chains):
        dk2 = jnp.concatenate([vec_scr[h, 2], vec_scr[h, 3]], axis=1)
        per_c = []
        for r in rows:
            k2 = (jnp.concatenate([kf[i][r], kf[i][r]], axis=1) * dk2).astype(BF16)
            per_c.append(lax.dot_general(k2, vb[i][r], tdims, preferred_element_type=F32))
        kv.append(per_c)
    for i, (s, h) in enumerate(chains):
        cdf = cd_scr[h:h + 1, :]
        cdb = cd_scr[H + h:H + h + 1, :]
        sf_in, sb_in = [None] * n, [None] * n
        st = s0f_ref[s, h] if has_init else None
        for c in range(n):
            sf_in[c] = st
            kvc = kv[i][c][0:E]
            st = kvc if st is None else st * cdf + kvc
        if emit_state:
            sf_ref[s, h] = st
        st = s0b_ref[s, h] if has_init else None
        for c in range(n - 1, -1, -1):
            sb_in[c] = st
            kvc = kv[i][c][E:2 * E]
            st = kvc if st is None else st * cdb + kvc
        if emit_state:
            sb_ref[s, h] = st
        for c in range(n):
            if sf_in[c] is not None and sb_in[c] is not None:
                s2 = jnp.concatenate([sf_in[c], sb_in[c]], axis=1).astype(BF16)
                inter = _dot(qb[i][rows[c]], s2)
                out[i][c] = (out[i][c] + inter[:, 0:E] * vec_scr[h, 0]
                             + inter[:, E:2 * E] * vec_scr[h, 1])
            elif sf_in[c] is not None:
                out[i][c] = (out[i][c]
                             + _dot(qb[i][rows[c]], sf_in[c].astype(BF16)) * vec_scr[h, 0])
            elif sb_in[c] is not None:
                out[i][c] = (out[i][c]
                             + _dot(qb[i][rows[c]], sb_in[c].astype(BF16)) * vec_scr[h, 1])
    for i, (s, h) in enumerate(chains):
        for c in range(n):
            o = out[i][c]
            mu = jnp.mean(o, axis=-1, keepdims=True)
            d = o - mu
            var = jnp.mean(d * d, axis=-1, keepdims=True)
            on = d * lax.rsqrt(var + EPS)
            gg = cols(s, 3, h)[rows[c]]
            g_scr[s, rows[c], h * E:(h + 1) * E] = (gg * jax.nn.sigmoid(gg) * on).astype(BF16)
    for s in range(nb):
        y_ref[s] = _dot(g_scr[s], wo_ref[...])


def _hy_core(hn, w_ref, cw_ref, cb_ref, fw_ref, bw_ref, fa_ref, fb_ref, fd_ref, hb_ref, wo_ref,
             y_ref, *, L, W, b, nb):
    m = L // b
    CB = HY_CBLK
    nblk = HY_W // CB
    pos = lax.broadcasted_iota(jnp.int32, (L, CB), 0) % W
    first = pos == 0
    last = pos == W - 1
    chains = [(s, blk) for s in range(nb) for blk in range(nblk)]

    def short_conv(s, base, blk):
        cs = slice(base + blk * CB, base + (blk + 1) * CB)
        ug = _dot(hn[s], w_ref[:, cs])
        prev = jnp.where(first, 0.0, pltpu.roll(ug, 1, axis=0))
        nxt = jnp.where(last, 0.0, pltpu.roll(ug, L - 1, axis=0))
        taps = [cw_ref[:, t * N_HY + cs.start:t * N_HY + cs.stop] for t in range(3)]
        u = prev * taps[0] + ug * taps[1] + nxt * taps[2] + cb_ref[:, cs]
        return [u[j * b:(j + 1) * b] for j in range(m)]

    def long_conv(sigs, o):
        spec = [[_dot(fw_ref[...], sj.astype(BF16)) for sj in sig] for sig in sigs]
        prods = []
        for (s, blk), sp in zip(chains, spec):
            cs = slice(blk * CB, (blk + 1) * CB)
            per_i = []
            for i in range(m):
                yre = yim = yim8 = None
                for j in range(m):
                    d = i - j + m - 1
                    sre, sim = sp[j][0:b], sp[j][b:2 * b]
                    ka, kb = fa_ref[o, d, :, cs], fb_ref[o, d, :, cs]
                    tre = sre * ka - sim * kb
                    tim = sre * kb + sim * ka
                    t8 = sre[0:8] * kb[0:8] + sim[0:8] * fd_ref[o, d, :, cs]
                    yre = tre if yre is None else yre + tre
                    yim = tim if yim is None else yim + tim
                    yim8 = t8 if yim8 is None else yim8 + t8
                yim = jnp.concatenate([yim8, yim[8:]], axis=0)
                per_i.append((yre.astype(BF16), yim.astype(BF16)))
            prods.append(per_i)
        return [[_dot(bw_ref[:, 0:b], yre) + _dot(bw_ref[:, b:2 * b], yim) for yre, yim in per_i]
                for per_i in prods]

    hv = [short_conv(s, 0, blk) for s, blk in chains]
    hx1 = [short_conv(s, HY_W, blk) for s, blk in chains]
    hx2 = [short_conv(s, 2 * HY_W, blk) for s, blk in chains]

    def gate(hx, conv, sig, o):
        out = []
        for (s, blk), hxc, cc, sc in zip(chains, hx, conv, sig):
            bias = hb_ref[o:o + 1, blk * CB:(blk + 1) * CB]
            out.append([hxc[i] * (cc[i] + sc[i] * bias) for i in range(m)])
        return out

    z = gate(hx1, long_conv(hv, 0), hv, 0)
    z = gate(hx2, long_conv(z, 1), z, 1)
    for s in range(nb):
        for i in range(m):
            acc = None
            for blk in range(nblk):
                zc = z[chains.index((s, blk))][i].astype(BF16)
                part = _dot(zc, wo_ref[blk * CB:(blk + 1) * CB, :])
                acc = part if acc is None else acc + part
            y_ref[s, i * b:(i + 1) * b, :] = acc


def _mix_kernel(*refs, L, C, W, b, nb, do_ret, do_hy, has_init, emit_state, casts, alternate):
    it = iter(refs)
    x_ref, mod_ref, n1_ref = next(it), next(it), next(it)
    s0f_ref = s0b_ref = sf_ref = sb_ref = None
    if do_ret:
        wq_ref, decf_ref, decb_ref = next(it), next(it), next(it)
        if has_init:
            s0f_ref, s0b_ref = next(it), next(it)
        wo_ret_ref = next(it)
    if do_hy:
        hy_in = [next(it) for _ in range(10)]
    cast_srcs = [next(it) for _ in casts]
    if do_ret:
        y_ret_ref = next(it)
        if emit_state:
            sf_ref, sb_ref = next(it), next(it)
    if do_hy:
        y_hy_ref = next(it)
    cast_dsts = [next(it) for _ in range(_n_cast_outputs(casts))]
    if do_ret:
        ret_scr = [next(it) for _ in range(4)]
        _ret_init(decf_ref, decb_ref, ret_scr[0], ret_scr[1], ret_scr[2], C)
    _do_casts(casts, cast_srcs, cast_dsts)
    mod = mod_ref[0]

    def normed():
        return [_modnorm(x_ref[s], n1_ref[...], _mod(mod, 1), _mod(mod, 0)).astype(BF16)
                for s in range(nb)]

    def run_hy(hn):
        _hy_core(hn, *hy_in, y_hy_ref, L=L, W=W, b=b, nb=nb)

    def run_ret(hn):
        _ret_core(hn, wq_ref, s0f_ref, s0b_ref, wo_ret_ref, y_ret_ref, sf_ref, sb_ref, *ret_scr,
                  L=L, C=C, nb=nb, has_init=has_init, emit_state=emit_state)

    if alternate:
        phase = pl.program_id(0) % 2
        pl.when(phase == 0)(lambda: run_ret(normed()))
        pl.when(phase == 1)(lambda: run_hy(normed()))
    else:
        hn = normed()
        if do_hy:
            run_hy(hn)
        if do_ret:
            run_ret(hn)


def _mixer(x, mods, mod_row, norm1, *, nb, ret=None, hy=None, casts=(), alternate=False):
    B, L, D = x.shape
    H, E = RET_HEADS, HEAD_DIM
    C = min(RET_CHUNK, L)
    per = 2 if alternate else 1
    assert not alternate or (ret is not None and hy is not None and not casts)
    seq_spec = pl.BlockSpec((nb, L, D), lambda g: (g // per, 0, 0))
    out_seq_spec = (pl.BlockSpec((nb, L, D), lambda g: (g // per, 0, 0),
                                 pipeline_mode=pl.Buffered(1)) if alternate else seq_spec)
    in_specs = [seq_spec,
                pl.BlockSpec((1, 1, N_MOD * D), lambda g: (mod_row((g // per) * nb), 0, 0)),
                _const_spec((1, D))]
    args = [x, mods, norm1]
    out_specs, out_shape, scratch = [], [], []
    has_init = emit_state = False
    W = b = None
    if ret is not None:
        w_qkvg, dec_f, dec_b, s0f, s0b, w_o, emit_state = ret
        has_init = s0f is not None
        st_spec = pl.BlockSpec((nb, H, E, E), lambda g: (g // per, 0, 0, 0))
        smem = pl.BlockSpec(memory_space=pltpu.SMEM)
        in_specs += [_const_spec((D, N_QKVG)), smem, smem]
        args += [w_qkvg, dec_f, dec_b]
        if has_init:
            in_specs += [st_spec, st_spec]
            args += [s0f, s0b]
        in_specs.append(_const_spec((RET_W, D)))
        args.append(w_o)
        out_specs.append(out_seq_spec)
        out_shape.append(jax.ShapeDtypeStruct((B, L, D), F32))
        if emit_state:
            out_specs += [st_spec, st_spec]
            out_shape += [jax.ShapeDtypeStruct((B, H, E, E), F32)] * 2
        scratch = [pltpu.VMEM((H, C, C), F32), pltpu.VMEM((H, 4, C, E), F32),
                   pltpu.VMEM((8, E), F32), pltpu.VMEM((nb, L, RET_W), BF16)]
    if hy is not None:
        w_hy, conv_w, conv_b, fw, bw, (fa, fb, fd), hy_bias, w_o, W, b = hy
        nd = fa.shape[1]
        in_specs += [_const_spec((D, N_HY)), _const_spec((1, 3 * N_HY)), _const_spec((1, N_HY)),
                     _const_spec((2 * b, b)), _const_spec((b, 2 * b)),
                     _const_spec((HY_ORDER, nd, b, HY_W)), _const_spec((HY_ORDER, nd, b, HY_W)),
                     _const_spec((HY_ORDER, nd, 8, HY_W)), _const_spec((HY_ORDER, HY_W)),
                     _const_spec((HY_W, D))]
        args += [w_hy, conv_w, conv_b, fw, bw, fa, fb, fd, hy_bias, w_o]
        out_specs.append(out_seq_spec)
        out_shape.append(jax.ShapeDtypeStruct((B, L, D), F32))
    c_in, c_out, c_shape, c_args = _cast_specs(casts, B // nb)
    name = ("ret" if ret is not None else "") + ("hy" if hy is not None else "")
    return pl.pallas_call(
        functools.partial(_mix_kernel, L=L, C=C, W=W, b=b, nb=nb, do_ret=ret is not None,
                          do_hy=hy is not None, has_init=has_init, emit_state=emit_state,
                          casts=tuple(cs for _, cs in casts), alternate=alternate),
        grid=(per * (B // nb),),
        in_specs=in_specs + c_in,
        out_specs=out_specs + c_out,
        out_shape=out_shape + c_shape,
        scratch_shapes=scratch,
        compiler_params=_params(1, VMEM_LIMIT_ALT if alternate else VMEM_LIMIT),
        name=f"{name}{L}",
    )(*args, *c_args)


def _mlp_kernel(x_ref, yr_ref, yh_ref, modp_ref, modq_ref, n1_ref, n2_ref, fg_ref, wg_ref,
                wout_ref, wfi_hbm, wfo_hbm, y_ref, x1_scr, h2_scr, wfi_ref, wfo_ref, sem, *,
                n_tiles):
    i = pl.program_id(0)
    wr = i % 2
    rd = 1 - wr
    nq = N_GATE // 4

    def pre_stages():
        mp = modp_ref[0]
        st = {}

        def p1():
            st["x"] = x_ref[...]
            st["hn"] = _modnorm(st["x"], n1_ref[...], _mod(mp, 1), _mod(mp, 0)).astype(BF16)

        def p2(q):
            def f():
                st["g%d" % q] = _dot(st["hn"], wg_ref[:, q * nq:(q + 1) * nq])
            return f

        def p3(h):
            def f():
                cs = slice(h * nq, (h + 1) * nq)
                st["mix%d" % h] = (jax.nn.sigmoid(st["g%d" % h]) * yr_ref[:, cs]
                                   + jax.nn.sigmoid(st["g%d" % (2 + h)]) * yh_ref[:, cs]
                                   ).astype(BF16)
            return f

        def p4():
            upd = (_dot(st["mix0"], wout_ref[0:nq, :]) + _dot(st["mix1"], wout_ref[nq:2 * nq, :]))
            st["x1"] = st["x"] + _mod(mp, 2) * upd

        def p5():
            x1_scr[wr] = st["x1"]
            h2_scr[wr] = _modnorm(st["x1"], n2_ref[...], _mod(mp, 4), _mod(mp, 3)).astype(BF16)

        return [p1, p2(0), p2(1), p2(2), p2(3), p3(0), p3(1), p4, p5]

    def ffn_stages():
        mq = modq_ref[0]
        st = {"acc": None}

        def f(j):
            def g():
                cs = slice(j * FF_CHUNK, (j + 1) * FF_CHUNK)
                h2 = h2_scr[rd]
                a = _dot(h2, wfi_ref[:, cs])
                b = _dot(h2, wfi_ref[:, D_FF + j * FF_CHUNK:D_FF + (j + 1) * FF_CHUNK])
                ff = (a * jax.nn.sigmoid(a) * b).astype(BF16)
                part = _dot(ff, wfo_ref[cs, :])
                st["acc"] = part if st["acc"] is None else st["acc"] + part
            return g

        def e():
            x2 = x1_scr[rd] + _mod(mq, 5) * st["acc"]
            ms = jnp.mean(x2 * x2, axis=-1, keepdims=True)
            y_ref[...] = x2 * lax.rsqrt(ms + EPS) * fg_ref[...]

        return [f(j) for j in range(D_FF // FF_CHUNK)] + [e]

    @pl.when(i == 0)
    def _():
        copies = [pltpu.make_async_copy(wfi_hbm, wfi_ref, sem.at[0]),
                  pltpu.make_async_copy(wfo_hbm, wfo_ref, sem.at[1])]
        for cp in copies:
            cp.start()
        for stage in pre_stages():
            stage()
        for cp in copies:
            cp.wait()

    @pl.when(jnp.logical_and(i > 0, i < n_tiles))
    def _():
        pre, ffn = pre_stages(), ffn_stages()
        order = []
        while pre or ffn:
            if ffn:
                order.append(ffn.pop(0))
            if pre:
                order.append(pre.pop(0))
        for stage in order:
            stage()

    @pl.when(i == n_tiles)
    def _():
        for stage in ffn_stages():
            stage()


def _mlp(x, y_ret, y_hy, mods, mod_row, norm1, norm2, final_g, w_gate, w_out, w_fi, w_fo):
    B, L, D = x.shape
    T = MLP_ROWS
    n_tiles = B * L // T
    flat = lambda a: a.reshape(B * L, D)
    pre_tile = lambda i: jnp.minimum(i, n_tiles - 1)
    post_tile = lambda i: jnp.maximum(i - 1, 0)
    act = pl.BlockSpec((T, D), lambda i: (pre_tile(i), 0))
    y = pl.pallas_call(
        functools.partial(_mlp_kernel, n_tiles=n_tiles),
        grid=(n_tiles + 1,),
        in_specs=[act, act, act,
                  pl.BlockSpec((1, 1, N_MOD * D),
                               lambda i: (mod_row((pre_tile(i) * T) // L), 0, 0)),
                  pl.BlockSpec((1, 1, N_MOD * D),
                               lambda i: (mod_row((post_tile(i) * T) // L), 0, 0)),
                  _const_spec((1, D)), _const_spec((1, D)), _const_spec((1, D)),
                  _const_spec((D, N_GATE)),
                  _const_spec((D, D)),
                  pl.BlockSpec(memory_space=pl.ANY),
                  pl.BlockSpec(memory_space=pl.ANY)],
        out_specs=pl.BlockSpec((T, D), lambda i: (post_tile(i), 0)),
        out_shape=jax.ShapeDtypeStruct((B * L, D), F32),
        scratch_shapes=[pltpu.VMEM((2, T, D), F32), pltpu.VMEM((2, T, D), BF16),
                        pltpu.VMEM((D, 2 * D_FF), BF16), pltpu.VMEM((D_FF, D), BF16),
                        pltpu.SemaphoreType.DMA((2,))],
        compiler_params=_params(1),
        name=f"mlp{L}",
    )(flat(x), flat(y_ret), flat(y_hy), mods, mods, norm1, norm2, final_g, w_gate, w_out, w_fi,
      w_fo)
    return y.reshape(B, L, D)


def kernel(x_prompt, x_sample, state_ret_fwd, state_ret_bwd, c, c_ctx, norm1_g, norm2_g, w_ada,
           b_ada, w_in, ret_decay_fwd, ret_decay_bwd, hy_conv_w, hy_conv_b, hy_pos_w1, hy_pos_b1,
           hy_pos_w2, hy_pos_b2, hy_pos_w3, hy_sin_freq, hy_bias, w_ret_o, w_hy_o, w_out,
           w_ffn_in, w_ffn_out, final_g):
    assert w_in.shape[0] == 1, "single-layer configuration"
    nb_lat = x_sample.shape[0]
    l_ctx = x_prompt.shape[1]

    assert 1 + nb_lat <= MOD_ROWS
    norm1 = norm1_g[0][None, :]
    norm2 = norm2_g[0][None, :]
    fg = final_g[None, :]
    conv_w = hy_conv_w[0].reshape(1, 3 * N_HY)
    conv_b = hy_conv_b[0][None, :]
    filt_params = (hy_pos_w1[0], hy_pos_b1, hy_pos_w2[0], hy_pos_b2, hy_pos_w3[0], hy_sin_freq)

    n_hy_end = N_QKVG + N_HY
    w_in_parts = (slice(0, N_QKVG), slice(N_QKVG, n_hy_end), slice(n_hy_end, N_IN))
    groups = []
    for L in (l_ctx, x_sample.shape[1]):
        blk = min(HY_TBLK, L)
        fw, bw, sgn = _dft_mats(blk)
        z, tdel = _filter_consts(L)
        groups.append((L, blk, jnp.asarray(z.T), jnp.asarray(tdel), jnp.asarray(sgn), fw))
    (mods, fa_c, fb_c, fd_c, fa_l, fb_l, fd_l, w_qkvg, w_hy, w_gate, w_ret_o_b,
     w_hy_o_b) = _ada(c_ctx[None, :], c, w_ada[0], b_ada, filt_params, groups,
                      casts=[(w_in[0], w_in_parts), (w_ret_o[0], None), (w_hy_o[0], None)])

    def branches(x, filt, s0f, s0b, grid_w, emit_state):
        blk = min(HY_TBLK, x.shape[1])
        fw, bw, _ = _dft_mats(blk)
        ret = (w_qkvg, ret_decay_fwd[0], ret_decay_bwd[0], s0f, s0b, w_ret_o_b, emit_state)
        hy = (w_hy, conv_w, conv_b, fw, bw, filt, hy_bias[0], w_hy_o_b, grid_w, blk)
        return ret, hy

    ctx_row = lambda b: 0
    lat_row = lambda b: b + 1
    ret, hy = branches(x_prompt, (fa_c, fb_c, fd_c), None, None, l_ctx, True)
    y_ret_c, s_f, s_b, y_hy_c, w_fi_b, w_fo_b, w_out_b = _mixer(
        x_prompt, mods, ctx_row, norm1, nb=CTX_SEQS, ret=ret, hy=hy,
        casts=[(w_ffn_in[0], None), (w_ffn_out[0], None), (w_out[0], None)])
    y_prompt = _mlp(x_prompt, y_ret_c, y_hy_c, mods, ctx_row, norm1, norm2, fg, w_gate, w_out_b,
                    w_fi_b, w_fo_b)
    ret, hy = branches(x_sample, (fa_l, fb_l, fd_l), state_ret_fwd[:, 0], state_ret_bwd[:, 0],
                       GRID_W, False)
    y_ret_l, y_hy_l = _mixer(x_sample, mods, lat_row, norm1, nb=1, ret=ret, hy=hy,
                             alternate=True)
    y_sample = _mlp(x_sample, y_ret_l, y_hy_l, mods, lat_row, norm1, norm2, fg, w_gate,
                    w_out_b, w_fi_b, w_fo_b)
    return (y_prompt, y_sample, s_f[:, None], s_b[:, None])
```

```python
import functools
import math

import numpy as np
import jax
import jax.numpy as jnp
from jax import lax
from jax.experimental import pallas as pl
from jax.experimental.pallas import tpu as pltpu

F32 = jnp.float32
BF16 = jnp.bfloat16

D_MODEL = 1024
RET_HEADS = 4
HEAD_DIM = 128
RET_W = RET_HEADS * HEAD_DIM
HY_W = 512
HY_ORDER = 2
HY_BANDS = 16
HY_EMB = 1 + 2 * HY_BANDS
HY_EMB_PAD = 40
HY_HIDDEN = 64
HY_FAST_DECAY = 0.3
HY_SLOW_DECAY = 1.5
HY_TARGET = 1e-2
D_FF = 2816
N_QKVG = 4 * RET_W
N_HY = 3 * HY_W
N_GATE = 2 * D_MODEL
N_IN = N_QKVG + N_HY + N_GATE
N_MOD = 6
MOD_ROWS = 8
EPS = 1e-6
GRID_W = 64
RET_CHUNK = 256
HY_CBLK = 512
HY_TBLK = 512
CTX_SEQS = 2
MLP_ROWS = 512
FF_CHUNK = 256
FILTER_ONE_STEP_LEN = 256
ADA_COLS = 768
VMEM_LIMIT = 56 * 1024 * 1024


def _const_spec(shape):
    nd = len(shape)
    return pl.BlockSpec(shape, lambda *_: (0,) * nd, pipeline_mode=pl.Buffered(1))


def _params(n_axes):
    return pltpu.CompilerParams(dimension_semantics=("arbitrary",) * n_axes,
                                vmem_limit_bytes=VMEM_LIMIT)


def _modnorm(x, g, scale, shift):
    ms = jnp.mean(x * x, axis=-1, keepdims=True)
    return (x * lax.rsqrt(ms + EPS) * g) * (1.0 + scale) + shift


def _mod(mod, k):
    return mod[:, k * D_MODEL:(k + 1) * D_MODEL]


def _dot(a, b):
    return jnp.dot(a, b, preferred_element_type=F32)


def _cast_specs(casts, steps):
    in_specs, out_specs, out_shape, args = [], [], [], []
    for arr, col_slices in casts:
        rows, width = arr.shape
        rb = rows // steps
        assert rb * steps == rows and rb % 16 == 0
        in_specs.append(pl.BlockSpec((rb, width), lambda g: (g, 0)))
        args.append(arr)
        for cs in col_slices or (slice(0, width),):
            cols = cs.stop - cs.start
            out_specs.append(pl.BlockSpec((rb, cols), lambda g: (g, 0)))
            out_shape.append(jax.ShapeDtypeStruct((rows, cols), BF16))
    return in_specs, out_specs, out_shape, args


def _n_cast_outputs(col_slices_per_src):
    return sum(1 if s is None else len(s) for s in col_slices_per_src)


def _do_casts(col_slices_per_src, srcs, dsts):
    dsts = iter(dsts)
    for col_slices, src in zip(col_slices_per_src, srcs):
        if col_slices is None:
            next(dsts)[...] = src[...].astype(BF16)
        else:
            for cs in col_slices:
                next(dsts)[...] = src[:, cs].astype(BF16)


def _ada_kernel(*refs, casts, groups):
    it = iter(refs)
    cctx_ref, c_ref, w_ref, b_ref = next(it), next(it), next(it), next(it)
    mlp_refs = [next(it) for _ in range(6)]
    g_in = [[next(it) for _ in range(4)] for _ in groups]
    cast_srcs = [next(it) for _ in casts]
    o_ref = next(it)
    g_out = [[next(it) for _ in range(3)] for _ in groups]
    cast_dsts = [next(it) for _ in range(_n_cast_outputs(casts))]
    h_scr, cond_scr, w1_scr, w2_scr = next(it), next(it), next(it), next(it)
    _do_casts(casts, cast_srcs, cast_dsts)
    nlat = c_ref.shape[0]
    cond_scr[...] = jnp.zeros_like(cond_scr)
    cond_scr[0:1, :] = cctx_ref[...]
    cond_scr[1:1 + nlat, :] = c_ref[...]
    c = cond_scr[...]
    s = (c * jax.nn.sigmoid(c)).astype(BF16)
    res = _dot(s, w_ref[...].astype(BF16)) + b_ref[...]
    for r in range(res.shape[0]):
        o_ref[r] = res[r:r + 1, :]

    step = 0
    for (L, blk), (zt_ref, tdel_ref, sgn_ref, fw_ref), outs in zip(groups, g_in, g_out):
        def mlp_job(zt_ref=zt_ref, L=L):
            h_scr[0:L, :] = _filter_mlp(zt_ref, *mlp_refs, w1_scr, w2_scr)

        def order_job(o, tdel_ref=tdel_ref, sgn_ref=sgn_ref, fw_ref=fw_ref, outs=outs, L=L,
                      blk=blk):
            _filter_spectra(h_scr, o, tdel_ref, sgn_ref, fw_ref, *outs, L=L, b=blk)

        def all_job(mlp_job=mlp_job, order_job=order_job):
            mlp_job()
            for o in range(HY_ORDER):
                order_job(o)

        if L <= FILTER_ONE_STEP_LEN:
            jobs = [all_job]
        else:
            jobs = [mlp_job] + [functools.partial(order_job, o) for o in range(HY_ORDER)]
        for job in jobs:
            pl.when(pl.program_id(0) == step)(job)
            step += 1


def _ada(c_ctx, c, w, b, filt_params, groups, casts=()):
    n = w.shape[1]
    steps = n // ADA_COLS
    c_in, c_out, c_shape, c_args = _cast_specs(casts, steps)
    in_specs = [_const_spec(c_ctx.shape), _const_spec(c.shape),
                pl.BlockSpec((D_MODEL, ADA_COLS), lambda j: (0, j)),
                pl.BlockSpec((1, ADA_COLS), lambda j: (0, j))]
    args = [c_ctx, c, w, b]
    for p in filt_params:
        in_specs.append(_const_spec(p.shape))
        args.append(p)
    out_specs = [pl.BlockSpec((MOD_ROWS, 1, ADA_COLS), lambda j: (0, 0, j))]
    out_shape = [jax.ShapeDtypeStruct((MOD_ROWS, 1, n), F32)]
    max_len = 8
    for L, blk, *consts in groups:
        nd = 2 * (L // blk) - 1
        max_len = max(max_len, L)
        for cst in consts:
            in_specs.append(_const_spec(cst.shape))
            args.append(cst)
        for shp in ((HY_ORDER, nd, blk, HY_W), (HY_ORDER, nd, blk, HY_W), (HY_ORDER, nd, 8, HY_W)):
            out_specs.append(pl.BlockSpec(shp, lambda j: (0, 0, 0, 0)))
            out_shape.append(jax.ShapeDtypeStruct(shp, F32))
    n_jobs = sum(1 if L <= FILTER_ONE_STEP_LEN else 1 + HY_ORDER for L, *_ in groups)
    assert n_jobs <= steps
    return pl.pallas_call(
        functools.partial(_ada_kernel, casts=tuple(cs for _, cs in casts),
                          groups=tuple((L, blk) for L, blk, *_ in groups)),
        grid=(steps,),
        in_specs=in_specs + c_in,
        out_specs=out_specs + c_out,
        out_shape=out_shape + c_shape,
        scratch_shapes=[pltpu.VMEM((max_len, HY_ORDER * 2 * HY_W), F32),
                        pltpu.VMEM((MOD_ROWS, D_MODEL), F32),
                        pltpu.VMEM((HY_EMB_PAD, HY_HIDDEN), F32),
                        pltpu.VMEM((HY_HIDDEN + 8, HY_HIDDEN), F32)],
        compiler_params=_params(1),
        name="ada",
    )(*args, *c_args)


@functools.lru_cache(maxsize=None)
def _dft_mats(L):
    n = 2 * L
    t = np.arange(L, dtype=np.int64)
    f = np.arange(L, dtype=np.int64)
    ang = 2.0 * np.pi * ((f[:, None] * t[None, :]) % n).astype(np.float64) / n
    cos = np.cos(ang)
    sin = np.sin(ang)
    nyq = np.where(t % 2 == 0, 1.0, -1.0)
    fwd = np.concatenate([cos, -sin], axis=0)
    fwd[L] = nyq
    wre = np.full((L,), 2.0 / n)
    wre[0] = 1.0 / n
    inv = np.concatenate([cos.T * wre[None, :], -sin.T * (2.0 / n)], axis=1)
    inv[:, L] = nyq / n
    sgn = np.broadcast_to(nyq[:, None], (L, HY_W))
    return (jnp.asarray(fwd, dtype=BF16), jnp.asarray(inv, dtype=BF16),
            np.asarray(sgn, dtype=np.float32))


@functools.lru_cache(maxsize=None)
def _filter_consts(L):
    t = np.linspace(0.0, 1.0, L)[:, None]
    ang = 2.0 * np.pi * np.arange(L, dtype=np.float64)[:, None] / L
    bands = np.linspace(1e-4, HY_BANDS - 1, HY_BANDS)[None]
    z = np.concatenate([t, np.cos(bands * ang), -np.sin(bands * ang)], axis=-1)
    z = np.pad(z, ((0, 0), (0, HY_EMB_PAD - HY_EMB)))
    z[:, HY_EMB] = 1.0
    max_decay = math.log(HY_TARGET) / HY_FAST_DECAY
    min_decay = math.log(HY_TARGET) / HY_SLOW_DECAY
    deltas = np.linspace(min_decay, max_decay, HY_W)
    tdel = t * np.abs(deltas)[None, :]
    return np.asarray(z, np.float32), np.asarray(tdel, np.float32)


def _filter_mlp(zt_ref, w1_ref, b1_ref, w2_ref, b2_ref, w3_ref, fr_ref, w1_scr, w2_scr):
    hi = lax.Precision.HIGHEST
    tdims = (((0,), (0,)), ((), ()))
    w1_scr[...] = jnp.zeros_like(w1_scr)
    w1_scr[0:HY_EMB, :] = w1_ref[...]
    w1_scr[HY_EMB:HY_EMB + 1, :] = b1_ref[...]
    w2_scr[...] = jnp.zeros_like(w2_scr)
    w2_scr[0:HY_HIDDEN, :] = w2_ref[...]
    w2_scr[HY_HIDDEN:HY_HIDDEN + 1, :] = b2_ref[...]
    fr = fr_ref[...]
    zt = zt_ref[...]
    h1 = jnp.sin(lax.dot_general(w1_scr[...] * fr, zt, tdims, precision=hi,
                                 preferred_element_type=F32))
    h1 = jnp.concatenate([h1, jnp.ones((8, zt.shape[1]), F32)], axis=0)
    h2 = jnp.sin(lax.dot_general(w2_scr[...] * fr, h1, tdims, precision=hi,
                                 preferred_element_type=F32))
    h2_hi = h2.astype(BF16)
    h2_lo = (h2 - h2_hi.astype(F32)).astype(BF16)
    w3 = w3_ref[...]
    w3_hi = w3.astype(BF16)
    w3_lo = (w3 - w3_hi.astype(F32)).astype(BF16)
    lhs = jnp.concatenate([h2_hi, h2_lo, h2_hi, jnp.zeros_like(h2_hi)], axis=0)
    rhs = jnp.concatenate([w3_hi, w3_hi, w3_lo, jnp.zeros_like(w3_hi)], axis=0)
    h = lax.dot_general(lhs, rhs, tdims, preferred_element_type=F32)
    return h


def _filter_spectra(h_ref, o, tdel_ref, sgn_ref, fw_ref, oa_ref, ob_ref, od_ref, *, L, b):
    m = L // b
    win = jnp.exp(-tdel_ref[...])
    sg = sgn_ref[...]
    row0_l = lax.broadcasted_iota(jnp.int32, (L, HY_W), 0) == 0
    row0_b = lax.broadcasted_iota(jnp.int32, (b, HY_W), 0) == 0
    row0_8 = lax.broadcasted_iota(jnp.int32, (8, HY_W), 0) == 0
    base = o * 2 * HY_W
    fwd = h_ref[0:L, base:base + HY_W] * win
    bwd = jnp.where(row0_l, 0.0, h_ref[0:L, base + HY_W:base + 2 * HY_W] * win)
    nrm = (jnp.sum(jnp.abs(fwd), axis=0, keepdims=True)
           + jnp.sum(jnp.abs(bwd), axis=0, keepdims=True))
    inv = 1.0 / nrm
    fn = fwd * inv
    bn = bwd * inv
    xr, xn, xi, wr, wn, wi = [], [], [], [], [], []
    for r in range(m):
        p = _dot(fw_ref[...], fn[r * b:(r + 1) * b].astype(BF16))
        q = _dot(fw_ref[...], bn[r * b:(r + 1) * b].astype(BF16))
        xr.append(p[0:b])
        xn.append(p[b:b + 1])
        xi.append(jnp.where(row0_b, 0.0, p[b:2 * b]))
        wr.append(q[0:b])
        wn.append(q[b:b + 1])
        wi.append(jnp.where(row0_b, 0.0, -q[b:2 * b]))

    def emit(d, ka, kn, kb):
        oa_ref[o, d + m - 1] = ka
        ob_ref[o, d + m - 1] = kb
        od_ref[o, d + m - 1] = jnp.where(row0_8, kn, ka[0:8])

    emit(0, xr[0] + wr[0], xn[0] + wn[0], xi[0] + wi[0])
    for d in range(1, m):
        f0 = fn[(d - 1) * b:(d - 1) * b + 1]
        b0 = bn[(d - 1) * b:(d - 1) * b + 1]
        emit(d, xr[d] + sg * (xr[d - 1] - f0), xn[d] + (xn[d - 1] - f0),
             xi[d] + sg * xi[d - 1])
        emit(-d, wr[d] + sg * (wr[d - 1] - b0), wn[d] + (wn[d - 1] - b0),
             wi[d] + sg * wi[d - 1])


def _ret_init(decf_ref, decb_ref, mask_scr, vec_scr, cd_scr, C):
    H, E = RET_HEADS, HEAD_DIM
    scale = float(E) ** -0.5

    @pl.when(pl.program_id(0) == 0)
    def _():
        dec = jnp.concatenate([jnp.full((1, C), ref[h], F32)
                               for ref in (decf_ref, decb_ref) for h in range(H)], axis=0)
        lg = jnp.log(jax.nn.sigmoid(dec))
        cd_scr[...] = jnp.exp(float(C) * lg[:, 0:E])
        ii = lax.broadcasted_iota(jnp.int32, (C, C), 0)
        jj = lax.broadcasted_iota(jnp.int32, (C, C), 1)
        rel = (ii - jj).astype(F32)
        ri = lax.broadcasted_iota(jnp.int32, (C, E), 0).astype(F32)
        for h in range(H):
            lf = lg[h:h + 1, :]
            lb = lg[H + h:H + h + 1, :]
            mf = jnp.where(rel >= 0, jnp.exp(jnp.maximum(rel, 0.0) * lf), 0.0)
            mb = jnp.where(rel <= 0, jnp.exp(jnp.maximum(-rel, 0.0) * lb), 0.0)
            mask_scr[h] = scale * (mf + mb)
            lfe, lbe = lf[:, 0:E], lb[:, 0:E]
            vec_scr[h, 0] = jnp.exp((ri + 1.0) * lfe)
            vec_scr[h, 1] = jnp.exp((float(C) - ri) * lbe)
            vec_scr[h, 2] = scale * jnp.exp((float(C) - 1.0 - ri) * lfe)
            vec_scr[h, 3] = scale * jnp.exp(ri * lbe)


def _ret_core(hn, w_ref, s0f_ref, s0b_ref, wo_ref, y_ref, sf_ref, sb_ref, mask_scr, vec_scr,
              cd_scr, g_scr, *, L, C, nb, has_init, emit_state):
    n = L // C
    H, E = RET_HEADS, HEAD_DIM
    tdims = (((0,), (0,)), ((), ()))
    ndims = (((1,), (1,)), ((), ()))
    chains = [(s, h) for s in range(nb) for h in range(H)]
    rows = [slice(c * C, (c + 1) * C) for c in range(n)]
    qkvg = [_dot(hn[s], w_ref[...]) for s in range(nb)]

    def cols(s, part, h):
        return qkvg[s][:, part * RET_W + h * E:part * RET_W + (h + 1) * E]

    qb = [cols(s, 0, h).astype(BF16) for s, h in chains]
    kf = [cols(s, 1, h) for s, h in chains]
    kb = [k.astype(BF16) for k in kf]
    vb = [cols(s, 2, h).astype(BF16) for s, h in chains]
    att = [[lax.dot_general(qb[i][r], kb[i][r], ndims, preferred_element_type=F32) for r in rows]
           for i in range(len(chains))]
    prob = [[(att[i][c] * mask_scr[h]).astype(BF16) for c in range(n)]
            for i, (s, h) in enumerate(chains)]
    out = [[_dot(prob[i][c], vb[i][rows[c]]) for c in range(n)] for i in range(len(chains))]
    kv = []
    for i, (s, h) in enumerate(chains):
        dk2 = jnp.concatenate([vec_scr[h, 2], vec_scr[h, 3]], axis=1)
        per_c = []
        for r in rows:
            k2 = (jnp.concatenate([kf[i][r], kf[i][r]], axis=1) * dk2).astype(BF16)
            per_c.append(lax.dot_general(k2, vb[i][r], tdims, preferred_element_type=F32))
        kv.append(per_c)
    for i, (s, h) in enumerate(chains):
        cdf = cd_scr[h:h + 1, :]
        cdb = cd_scr[H + h:H + h + 1, :]
        sf_in, sb_in = [None] * n, [None] * n
        st = s0f_ref[s, h] if has_init else None
        for c in range(n):
            sf_in[c] = st
            kvc = kv[i][c][0:E]
            st = kvc if st is None else st * cdf + kvc
        if emit_state:
            sf_ref[s, h] = st
        st = s0b_ref[s, h] if has_init else None
        for c in range(n - 1, -1, -1):
            sb_in[c] = st
            kvc = kv[i][c][E:2 * E]
            st = kvc if st is None else st * cdb + kvc
        if emit_state:
            sb_ref[s, h] = st
        for c in range(n):
            if sf_in[c] is not None and sb_in[c] is not None:
                s2 = jnp.concatenate([sf_in[c], sb_in[c]], axis=1).astype(BF16)
                inter = _dot(qb[i][rows[c]], s2)
                out[i][c] = (out[i][c] + inter[:, 0:E] * vec_scr[h, 0]
                             + inter[:, E:2 * E] * vec_scr[h, 1])
            elif sf_in[c] is not None:
                out[i][c] = (out[i][c]
                             + _dot(qb[i][rows[c]], sf_in[c].astype(BF16)) * vec_scr[h, 0])
            elif sb_in[c] is not None:
                out[i][c] = (out[i][c]
                             + _dot(qb[i][rows[c]], sb_in[c].astype(BF16)) * vec_scr[h, 1])
    for i, (s, h) in enumerate(chains):
        for c in range(n):
            o = out[i][c]
            mu = jnp.mean(o, axis=-1, keepdims=True)
            d = o - mu
            var = jnp.mean(d * d, axis=-1, keepdims=True)
            on = d * lax.rsqrt(var + EPS)
            gg = cols(s, 3, h)[rows[c]]
            g_scr[s, rows[c], h * E:(h + 1) * E] = (gg * jax.nn.sigmoid(gg) * on).astype(BF16)
    for s in range(nb):
        y_ref[s] = _dot(g_scr[s], wo_ref[...])


def _hy_core(hn, w_ref, cw_ref, cb_ref, fw_ref, bw_ref, fa_ref, fb_ref, fd_ref, hb_ref, wo_ref,
             y_ref, *, L, W, b, nb):
    m = L // b
    CB = HY_CBLK
    nblk = HY_W // CB
    pos = lax.broadcasted_iota(jnp.int32, (L, CB), 0) % W
    first = pos == 0
    last = pos == W - 1
    chains = [(s, blk) for s in range(nb) for blk in range(nblk)]

    def short_conv(s, base, blk):
        cs = slice(base + blk * CB, base + (blk + 1) * CB)
        ug = _dot(hn[s], w_ref[:, cs])
        prev = jnp.where(first, 0.0, pltpu.roll(ug, 1, axis=0))
        nxt = jnp.where(last, 0.0, pltpu.roll(ug, L - 1, axis=0))
        taps = [cw_ref[:, t * N_HY + cs.start:t * N_HY + cs.stop] for t in range(3)]
        u = prev * taps[0] + ug * taps[1] + nxt * taps[2] + cb_ref[:, cs]
        return [u[j * b:(j + 1) * b] for j in range(m)]

    def long_conv(sigs, o):
        spec = [[_dot(fw_ref[...], sj.astype(BF16)) for sj in sig] for sig in sigs]
        prods = []
        for (s, blk), sp in zip(chains, spec):
            cs = slice(blk * CB, (blk + 1) * CB)
            per_i = []
            for i in range(m):
                yre = yim = yim8 = None
                for j in range(m):
                    d = i - j + m - 1
                    sre, sim = sp[j][0:b], sp[j][b:2 * b]
                    ka, kb = fa_ref[o, d, :, cs], fb_ref[o, d, :, cs]
                    tre = sre * ka - sim * kb
                    tim = sre * kb + sim * ka
                    t8 = sre[0:8] * kb[0:8] + sim[0:8] * fd_ref[o, d, :, cs]
                    yre = tre if yre is None else yre + tre
                    yim = tim if yim is None else yim + tim
                    yim8 = t8 if yim8 is None else yim8 + t8
                yim = jnp.concatenate([yim8, yim[8:]], axis=0)
                per_i.append((yre.astype(BF16), yim.astype(BF16)))
            prods.append(per_i)
        return [[_dot(bw_ref[:, 0:b], yre) + _dot(bw_ref[:, b:2 * b], yim) for yre, yim in per_i]
                for per_i in prods]

    hv = [short_conv(s, 0, blk) for s, blk in chains]
    hx1 = [short_conv(s, HY_W, blk) for s, blk in chains]
    hx2 = [short_conv(s, 2 * HY_W, blk) for s, blk in chains]

    def gate(hx, conv, sig, o):
        out = []
        for (s, blk), hxc, cc, sc in zip(chains, hx, conv, sig):
            bias = hb_ref[o:o + 1, blk * CB:(blk + 1) * CB]
            out.append([hxc[i] * (cc[i] + sc[i] * bias) for i in range(m)])
        return out

    z = gate(hx1, long_conv(hv, 0), hv, 0)
    z = gate(hx2, long_conv(z, 1), z, 1)
    for s in range(nb):
        for i in range(m):
            acc = None
            for blk in range(nblk):
                zc = z[chains.index((s, blk))][i].astype(BF16)
                part = _dot(zc, wo_ref[blk * CB:(blk + 1) * CB, :])
                acc = part if acc is None else acc + part
            y_ref[s, i * b:(i + 1) * b, :] = acc


def _mix_kernel(*refs, L, C, W, b, nb, do_ret, do_hy, has_init, emit_state, casts):
    it = iter(refs)
    x_ref, mod_ref, n1_ref = next(it), next(it), next(it)
    s0f_ref = s0b_ref = sf_ref = sb_ref = None
    if do_ret:
        wq_ref, decf_ref, decb_ref = next(it), next(it), next(it)
        if has_init:
            s0f_ref, s0b_ref = next(it), next(it)
        wo_ret_ref = next(it)
    if do_hy:
        hy_in = [next(it) for _ in range(10)]
    cast_srcs = [next(it) for _ in casts]
    if do_ret:
        y_ret_ref = next(it)
        if emit_state:
            sf_ref, sb_ref = next(it), next(it)
    if do_hy:
        y_hy_ref = next(it)
    cast_dsts = [next(it) for _ in range(_n_cast_outputs(casts))]
    if do_ret:
        ret_scr = [next(it) for _ in range(4)]
        _ret_init(decf_ref, decb_ref, ret_scr[0], ret_scr[1], ret_scr[2], C)
    _do_casts(casts, cast_srcs, cast_dsts)
    mod = mod_ref[0]
    hn = [_modnorm(x_ref[s], n1_ref[...], _mod(mod, 1), _mod(mod, 0)).astype(BF16)
          for s in range(nb)]
    if do_hy:
        _hy_core(hn, *hy_in, y_hy_ref, L=L, W=W, b=b, nb=nb)
    if do_ret:
        _ret_core(hn, wq_ref, s0f_ref, s0b_ref, wo_ret_ref, y_ret_ref, sf_ref, sb_ref, *ret_scr,
                  L=L, C=C, nb=nb, has_init=has_init, emit_state=emit_state)


def _mixer(x, mods, mod_row, norm1, *, nb, ret=None, hy=None, casts=()):
    B, L, D = x.shape
    H, E = RET_HEADS, HEAD_DIM
    C = min(RET_CHUNK, L)
    seq_spec = pl.BlockSpec((nb, L, D), lambda g: (g, 0, 0))
    in_specs = [seq_spec,
                pl.BlockSpec((1, 1, N_MOD * D), lambda g: (mod_row(g * nb), 0, 0)),
                _const_spec((1, D))]
    args = [x, mods, norm1]
    out_specs, out_shape, scratch = [], [], []
    has_init = emit_state = False
    W = b = None
    if ret is not None:
        w_qkvg, dec_f, dec_b, s0f, s0b, w_o, emit_state = ret
        has_init = s0f is not None
        st_spec = pl.BlockSpec((nb, H, E, E), lambda g: (g, 0, 0, 0))
        smem = pl.BlockSpec(memory_space=pltpu.SMEM)
        in_specs += [_const_spec((D, N_QKVG)), smem, smem]
        args += [w_qkvg, dec_f, dec_b]
        if has_init:
            in_specs += [st_spec, st_spec]
            args += [s0f, s0b]
        in_specs.append(_const_spec((RET_W, D)))
        args.append(w_o)
        out_specs.append(seq_spec)
        out_shape.append(jax.ShapeDtypeStruct((B, L, D), F32))
        if emit_state:
            out_specs += [st_spec, st_spec]
            out_shape += [jax.ShapeDtypeStruct((B, H, E, E), F32)] * 2
        scratch = [pltpu.VMEM((H, C, C), F32), pltpu.VMEM((H, 4, C, E), F32),
                   pltpu.VMEM((8, E), F32), pltpu.VMEM((nb, L, RET_W), BF16)]
    if hy is not None:
        w_hy, conv_w, conv_b, fw, bw, (fa, fb, fd), hy_bias, w_o, W, b = hy
        nd = fa.shape[1]
        in_specs += [_const_spec((D, N_HY)), _const_spec((1, 3 * N_HY)), _const_spec((1, N_HY)),
                     _const_spec((2 * b, b)), _const_spec((b, 2 * b)),
                     _const_spec((HY_ORDER, nd, b, HY_W)), _const_spec((HY_ORDER, nd, b, HY_W)),
                     _const_spec((HY_ORDER, nd, 8, HY_W)), _const_spec((HY_ORDER, HY_W)),
                     _const_spec((HY_W, D))]
        args += [w_hy, conv_w, conv_b, fw, bw, fa, fb, fd, hy_bias, w_o]
        out_specs.append(seq_spec)
        out_shape.append(jax.ShapeDtypeStruct((B, L, D), F32))
    c_in, c_out, c_shape, c_args = _cast_specs(casts, B // nb)
    name = ("ret" if ret is not None else "") + ("hy" if hy is not None else "")
    return pl.pallas_call(
        functools.partial(_mix_kernel, L=L, C=C, W=W, b=b, nb=nb, do_ret=ret is not None,
                          do_hy=hy is not None, has_init=has_init, emit_state=emit_state,
                          casts=tuple(cs for _, cs in casts)),
        grid=(B // nb,),
        in_specs=in_specs + c_in,
        out_specs=out_specs + c_out,
        out_shape=out_shape + c_shape,
        scratch_shapes=scratch,
        compiler_params=_params(1),
        name=f"{name}{L}",
    )(*args, *c_args)


def _mlp_kernel(x_ref, yr_ref, yh_ref, modp_ref, modq_ref, n1_ref, n2_ref, fg_ref, wg_ref,
                wout_ref, wfi_hbm, wfo_hbm, y_ref, x1_scr, h2_scr, wfi_ref, wfo_ref, sem, *,
                n_tiles):
    i = pl.program_id(0)
    wr = i % 2
    rd = 1 - wr
    nq = N_GATE // 4

    def pre_stages():
        mp = modp_ref[0]
        st = {}

        def p1():
            st["x"] = x_ref[...]
            st["hn"] = _modnorm(st["x"], n1_ref[...], _mod(mp, 1), _mod(mp, 0)).astype(BF16)

        def p2(q):
            def f():
                st["g%d" % q] = _dot(st["hn"], wg_ref[:, q * nq:(q + 1) * nq])
            return f

        def p3(h):
            def f():
                cs = slice(h * nq, (h + 1) * nq)
                st["mix%d" % h] = (jax.nn.sigmoid(st["g%d" % h]) * yr_ref[:, cs]
                                   + jax.nn.sigmoid(st["g%d" % (2 + h)]) * yh_ref[:, cs]
                                   ).astype(BF16)
            return f

        def p4():
            upd = (_dot(st["mix0"], wout_ref[0:nq, :]) + _dot(st["mix1"], wout_ref[nq:2 * nq, :]))
            st["x1"] = st["x"] + _mod(mp, 2) * upd

        def p5():
            x1_scr[wr] = st["x1"]
            h2_scr[wr] = _modnorm(st["x1"], n2_ref[...], _mod(mp, 4), _mod(mp, 3)).astype(BF16)

        return [p1, p2(0), p2(1), p2(2), p2(3), p3(0), p3(1), p4, p5]

    def ffn_stages():
        mq = modq_ref[0]
        st = {"acc": None}

        def f(j):
            def g():
                cs = slice(j * FF_CHUNK, (j + 1) * FF_CHUNK)
                h2 = h2_scr[rd]
                a = _dot(h2, wfi_ref[:, cs])
                b = _dot(h2, wfi_ref[:, D_FF + j * FF_CHUNK:D_FF + (j + 1) * FF_CHUNK])
                ff = (a * jax.nn.sigmoid(a) * b).astype(BF16)
                part = _dot(ff, wfo_ref[cs, :])
                st["acc"] = part if st["acc"] is None else st["acc"] + part
            return g

        def e():
            x2 = x1_scr[rd] + _mod(mq, 5) * st["acc"]
            ms = jnp.mean(x2 * x2, axis=-1, keepdims=True)
            y_ref[...] = x2 * lax.rsqrt(ms + EPS) * fg_ref[...]

        return [f(j) for j in range(D_FF // FF_CHUNK)] + [e]

    @pl.when(i == 0)
    def _():
        copies = [pltpu.make_async_copy(wfi_hbm, wfi_ref, sem.at[0]),
                  pltpu.make_async_copy(wfo_hbm, wfo_ref, sem.at[1])]
        for cp in copies:
            cp.start()
        for stage in pre_stages():
            stage()
        for cp in copies:
            cp.wait()

    @pl.when(jnp.logical_and(i > 0, i < n_tiles))
    def _():
        pre, ffn = pre_stages(), ffn_stages()
        order = []
        while pre or ffn:
            if ffn:
                order.append(ffn.pop(0))
            if pre:
                order.append(pre.pop(0))
        for stage in order:
            stage()

    @pl.when(i == n_tiles)
    def _():
        for stage in ffn_stages():
            stage()


def _mlp(x, y_ret, y_hy, mods, mod_row, norm1, norm2, final_g, w_gate, w_out, w_fi, w_fo):
    B, L, D = x.shape
    T = MLP_ROWS
    n_tiles = B * L // T
    flat = lambda a: a.reshape(B * L, D)
    pre_tile = lambda i: jnp.minimum(i, n_tiles - 1)
    post_tile = lambda i: jnp.maximum(i - 1, 0)
    act = pl.BlockSpec((T, D), lambda i: (pre_tile(i), 0))
    y = pl.pallas_call(
        functools.partial(_mlp_kernel, n_tiles=n_tiles),
        grid=(n_tiles + 1,),
        in_specs=[act, act, act,
                  pl.BlockSpec((1, 1, N_MOD * D),
                               lambda i: (mod_row((pre_tile(i) * T) // L), 0, 0)),
                  pl.BlockSpec((1, 1, N_MOD * D),
                               lambda i: (mod_row((post_tile(i) * T) // L), 0, 0)),
                  _const_spec((1, D)), _const_spec((1, D)), _const_spec((1, D)),
                  _const_spec((D, N_GATE)),
                  _const_spec((D, D)),
                  pl.BlockSpec(memory_space=pl.ANY),
                  pl.BlockSpec(memory_space=pl.ANY)],
        out_specs=pl.BlockSpec((T, D), lambda i: (post_tile(i), 0)),
        out_shape=jax.ShapeDtypeStruct((B * L, D), F32),
        scratch_shapes=[pltpu.VMEM((2, T, D), F32), pltpu.VMEM((2, T, D), BF16),
                        pltpu.VMEM((D, 2 * D_FF), BF16), pltpu.VMEM((D_FF, D), BF16),
                        pltpu.SemaphoreType.DMA((2,))],
        compiler_params=_params(1),
        name=f"mlp{L}",
    )(flat(x), flat(y_ret), flat(y_hy), mods, mods, norm1, norm2, final_g, w_gate, w_out, w_fi,
      w_fo)
    return y.reshape(B, L, D)


def kernel(x_prompt, x_sample, state_ret_fwd, state_ret_bwd, c, c_ctx, norm1_g, norm2_g, w_ada,
           b_ada, w_in, ret_decay_fwd, ret_decay_bwd, hy_conv_w, hy_conv_b, hy_pos_w1, hy_pos_b1,
           hy_pos_w2, hy_pos_b2, hy_pos_w3, hy_sin_freq, hy_bias, w_ret_o, w_hy_o, w_out,
           w_ffn_in, w_ffn_out, final_g):
    assert w_in.shape[0] == 1, "single-layer configuration"
    nb_lat = x_sample.shape[0]
    l_ctx = x_prompt.shape[1]

    assert 1 + nb_lat <= MOD_ROWS
    norm1 = norm1_g[0][None, :]
    norm2 = norm2_g[0][None, :]
    fg = final_g[None, :]
    conv_w = hy_conv_w[0].reshape(1, 3 * N_HY)
    conv_b = hy_conv_b[0][None, :]
    filt_params = (hy_pos_w1[0], hy_pos_b1, hy_pos_w2[0], hy_pos_b2, hy_pos_w3[0], hy_sin_freq)

    n_hy_end = N_QKVG + N_HY
    w_in_parts = (slice(0, N_QKVG), slice(N_QKVG, n_hy_end), slice(n_hy_end, N_IN))
    groups = []
    for L in (l_ctx, x_sample.shape[1]):
        blk = min(HY_TBLK, L)
        fw, bw, sgn = _dft_mats(blk)
        z, tdel = _filter_consts(L)
        groups.append((L, blk, jnp.asarray(z.T), jnp.asarray(tdel), jnp.asarray(sgn), fw))
    (mods, fa_c, fb_c, fd_c, fa_l, fb_l, fd_l, w_qkvg, w_hy, w_gate, w_ret_o_b,
     w_hy_o_b) = _ada(c_ctx[None, :], c, w_ada[0], b_ada, filt_params, groups,
                      casts=[(w_in[0], w_in_parts), (w_ret_o[0], None), (w_hy_o[0], None)])

    def branches(x, filt, s0f, s0b, grid_w, emit_state):
        blk = min(HY_TBLK, x.shape[1])
        fw, bw, _ = _dft_mats(blk)
        ret = (w_qkvg, ret_decay_fwd[0], ret_decay_bwd[0], s0f, s0b, w_ret_o_b, emit_state)
        hy = (w_hy, conv_w, conv_b, fw, bw, filt, hy_bias[0], w_hy_o_b, grid_w, blk)
        return ret, hy

    ctx_row = lambda b: 0
    lat_row = lambda b: b + 1
    ret, hy = branches(x_prompt, (fa_c, fb_c, fd_c), None, None, l_ctx, True)
    y_ret_c, s_f, s_b, y_hy_c, w_fi_b, w_fo_b, w_out_b = _mixer(
        x_prompt, mods, ctx_row, norm1, nb=CTX_SEQS, ret=ret, hy=hy,
        casts=[(w_ffn_in[0], None), (w_ffn_out[0], None), (w_out[0], None)])
    y_prompt = _mlp(x_prompt, y_ret_c, y_hy_c, mods, ctx_row, norm1, norm2, fg, w_gate, w_out_b,
                    w_fi_b, w_fo_b)
    ret, hy = branches(x_sample, (fa_l, fb_l, fd_l), state_ret_fwd[:, 0], state_ret_bwd[:, 0],
                       GRID_W, False)
    y_ret_l, = _mixer(x_sample, mods, lat_row, norm1, nb=1, ret=ret)
    y_hy_l, = _mixer(x_sample, mods, lat_row, norm1, nb=1, hy=hy)
    y_sample = _mlp(x_sample, y_ret_l, y_hy_l, mods, lat_row, norm1, norm2, fg, w_gate,
                    w_out_b, w_fi_b, w_fo_b)
    return (y_prompt, y_sample, s_f[:, None], s_b[:, None])
```

```python
import functools
import math

import numpy as np
import jax
import jax.numpy as jnp
from jax import lax
from jax.experimental import pallas as pl
from jax.experimental.pallas import tpu as pltpu

F32 = jnp.float32
BF16 = jnp.bfloat16

D_MODEL = 1024
RET_HEADS = 4
HEAD_DIM = 128
RET_W = RET_HEADS * HEAD_DIM
HY_W = 512
HY_ORDER = 2
HY_BANDS = 16
HY_EMB = 1 + 2 * HY_BANDS
HY_EMB_PAD = 40
HY_HIDDEN = 64
HY_FAST_DECAY = 0.3
HY_SLOW_DECAY = 1.5
HY_TARGET = 1e-2
D_FF = 2816
N_QKVG = 4 * RET_W
N_HY = 3 * HY_W
N_GATE = 2 * D_MODEL
N_IN = N_QKVG + N_HY + N_GATE
N_MOD = 6
MOD_ROWS = 8
EPS = 1e-6
GRID_W = 64
RET_CHUNK = 256
HY_CBLK = 256
HY_TBLK = 512
CTX_SEQS = 2
MLP_ROWS = 512
FF_CHUNK = 256
FILTER_ONE_STEP_LEN = 256
ADA_COLS = 768
VMEM_LIMIT = 56 * 1024 * 1024


def _const_spec(shape):
    nd = len(shape)
    return pl.BlockSpec(shape, lambda *_: (0,) * nd, pipeline_mode=pl.Buffered(1))


def _params(n_axes):
    return pltpu.CompilerParams(dimension_semantics=("arbitrary",) * n_axes,
                                vmem_limit_bytes=VMEM_LIMIT)


def _modnorm(x, g, scale, shift):
    ms = jnp.mean(x * x, axis=-1, keepdims=True)
    return (x * lax.rsqrt(ms + EPS) * g) * (1.0 + scale) + shift


def _mod(mod, k):
    return mod[:, k * D_MODEL:(k + 1) * D_MODEL]


def _dot(a, b):
    return jnp.dot(a, b, preferred_element_type=F32)


def _cast_specs(casts, steps):
    in_specs, out_specs, out_shape, args = [], [], [], []
    for arr, col_slices in casts:
        rows, width = arr.shape
        rb = rows // steps
        assert rb * steps == rows and rb % 16 == 0
        in_specs.append(pl.BlockSpec((rb, width), lambda g: (g, 0)))
        args.append(arr)
        for cs in col_slices or (slice(0, width),):
            cols = cs.stop - cs.start
            out_specs.append(pl.BlockSpec((rb, cols), lambda g: (g, 0)))
            out_shape.append(jax.ShapeDtypeStruct((rows, cols), BF16))
    return in_specs, out_specs, out_shape, args


def _n_cast_outputs(col_slices_per_src):
    return sum(1 if s is None else len(s) for s in col_slices_per_src)


def _do_casts(col_slices_per_src, srcs, dsts):
    dsts = iter(dsts)
    for col_slices, src in zip(col_slices_per_src, srcs):
        if col_slices is None:
            next(dsts)[...] = src[...].astype(BF16)
        else:
            for cs in col_slices:
                next(dsts)[...] = src[:, cs].astype(BF16)


def _ada_kernel(*refs, casts, groups):
    it = iter(refs)
    cctx_ref, c_ref, w_ref, b_ref = next(it), next(it), next(it), next(it)
    mlp_refs = [next(it) for _ in range(6)]
    g_in = [[next(it) for _ in range(3)] for _ in groups]
    cast_srcs = [next(it) for _ in casts]
    o_ref = next(it)
    g_out = [[next(it) for _ in range(3)] for _ in groups]
    cast_dsts = [next(it) for _ in range(_n_cast_outputs(casts))]
    h_scr, cond_scr, w1_scr, w2_scr = next(it), next(it), next(it), next(it)
    _do_casts(casts, cast_srcs, cast_dsts)
    nlat = c_ref.shape[0]
    cond_scr[...] = jnp.zeros_like(cond_scr)
    cond_scr[0:1, :] = cctx_ref[...]
    cond_scr[1:1 + nlat, :] = c_ref[...]
    c = cond_scr[...]
    s = (c * jax.nn.sigmoid(c)).astype(BF16)
    res = _dot(s, w_ref[...].astype(BF16)) + b_ref[...]
    for r in range(res.shape[0]):
        o_ref[r] = res[r:r + 1, :]

    step = 0
    for (L, blk), (zt_ref, rate_ref, fw_ref), outs in zip(groups, g_in, g_out):
        def mlp_job(zt_ref=zt_ref, L=L):
            h_scr[0:L, :] = _filter_mlp(zt_ref, *mlp_refs, w1_scr, w2_scr)

        def order_job(o, rate_ref=rate_ref, fw_ref=fw_ref, outs=outs, L=L, blk=blk):
            _filter_spectra(h_scr, o, rate_ref, fw_ref, *outs, L=L, b=blk)

        def all_job(mlp_job=mlp_job, order_job=order_job):
            mlp_job()
            for o in range(HY_ORDER):
                order_job(o)

        if L <= FILTER_ONE_STEP_LEN:
            jobs = [all_job]
        else:
            jobs = [mlp_job] + [functools.partial(order_job, o) for o in range(HY_ORDER)]
        for job in jobs:
            pl.when(pl.program_id(0) == step)(job)
            step += 1


def _ada(c_ctx, c, w, b, filt_params, groups, casts=()):
    n = w.shape[1]
    steps = n // ADA_COLS
    c_in, c_out, c_shape, c_args = _cast_specs(casts, steps)
    in_specs = [_const_spec(c_ctx.shape), _const_spec(c.shape),
                pl.BlockSpec((D_MODEL, ADA_COLS), lambda j: (0, j)),
                pl.BlockSpec((1, ADA_COLS), lambda j: (0, j))]
    args = [c_ctx, c, w, b]
    for p in filt_params:
        in_specs.append(_const_spec(p.shape))
        args.append(p)
    out_specs = [pl.BlockSpec((MOD_ROWS, 1, ADA_COLS), lambda j: (0, 0, j))]
    out_shape = [jax.ShapeDtypeStruct((MOD_ROWS, 1, n), F32)]
    max_len = 8
    for L, blk, *consts in groups:
        nd = 2 * (L // blk) - 1
        max_len = max(max_len, L)
        for cst in consts:
            in_specs.append(_const_spec(cst.shape))
            args.append(cst)
        for shp in ((HY_ORDER, nd, blk, HY_W), (HY_ORDER, nd, blk, HY_W), (HY_ORDER, nd, 8, HY_W)):
            out_specs.append(pl.BlockSpec(shp, lambda j: (0, 0, 0, 0)))
            out_shape.append(jax.ShapeDtypeStruct(shp, F32))
    n_jobs = sum(1 if L <= FILTER_ONE_STEP_LEN else 1 + HY_ORDER for L, *_ in groups)
    assert n_jobs <= steps
    return pl.pallas_call(
        functools.partial(_ada_kernel, casts=tuple(cs for _, cs in casts),
                          groups=tuple((L, blk) for L, blk, *_ in groups)),
        grid=(steps,),
        in_specs=in_specs + c_in,
        out_specs=out_specs + c_out,
        out_shape=out_shape + c_shape,
        scratch_shapes=[pltpu.VMEM((max_len, HY_ORDER * 2 * HY_W), F32),
                        pltpu.VMEM((MOD_ROWS, D_MODEL), F32),
                        pltpu.VMEM((HY_EMB_PAD, HY_HIDDEN), F32),
                        pltpu.VMEM((HY_HIDDEN + 8, HY_HIDDEN), F32)],
        compiler_params=_params(1),
        name="ada",
    )(*args, *c_args)


@functools.lru_cache(maxsize=None)
def _dft_mats(L):
    n = 2 * L
    t = np.arange(L, dtype=np.int64)
    f = np.arange(L, dtype=np.int64)
    ang = 2.0 * np.pi * ((f[:, None] * t[None, :]) % n).astype(np.float64) / n
    cos = np.cos(ang)
    sin = np.sin(ang)
    nyq = np.where(t % 2 == 0, 1.0, -1.0)
    fwd = np.concatenate([cos, -sin], axis=0)
    fwd[L] = nyq
    wre = np.full((L,), 2.0 / n)
    wre[0] = 1.0 / n
    inv = np.concatenate([cos.T * wre[None, :], -sin.T * (2.0 / n)], axis=1)
    inv[:, L] = nyq / n
    return jnp.asarray(fwd, dtype=BF16), jnp.asarray(inv, dtype=BF16)


@functools.lru_cache(maxsize=None)
def _filter_consts(L):
    t = np.linspace(0.0, 1.0, L)[:, None]
    ang = 2.0 * np.pi * np.arange(L, dtype=np.float64)[:, None] / L
    bands = np.linspace(1e-4, HY_BANDS - 1, HY_BANDS)[None]
    z = np.concatenate([t, np.cos(bands * ang), -np.sin(bands * ang)], axis=-1)
    z = np.pad(z, ((0, 0), (0, HY_EMB_PAD - HY_EMB)))
    z[:, HY_EMB] = 1.0
    max_decay = math.log(HY_TARGET) / HY_FAST_DECAY
    min_decay = math.log(HY_TARGET) / HY_SLOW_DECAY
    rates = np.abs(np.linspace(min_decay, max_decay, HY_W))[None, :]
    return np.asarray(z, np.float32), np.asarray(rates, np.float32)


def _filter_mlp(zt_ref, w1_ref, b1_ref, w2_ref, b2_ref, w3_ref, fr_ref, w1_scr, w2_scr):
    hi = lax.Precision.HIGHEST
    tdims = (((0,), (0,)), ((), ()))
    w1_scr[...] = jnp.zeros_like(w1_scr)
    w1_scr[0:HY_EMB, :] = w1_ref[...]
    w1_scr[HY_EMB:HY_EMB + 1, :] = b1_ref[...]
    w2_scr[...] = jnp.zeros_like(w2_scr)
    w2_scr[0:HY_HIDDEN, :] = w2_ref[...]
    w2_scr[HY_HIDDEN:HY_HIDDEN + 1, :] = b2_ref[...]
    fr = fr_ref[...]
    zt = zt_ref[...]
    h1 = jnp.sin(lax.dot_general(w1_scr[...] * fr, zt, tdims, precision=hi,
                                 preferred_element_type=F32))
    h1 = jnp.concatenate([h1, jnp.ones((8, zt.shape[1]), F32)], axis=0)
    h2 = jnp.sin(lax.dot_general(w2_scr[...] * fr, h1, tdims, precision=hi,
                                 preferred_element_type=F32))
    h2_hi = h2.astype(BF16)
    h2_lo = (h2 - h2_hi.astype(F32)).astype(BF16)
    w3 = w3_ref[...]
    w3_hi = w3.astype(BF16)
    w3_lo = (w3 - w3_hi.astype(F32)).astype(BF16)
    lhs = jnp.concatenate([h2_hi, h2_lo, h2_hi, jnp.zeros_like(h2_hi)], axis=0)
    rhs = jnp.concatenate([w3_hi, w3_hi, w3_lo, jnp.zeros_like(w3_hi)], axis=0)
    h = lax.dot_general(lhs, rhs, tdims, preferred_element_type=F32)
    return h


def _filter_spectra(h_ref, o, rate_ref, fw_ref, oa_ref, ob_ref, od_ref, *, L, b):
    m = L // b
    row_l = lax.broadcasted_iota(jnp.int32, (L, HY_W), 0)
    row_b = lax.broadcasted_iota(jnp.int32, (b, HY_W), 0)
    t = row_l.astype(F32) * (1.0 / (L - 1))
    win = jnp.exp(-t * rate_ref[...])
    sg = jnp.where(row_b % 2 == 0, 1.0, -1.0)
    row0_l = row_l == 0
    row0_b = row_b == 0
    row0_8 = lax.broadcasted_iota(jnp.int32, (8, HY_W), 0) == 0
    base = o * 2 * HY_W
    fwd = h_ref[0:L, base:base + HY_W] * win
    bwd = jnp.where(row0_l, 0.0, h_ref[0:L, base + HY_W:base + 2 * HY_W] * win)
    nrm = (jnp.sum(jnp.abs(fwd), axis=0, keepdims=True)
           + jnp.sum(jnp.abs(bwd), axis=0, keepdims=True))
    inv = 1.0 / nrm
    fn = fwd * inv
    bn = bwd * inv
    xr, xn, xi, wr, wn, wi = [], [], [], [], [], []
    for r in range(m):
        p = _dot(fw_ref[...], fn[r * b:(r + 1) * b].astype(BF16))
        q = _dot(fw_ref[...], bn[r * b:(r + 1) * b].astype(BF16))
        xr.append(p[0:b])
        xn.append(p[b:b + 1])
        xi.append(jnp.where(row0_b, 0.0, p[b:2 * b]))
        wr.append(q[0:b])
        wn.append(q[b:b + 1])
        wi.append(jnp.where(row0_b, 0.0, -q[b:2 * b]))

    def emit(d, ka, kn, kb):
        oa_ref[o, d + m - 1] = ka
        ob_ref[o, d + m - 1] = kb
        od_ref[o, d + m - 1] = jnp.where(row0_8, kn, ka[0:8])

    emit(0, xr[0] + wr[0], xn[0] + wn[0], xi[0] + wi[0])
    for d in range(1, m):
        f0 = fn[(d - 1) * b:(d - 1) * b + 1]
        b0 = bn[(d - 1) * b:(d - 1) * b + 1]
        emit(d, xr[d] + sg * (xr[d - 1] - f0), xn[d] + (xn[d - 1] - f0),
             xi[d] + sg * xi[d - 1])
        emit(-d, wr[d] + sg * (wr[d - 1] - b0), wn[d] + (wn[d - 1] - b0),
             wi[d] + sg * wi[d - 1])


def _ret_init(decf_ref, decb_ref, mask_scr, vec_scr, cd_scr, C):
    H, E = RET_HEADS, HEAD_DIM
    scale = float(E) ** -0.5

    @pl.when(pl.program_id(0) == 0)
    def _():
        dec = jnp.concatenate([jnp.full((1, C), ref[h], F32)
                               for ref in (decf_ref, decb_ref) for h in range(H)], axis=0)
        lg = jnp.log(jax.nn.sigmoid(dec))
        cd_scr[...] = jnp.exp(float(C) * lg[:, 0:E])
        ii = lax.broadcasted_iota(jnp.int32, (C, C), 0)
        jj = lax.broadcasted_iota(jnp.int32, (C, C), 1)
        rel = (ii - jj).astype(F32)
        ri = lax.broadcasted_iota(jnp.int32, (C, E), 0).astype(F32)
        for h in range(H):
            lf = lg[h:h + 1, :]
            lb = lg[H + h:H + h + 1, :]
            mf = jnp.where(rel >= 0, jnp.exp(jnp.maximum(rel, 0.0) * lf), 0.0)
            mb = jnp.where(rel <= 0, jnp.exp(jnp.maximum(-rel, 0.0) * lb), 0.0)
            mask_scr[h] = scale * (mf + mb)
            lfe, lbe = lf[:, 0:E], lb[:, 0:E]
            vec_scr[h, 0] = jnp.exp((ri + 1.0) * lfe)
            vec_scr[h, 1] = jnp.exp((float(C) - ri) * lbe)
            vec_scr[h, 2] = scale * jnp.exp((float(C) - 1.0 - ri) * lfe)
            vec_scr[h, 3] = scale * jnp.exp(ri * lbe)


def _ret_core(hn, w_ref, s0f_ref, s0b_ref, wo_ref, y_ref, sf_ref, sb_ref, mask_scr, vec_scr,
              cd_scr, g_scr, *, L, C, nb, has_init, emit_state):
    n = L // C
    H, E = RET_HEADS, HEAD_DIM
    tdims = (((0,), (0,)), ((), ()))
    ndims = (((1,), (1,)), ((), ()))
    chains = [(s, h) for s in range(nb) for h in range(H)]
    rows = [slice(c * C, (c + 1) * C) for c in range(n)]
    qkvg = [_dot(hn[s], w_ref[...]) for s in range(nb)]

    def cols(s, part, h):
        return qkvg[s][:, part * RET_W + h * E:part * RET_W + (h + 1) * E]

    qb = [cols(s, 0, h).astype(BF16) for s, h in chains]
    kf = [cols(s, 1, h) for s, h in chains]
    kb = [k.astype(BF16) for k in kf]
    vb = [cols(s, 2, h).astype(BF16) for s, h in chains]
    att = [[lax.dot_general(qb[i][r], kb[i][r], ndims, preferred_element_type=F32) for r in rows]
           for i in range(len(chains))]
    prob = [[(att[i][c] * mask_scr[h]).astype(BF16) for c in range(n)]
            for i, (s, h) in enumerate(chains)]
    out = [[_dot(prob[i][c], vb[i][rows[c]]) for c in range(n)] for i in range(len(chains))]
    kv = []
    for i, (s, h) in enumerate(chains):
        dk2 = jnp.concatenate([vec_scr[h, 2], vec_scr[h, 3]], axis=1)
        per_c = []
        for r in rows:
            k2 = (jnp.concatenate([kf[i][r], kf[i][r]], axis=1) * dk2).astype(BF16)
            per_c.append(lax.dot_general(k2, vb[i][r], tdims, preferred_element_type=F32))
        kv.append(per_c)
    for i, (s, h) in enumerate(chains):
        cdf = cd_scr[h:h + 1, :]
        cdb = cd_scr[H + h:H + h + 1, :]
        sf_in, sb_in = [None] * n, [None] * n
        st = s0f_ref[s, h] if has_init else None
        for c in range(n):
            sf_in[c] = st
            kvc = kv[i][c][0:E]
            st = kvc if st is None else st * cdf + kvc
        if emit_state:
            sf_ref[s, h] = st
        st = s0b_ref[s, h] if has_init else None
        for c in range(n - 1, -1, -1):
            sb_in[c] = st
            kvc = kv[i][c][E:2 * E]
            st = kvc if st is None else st * cdb + kvc
        if emit_state:
            sb_ref[s, h] = st
        for c in range(n):
            if sf_in[c] is not None and sb_in[c] is not None:
                s2 = jnp.concatenate([sf_in[c], sb_in[c]], axis=1).astype(BF16)
                inter = _dot(qb[i][rows[c]], s2)
                out[i][c] = (out[i][c] + inter[:, 0:E] * vec_scr[h, 0]
                             + inter[:, E:2 * E] * vec_scr[h, 1])
            elif sf_in[c] is not None:
                out[i][c] = (out[i][c]
                             + _dot(qb[i][rows[c]], sf_in[c].astype(BF16)) * vec_scr[h, 0])
            elif sb_in[c] is not None:
                out[i][c] = (out[i][c]
                             + _dot(qb[i][rows[c]], sb_in[c].astype(BF16)) * vec_scr[h, 1])
    for i, (s, h) in enumerate(chains):
        for c in range(n):
            o = out[i][c]
            mu = jnp.mean(o, axis=-1, keepdims=True)
            d = o - mu
            var = jnp.mean(d * d, axis=-1, keepdims=True)
            on = d * lax.rsqrt(var + EPS)
            gg = cols(s, 3, h)[rows[c]]
            g_scr[s, rows[c], h * E:(h + 1) * E] = (gg * jax.nn.sigmoid(gg) * on).astype(BF16)
    for s in range(nb):
        y_ref[s] = _dot(g_scr[s], wo_ref[...])


def _hy_core(hn, w_ref, cw_ref, cb_ref, fw_ref, bw_ref, fa_ref, fb_ref, fd_ref, hb_ref, wo_ref,
             y_ref, *, L, W, b, nb):
    m = L // b
    CB = HY_CBLK
    nblk = HY_W // CB
    pos = lax.broadcasted_iota(jnp.int32, (L, CB), 0) % W
    first = pos == 0
    last = pos == W - 1
    chains = [(s, blk) for s in range(nb) for blk in range(nblk)]

    def short_conv(s, base, blk):
        cs = slice(base + blk * CB, base + (blk + 1) * CB)
        ug = _dot(hn[s], w_ref[:, cs])
        prev = jnp.where(first, 0.0, pltpu.roll(ug, 1, axis=0))
        nxt = jnp.where(last, 0.0, pltpu.roll(ug, L - 1, axis=0))
        taps = [cw_ref[:, t * N_HY + cs.start:t * N_HY + cs.stop] for t in range(3)]
        u = prev * taps[0] + ug * taps[1] + nxt * taps[2] + cb_ref[:, cs]
        return [u[j * b:(j + 1) * b] for j in range(m)]

    def long_conv(sigs, o):
        spec = [[_dot(fw_ref[...], sj.astype(BF16)) for sj in sig] for sig in sigs]
        prods = []
        for (s, blk), sp in zip(chains, spec):
            cs = slice(blk * CB, (blk + 1) * CB)
            per_i = []
            for i in range(m):
                yre = yim = yim8 = None
                for j in range(m):
                    d = i - j + m - 1
                    sre, sim = sp[j][0:b], sp[j][b:2 * b]
                    ka, kb = fa_ref[o, d, :, cs], fb_ref[o, d, :, cs]
                    tre = sre * ka - sim * kb
                    tim = sre * kb + sim * ka
                    t8 = sre[0:8] * kb[0:8] + sim[0:8] * fd_ref[o, d, :, cs]
                    yre = tre if yre is None else yre + tre
                    yim = tim if yim is None else yim + tim
                    yim8 = t8 if yim8 is None else yim8 + t8
                yim = jnp.concatenate([yim8, yim[8:]], axis=0)
                per_i.append((yre.astype(BF16), yim.astype(BF16)))
            prods.append(per_i)
        return [[_dot(bw_ref[:, 0:b], yre) + _dot(bw_ref[:, b:2 * b], yim) for yre, yim in per_i]
                for per_i in prods]

    hv = [short_conv(s, 0, blk) for s, blk in chains]
    hx1 = [short_conv(s, HY_W, blk) for s, blk in chains]
    hx2 = [short_conv(s, 2 * HY_W, blk) for s, blk in chains]

    def gate(hx, conv, sig, o):
        out = []
        for (s, blk), hxc, cc, sc in zip(chains, hx, conv, sig):
            bias = hb_ref[o:o + 1, blk * CB:(blk + 1) * CB]
            out.append([hxc[i] * (cc[i] + sc[i] * bias) for i in range(m)])
        return out

    z = gate(hx1, long_conv(hv, 0), hv, 0)
    z = gate(hx2, long_conv(z, 1), z, 1)
    for s in range(nb):
        for i in range(m):
            acc = None
            for blk in range(nblk):
                zc = z[chains.index((s, blk))][i].astype(BF16)
                part = _dot(zc, wo_ref[blk * CB:(blk + 1) * CB, :])
                acc = part if acc is None else acc + part
            y_ref[s, i * b:(i + 1) * b, :] = acc


def _mix_kernel(*refs, L, C, W, b, nb, do_ret, do_hy, has_init, emit_state, casts):
    it = iter(refs)
    x_ref, mod_ref, n1_ref = next(it), next(it), next(it)
    s0f_ref = s0b_ref = sf_ref = sb_ref = None
    if do_ret:
        wq_ref, decf_ref, decb_ref = next(it), next(it), next(it)
        if has_init:
            s0f_ref, s0b_ref = next(it), next(it)
        wo_ret_ref = next(it)
    if do_hy:
        hy_in = [next(it) for _ in range(10)]
    cast_srcs = [next(it) for _ in casts]
    if do_ret:
        y_ret_ref = next(it)
        if emit_state:
            sf_ref, sb_ref = next(it), next(it)
    if do_hy:
        y_hy_ref = next(it)
    cast_dsts = [next(it) for _ in range(_n_cast_outputs(casts))]
    if do_ret:
        ret_scr = [next(it) for _ in range(4)]
        _ret_init(decf_ref, decb_ref, ret_scr[0], ret_scr[1], ret_scr[2], C)
    _do_casts(casts, cast_srcs, cast_dsts)
    mod = mod_ref[0]
    hn = [_modnorm(x_ref[s], n1_ref[...], _mod(mod, 1), _mod(mod, 0)).astype(BF16)
          for s in range(nb)]
    if do_hy:
        _hy_core(hn, *hy_in, y_hy_ref, L=L, W=W, b=b, nb=nb)
    if do_ret:
        _ret_core(hn, wq_ref, s0f_ref, s0b_ref, wo_ret_ref, y_ret_ref, sf_ref, sb_ref, *ret_scr,
                  L=L, C=C, nb=nb, has_init=has_init, emit_state=emit_state)


def _mixer(x, mods, mod_row, norm1, *, nb, ret=None, hy=None, casts=()):
    B, L, D = x.shape
    H, E = RET_HEADS, HEAD_DIM
    C = min(RET_CHUNK, L)
    seq_spec = pl.BlockSpec((nb, L, D), lambda g: (g, 0, 0))
    in_specs = [seq_spec,
                pl.BlockSpec((1, 1, N_MOD * D), lambda g: (mod_row(g * nb), 0, 0)),
                _const_spec((1, D))]
    args = [x, mods, norm1]
    out_specs, out_shape, scratch = [], [], []
    has_init = emit_state = False
    W = b = None
    if ret is not None:
        w_qkvg, dec_f, dec_b, s0f, s0b, w_o, emit_state = ret
        has_init = s0f is not None
        st_spec = pl.BlockSpec((nb, H, E, E), lambda g: (g, 0, 0, 0))
        smem = pl.BlockSpec(memory_space=pltpu.SMEM)
        in_specs += [_const_spec((D, N_QKVG)), smem, smem]
        args += [w_qkvg, dec_f, dec_b]
        if has_init:
            in_specs += [st_spec, st_spec]
            args += [s0f, s0b]
        in_specs.append(_const_spec((RET_W, D)))
        args.append(w_o)
        out_specs.append(seq_spec)
        out_shape.append(jax.ShapeDtypeStruct((B, L, D), F32))
        if emit_state:
            out_specs += [st_spec, st_spec]
            out_shape += [jax.ShapeDtypeStruct((B, H, E, E), F32)] * 2
        scratch = [pltpu.VMEM((H, C, C), F32), pltpu.VMEM((H, 4, C, E), F32),
                   pltpu.VMEM((8, E), F32), pltpu.VMEM((nb, L, RET_W), BF16)]
    if hy is not None:
        w_hy, conv_w, conv_b, fw, bw, (fa, fb, fd), hy_bias, w_o, W, b = hy
        nd = fa.shape[1]
        in_specs += [_const_spec((D, N_HY)), _const_spec((1, 3 * N_HY)), _const_spec((1, N_HY)),
                     _const_spec((2 * b, b)), _const_spec((b, 2 * b)),
                     _const_spec((HY_ORDER, nd, b, HY_W)), _const_spec((HY_ORDER, nd, b, HY_W)),
                     _const_spec((HY_ORDER, nd, 8, HY_W)), _const_spec((HY_ORDER, HY_W)),
                     _const_spec((HY_W, D))]
        args += [w_hy, conv_w, conv_b, fw, bw, fa, fb, fd, hy_bias, w_o]
        out_specs.append(seq_spec)
        out_shape.append(jax.ShapeDtypeStruct((B, L, D), F32))
    c_in, c_out, c_shape, c_args = _cast_specs(casts, B // nb)
    name = ("ret" if ret is not None else "") + ("hy" if hy is not None else "")
    return pl.pallas_call(
        functools.partial(_mix_kernel, L=L, C=C, W=W, b=b, nb=nb, do_ret=ret is not None,
                          do_hy=hy is not None, has_init=has_init, emit_state=emit_state,
                          casts=tuple(cs for _, cs in casts)),
        grid=(B // nb,),
        in_specs=in_specs + c_in,
        out_specs=out_specs + c_out,
        out_shape=out_shape + c_shape,
        scratch_shapes=scratch,
        compiler_params=_params(1),
        name=f"{name}{L}",
    )(*args, *c_args)


def _mlp_kernel(x_ref, yr_ref, yh_ref, modp_ref, modq_ref, n1_ref, n2_ref, fg_ref, wg_ref,
                wout_ref, wfi_hbm, wfo_hbm, y_ref, x1_scr, h2_scr, wfi_ref, wfo_ref, sem, *,
                n_tiles):
    i = pl.program_id(0)
    wr = i % 2
    rd = 1 - wr
    nq = N_GATE // 4

    def pre_stages():
        mp = modp_ref[0]
        st = {}

        def p1():
            st["x"] = x_ref[...]
            st["hn"] = _modnorm(st["x"], n1_ref[...], _mod(mp, 1), _mod(mp, 0)).astype(BF16)

        def p2(q):
            def f():
                st["g%d" % q] = _dot(st["hn"], wg_ref[:, q * nq:(q + 1) * nq])
            return f

        def p3(h):
            def f():
                cs = slice(h * nq, (h + 1) * nq)
                st["mix%d" % h] = (jax.nn.sigmoid(st["g%d" % h]) * yr_ref[:, cs]
                                   + jax.nn.sigmoid(st["g%d" % (2 + h)]) * yh_ref[:, cs]
                                   ).astype(BF16)
            return f

        def p4():
            upd = (_dot(st["mix0"], wout_ref[0:nq, :]) + _dot(st["mix1"], wout_ref[nq:2 * nq, :]))
            st["x1"] = st["x"] + _mod(mp, 2) * upd

        def p5():
            x1_scr[wr] = st["x1"]
            h2_scr[wr] = _modnorm(st["x1"], n2_ref[...], _mod(mp, 4), _mod(mp, 3)).astype(BF16)

        return [p1, p2(0), p2(1), p2(2), p2(3), p3(0), p3(1), p4, p5]

    def ffn_stages():
        mq = modq_ref[0]
        st = {"acc": None}

        def f(j):
            def g():
                cs = slice(j * FF_CHUNK, (j + 1) * FF_CHUNK)
                h2 = h2_scr[rd]
                a = _dot(h2, wfi_ref[:, cs])
                b = _dot(h2, wfi_ref[:, D_FF + j * FF_CHUNK:D_FF + (j + 1) * FF_CHUNK])
                ff = (a * jax.nn.sigmoid(a) * b).astype(BF16)
                part = _dot(ff, wfo_ref[cs, :])
                st["acc"] = part if st["acc"] is None else st["acc"] + part
            return g

        def e():
            x2 = x1_scr[rd] + _mod(mq, 5) * st["acc"]
            ms = jnp.mean(x2 * x2, axis=-1, keepdims=True)
            y_ref[...] = x2 * lax.rsqrt(ms + EPS) * fg_ref[...]

        return [f(j) for j in range(D_FF // FF_CHUNK)] + [e]

    @pl.when(i == 0)
    def _():
        copies = [pltpu.make_async_copy(wfi_hbm, wfi_ref, sem.at[0]),
                  pltpu.make_async_copy(wfo_hbm, wfo_ref, sem.at[1])]
        for cp in copies:
            cp.start()
        for stage in pre_stages():
            stage()
        for cp in copies:
            cp.wait()

    @pl.when(jnp.logical_and(i > 0, i < n_tiles))
    def _():
        pre, ffn = pre_stages(), ffn_stages()
        order = []
        while pre or ffn:
            if ffn:
                order.append(ffn.pop(0))
            if pre:
                order.append(pre.pop(0))
        for stage in order:
            stage()

    @pl.when(i == n_tiles)
    def _():
        for stage in ffn_stages():
            stage()


def _mlp(x, y_ret, y_hy, mods, mod_row, norm1, norm2, final_g, w_gate, w_out, w_fi, w_fo):
    B, L, D = x.shape
    T = MLP_ROWS
    n_tiles = B * L // T
    flat = lambda a: a.reshape(B * L, D)
    pre_tile = lambda i: jnp.minimum(i, n_tiles - 1)
    post_tile = lambda i: jnp.maximum(i - 1, 0)
    act = pl.BlockSpec((T, D), lambda i: (pre_tile(i), 0))
    y = pl.pallas_call(
        functools.partial(_mlp_kernel, n_tiles=n_tiles),
        grid=(n_tiles + 1,),
        in_specs=[act, act, act,
                  pl.BlockSpec((1, 1, N_MOD * D),
                               lambda i: (mod_row((pre_tile(i) * T) // L), 0, 0)),
                  pl.BlockSpec((1, 1, N_MOD * D),
                               lambda i: (mod_row((post_tile(i) * T) // L), 0, 0)),
                  _const_spec((1, D)), _const_spec((1, D)), _const_spec((1, D)),
                  _const_spec((D, N_GATE)),
                  _const_spec((D, D)),
                  pl.BlockSpec(memory_space=pl.ANY),
                  pl.BlockSpec(memory_space=pl.ANY)],
        out_specs=pl.BlockSpec((T, D), lambda i: (post_tile(i), 0)),
        out_shape=jax.ShapeDtypeStruct((B * L, D), F32),
        scratch_shapes=[pltpu.VMEM((2, T, D), F32), pltpu.VMEM((2, T, D), BF16),
                        pltpu.VMEM((D, 2 * D_FF), BF16), pltpu.VMEM((D_FF, D), BF16),
                        pltpu.SemaphoreType.DMA((2,))],
        compiler_params=_params(1),
        name=f"mlp{L}",
    )(flat(x), flat(y_ret), flat(y_hy), mods, mods, norm1, norm2, final_g, w_gate, w_out, w_fi,
      w_fo)
    return y.reshape(B, L, D)


def kernel(x_prompt, x_sample, state_ret_fwd, state_ret_bwd, c, c_ctx, norm1_g, norm2_g, w_ada,
           b_ada, w_in, ret_decay_fwd, ret_decay_bwd, hy_conv_w, hy_conv_b, hy_pos_w1, hy_pos_b1,
           hy_pos_w2, hy_pos_b2, hy_pos_w3, hy_sin_freq, hy_bias, w_ret_o, w_hy_o, w_out,
           w_ffn_in, w_ffn_out, final_g):
    assert w_in.shape[0] == 1, "single-layer configuration"
    nb_lat = x_sample.shape[0]
    l_ctx = x_prompt.shape[1]

    assert 1 + nb_lat <= MOD_ROWS
    norm1 = norm1_g[0][None, :]
    norm2 = norm2_g[0][None, :]
    fg = final_g[None, :]
    conv_w = hy_conv_w[0].reshape(1, 3 * N_HY)
    conv_b = hy_conv_b[0][None, :]
    filt_params = (hy_pos_w1[0], hy_pos_b1, hy_pos_w2[0], hy_pos_b2, hy_pos_w3[0], hy_sin_freq)

    n_hy_end = N_QKVG + N_HY
    w_in_parts = (slice(0, N_QKVG), slice(N_QKVG, n_hy_end), slice(n_hy_end, N_IN))
    groups = []
    for L in (l_ctx, x_sample.shape[1]):
        blk = min(HY_TBLK, L)
        fw, _ = _dft_mats(blk)
        z, rates = _filter_consts(L)
        groups.append((L, blk, jnp.asarray(z.T), jnp.asarray(rates), fw))
    (mods, fa_c, fb_c, fd_c, fa_l, fb_l, fd_l, w_qkvg, w_hy, w_gate, w_ret_o_b,
     w_hy_o_b) = _ada(c_ctx[None, :], c, w_ada[0], b_ada, filt_params, groups,
                      casts=[(w_in[0], w_in_parts), (w_ret_o[0], None), (w_hy_o[0], None)])

    def branches(x, filt, s0f, s0b, grid_w, emit_state):
        blk = min(HY_TBLK, x.shape[1])
        fw, bw = _dft_mats(blk)
        ret = (w_qkvg, ret_decay_fwd[0], ret_decay_bwd[0], s0f, s0b, w_ret_o_b, emit_state)
        hy = (w_hy, conv_w, conv_b, fw, bw, filt, hy_bias[0], w_hy_o_b, grid_w, blk)
        return ret, hy

    ctx_row = lambda b: 0
    lat_row = lambda b: b + 1
    ret, hy = branches(x_prompt, (fa_c, fb_c, fd_c), None, None, l_ctx, True)
    y_ret_c, s_f, s_b, y_hy_c, w_fi_b, w_fo_b, w_out_b = _mixer(
        x_prompt, mods, ctx_row, norm1, nb=CTX_SEQS, ret=ret, hy=hy,
        casts=[(w_ffn_in[0], None), (w_ffn_out[0], None), (w_out[0], None)])
    y_prompt = _mlp(x_prompt, y_ret_c, y_hy_c, mods, ctx_row, norm1, norm2, fg, w_gate, w_out_b,
                    w_fi_b, w_fo_b)
    ret, hy = branches(x_sample, (fa_l, fb_l, fd_l), state_ret_fwd[:, 0], state_ret_bwd[:, 0],
                       GRID_W, False)
    y_ret_l, = _mixer(x_sample, mods, lat_row, norm1, nb=1, ret=ret)
    y_hy_l, = _mixer(x_sample, mods, lat_row, norm1, nb=1, hy=hy)
    y_sample = _mlp(x_sample, y_ret_l, y_hy_l, mods, lat_row, norm1, norm2, fg, w_gate,
                    w_out_b, w_fi_b, w_fo_b)
    return (y_prompt, y_sample, s_f[:, None], s_b[:, None])
```

```python
import functools
import math

import numpy as np
import jax
import jax.numpy as jnp
from jax import lax
from jax.experimental import pallas as pl
from jax.experimental.pallas import tpu as pltpu

F32 = jnp.float32
BF16 = jnp.bfloat16

D_MODEL = 1024
RET_HEADS = 4
HEAD_DIM = 128
RET_W = RET_HEADS * HEAD_DIM
HY_W = 512
HY_ORDER = 2
HY_BANDS = 16
HY_EMB = 1 + 2 * HY_BANDS
HY_EMB_PAD = 40
HY_HIDDEN = 64
HY_FAST_DECAY = 0.3
HY_SLOW_DECAY = 1.5
HY_TARGET = 1e-2
D_FF = 2816
N_QKVG = 4 * RET_W
N_HY = 3 * HY_W
N_GATE = 2 * D_MODEL
N_IN = N_QKVG + N_HY + N_GATE
N_MOD = 6
MOD_ROWS = 8
EPS = 1e-6
GRID_W = 64
RET_CHUNK = 256
HY_CBLK = 256
HY_TBLK = 512
CTX_SEQS = 2
MLP_ROWS = 512
FF_CHUNK = 256
FILTER_ONE_STEP_LEN = 256
ADA_COLS = 768
VMEM_LIMIT = 56 * 1024 * 1024


def _const_spec(shape):
    nd = len(shape)
    return pl.BlockSpec(shape, lambda *_: (0,) * nd, pipeline_mode=pl.Buffered(1))


def _params(n_axes):
    return pltpu.CompilerParams(dimension_semantics=("arbitrary",) * n_axes,
                                vmem_limit_bytes=VMEM_LIMIT)


def _modnorm(x, g, scale, shift):
    ms = jnp.mean(x * x, axis=-1, keepdims=True)
    return (x * lax.rsqrt(ms + EPS) * g) * (1.0 + scale) + shift


def _mod(mod, k):
    return mod[:, k * D_MODEL:(k + 1) * D_MODEL]


def _dot(a, b):
    return jnp.dot(a, b, preferred_element_type=F32)


def _cast_specs(casts, steps):
    in_specs, out_specs, out_shape, args = [], [], [], []
    for arr, col_slices in casts:
        rows, width = arr.shape
        rb = rows // steps
        assert rb * steps == rows and rb % 16 == 0
        in_specs.append(pl.BlockSpec((rb, width), lambda g: (g, 0)))
        args.append(arr)
        for cs in col_slices or (slice(0, width),):
            cols = cs.stop - cs.start
            out_specs.append(pl.BlockSpec((rb, cols), lambda g: (g, 0)))
            out_shape.append(jax.ShapeDtypeStruct((rows, cols), BF16))
    return in_specs, out_specs, out_shape, args


def _n_cast_outputs(col_slices_per_src):
    return sum(1 if s is None else len(s) for s in col_slices_per_src)


def _do_casts(col_slices_per_src, srcs, dsts):
    dsts = iter(dsts)
    for col_slices, src in zip(col_slices_per_src, srcs):
        if col_slices is None:
            next(dsts)[...] = src[...].astype(BF16)
        else:
            for cs in col_slices:
                next(dsts)[...] = src[:, cs].astype(BF16)


def _ada_kernel(*refs, casts, groups, n_steps):
    it = iter(refs)
    cctx_ref, c_ref, w_ref, b_ref = next(it), next(it), next(it), next(it)
    mlp_refs = [next(it) for _ in range(6)]
    g_in = [[next(it) for _ in range(3)] for _ in groups]
    cast_srcs = [next(it) for _ in casts]
    o_ref = next(it)
    g_out = [[next(it) for _ in range(3)] for _ in groups]
    cast_dsts = [next(it) for _ in range(_n_cast_outputs(casts))]
    h_scr, cond_scr, w1_scr, w2_scr = next(it), next(it), next(it), next(it)
    g_scr = [[next(it) for _ in range(3)] for _ in groups]
    sem = next(it)
    _do_casts(casts, cast_srcs, cast_dsts)
    nlat = c_ref.shape[0]
    cond_scr[...] = jnp.zeros_like(cond_scr)
    cond_scr[0:1, :] = cctx_ref[...]
    cond_scr[1:1 + nlat, :] = c_ref[...]
    c = cond_scr[...]
    s = (c * jax.nn.sigmoid(c)).astype(BF16)
    res = _dot(s, w_ref[...].astype(BF16)) + b_ref[...]
    for r in range(res.shape[0]):
        o_ref[r] = res[r:r + 1, :]

    def writeback(g, o):
        return [pltpu.make_async_copy(scr.at[o], out.at[o], sem.at[(g * HY_ORDER + o) * 3 + k])
                for k, (scr, out) in enumerate(zip(g_scr[g], g_out[g]))]

    step = 0
    for g, ((L, blk), (zt_ref, rate_ref, fw_ref)) in enumerate(zip(groups, g_in)):
        def mlp_job(zt_ref=zt_ref, L=L):
            h_scr[0:L, :] = _filter_mlp(zt_ref, *mlp_refs, w1_scr, w2_scr)

        def order_job(o, g=g, rate_ref=rate_ref, fw_ref=fw_ref, L=L, blk=blk):
            _filter_spectra(h_scr, o, rate_ref, fw_ref, *g_scr[g], L=L, b=blk)
            for cp in writeback(g, o):
                cp.start()

        def all_job(mlp_job=mlp_job, order_job=order_job):
            mlp_job()
            for o in range(HY_ORDER):
                order_job(o)

        if L <= FILTER_ONE_STEP_LEN:
            jobs = [all_job]
        else:
            jobs = [mlp_job] + [functools.partial(order_job, o) for o in range(HY_ORDER)]
        for job in jobs:
            pl.when(pl.program_id(0) == step)(job)
            step += 1

    @pl.when(pl.program_id(0) == n_steps - 1)
    def _():
        for g in range(len(groups)):
            for o in range(HY_ORDER):
                for cp in writeback(g, o):
                    cp.wait()


def _ada(c_ctx, c, w, b, filt_params, groups, casts=()):
    n = w.shape[1]
    steps = n // ADA_COLS
    c_in, c_out, c_shape, c_args = _cast_specs(casts, steps)
    in_specs = [_const_spec(c_ctx.shape), _const_spec(c.shape),
                pl.BlockSpec((D_MODEL, ADA_COLS), lambda j: (0, j)),
                pl.BlockSpec((1, ADA_COLS), lambda j: (0, j))]
    args = [c_ctx, c, w, b]
    for p in filt_params:
        in_specs.append(_const_spec(p.shape))
        args.append(p)
    out_specs = [pl.BlockSpec((MOD_ROWS, 1, ADA_COLS), lambda j: (0, 0, j))]
    out_shape = [jax.ShapeDtypeStruct((MOD_ROWS, 1, n), F32)]
    max_len = 8
    spectra_scratch = []
    for L, blk, *consts in groups:
        nd = 2 * (L // blk) - 1
        max_len = max(max_len, L)
        for cst in consts:
            in_specs.append(_const_spec(cst.shape))
            args.append(cst)
        for shp in ((HY_ORDER, nd, blk, HY_W), (HY_ORDER, nd, blk, HY_W), (HY_ORDER, nd, 8, HY_W)):
            out_specs.append(pl.BlockSpec(memory_space=pl.ANY))
            out_shape.append(jax.ShapeDtypeStruct(shp, F32))
            spectra_scratch.append(pltpu.VMEM(shp, F32))
    n_jobs = sum(1 if L <= FILTER_ONE_STEP_LEN else 1 + HY_ORDER for L, *_ in groups)
    assert n_jobs <= steps
    return pl.pallas_call(
        functools.partial(_ada_kernel, casts=tuple(cs for _, cs in casts),
                          groups=tuple((L, blk) for L, blk, *_ in groups), n_steps=steps),
        grid=(steps,),
        in_specs=in_specs + c_in,
        out_specs=out_specs + c_out,
        out_shape=out_shape + c_shape,
        scratch_shapes=[pltpu.VMEM((max_len, HY_ORDER * 2 * HY_W), F32),
                        pltpu.VMEM((MOD_ROWS, D_MODEL), F32),
                        pltpu.VMEM((HY_EMB_PAD, HY_HIDDEN), F32),
                        pltpu.VMEM((HY_HIDDEN + 8, HY_HIDDEN), F32)]
        + spectra_scratch + [pltpu.SemaphoreType.DMA((len(groups) * HY_ORDER * 3,))],
        compiler_params=_params(1),
        name="ada",
    )(*args, *c_args)


@functools.lru_cache(maxsize=None)
def _dft_mats(L):
    n = 2 * L
    t = np.arange(L, dtype=np.int64)
    f = np.arange(L, dtype=np.int64)
    ang = 2.0 * np.pi * ((f[:, None] * t[None, :]) % n).astype(np.float64) / n
    cos = np.cos(ang)
    sin = np.sin(ang)
    nyq = np.where(t % 2 == 0, 1.0, -1.0)
    fwd = np.concatenate([cos, -sin], axis=0)
    fwd[L] = nyq
    wre = np.full((L,), 2.0 / n)
    wre[0] = 1.0 / n
    inv = np.concatenate([cos.T * wre[None, :], -sin.T * (2.0 / n)], axis=1)
    inv[:, L] = nyq / n
    return jnp.asarray(fwd, dtype=BF16), jnp.asarray(inv, dtype=BF16)


@functools.lru_cache(maxsize=None)
def _filter_consts(L):
    t = np.linspace(0.0, 1.0, L)[:, None]
    ang = 2.0 * np.pi * np.arange(L, dtype=np.float64)[:, None] / L
    bands = np.linspace(1e-4, HY_BANDS - 1, HY_BANDS)[None]
    z = np.concatenate([t, np.cos(bands * ang), -np.sin(bands * ang)], axis=-1)
    z = np.pad(z, ((0, 0), (0, HY_EMB_PAD - HY_EMB)))
    z[:, HY_EMB] = 1.0
    max_decay = math.log(HY_TARGET) / HY_FAST_DECAY
    min_decay = math.log(HY_TARGET) / HY_SLOW_DECAY
    rates = np.abs(np.linspace(min_decay, max_decay, HY_W))[None, :]
    return np.asarray(z, np.float32), np.asarray(rates, np.float32)


def _filter_mlp(zt_ref, w1_ref, b1_ref, w2_ref, b2_ref, w3_ref, fr_ref, w1_scr, w2_scr):
    hi = lax.Precision.HIGHEST
    tdims = (((0,), (0,)), ((), ()))
    w1_scr[...] = jnp.zeros_like(w1_scr)
    w1_scr[0:HY_EMB, :] = w1_ref[...]
    w1_scr[HY_EMB:HY_EMB + 1, :] = b1_ref[...]
    w2_scr[...] = jnp.zeros_like(w2_scr)
    w2_scr[0:HY_HIDDEN, :] = w2_ref[...]
    w2_scr[HY_HIDDEN:HY_HIDDEN + 1, :] = b2_ref[...]
    fr = fr_ref[...]
    zt = zt_ref[...]
    h1 = jnp.sin(lax.dot_general(w1_scr[...] * fr, zt, tdims, precision=hi,
                                 preferred_element_type=F32))
    h1 = jnp.concatenate([h1, jnp.ones((8, zt.shape[1]), F32)], axis=0)
    h2 = jnp.sin(lax.dot_general(w2_scr[...] * fr, h1, tdims, precision=hi,
                                 preferred_element_type=F32))
    h2_hi = h2.astype(BF16)
    h2_lo = (h2 - h2_hi.astype(F32)).astype(BF16)
    w3 = w3_ref[...]
    w3_hi = w3.astype(BF16)
    w3_lo = (w3 - w3_hi.astype(F32)).astype(BF16)
    lhs = jnp.concatenate([h2_hi, h2_lo, h2_hi, jnp.zeros_like(h2_hi)], axis=0)
    rhs = jnp.concatenate([w3_hi, w3_hi, w3_lo, jnp.zeros_like(w3_hi)], axis=0)
    h = lax.dot_general(lhs, rhs, tdims, preferred_element_type=F32)
    return h


def _filter_spectra(h_ref, o, rate_ref, fw_ref, oa_ref, ob_ref, od_ref, *, L, b):
    m = L // b
    row_l = lax.broadcasted_iota(jnp.int32, (L, HY_W), 0)
    row_b = lax.broadcasted_iota(jnp.int32, (b, HY_W), 0)
    t = row_l.astype(F32) * (1.0 / (L - 1))
    win = jnp.exp(-t * rate_ref[...])
    sg = jnp.where(row_b % 2 == 0, 1.0, -1.0)
    row0_l = row_l == 0
    row0_b = row_b == 0
    row0_8 = lax.broadcasted_iota(jnp.int32, (8, HY_W), 0) == 0
    base = o * 2 * HY_W
    fwd = h_ref[0:L, base:base + HY_W] * win
    bwd = jnp.where(row0_l, 0.0, h_ref[0:L, base + HY_W:base + 2 * HY_W] * win)
    nrm = (jnp.sum(jnp.abs(fwd), axis=0, keepdims=True)
           + jnp.sum(jnp.abs(bwd), axis=0, keepdims=True))
    inv = 1.0 / nrm
    fn = fwd * inv
    bn = bwd * inv
    xr, xn, xi, wr, wn, wi = [], [], [], [], [], []
    for r in range(m):
        p = _dot(fw_ref[...], fn[r * b:(r + 1) * b].astype(BF16))
        q = _dot(fw_ref[...], bn[r * b:(r + 1) * b].astype(BF16))
        xr.append(p[0:b])
        xn.append(p[b:b + 1])
        xi.append(jnp.where(row0_b, 0.0, p[b:2 * b]))
        wr.append(q[0:b])
        wn.append(q[b:b + 1])
        wi.append(jnp.where(row0_b, 0.0, -q[b:2 * b]))

    def emit(d, ka, kn, kb):
        oa_ref[o, d + m - 1] = ka
        ob_ref[o, d + m - 1] = kb
        od_ref[o, d + m - 1] = jnp.where(row0_8, kn, ka[0:8])

    emit(0, xr[0] + wr[0], xn[0] + wn[0], xi[0] + wi[0])
    for d in range(1, m):
        f0 = fn[(d - 1) * b:(d - 1) * b + 1]
        b0 = bn[(d - 1) * b:(d - 1) * b + 1]
        emit(d, xr[d] + sg * (xr[d - 1] - f0), xn[d] + (xn[d - 1] - f0),
             xi[d] + sg * xi[d - 1])
        emit(-d, wr[d] + sg * (wr[d - 1] - b0), wn[d] + (wn[d - 1] - b0),
             wi[d] + sg * wi[d - 1])


def _ret_init(decf_ref, decb_ref, mask_scr, vec_scr, cd_scr, C):
    H, E = RET_HEADS, HEAD_DIM
    scale = float(E) ** -0.5

    @pl.when(pl.program_id(0) == 0)
    def _():
        dec = jnp.concatenate([jnp.full((1, C), ref[h], F32)
                               for ref in (decf_ref, decb_ref) for h in range(H)], axis=0)
        lg = jnp.log(jax.nn.sigmoid(dec))
        cd_scr[...] = jnp.exp(float(C) * lg[:, 0:E])
        ii = lax.broadcasted_iota(jnp.int32, (C, C), 0)
        jj = lax.broadcasted_iota(jnp.int32, (C, C), 1)
        rel = (ii - jj).astype(F32)
        ri = lax.broadcasted_iota(jnp.int32, (C, E), 0).astype(F32)
        for h in range(H):
            lf = lg[h:h + 1, :]
            lb = lg[H + h:H + h + 1, :]
            mf = jnp.where(rel >= 0, jnp.exp(jnp.maximum(rel, 0.0) * lf), 0.0)
            mb = jnp.where(rel <= 0, jnp.exp(jnp.maximum(-rel, 0.0) * lb), 0.0)
            mask_scr[h] = scale * (mf + mb)
            lfe, lbe = lf[:, 0:E], lb[:, 0:E]
            vec_scr[h, 0] = jnp.exp((ri + 1.0) * lfe)
            vec_scr[h, 1] = jnp.exp((float(C) - ri) * lbe)
            vec_scr[h, 2] = scale * jnp.exp((float(C) - 1.0 - ri) * lfe)
            vec_scr[h, 3] = scale * jnp.exp(ri * lbe)


def _ret_core(hn, w_ref, s0f_ref, s0b_ref, wo_ref, y_ref, sf_ref, sb_ref, mask_scr, vec_scr,
              cd_scr, g_scr, *, L, C, nb, has_init, emit_state):
    n = L // C
    H, E = RET_HEADS, HEAD_DIM
    tdims = (((0,), (0,)), ((), ()))
    ndims = (((1,), (1,)), ((), ()))
    chains = [(s, h) for s in range(nb) for h in range(H)]
    rows = [slice(c * C, (c + 1) * C) for c in range(n)]
    qkvg = [_dot(hn[s], w_ref[...]) for s in range(nb)]

    def cols(s, part, h):
        return qkvg[s][:, part * RET_W + h * E:part * RET_W + (h + 1) * E]

    qb = [cols(s, 0, h).astype(BF16) for s, h in chains]
    kf = [cols(s, 1, h) for s, h in chains]
    kb = [k.astype(BF16) for k in kf]
    vb = [cols(s, 2, h).astype(BF16) for s, h in chains]
    att = [[lax.dot_general(qb[i][r], kb[i][r], ndims, preferred_element_type=F32) for r in rows]
           for i in range(len(chains))]
    prob = [[(att[i][c] * mask_scr[h]).astype(BF16) for c in range(n)]
            for i, (s, h) in enumerate(chains)]
    out = [[_dot(prob[i][c], vb[i][rows[c]]) for c in range(n)] for i in range(len(chains))]
    kv = []
    for i, (s, h) in enumerate(chains):
        dk2 = jnp.concatenate([vec_scr[h, 2], vec_scr[h, 3]], axis=1)
        per_c = []
        for r in rows:
            k2 = (jnp.concatenate([kf[i][r], kf[i][r]], axis=1) * dk2).astype(BF16)
            per_c.append(lax.dot_general(k2, vb[i][r], tdims, preferred_element_type=F32))
        kv.append(per_c)
    for i, (s, h) in enumerate(chains):
        cdf = cd_scr[h:h + 1, :]
        cdb = cd_scr[H + h:H + h + 1, :]
        sf_in, sb_in = [None] * n, [None] * n
        st = s0f_ref[s, h] if has_init else None
        for c in range(n):
            sf_in[c] = st
            kvc = kv[i][c][0:E]
            st = kvc if st is None else st * cdf + kvc
        if emit_state:
            sf_ref[s, h] = st
        st = s0b_ref[s, h] if has_init else None
        for c in range(n - 1, -1, -1):
            sb_in[c] = st
            kvc = kv[i][c][E:2 * E]
            st = kvc if st is None else st * cdb + kvc
        if emit_state:
            sb_ref[s, h] = st
        for c in range(n):
            if sf_in[c] is not None and sb_in[c] is not None:
                s2 = jnp.concatenate([sf_in[c], sb_in[c]], axis=1).astype(BF16)
                inter = _dot(qb[i][rows[c]], s2)
                out[i][c] = (out[i][c] + inter[:, 0:E] * vec_scr[h, 0]
                             + inter[:, E:2 * E] * vec_scr[h, 1])
            elif sf_in[c] is not None:
                out[i][c] = (out[i][c]
                             + _dot(qb[i][rows[c]], sf_in[c].astype(BF16)) * vec_scr[h, 0])
            elif sb_in[c] is not None:
                out[i][c] = (out[i][c]
                             + _dot(qb[i][rows[c]], sb_in[c].astype(BF16)) * vec_scr[h, 1])
    for i, (s, h) in enumerate(chains):
        for c in range(n):
            o = out[i][c]
            mu = jnp.mean(o, axis=-1, keepdims=True)
            d = o - mu
            var = jnp.mean(d * d, axis=-1, keepdims=True)
            on = d * lax.rsqrt(var + EPS)
            gg = cols(s, 3, h)[rows[c]]
            g_scr[s, rows[c], h * E:(h + 1) * E] = (gg * jax.nn.sigmoid(gg) * on).astype(BF16)
    for s in range(nb):
        y_ref[s] = _dot(g_scr[s], wo_ref[...])


def _hy_core(hn, w_ref, cw_ref, cb_ref, fw_ref, bw_ref, fa_ref, fb_ref, fd_ref, hb_ref, wo_ref,
             y_ref, *, L, W, b, nb):
    m = L // b
    CB = HY_CBLK
    nblk = HY_W // CB
    pos = lax.broadcasted_iota(jnp.int32, (L, CB), 0) % W
    first = pos == 0
    last = pos == W - 1
    chains = [(s, blk) for s in range(nb) for blk in range(nblk)]

    def short_conv(s, base, blk):
        cs = slice(base + blk * CB, base + (blk + 1) * CB)
        ug = _dot(hn[s], w_ref[:, cs])
        prev = jnp.where(first, 0.0, pltpu.roll(ug, 1, axis=0))
        nxt = jnp.where(last, 0.0, pltpu.roll(ug, L - 1, axis=0))
        taps = [cw_ref[:, t * N_HY + cs.start:t * N_HY + cs.stop] for t in range(3)]
        u = prev * taps[0] + ug * taps[1] + nxt * taps[2] + cb_ref[:, cs]
        return [u[j * b:(j + 1) * b] for j in range(m)]

    def long_conv(sigs, o):
        spec = [[_dot(fw_ref[...], sj.astype(BF16)) for sj in sig] for sig in sigs]
        prods = []
        for (s, blk), sp in zip(chains, spec):
            cs = slice(blk * CB, (blk + 1) * CB)
            per_i = []
            for i in range(m):
                yre = yim = yim8 = None
                for j in range(m):
                    d = i - j + m - 1
                    sre, sim = sp[j][0:b], sp[j][b:2 * b]
                    ka, kb = fa_ref[o, d, :, cs], fb_ref[o, d, :, cs]
                    tre = sre * ka - sim * kb
                    tim = sre * kb + sim * ka
                    t8 = sre[0:8] * kb[0:8] + sim[0:8] * fd_ref[o, d, :, cs]
                    yre = tre if yre is None else yre + tre
                    yim = tim if yim is None else yim + tim
                    yim8 = t8 if yim8 is None else yim8 + t8
                yim = jnp.concatenate([yim8, yim[8:]], axis=0)
                per_i.append((yre.astype(BF16), yim.astype(BF16)))
            prods.append(per_i)
        return [[_dot(bw_ref[:, 0:b], yre) + _dot(bw_ref[:, b:2 * b], yim) for yre, yim in per_i]
                for per_i in prods]

    hv = [short_conv(s, 0, blk) for s, blk in chains]
    hx1 = [short_conv(s, HY_W, blk) for s, blk in chains]
    hx2 = [short_conv(s, 2 * HY_W, blk) for s, blk in chains]

    def gate(hx, conv, sig, o):
        out = []
        for (s, blk), hxc, cc, sc in zip(chains, hx, conv, sig):
            bias = hb_ref[o:o + 1, blk * CB:(blk + 1) * CB]
            out.append([hxc[i] * (cc[i] + sc[i] * bias) for i in range(m)])
        return out

    z = gate(hx1, long_conv(hv, 0), hv, 0)
    z = gate(hx2, long_conv(z, 1), z, 1)
    for s in range(nb):
        for i in range(m):
            acc = None
            for blk in range(nblk):
                zc = z[chains.index((s, blk))][i].astype(BF16)
                part = _dot(zc, wo_ref[blk * CB:(blk + 1) * CB, :])
                acc = part if acc is None else acc + part
            y_ref[s, i * b:(i + 1) * b, :] = acc


def _mix_kernel(*refs, L, C, W, b, nb, do_ret, do_hy, has_init, emit_state, casts):
    it = iter(refs)
    x_ref, mod_ref, n1_ref = next(it), next(it), next(it)
    s0f_ref = s0b_ref = sf_ref = sb_ref = None
    if do_ret:
        wq_ref, decf_ref, decb_ref = next(it), next(it), next(it)
        if has_init:
            s0f_ref, s0b_ref = next(it), next(it)
        wo_ret_ref = next(it)
    if do_hy:
        hy_in = [next(it) for _ in range(10)]
    cast_srcs = [next(it) for _ in casts]
    if do_ret:
        y_ret_ref = next(it)
        if emit_state:
            sf_ref, sb_ref = next(it), next(it)
    if do_hy:
        y_hy_ref = next(it)
    cast_dsts = [next(it) for _ in range(_n_cast_outputs(casts))]
    if do_ret:
        ret_scr = [next(it) for _ in range(4)]
        _ret_init(decf_ref, decb_ref, ret_scr[0], ret_scr[1], ret_scr[2], C)
    _do_casts(casts, cast_srcs, cast_dsts)
    mod = mod_ref[0]
    hn = [_modnorm(x_ref[s], n1_ref[...], _mod(mod, 1), _mod(mod, 0)).astype(BF16)
          for s in range(nb)]
    if do_hy:
        _hy_core(hn, *hy_in, y_hy_ref, L=L, W=W, b=b, nb=nb)
    if do_ret:
        _ret_core(hn, wq_ref, s0f_ref, s0b_ref, wo_ret_ref, y_ret_ref, sf_ref, sb_ref, *ret_scr,
                  L=L, C=C, nb=nb, has_init=has_init, emit_state=emit_state)


def _mixer(x, mods, mod_row, norm1, *, nb, ret=None, hy=None, casts=()):
    B, L, D = x.shape
    H, E = RET_HEADS, HEAD_DIM
    C = min(RET_CHUNK, L)
    seq_spec = pl.BlockSpec((nb, L, D), lambda g: (g, 0, 0))
    in_specs = [seq_spec,
                pl.BlockSpec((1, 1, N_MOD * D), lambda g: (mod_row(g * nb), 0, 0)),
                _const_spec((1, D))]
    args = [x, mods, norm1]
    out_specs, out_shape, scratch = [], [], []
    has_init = emit_state = False
    W = b = None
    if ret is not None:
        w_qkvg, dec_f, dec_b, s0f, s0b, w_o, emit_state = ret
        has_init = s0f is not None
        st_spec = pl.BlockSpec((nb, H, E, E), lambda g: (g, 0, 0, 0))
        smem = pl.BlockSpec(memory_space=pltpu.SMEM)
        in_specs += [_const_spec((D, N_QKVG)), smem, smem]
        args += [w_qkvg, dec_f, dec_b]
        if has_init:
            in_specs += [st_spec, st_spec]
            args += [s0f, s0b]
        in_specs.append(_const_spec((RET_W, D)))
        args.append(w_o)
        out_specs.append(seq_spec)
        out_shape.append(jax.ShapeDtypeStruct((B, L, D), F32))
        if emit_state:
            out_specs += [st_spec, st_spec]
            out_shape += [jax.ShapeDtypeStruct((B, H, E, E), F32)] * 2
        scratch = [pltpu.VMEM((H, C, C), F32), pltpu.VMEM((H, 4, C, E), F32),
                   pltpu.VMEM((8, E), F32), pltpu.VMEM((nb, L, RET_W), BF16)]
    if hy is not None:
        w_hy, conv_w, conv_b, fw, bw, (fa, fb, fd), hy_bias, w_o, W, b = hy
        nd = fa.shape[1]
        in_specs += [_const_spec((D, N_HY)), _const_spec((1, 3 * N_HY)), _const_spec((1, N_HY)),
                     _const_spec((2 * b, b)), _const_spec((b, 2 * b)),
                     _const_spec((HY_ORDER, nd, b, HY_W)), _const_spec((HY_ORDER, nd, b, HY_W)),
                     _const_spec((HY_ORDER, nd, 8, HY_W)), _const_spec((HY_ORDER, HY_W)),
                     _const_spec((HY_W, D))]
        args += [w_hy, conv_w, conv_b, fw, bw, fa, fb, fd, hy_bias, w_o]
        out_specs.append(seq_spec)
        out_shape.append(jax.ShapeDtypeStruct((B, L, D), F32))
    c_in, c_out, c_shape, c_args = _cast_specs(casts, B // nb)
    name = ("ret" if ret is not None else "") + ("hy" if hy is not None else "")
    return pl.pallas_call(
        functools.partial(_mix_kernel, L=L, C=C, W=W, b=b, nb=nb, do_ret=ret is not None,
                          do_hy=hy is not None, has_init=has_init, emit_state=emit_state,
                          casts=tuple(cs for _, cs in casts)),
        grid=(B // nb,),
        in_specs=in_specs + c_in,
        out_specs=out_specs + c_out,
        out_shape=out_shape + c_shape,
        scratch_shapes=scratch,
        compiler_params=_params(1),
        name=f"{name}{L}",
    )(*args, *c_args)


def _mlp_kernel(x_ref, yr_ref, yh_ref, modp_ref, modq_ref, n1_ref, n2_ref, fg_ref, wg_ref,
                wout_ref, wfi_hbm, wfo_hbm, y_ref, x1_scr, h2_scr, wfi_ref, wfo_ref, sem, *,
                n_tiles):
    i = pl.program_id(0)
    wr = i % 2
    rd = 1 - wr
    nq = N_GATE // 4

    def pre_stages():
        mp = modp_ref[0]
        st = {}

        def p1():
            st["x"] = x_ref[...]
            st["hn"] = _modnorm(st["x"], n1_ref[...], _mod(mp, 1), _mod(mp, 0)).astype(BF16)

        def p2(q):
            def f():
                st["g%d" % q] = _dot(st["hn"], wg_ref[:, q * nq:(q + 1) * nq])
            return f

        def p3(h):
            def f():
                cs = slice(h * nq, (h + 1) * nq)
                st["mix%d" % h] = (jax.nn.sigmoid(st["g%d" % h]) * yr_ref[:, cs]
                                   + jax.nn.sigmoid(st["g%d" % (2 + h)]) * yh_ref[:, cs]
                                   ).astype(BF16)
            return f

        def p4():
            upd = (_dot(st["mix0"], wout_ref[0:nq, :]) + _dot(st["mix1"], wout_ref[nq:2 * nq, :]))
            st["x1"] = st["x"] + _mod(mp, 2) * upd

        def p5():
            x1_scr[wr] = st["x1"]
            h2_scr[wr] = _modnorm(st["x1"], n2_ref[...], _mod(mp, 4), _mod(mp, 3)).astype(BF16)

        return [p1, p2(0), p2(1), p2(2), p2(3), p3(0), p3(1), p4, p5]

    def ffn_stages():
        mq = modq_ref[0]
        st = {"acc": None}

        def f(j):
            def g():
                cs = slice(j * FF_CHUNK, (j + 1) * FF_CHUNK)
                h2 = h2_scr[rd]
                a = _dot(h2, wfi_ref[:, cs])
                b = _dot(h2, wfi_ref[:, D_FF + j * FF_CHUNK:D_FF + (j + 1) * FF_CHUNK])
                ff = (a * jax.nn.sigmoid(a) * b).astype(BF16)
                part = _dot(ff, wfo_ref[cs, :])
                st["acc"] = part if st["acc"] is None else st["acc"] + part
            return g

        def e():
            x2 = x1_scr[rd] + _mod(mq, 5) * st["acc"]
            ms = jnp.mean(x2 * x2, axis=-1, keepdims=True)
            y_ref[...] = x2 * lax.rsqrt(ms + EPS) * fg_ref[...]

        return [f(j) for j in range(D_FF // FF_CHUNK)] + [e]

    @pl.when(i == 0)
    def _():
        copies = [pltpu.make_async_copy(wfi_hbm, wfi_ref, sem.at[0]),
                  pltpu.make_async_copy(wfo_hbm, wfo_ref, sem.at[1])]
        for cp in copies:
            cp.start()
        for stage in pre_stages():
            stage()
        for cp in copies:
            cp.wait()

    @pl.when(jnp.logical_and(i > 0, i < n_tiles))
    def _():
        pre, ffn = pre_stages(), ffn_stages()
        order = []
        while pre or ffn:
            if ffn:
                order.append(ffn.pop(0))
            if pre:
                order.append(pre.pop(0))
        for stage in order:
            stage()

    @pl.when(i == n_tiles)
    def _():
        for stage in ffn_stages():
            stage()


def _mlp(x, y_ret, y_hy, mods, mod_row, norm1, norm2, final_g, w_gate, w_out, w_fi, w_fo):
    B, L, D = x.shape
    T = MLP_ROWS
    n_tiles = B * L // T
    flat = lambda a: a.reshape(B * L, D)
    pre_tile = lambda i: jnp.minimum(i, n_tiles - 1)
    post_tile = lambda i: jnp.maximum(i - 1, 0)
    act = pl.BlockSpec((T, D), lambda i: (pre_tile(i), 0))
    y = pl.pallas_call(
        functools.partial(_mlp_kernel, n_tiles=n_tiles),
        grid=(n_tiles + 1,),
        in_specs=[act, act, act,
                  pl.BlockSpec((1, 1, N_MOD * D),
                               lambda i: (mod_row((pre_tile(i) * T) // L), 0, 0)),
                  pl.BlockSpec((1, 1, N_MOD * D),
                               lambda i: (mod_row((post_tile(i) * T) // L), 0, 0)),
                  _const_spec((1, D)), _const_spec((1, D)), _const_spec((1, D)),
                  _const_spec((D, N_GATE)),
                  _const_spec((D, D)),
                  pl.BlockSpec(memory_space=pl.ANY),
                  pl.BlockSpec(memory_space=pl.ANY)],
        out_specs=pl.BlockSpec((T, D), lambda i: (post_tile(i), 0)),
        out_shape=jax.ShapeDtypeStruct((B * L, D), F32),
        scratch_shapes=[pltpu.VMEM((2, T, D), F32), pltpu.VMEM((2, T, D), BF16),
                        pltpu.VMEM((D, 2 * D_FF), BF16), pltpu.VMEM((D_FF, D), BF16),
                        pltpu.SemaphoreType.DMA((2,))],
        compiler_params=_params(1),
        name=f"mlp{L}",
    )(flat(x), flat(y_ret), flat(y_hy), mods, mods, norm1, norm2, final_g, w_gate, w_out, w_fi,
      w_fo)
    return y.reshape(B, L, D)


def kernel(x_prompt, x_sample, state_ret_fwd, state_ret_bwd, c, c_ctx, norm1_g, norm2_g, w_ada,
           b_ada, w_in, ret_decay_fwd, ret_decay_bwd, hy_conv_w, hy_conv_b, hy_pos_w1, hy_pos_b1,
           hy_pos_w2, hy_pos_b2, hy_pos_w3, hy_sin_freq, hy_bias, w_ret_o, w_hy_o, w_out,
           w_ffn_in, w_ffn_out, final_g):
    assert w_in.shape[0] == 1, "single-layer configuration"
    nb_lat = x_sample.shape[0]
    l_ctx = x_prompt.shape[1]

    assert 1 + nb_lat <= MOD_ROWS
    norm1 = norm1_g[0][None, :]
    norm2 = norm2_g[0][None, :]
    fg = final_g[None, :]
    conv_w = hy_conv_w[0].reshape(1, 3 * N_HY)
    conv_b = hy_conv_b[0][None, :]
    filt_params = (hy_pos_w1[0], hy_pos_b1, hy_pos_w2[0], hy_pos_b2, hy_pos_w3[0], hy_sin_freq)

    n_hy_end = N_QKVG + N_HY
    w_in_parts = (slice(0, N_QKVG), slice(N_QKVG, n_hy_end), slice(n_hy_end, N_IN))
    groups = []
    for L in (l_ctx, x_sample.shape[1]):
        blk = min(HY_TBLK, L)
        fw, _ = _dft_mats(blk)
        z, rates = _filter_consts(L)
        groups.append((L, blk, jnp.asarray(z.T), jnp.asarray(rates), fw))
    (mods, fa_c, fb_c, fd_c, fa_l, fb_l, fd_l, w_qkvg, w_hy, w_gate, w_ret_o_b,
     w_hy_o_b) = _ada(c_ctx[None, :], c, w_ada[0], b_ada, filt_params, groups,
                      casts=[(w_in[0], w_in_parts), (w_ret_o[0], None), (w_hy_o[0], None)])

    def branches(x, filt, s0f, s0b, grid_w, emit_state):
        blk = min(HY_TBLK, x.shape[1])
        fw, bw = _dft_mats(blk)
        ret = (w_qkvg, ret_decay_fwd[0], ret_decay_bwd[0], s0f, s0b, w_ret_o_b, emit_state)
        hy = (w_hy, conv_w, conv_b, fw, bw, filt, hy_bias[0], w_hy_o_b, grid_w, blk)
        return ret, hy

    ctx_row = lambda b: 0
    lat_row = lambda b: b + 1
    ret, hy = branches(x_prompt, (fa_c, fb_c, fd_c), None, None, l_ctx, True)
    y_ret_c, s_f, s_b, y_hy_c, w_fi_b, w_fo_b, w_out_b = _mixer(
        x_prompt, mods, ctx_row, norm1, nb=CTX_SEQS, ret=ret, hy=hy,
        casts=[(w_ffn_in[0], None), (w_ffn_out[0], None), (w_out[0], None)])
    y_prompt = _mlp(x_prompt, y_ret_c, y_hy_c, mods, ctx_row, norm1, norm2, fg, w_gate, w_out_b,
                    w_fi_b, w_fo_b)
    ret, hy = branches(x_sample, (fa_l, fb_l, fd_l), state_ret_fwd[:, 0], state_ret_bwd[:, 0],
                       GRID_W, False)
    y_ret_l, = _mixer(x_sample, mods, lat_row, norm1, nb=1, ret=ret)
    y_hy_l, = _mixer(x_sample, mods, lat_row, norm1, nb=1, hy=hy)
    y_sample = _mlp(x_sample, y_ret_l, y_hy_l, mods, lat_row, norm1, norm2, fg, w_gate,
                    w_out_b, w_fi_b, w_fo_b)
    return (y_prompt, y_sample, s_f[:, None], s_b[:, None])
```

```python
import functools
import math

import numpy as np
import jax
import jax.numpy as jnp
from jax import lax
from jax.experimental import pallas as pl
from jax.experimental.pallas import tpu as pltpu

F32 = jnp.float32
BF16 = jnp.bfloat16

D_MODEL = 1024
RET_HEADS = 4
HEAD_DIM = 128
RET_W = RET_HEADS * HEAD_DIM
HY_W = 512
HY_ORDER = 2
HY_BANDS = 16
HY_EMB = 1 + 2 * HY_BANDS
HY_EMB_PAD = 40
HY_HIDDEN = 64
HY_FAST_DECAY = 0.3
HY_SLOW_DECAY = 1.5
HY_TARGET = 1e-2
D_FF = 2816
N_QKVG = 4 * RET_W
N_HY = 3 * HY_W
N_GATE = 2 * D_MODEL
N_IN = N_QKVG + N_HY + N_GATE
N_MOD = 6
MOD_ROWS = 8
EPS = 1e-6
GRID_W = 64
RET_CHUNK = 256
HY_CBLK = 256
HY_TBLK = 512
CTX_SEQS = 2
MLP_ROWS = 512
FF_CHUNK = 256
FILTER_ONE_STEP_LEN = 256
ADA_COLS = 768
VMEM_LIMIT = 56 * 1024 * 1024


def _const_spec(shape):
    nd = len(shape)
    return pl.BlockSpec(shape, lambda *_: (0,) * nd, pipeline_mode=pl.Buffered(1))


def _params(n_axes):
    return pltpu.CompilerParams(dimension_semantics=("arbitrary",) * n_axes,
                                vmem_limit_bytes=VMEM_LIMIT)


def _modnorm(x, g, scale, shift):
    ms = jnp.mean(x * x, axis=-1, keepdims=True)
    return (x * lax.rsqrt(ms + EPS) * g) * (1.0 + scale) + shift


def _mod(mod, k):
    return mod[:, k * D_MODEL:(k + 1) * D_MODEL]


def _dot(a, b):
    return jnp.dot(a, b, preferred_element_type=F32)


def _cast_specs(casts, steps):
    in_specs, out_specs, out_shape, args = [], [], [], []
    for arr, col_slices in casts:
        rows, width = arr.shape
        rb = rows // steps
        assert rb * steps == rows and rb % 16 == 0
        in_specs.append(pl.BlockSpec((rb, width), lambda g: (g, 0)))
        args.append(arr)
        for cs in col_slices or (slice(0, width),):
            cols = cs.stop - cs.start
            out_specs.append(pl.BlockSpec((rb, cols), lambda g: (g, 0)))
            out_shape.append(jax.ShapeDtypeStruct((rows, cols), BF16))
    return in_specs, out_specs, out_shape, args


def _n_cast_outputs(col_slices_per_src):
    return sum(1 if s is None else len(s) for s in col_slices_per_src)


def _do_casts(col_slices_per_src, srcs, dsts):
    dsts = iter(dsts)
    for col_slices, src in zip(col_slices_per_src, srcs):
        if col_slices is None:
            next(dsts)[...] = src[...].astype(BF16)
        else:
            for cs in col_slices:
                next(dsts)[...] = src[:, cs].astype(BF16)


def _ada_kernel(*refs, casts, groups, n_steps):
    it = iter(refs)
    cctx_ref, c_ref, w_ref, b_ref = next(it), next(it), next(it), next(it)
    mlp_refs = [next(it) for _ in range(6)]
    g_in = [[next(it) for _ in range(3)] for _ in groups]
    cast_srcs = [next(it) for _ in casts]
    o_ref = next(it)
    g_out = [[next(it) for _ in range(3)] for _ in groups]
    cast_dsts = [next(it) for _ in range(_n_cast_outputs(casts))]
    h_scr, cond_scr, w1_scr, w2_scr = next(it), next(it), next(it), next(it)
    g_scr = [[next(it) for _ in range(3)] for _ in groups]
    sem = next(it)
    _do_casts(casts, cast_srcs, cast_dsts)
    nlat = c_ref.shape[0]
    cond_scr[...] = jnp.zeros_like(cond_scr)
    cond_scr[0:1, :] = cctx_ref[...]
    cond_scr[1:1 + nlat, :] = c_ref[...]
    c = cond_scr[...]
    s = (c * jax.nn.sigmoid(c)).astype(BF16)
    res = _dot(s, w_ref[...].astype(BF16)) + b_ref[...]
    for r in range(res.shape[0]):
        o_ref[r] = res[r:r + 1, :]

    def writeback(g, o):
        return [pltpu.make_async_copy(scr.at[o], out.at[o], sem.at[(g * HY_ORDER + o) * 3 + k])
                for k, (scr, out) in enumerate(zip(g_scr[g], g_out[g]))]

    step = 0
    for g, ((L, blk), (zt_ref, rate_ref, fw_ref)) in enumerate(zip(groups, g_in)):
        def mlp_job(zt_ref=zt_ref, L=L):
            h_scr[0:L, :] = _filter_mlp(zt_ref, *mlp_refs, w1_scr, w2_scr)

        def order_job(o, g=g, rate_ref=rate_ref, fw_ref=fw_ref, L=L, blk=blk):
            _filter_spectra(h_scr, o, rate_ref, fw_ref, *g_scr[g], L=L, b=blk)
            for cp in writeback(g, o):
                cp.start()

        def all_job(mlp_job=mlp_job, order_job=order_job):
            mlp_job()
            for o in range(HY_ORDER):
                order_job(o)

        if L <= FILTER_ONE_STEP_LEN:
            jobs = [all_job]
        else:
            jobs = [mlp_job] + [functools.partial(order_job, o) for o in range(HY_ORDER)]
        for job in jobs:
            pl.when(pl.program_id(0) == step)(job)
            step += 1

    @pl.when(pl.program_id(0) == n_steps - 1)
    def _():
        for g in range(len(groups)):
            for o in range(HY_ORDER):
                for cp in writeback(g, o):
                    cp.wait()


def _ada(c_ctx, c, w, b, filt_params, groups, casts=()):
    n = w.shape[1]
    steps = n // ADA_COLS
    c_in, c_out, c_shape, c_args = _cast_specs(casts, steps)
    in_specs = [_const_spec(c_ctx.shape), _const_spec(c.shape),
                pl.BlockSpec((D_MODEL, ADA_COLS), lambda j: (0, j)),
                pl.BlockSpec((1, ADA_COLS), lambda j: (0, j))]
    args = [c_ctx, c, w, b]
    for p in filt_params:
        in_specs.append(_const_spec(p.shape))
        args.append(p)
    out_specs = [pl.BlockSpec((MOD_ROWS, 1, ADA_COLS), lambda j: (0, 0, j))]
    out_shape = [jax.ShapeDtypeStruct((MOD_ROWS, 1, n), F32)]
    max_len = 8
    spectra_scratch = []
    for L, blk, *consts in groups:
        nd = 2 * (L // blk) - 1
        max_len = max(max_len, L)
        for cst in consts:
            in_specs.append(_const_spec(cst.shape))
            args.append(cst)
        for shp in ((HY_ORDER, nd, blk, HY_W), (HY_ORDER, nd, blk, HY_W), (HY_ORDER, nd, 8, HY_W)):
            out_specs.append(pl.BlockSpec(memory_space=pl.ANY))
            out_shape.append(jax.ShapeDtypeStruct(shp, F32))
            spectra_scratch.append(pltpu.VMEM(shp, F32))
    n_jobs = sum(1 if L <= FILTER_ONE_STEP_LEN else 1 + HY_ORDER for L, *_ in groups)
    assert n_jobs <= steps
    return pl.pallas_call(
        functools.partial(_ada_kernel, casts=tuple(cs for _, cs in casts),
                          groups=tuple((L, blk) for L, blk, *_ in groups), n_steps=steps),
        grid=(steps,),
        in_specs=in_specs + c_in,
        out_specs=out_specs + c_out,
        out_shape=out_shape + c_shape,
        scratch_shapes=[pltpu.VMEM((max_len, HY_ORDER * 2 * HY_W), F32),
                        pltpu.VMEM((MOD_ROWS, D_MODEL), F32),
                        pltpu.VMEM((HY_EMB_PAD, HY_HIDDEN), F32),
                        pltpu.VMEM((HY_HIDDEN + 8, HY_HIDDEN), F32)]
        + spectra_scratch + [pltpu.SemaphoreType.DMA((len(groups) * HY_ORDER * 3,))],
        compiler_params=_params(1),
        name="ada",
    )(*args, *c_args)


@functools.lru_cache(maxsize=None)
def _dft_mats(L):
    n = 2 * L
    t = np.arange(L, dtype=np.int64)
    f = np.arange(L, dtype=np.int64)
    ang = 2.0 * np.pi * ((f[:, None] * t[None, :]) % n).astype(np.float64) / n
    cos = np.cos(ang)
    sin = np.sin(ang)
    nyq = np.where(t % 2 == 0, 1.0, -1.0)
    fwd = np.concatenate([cos, -sin], axis=0)
    fwd[L] = nyq
    wre = np.full((L,), 2.0 / n)
    wre[0] = 1.0 / n
    inv = np.concatenate([cos.T * wre[None, :], -sin.T * (2.0 / n)], axis=1)
    inv[:, L] = nyq / n
    return jnp.asarray(fwd, dtype=BF16), jnp.asarray(inv, dtype=BF16)


@functools.lru_cache(maxsize=None)
def _filter_consts(L):
    t = np.linspace(0.0, 1.0, L)[:, None]
    ang = 2.0 * np.pi * np.arange(L, dtype=np.float64)[:, None] / L
    bands = np.linspace(1e-4, HY_BANDS - 1, HY_BANDS)[None]
    z = np.concatenate([t, np.cos(bands * ang), -np.sin(bands * ang)], axis=-1)
    z = np.pad(z, ((0, 0), (0, HY_EMB_PAD - HY_EMB)))
    z[:, HY_EMB] = 1.0
    max_decay = math.log(HY_TARGET) / HY_FAST_DECAY
    min_decay = math.log(HY_TARGET) / HY_SLOW_DECAY
    rates = np.abs(np.linspace(min_decay, max_decay, HY_W))[None, :]
    return np.asarray(z, np.float32), np.asarray(rates, np.float32)


def _filter_mlp(zt_ref, w1_ref, b1_ref, w2_ref, b2_ref, w3_ref, fr_ref, w1_scr, w2_scr):
    hi = lax.Precision.HIGHEST
    tdims = (((0,), (0,)), ((), ()))
    w1_scr[...] = jnp.zeros_like(w1_scr)
    w1_scr[0:HY_EMB, :] = w1_ref[...]
    w1_scr[HY_EMB:HY_EMB + 1, :] = b1_ref[...]
    w2_scr[...] = jnp.zeros_like(w2_scr)
    w2_scr[0:HY_HIDDEN, :] = w2_ref[...]
    w2_scr[HY_HIDDEN:HY_HIDDEN + 1, :] = b2_ref[...]
    fr = fr_ref[...]
    zt = zt_ref[...]
    h1 = jnp.sin(lax.dot_general(w1_scr[...] * fr, zt, tdims, precision=hi,
                                 preferred_element_type=F32))
    h1 = jnp.concatenate([h1, jnp.ones((8, zt.shape[1]), F32)], axis=0)
    h2 = jnp.sin(lax.dot_general(w2_scr[...] * fr, h1, tdims, precision=hi,
                                 preferred_element_type=F32))
    h2_hi = h2.astype(BF16)
    h2_lo = (h2 - h2_hi.astype(F32)).astype(BF16)
    w3 = w3_ref[...]
    w3_hi = w3.astype(BF16)
    w3_lo = (w3 - w3_hi.astype(F32)).astype(BF16)
    lhs = jnp.concatenate([h2_hi, h2_lo, h2_hi, jnp.zeros_like(h2_hi)], axis=0)
    rhs = jnp.concatenate([w3_hi, w3_hi, w3_lo, jnp.zeros_like(w3_hi)], axis=0)
    h = lax.dot_general(lhs, rhs, tdims, preferred_element_type=F32)
    return h


def _filter_spectra(h_ref, o, rate_ref, fw_ref, oa_ref, ob_ref, od_ref, *, L, b):
    m = L // b
    row_l = lax.broadcasted_iota(jnp.int32, (L, HY_W), 0)
    row_b = lax.broadcasted_iota(jnp.int32, (b, HY_W), 0)
    t = row_l.astype(F32) * (1.0 / (L - 1))
    win = jnp.exp(-t * rate_ref[...])
    sg = jnp.where(row_b % 2 == 0, 1.0, -1.0)
    row0_l = row_l == 0
    row0_b = row_b == 0
    row0_8 = lax.broadcasted_iota(jnp.int32, (8, HY_W), 0) == 0
    base = o * 2 * HY_W
    fwd = h_ref[0:L, base:base + HY_W] * win
    bwd = jnp.where(row0_l, 0.0, h_ref[0:L, base + HY_W:base + 2 * HY_W] * win)
    nrm = (jnp.sum(jnp.abs(fwd), axis=0, keepdims=True)
           + jnp.sum(jnp.abs(bwd), axis=0, keepdims=True))
    inv = 1.0 / nrm
    fn = fwd * inv
    bn = bwd * inv
    xr, xn, xi, wr, wn, wi = [], [], [], [], [], []
    for r in range(m):
        p = _dot(fw_ref[...], fn[r * b:(r + 1) * b].astype(BF16))
        q = _dot(fw_ref[...], bn[r * b:(r + 1) * b].astype(BF16))
        xr.append(p[0:b])
        xn.append(p[b:b + 1])
        xi.append(jnp.where(row0_b, 0.0, p[b:2 * b]))
        wr.append(q[0:b])
        wn.append(q[b:b + 1])
        wi.append(jnp.where(row0_b, 0.0, -q[b:2 * b]))

    def emit(d, ka, kn, kb):
        oa_ref[o, d + m - 1] = ka
        ob_ref[o, d + m - 1] = kb
        od_ref[o, d + m - 1] = jnp.where(row0_8, kn, ka[0:8])

    emit(0, xr[0] + wr[0], xn[0] + wn[0], xi[0] + wi[0])
    for d in range(1, m):
        f0 = fn[(d - 1) * b:(d - 1) * b + 1]
        b0 = bn[(d - 1) * b:(d - 1) * b + 1]
        emit(d, xr[d] + sg * (xr[d - 1] - f0), xn[d] + (xn[d - 1] - f0),
             xi[d] + sg * xi[d - 1])
        emit(-d, wr[d] + sg * (wr[d - 1] - b0), wn[d] + (wn[d - 1] - b0),
             wi[d] + sg * wi[d - 1])


def _ret_init(decf_ref, decb_ref, mask_scr, vec_scr, cd_scr, C):
    H, E = RET_HEADS, HEAD_DIM
    scale = float(E) ** -0.5

    @pl.when(pl.program_id(0) == 0)
    def _():
        dec = jnp.concatenate([jnp.full((1, C), ref[h], F32)
                               for ref in (decf_ref, decb_ref) for h in range(H)], axis=0)
        lg = jnp.log(jax.nn.sigmoid(dec))
        cd_scr[...] = jnp.exp(float(C) * lg[:, 0:E])
        ii = lax.broadcasted_iota(jnp.int32, (C, C), 0)
        jj = lax.broadcasted_iota(jnp.int32, (C, C), 1)
        rel = (ii - jj).astype(F32)
        ri = lax.broadcasted_iota(jnp.int32, (C, E), 0).astype(F32)
        for h in range(H):
            lf = lg[h:h + 1, :]
            lb = lg[H + h:H + h + 1, :]
            mf = jnp.where(rel >= 0, jnp.exp(jnp.maximum(rel, 0.0) * lf), 0.0)
            mb = jnp.where(rel <= 0, jnp.exp(jnp.maximum(-rel, 0.0) * lb), 0.0)
            mask_scr[h] = scale * (mf + mb)
            lfe, lbe = lf[:, 0:E], lb[:, 0:E]
            vec_scr[h, 0] = jnp.exp((ri + 1.0) * lfe)
            vec_scr[h, 1] = jnp.exp((float(C) - ri) * lbe)
            vec_scr[h, 2] = scale * jnp.exp((float(C) - 1.0 - ri) * lfe)
            vec_scr[h, 3] = scale * jnp.exp(ri * lbe)


def _ret_core(hn, w_ref, s0f_ref, s0b_ref, wo_ref, y_ref, sf_ref, sb_ref, mask_scr, vec_scr,
              cd_scr, g_scr, *, L, C, nb, has_init, emit_state):
    n = L // C
    H, E = RET_HEADS, HEAD_DIM
    tdims = (((0,), (0,)), ((), ()))
    ndims = (((1,), (1,)), ((), ()))
    chains = [(s, h) for s in range(nb) for h in range(H)]
    rows = [slice(c * C, (c + 1) * C) for c in range(n)]
    qkvg = [_dot(hn[s], w_ref[...]) for s in range(nb)]

    def cols(s, part, h):
        return qkvg[s][:, part * RET_W + h * E:part * RET_W + (h + 1) * E]

    qb = [cols(s, 0, h).astype(BF16) for s, h in chains]
    kf = [cols(s, 1, h) for s, h in chains]
    kb = [k.astype(BF16) for k in kf]
    vb = [cols(s, 2, h).astype(BF16) for s, h in chains]
    att = [[lax.dot_general(qb[i][r], kb[i][r], ndims, preferred_element_type=F32) for r in rows]
           for i in range(len(chains))]
    prob = [[(att[i][c] * mask_scr[h]).astype(BF16) for c in range(n)]
            for i, (s, h) in enumerate(chains)]
    out = [[_dot(prob[i][c], vb[i][rows[c]]) for c in range(n)] for i in range(len(chains))]
    kv = []
    for i, (s, h) in enumerate(chains):
        dk2 = jnp.concatenate([vec_scr[h, 2], vec_scr[h, 3]], axis=1)
        per_c = []
        for r in rows:
            k2 = (jnp.concatenate([kf[i][r], kf[i][r]], axis=1) * dk2).astype(BF16)
            per_c.append(lax.dot_general(k2, vb[i][r], tdims, preferred_element_type=F32))
        kv.append(per_c)
    for i, (s, h) in enumerate(chains):
        cdf = cd_scr[h:h + 1, :]
        cdb = cd_scr[H + h:H + h + 1, :]
        sf_in, sb_in = [None] * n, [None] * n
        st = s0f_ref[s, h] if has_init else None
        for c in range(n):
            sf_in[c] = st
            kvc = kv[i][c][0:E]
            st = kvc if st is None else st * cdf + kvc
        if emit_state:
            sf_ref[s, h] = st
        st = s0b_ref[s, h] if has_init else None
        for c in range(n - 1, -1, -1):
            sb_in[c] = st
            kvc = kv[i][c][E:2 * E]
            st = kvc if st is None else st * cdb + kvc
        if emit_state:
            sb_ref[s, h] = st
        for c in range(n):
            if sf_in[c] is not None and sb_in[c] is not None:
                s2 = jnp.concatenate([sf_in[c], sb_in[c]], axis=1).astype(BF16)
                inter = _dot(qb[i][rows[c]], s2)
                out[i][c] = (out[i][c] + inter[:, 0:E] * vec_scr[h, 0]
                             + inter[:, E:2 * E] * vec_scr[h, 1])
            elif sf_in[c] is not None:
                out[i][c] = (out[i][c]
                             + _dot(qb[i][rows[c]], sf_in[c].astype(BF16)) * vec_scr[h, 0])
            elif sb_in[c] is not None:
                out[i][c] = (out[i][c]
                             + _dot(qb[i][rows[c]], sb_in[c].astype(BF16)) * vec_scr[h, 1])
    for i, (s, h) in enumerate(chains):
        for c in range(n):
            o = out[i][c]
            mu = jnp.mean(o, axis=-1, keepdims=True)
            d = o - mu
            var = jnp.mean(d * d, axis=-1, keepdims=True)
            on = d * lax.rsqrt(var + EPS)
            gg = cols(s, 3, h)[rows[c]]
            g_scr[s, rows[c], h * E:(h + 1) * E] = (gg * jax.nn.sigmoid(gg) * on).astype(BF16)
    for s in range(nb):
        y_ref[s] = _dot(g_scr[s], wo_ref[...])


def _hy_core(hn, w_ref, cw_ref, cb_ref, fw_ref, bw_ref, fa_ref, fb_ref, fd_ref, hb_ref, wo_ref,
             y_ref, *, L, W, b, nb):
    m = L // b
    CB = HY_CBLK
    nblk = HY_W // CB
    pos = lax.broadcasted_iota(jnp.int32, (L, CB), 0) % W
    first = pos == 0
    last = pos == W - 1
    chains = [(s, blk) for s in range(nb) for blk in range(nblk)]

    def short_conv(s, base, blk):
        cs = slice(base + blk * CB, base + (blk + 1) * CB)
        ug = _dot(hn[s], w_ref[:, cs])
        prev = jnp.where(first, 0.0, pltpu.roll(ug, 1, axis=0))
        nxt = jnp.where(last, 0.0, pltpu.roll(ug, L - 1, axis=0))
        taps = [cw_ref[:, t * N_HY + cs.start:t * N_HY + cs.stop] for t in range(3)]
        u = prev * taps[0] + ug * taps[1] + nxt * taps[2] + cb_ref[:, cs]
        return [u[j * b:(j + 1) * b] for j in range(m)]

    def long_conv(sigs, o):
        spec = [[_dot(fw_ref[...], sj.astype(BF16)) for sj in sig] for sig in sigs]
        prods = []
        for (s, blk), sp in zip(chains, spec):
            cs = slice(blk * CB, (blk + 1) * CB)
            per_i = []
            for i in range(m):
                yre = yim = yim8 = None
                for j in range(m):
                    d = i - j + m - 1
                    sre, sim = sp[j][0:b], sp[j][b:2 * b]
                    ka, kb = fa_ref[o, d, :, cs], fb_ref[o, d, :, cs]
                    tre = sre * ka - sim * kb
                    tim = sre * kb + sim * ka
                    t8 = sre[0:8] * kb[0:8] + sim[0:8] * fd_ref[o, d, :, cs]
                    yre = tre if yre is None else yre + tre
                    yim = tim if yim is None else yim + tim
                    yim8 = t8 if yim8 is None else yim8 + t8
                yim = jnp.concatenate([yim8, yim[8:]], axis=0)
                per_i.append((yre.astype(BF16), yim.astype(BF16)))
            prods.append(per_i)
        return [[_dot(bw_ref[:, 0:b], yre) + _dot(bw_ref[:, b:2 * b], yim) for yre, yim in per_i]
                for per_i in prods]

    hv = [short_conv(s, 0, blk) for s, blk in chains]
    hx1 = [short_conv(s, HY_W, blk) for s, blk in chains]
    hx2 = [short_conv(s, 2 * HY_W, blk) for s, blk in chains]

    def gate(hx, conv, sig, o):
        out = []
        for (s, blk), hxc, cc, sc in zip(chains, hx, conv, sig):
            bias = hb_ref[o:o + 1, blk * CB:(blk + 1) * CB]
            out.append([hxc[i] * (cc[i] + sc[i] * bias) for i in range(m)])
        return out

    z = gate(hx1, long_conv(hv, 0), hv, 0)
    z = gate(hx2, long_conv(z, 1), z, 1)
    for s in range(nb):
        for i in range(m):
            acc = None
            for blk in range(nblk):
                zc = z[chains.index((s, blk))][i].astype(BF16)
                part = _dot(zc, wo_ref[blk * CB:(blk + 1) * CB, :])
                acc = part if acc is None else acc + part
            y_ref[s, i * b:(i + 1) * b, :] = acc


def _mix_kernel(*refs, L, C, W, b, nb, do_ret, do_hy, has_init, emit_state, casts, n_after):
    it = iter(refs)
    x_ref, mod_ref, n1_ref = next(it), next(it), next(it)
    s0f_ref = s0b_ref = sf_ref = sb_ref = None
    if do_ret:
        wq_ref, decf_ref, decb_ref = next(it), next(it), next(it)
        if has_init:
            s0f_ref, s0b_ref = next(it), next(it)
        wo_ret_ref = next(it)
    if do_hy:
        hy_in = [next(it) for _ in range(10)]
    cast_srcs = [next(it) for _ in casts]
    for _ in range(n_after):
        next(it)
    if do_ret:
        y_ret_ref = next(it)
        if emit_state:
            sf_ref, sb_ref = next(it), next(it)
    if do_hy:
        y_hy_ref = next(it)
    cast_dsts = [next(it) for _ in range(_n_cast_outputs(casts))]
    if do_ret:
        ret_scr = [next(it) for _ in range(4)]
        _ret_init(decf_ref, decb_ref, ret_scr[0], ret_scr[1], ret_scr[2], C)
    _do_casts(casts, cast_srcs, cast_dsts)
    mod = mod_ref[0]
    hn = [_modnorm(x_ref[s], n1_ref[...], _mod(mod, 1), _mod(mod, 0)).astype(BF16)
          for s in range(nb)]
    if do_hy:
        _hy_core(hn, *hy_in, y_hy_ref, L=L, W=W, b=b, nb=nb)
    if do_ret:
        _ret_core(hn, wq_ref, s0f_ref, s0b_ref, wo_ret_ref, y_ret_ref, sf_ref, sb_ref, *ret_scr,
                  L=L, C=C, nb=nb, has_init=has_init, emit_state=emit_state)


def _mixer(x, mods, mod_row, norm1, *, nb, ret=None, hy=None, casts=(), after=()):
    B, L, D = x.shape
    H, E = RET_HEADS, HEAD_DIM
    C = min(RET_CHUNK, L)
    seq_spec = pl.BlockSpec((nb, L, D), lambda g: (g, 0, 0))
    in_specs = [seq_spec,
                pl.BlockSpec((1, 1, N_MOD * D), lambda g: (mod_row(g * nb), 0, 0)),
                _const_spec((1, D))]
    args = [x, mods, norm1]
    out_specs, out_shape, scratch = [], [], []
    has_init = emit_state = False
    W = b = None
    if ret is not None:
        w_qkvg, dec_f, dec_b, s0f, s0b, w_o, emit_state = ret
        has_init = s0f is not None
        st_spec = pl.BlockSpec((nb, H, E, E), lambda g: (g, 0, 0, 0))
        smem = pl.BlockSpec(memory_space=pltpu.SMEM)
        in_specs += [_const_spec((D, N_QKVG)), smem, smem]
        args += [w_qkvg, dec_f, dec_b]
        if has_init:
            in_specs += [st_spec, st_spec]
            args += [s0f, s0b]
        in_specs.append(_const_spec((RET_W, D)))
        args.append(w_o)
        out_specs.append(seq_spec)
        out_shape.append(jax.ShapeDtypeStruct((B, L, D), F32))
        if emit_state:
            out_specs += [st_spec, st_spec]
            out_shape += [jax.ShapeDtypeStruct((B, H, E, E), F32)] * 2
        scratch = [pltpu.VMEM((H, C, C), F32), pltpu.VMEM((H, 4, C, E), F32),
                   pltpu.VMEM((8, E), F32), pltpu.VMEM((nb, L, RET_W), BF16)]
    if hy is not None:
        w_hy, conv_w, conv_b, fw, bw, (fa, fb, fd), hy_bias, w_o, W, b = hy
        nd = fa.shape[1]
        in_specs += [_const_spec((D, N_HY)), _const_spec((1, 3 * N_HY)), _const_spec((1, N_HY)),
                     _const_spec((2 * b, b)), _const_spec((b, 2 * b)),
                     _const_spec((HY_ORDER, nd, b, HY_W)), _const_spec((HY_ORDER, nd, b, HY_W)),
                     _const_spec((HY_ORDER, nd, 8, HY_W)), _const_spec((HY_ORDER, HY_W)),
                     _const_spec((HY_W, D))]
        args += [w_hy, conv_w, conv_b, fw, bw, fa, fb, fd, hy_bias, w_o]
        out_specs.append(seq_spec)
        out_shape.append(jax.ShapeDtypeStruct((B, L, D), F32))
    c_in, c_out, c_shape, c_args = _cast_specs(casts, B // nb)
    name = ("ret" if ret is not None else "") + ("hy" if hy is not None else "")
    return pl.pallas_call(
        functools.partial(_mix_kernel, L=L, C=C, W=W, b=b, nb=nb, do_ret=ret is not None,
                          do_hy=hy is not None, has_init=has_init, emit_state=emit_state,
                          casts=tuple(cs for _, cs in casts), n_after=len(after)),
        grid=(B // nb,),
        in_specs=in_specs + c_in + [pl.BlockSpec(memory_space=pl.ANY)] * len(after),
        out_specs=out_specs + c_out,
        out_shape=out_shape + c_shape,
        scratch_shapes=scratch,
        compiler_params=_params(1),
        name=f"{name}{L}",
    )(*args, *c_args, *after)


def _mlp_kernel(x_ref, yr_ref, yh_ref, modp_ref, modq_ref, n1_ref, n2_ref, fg_ref, wg_ref,
                wout_ref, wfi_hbm, wfo_hbm, y_ref, x1_scr, h2_scr, wfi_ref, wfo_ref, sem, *,
                n_tiles):
    i = pl.program_id(0)
    wr = i % 2
    rd = 1 - wr
    nq = N_GATE // 4

    def pre_stages():
        mp = modp_ref[0]
        st = {}

        def p1():
            st["x"] = x_ref[...]
            st["hn"] = _modnorm(st["x"], n1_ref[...], _mod(mp, 1), _mod(mp, 0)).astype(BF16)

        def p2(q):
            def f():
                st["g%d" % q] = _dot(st["hn"], wg_ref[:, q * nq:(q + 1) * nq])
            return f

        def p3(h):
            def f():
                cs = slice(h * nq, (h + 1) * nq)
                st["mix%d" % h] = (jax.nn.sigmoid(st["g%d" % h]) * yr_ref[:, cs]
                                   + jax.nn.sigmoid(st["g%d" % (2 + h)]) * yh_ref[:, cs]
                                   ).astype(BF16)
            return f

        def p4():
            upd = (_dot(st["mix0"], wout_ref[0:nq, :]) + _dot(st["mix1"], wout_ref[nq:2 * nq, :]))
            st["x1"] = st["x"] + _mod(mp, 2) * upd

        def p5():
            x1_scr[wr] = st["x1"]
            h2_scr[wr] = _modnorm(st["x1"], n2_ref[...], _mod(mp, 4), _mod(mp, 3)).astype(BF16)

        return [p1, p2(0), p2(1), p2(2), p2(3), p3(0), p3(1), p4, p5]

    def ffn_stages():
        mq = modq_ref[0]
        st = {"acc": None}

        def f(j):
            def g():
                cs = slice(j * FF_CHUNK, (j + 1) * FF_CHUNK)
                h2 = h2_scr[rd]
                a = _dot(h2, wfi_ref[:, cs])
                b = _dot(h2, wfi_ref[:, D_FF + j * FF_CHUNK:D_FF + (j + 1) * FF_CHUNK])
                ff = (a * jax.nn.sigmoid(a) * b).astype(BF16)
                part = _dot(ff, wfo_ref[cs, :])
                st["acc"] = part if st["acc"] is None else st["acc"] + part
            return g

        def e():
            x2 = x1_scr[rd] + _mod(mq, 5) * st["acc"]
            ms = jnp.mean(x2 * x2, axis=-1, keepdims=True)
            y_ref[...] = x2 * lax.rsqrt(ms + EPS) * fg_ref[...]

        return [f(j) for j in range(D_FF // FF_CHUNK)] + [e]

    @pl.when(i == 0)
    def _():
        copies = [pltpu.make_async_copy(wfi_hbm, wfi_ref, sem.at[0]),
                  pltpu.make_async_copy(wfo_hbm, wfo_ref, sem.at[1])]
        for cp in copies:
            cp.start()
        for stage in pre_stages():
            stage()
        for cp in copies:
            cp.wait()

    @pl.when(jnp.logical_and(i > 0, i < n_tiles))
    def _():
        pre, ffn = pre_stages(), ffn_stages()
        order = []
        while pre or ffn:
            if ffn:
                order.append(ffn.pop(0))
            if pre:
                order.append(pre.pop(0))
        for stage in order:
            stage()

    @pl.when(i == n_tiles)
    def _():
        for stage in ffn_stages():
            stage()


def _mlp(x, y_ret, y_hy, mods, mod_row, norm1, norm2, final_g, w_gate, w_out, w_fi, w_fo):
    B, L, D = x.shape
    T = MLP_ROWS
    n_tiles = B * L // T
    flat = lambda a: a.reshape(B * L, D)
    pre_tile = lambda i: jnp.minimum(i, n_tiles - 1)
    post_tile = lambda i: jnp.maximum(i - 1, 0)
    act = pl.BlockSpec((T, D), lambda i: (pre_tile(i), 0))
    y = pl.pallas_call(
        functools.partial(_mlp_kernel, n_tiles=n_tiles),
        grid=(n_tiles + 1,),
        in_specs=[act, act, act,
                  pl.BlockSpec((1, 1, N_MOD * D),
                               lambda i: (mod_row((pre_tile(i) * T) // L), 0, 0)),
                  pl.BlockSpec((1, 1, N_MOD * D),
                               lambda i: (mod_row((post_tile(i) * T) // L), 0, 0)),
                  _const_spec((1, D)), _const_spec((1, D)), _const_spec((1, D)),
                  _const_spec((D, N_GATE)),
                  _const_spec((D, D)),
                  pl.BlockSpec(memory_space=pl.ANY),
                  pl.BlockSpec(memory_space=pl.ANY)],
        out_specs=pl.BlockSpec((T, D), lambda i: (post_tile(i), 0)),
        out_shape=jax.ShapeDtypeStruct((B * L, D), F32),
        scratch_shapes=[pltpu.VMEM((2, T, D), F32), pltpu.VMEM((2, T, D), BF16),
                        pltpu.VMEM((D, 2 * D_FF), BF16), pltpu.VMEM((D_FF, D), BF16),
                        pltpu.SemaphoreType.DMA((2,))],
        compiler_params=_params(1),
        name=f"mlp{L}",
    )(flat(x), flat(y_ret), flat(y_hy), mods, mods, norm1, norm2, final_g, w_gate, w_out, w_fi,
      w_fo)
    return y.reshape(B, L, D)


def kernel(x_prompt, x_sample, state_ret_fwd, state_ret_bwd, c, c_ctx, norm1_g, norm2_g, w_ada,
           b_ada, w_in, ret_decay_fwd, ret_decay_bwd, hy_conv_w, hy_conv_b, hy_pos_w1, hy_pos_b1,
           hy_pos_w2, hy_pos_b2, hy_pos_w3, hy_sin_freq, hy_bias, w_ret_o, w_hy_o, w_out,
           w_ffn_in, w_ffn_out, final_g):
    assert w_in.shape[0] == 1, "single-layer configuration"
    nb_lat = x_sample.shape[0]
    l_ctx = x_prompt.shape[1]

    assert 1 + nb_lat <= MOD_ROWS
    norm1 = norm1_g[0][None, :]
    norm2 = norm2_g[0][None, :]
    fg = final_g[None, :]
    conv_w = hy_conv_w[0].reshape(1, 3 * N_HY)
    conv_b = hy_conv_b[0][None, :]
    filt_params = (hy_pos_w1[0], hy_pos_b1, hy_pos_w2[0], hy_pos_b2, hy_pos_w3[0], hy_sin_freq)

    n_hy_end = N_QKVG + N_HY
    w_in_parts = (slice(0, N_QKVG), slice(N_QKVG, n_hy_end), slice(n_hy_end, N_IN))
    groups = []
    for L in (l_ctx, x_sample.shape[1]):
        blk = min(HY_TBLK, L)
        fw, _ = _dft_mats(blk)
        z, rates = _filter_consts(L)
        groups.append((L, blk, jnp.asarray(z.T), jnp.asarray(rates), fw))
    (mods, fa_c, fb_c, fd_c, fa_l, fb_l, fd_l, w_qkvg, w_hy, w_gate, w_ret_o_b,
     w_hy_o_b) = _ada(c_ctx[None, :], c, w_ada[0], b_ada, filt_params, groups,
                      casts=[(w_in[0], w_in_parts), (w_ret_o[0], None), (w_hy_o[0], None)])

    def branches(x, filt, s0f, s0b, grid_w, emit_state):
        blk = min(HY_TBLK, x.shape[1])
        fw, bw = _dft_mats(blk)
        ret = (w_qkvg, ret_decay_fwd[0], ret_decay_bwd[0], s0f, s0b, w_ret_o_b, emit_state)
        hy = (w_hy, conv_w, conv_b, fw, bw, filt, hy_bias[0], w_hy_o_b, grid_w, blk)
        return ret, hy

    ctx_row = lambda b: 0
    lat_row = lambda b: b + 1
    ret, hy = branches(x_prompt, (fa_c, fb_c, fd_c), None, None, l_ctx, True)
    y_ret_c, s_f, s_b, y_hy_c, w_fi_b, w_fo_b, w_out_b = _mixer(
        x_prompt, mods, ctx_row, norm1, nb=CTX_SEQS, ret=ret, hy=hy,
        casts=[(w_ffn_in[0], None), (w_ffn_out[0], None), (w_out[0], None)])
    y_prompt = _mlp(x_prompt, y_ret_c, y_hy_c, mods, ctx_row, norm1, norm2, fg, w_gate, w_out_b,
                    w_fi_b, w_fo_b)
    ret, hy = branches(x_sample, (fa_l, fb_l, fd_l), state_ret_fwd[:, 0], state_ret_bwd[:, 0],
                       GRID_W, False)
    y_ret_l, = _mixer(x_sample, mods, lat_row, norm1, nb=1, ret=ret, after=(w_out_b,))
    y_hy_l, = _mixer(x_sample, mods, lat_row, norm1, nb=1, hy=hy, after=(w_out_b,))
    y_sample = _mlp(x_sample, y_ret_l, y_hy_l, mods, lat_row, norm1, norm2, fg, w_gate,
                    w_out_b, w_fi_b, w_fo_b)
    return (y_prompt, y_sample, s_f[:, None], s_b[:, None])
```

```python
import functools
import math

import numpy as np
import jax
import jax.numpy as jnp
from jax import lax
from jax.experimental import pallas as pl
from jax.experimental.pallas import tpu as pltpu

F32 = jnp.float32
BF16 = jnp.bfloat16

D_MODEL = 1024
RET_HEADS = 4
HEAD_DIM = 128
RET_W = RET_HEADS * HEAD_DIM
HY_W = 512
HY_ORDER = 2
HY_BANDS = 16
HY_EMB = 1 + 2 * HY_BANDS
HY_EMB_PAD = 40
HY_HIDDEN = 64
HY_FAST_DECAY = 0.3
HY_SLOW_DECAY = 1.5
HY_TARGET = 1e-2
D_FF = 2816
N_QKVG = 4 * RET_W
N_HY = 3 * HY_W
N_GATE = 2 * D_MODEL
N_IN = N_QKVG + N_HY + N_GATE
N_MOD = 6
MOD_ROWS = 8
EPS = 1e-6
GRID_W = 64
RET_CHUNK = 256
HY_CBLK = 256
HY_TBLK = 512
CTX_SEQS = 2
MLP_ROWS = 512
FF_CHUNK = 256
FILTER_ONE_STEP_LEN = 256
ADA_STEPS = 8
ADA_RING = 3
VMEM_LIMIT = 56 * 1024 * 1024


def _const_spec(shape):
    nd = len(shape)
    return pl.BlockSpec(shape, lambda *_: (0,) * nd, pipeline_mode=pl.Buffered(1))


def _params(n_axes):
    return pltpu.CompilerParams(dimension_semantics=("arbitrary",) * n_axes,
                                vmem_limit_bytes=VMEM_LIMIT)


def _modnorm(x, g, scale, shift):
    ms = jnp.mean(x * x, axis=-1, keepdims=True)
    return (x * lax.rsqrt(ms + EPS) * g) * (1.0 + scale) + shift


def _mod(mod, k):
    return mod[:, k * D_MODEL:(k + 1) * D_MODEL]


def _dot(a, b):
    return jnp.dot(a, b, preferred_element_type=F32)


def _cast_specs(casts, steps):
    in_specs, out_specs, out_shape, args = [], [], [], []
    for arr, col_slices in casts:
        rows, width = arr.shape
        rb = rows // steps
        assert rb * steps == rows and rb % 16 == 0
        in_specs.append(pl.BlockSpec((rb, width), lambda g: (g, 0)))
        args.append(arr)
        for cs in col_slices or (slice(0, width),):
            cols = cs.stop - cs.start
            out_specs.append(pl.BlockSpec((rb, cols), lambda g: (g, 0)))
            out_shape.append(jax.ShapeDtypeStruct((rows, cols), BF16))
    return in_specs, out_specs, out_shape, args


def _n_cast_outputs(col_slices_per_src):
    return sum(1 if s is None else len(s) for s in col_slices_per_src)


def _do_casts(col_slices_per_src, srcs, dsts):
    dsts = iter(dsts)
    for col_slices, src in zip(col_slices_per_src, srcs):
        if col_slices is None:
            next(dsts)[...] = src[...].astype(BF16)
        else:
            for cs in col_slices:
                next(dsts)[...] = src[:, cs].astype(BF16)


def _ada_kernel(*refs, casts, groups, n_steps):
    it = iter(refs)
    cctx_ref, c_ref, w_hbm, b_ref = next(it), next(it), next(it), next(it)
    mlp_refs = [next(it) for _ in range(6)]
    g_in = [[next(it) for _ in range(3)] for _ in groups]
    cast_srcs = [next(it) for _ in casts]
    o_ref = next(it)
    g_out = [[next(it) for _ in range(3)] for _ in groups]
    cast_dsts = [next(it) for _ in range(_n_cast_outputs(casts))]
    h_scr, cond_scr, w1_scr, w2_scr = next(it), next(it), next(it), next(it)
    g_scr = [[next(it) for _ in range(3)] for _ in groups]
    sem = next(it)
    wbuf, acc_scr, wsem = next(it), next(it), next(it)
    j = pl.program_id(0)
    kt = D_MODEL // n_steps

    def w_copy(t):
        slot = t % ADA_RING
        return pltpu.make_async_copy(w_hbm.at[pl.ds(t * kt, kt), :], wbuf.at[slot], wsem.at[slot])

    @pl.when(j == 0)
    def _():
        for t in range(ADA_RING - 1):
            w_copy(t).start()

    @pl.when(j + ADA_RING - 1 < n_steps)
    def _():
        w_copy(j + ADA_RING - 1).start()

    _do_casts(casts, cast_srcs, cast_dsts)
    nlat = c_ref.shape[0]
    cond_scr[...] = jnp.zeros_like(cond_scr)
    cond_scr[0:1, :] = cctx_ref[...]
    cond_scr[1:1 + nlat, :] = c_ref[...]
    c = cond_scr[:, pl.ds(pl.multiple_of(j * kt, kt), kt)]
    s = (c * jax.nn.sigmoid(c)).astype(BF16)
    w_copy(j).wait()
    part = _dot(s, wbuf[j % ADA_RING].astype(BF16))

    @pl.when(j == 0)
    def _():
        acc_scr[...] = part + b_ref[...]

    @pl.when(j > 0)
    def _():
        acc_scr[...] += part

    @pl.when(j == n_steps - 1)
    def _():
        res = acc_scr[...]
        for r in range(res.shape[0]):
            o_ref[r] = res[r:r + 1, :]

    def writeback(g, o):
        return [pltpu.make_async_copy(scr.at[o], out.at[o], sem.at[(g * HY_ORDER + o) * 3 + k])
                for k, (scr, out) in enumerate(zip(g_scr[g], g_out[g]))]

    step = 0
    for g, ((L, blk), (zt_ref, rate_ref, fw_ref)) in enumerate(zip(groups, g_in)):
        def mlp_job(zt_ref=zt_ref, L=L):
            h_scr[0:L, :] = _filter_mlp(zt_ref, *mlp_refs, w1_scr, w2_scr)

        def order_job(o, g=g, rate_ref=rate_ref, fw_ref=fw_ref, L=L, blk=blk):
            _filter_spectra(h_scr, o, rate_ref, fw_ref, *g_scr[g], L=L, b=blk)
            for cp in writeback(g, o):
                cp.start()

        def all_job(mlp_job=mlp_job, order_job=order_job):
            mlp_job()
            for o in range(HY_ORDER):
                order_job(o)

        if L <= FILTER_ONE_STEP_LEN:
            jobs = [all_job]
        else:
            jobs = [mlp_job] + [functools.partial(order_job, o) for o in range(HY_ORDER)]
        for job in jobs:
            pl.when(pl.program_id(0) == step)(job)
            step += 1

    @pl.when(pl.program_id(0) == n_steps - 1)
    def _():
        for g in range(len(groups)):
            for o in range(HY_ORDER):
                for cp in writeback(g, o):
                    cp.wait()


def _ada(c_ctx, c, w, b, filt_params, groups, casts=()):
    n = w.shape[1]
    steps = ADA_STEPS
    kt = D_MODEL // steps
    c_in, c_out, c_shape, c_args = _cast_specs(casts, steps)
    in_specs = [_const_spec(c_ctx.shape), _const_spec(c.shape),
                pl.BlockSpec(memory_space=pl.ANY), _const_spec((1, n))]
    args = [c_ctx, c, w, b]
    for p in filt_params:
        in_specs.append(_const_spec(p.shape))
        args.append(p)
    out_specs = [pl.BlockSpec((MOD_ROWS, 1, n), lambda j: (0, 0, 0))]
    out_shape = [jax.ShapeDtypeStruct((MOD_ROWS, 1, n), F32)]
    max_len = 8
    spectra_scratch = []
    for L, blk, *consts in groups:
        nd = 2 * (L // blk) - 1
        max_len = max(max_len, L)
        for cst in consts:
            in_specs.append(_const_spec(cst.shape))
            args.append(cst)
        for shp in ((HY_ORDER, nd, blk, HY_W), (HY_ORDER, nd, blk, HY_W), (HY_ORDER, nd, 8, HY_W)):
            out_specs.append(pl.BlockSpec(memory_space=pl.ANY))
            out_shape.append(jax.ShapeDtypeStruct(shp, F32))
            spectra_scratch.append(pltpu.VMEM(shp, F32))
    n_jobs = sum(1 if L <= FILTER_ONE_STEP_LEN else 1 + HY_ORDER for L, *_ in groups)
    assert n_jobs <= steps
    return pl.pallas_call(
        functools.partial(_ada_kernel, casts=tuple(cs for _, cs in casts),
                          groups=tuple((L, blk) for L, blk, *_ in groups), n_steps=steps),
        grid=(steps,),
        in_specs=in_specs + c_in,
        out_specs=out_specs + c_out,
        out_shape=out_shape + c_shape,
        scratch_shapes=[pltpu.VMEM((max_len, HY_ORDER * 2 * HY_W), F32),
                        pltpu.VMEM((MOD_ROWS, D_MODEL), F32),
                        pltpu.VMEM((HY_EMB_PAD, HY_HIDDEN), F32),
                        pltpu.VMEM((HY_HIDDEN + 8, HY_HIDDEN), F32)]
        + spectra_scratch + [pltpu.SemaphoreType.DMA((len(groups) * HY_ORDER * 3,)),
                             pltpu.VMEM((ADA_RING, kt, n), F32), pltpu.VMEM((MOD_ROWS, n), F32),
                             pltpu.SemaphoreType.DMA((ADA_RING,))],
        compiler_params=_params(1),
        name="ada",
    )(*args, *c_args)


@functools.lru_cache(maxsize=None)
def _dft_mats(L):
    n = 2 * L
    t = np.arange(L, dtype=np.int64)
    f = np.arange(L, dtype=np.int64)
    ang = 2.0 * np.pi * ((f[:, None] * t[None, :]) % n).astype(np.float64) / n
    cos = np.cos(ang)
    sin = np.sin(ang)
    nyq = np.where(t % 2 == 0, 1.0, -1.0)
    fwd = np.concatenate([cos, -sin], axis=0)
    fwd[L] = nyq
    wre = np.full((L,), 2.0 / n)
    wre[0] = 1.0 / n
    inv = np.concatenate([cos.T * wre[None, :], -sin.T * (2.0 / n)], axis=1)
    inv[:, L] = nyq / n
    return jnp.asarray(fwd, dtype=BF16), jnp.asarray(inv, dtype=BF16)


@functools.lru_cache(maxsize=None)
def _filter_consts(L):
    t = np.linspace(0.0, 1.0, L)[:, None]
    ang = 2.0 * np.pi * np.arange(L, dtype=np.float64)[:, None] / L
    bands = np.linspace(1e-4, HY_BANDS - 1, HY_BANDS)[None]
    z = np.concatenate([t, np.cos(bands * ang), -np.sin(bands * ang)], axis=-1)
    z = np.pad(z, ((0, 0), (0, HY_EMB_PAD - HY_EMB)))
    z[:, HY_EMB] = 1.0
    max_decay = math.log(HY_TARGET) / HY_FAST_DECAY
    min_decay = math.log(HY_TARGET) / HY_SLOW_DECAY
    rates = np.abs(np.linspace(min_decay, max_decay, HY_W))[None, :]
    return np.asarray(z, np.float32), np.asarray(rates, np.float32)


def _filter_mlp(zt_ref, w1_ref, b1_ref, w2_ref, b2_ref, w3_ref, fr_ref, w1_scr, w2_scr):
    hi = lax.Precision.HIGHEST
    tdims = (((0,), (0,)), ((), ()))
    w1_scr[...] = jnp.zeros_like(w1_scr)
    w1_scr[0:HY_EMB, :] = w1_ref[...]
    w1_scr[HY_EMB:HY_EMB + 1, :] = b1_ref[...]
    w2_scr[...] = jnp.zeros_like(w2_scr)
    w2_scr[0:HY_HIDDEN, :] = w2_ref[...]
    w2_scr[HY_HIDDEN:HY_HIDDEN + 1, :] = b2_ref[...]
    fr = fr_ref[...]
    zt = zt_ref[...]
    h1 = jnp.sin(lax.dot_general(w1_scr[...] * fr, zt, tdims, precision=hi,
                                 preferred_element_type=F32))
    h1 = jnp.concatenate([h1, jnp.ones((8, zt.shape[1]), F32)], axis=0)
    h2 = jnp.sin(lax.dot_general(w2_scr[...] * fr, h1, tdims, precision=hi,
                                 preferred_element_type=F32))
    h2_hi = h2.astype(BF16)
    h2_lo = (h2 - h2_hi.astype(F32)).astype(BF16)
    w3 = w3_ref[...]
    w3_hi = w3.astype(BF16)
    w3_lo = (w3 - w3_hi.astype(F32)).astype(BF16)
    lhs = jnp.concatenate([h2_hi, h2_lo, h2_hi, jnp.zeros_like(h2_hi)], axis=0)
    rhs = jnp.concatenate([w3_hi, w3_hi, w3_lo, jnp.zeros_like(w3_hi)], axis=0)
    h = lax.dot_general(lhs, rhs, tdims, preferred_element_type=F32)
    return h


def _filter_spectra(h_ref, o, rate_ref, fw_ref, oa_ref, ob_ref, od_ref, *, L, b):
    m = L // b
    row_l = lax.broadcasted_iota(jnp.int32, (L, HY_W), 0)
    row_b = lax.broadcasted_iota(jnp.int32, (b, HY_W), 0)
    t = row_l.astype(F32) * (1.0 / (L - 1))
    win = jnp.exp(-t * rate_ref[...])
    sg = jnp.where(row_b % 2 == 0, 1.0, -1.0)
    row0_l = row_l == 0
    row0_b = row_b == 0
    row0_8 = lax.broadcasted_iota(jnp.int32, (8, HY_W), 0) == 0
    base = o * 2 * HY_W
    fwd = h_ref[0:L, base:base + HY_W] * win
    bwd = jnp.where(row0_l, 0.0, h_ref[0:L, base + HY_W:base + 2 * HY_W] * win)
    nrm = (jnp.sum(jnp.abs(fwd), axis=0, keepdims=True)
           + jnp.sum(jnp.abs(bwd), axis=0, keepdims=True))
    inv = 1.0 / nrm
    fn = fwd * inv
    bn = bwd * inv
    xr, xn, xi, wr, wn, wi = [], [], [], [], [], []
    for r in range(m):
        p = _dot(fw_ref[...], fn[r * b:(r + 1) * b].astype(BF16))
        q = _dot(fw_ref[...], bn[r * b:(r + 1) * b].astype(BF16))
        xr.append(p[0:b])
        xn.append(p[b:b + 1])
        xi.append(jnp.where(row0_b, 0.0, p[b:2 * b]))
        wr.append(q[0:b])
        wn.append(q[b:b + 1])
        wi.append(jnp.where(row0_b, 0.0, -q[b:2 * b]))

    def emit(d, ka, kn, kb):
        oa_ref[o, d + m - 1] = ka
        ob_ref[o, d + m - 1] = kb
        od_ref[o, d + m - 1] = jnp.where(row0_8, kn, ka[0:8])

    emit(0, xr[0] + wr[0], xn[0] + wn[0], xi[0] + wi[0])
    for d in range(1, m):
        f0 = fn[(d - 1) * b:(d - 1) * b + 1]
        b0 = bn[(d - 1) * b:(d - 1) * b + 1]
        emit(d, xr[d] + sg * (xr[d - 1] - f0), xn[d] + (xn[d - 1] - f0),
             xi[d] + sg * xi[d - 1])
        emit(-d, wr[d] + sg * (wr[d - 1] - b0), wn[d] + (wn[d - 1] - b0),
             wi[d] + sg * wi[d - 1])


def _ret_init(decf_ref, decb_ref, mask_scr, vec_scr, cd_scr, C):
    H, E = RET_HEADS, HEAD_DIM
    scale = float(E) ** -0.5

    @pl.when(pl.program_id(0) == 0)
    def _():
        dec = jnp.concatenate([jnp.full((1, C), ref[h], F32)
                               for ref in (decf_ref, decb_ref) for h in range(H)], axis=0)
        lg = jnp.log(jax.nn.sigmoid(dec))
        cd_scr[...] = jnp.exp(float(C) * lg[:, 0:E])
        ii = lax.broadcasted_iota(jnp.int32, (C, C), 0)
        jj = lax.broadcasted_iota(jnp.int32, (C, C), 1)
        rel = (ii - jj).astype(F32)
        ri = lax.broadcasted_iota(jnp.int32, (C, E), 0).astype(F32)
        for h in range(H):
            lf = lg[h:h + 1, :]
            lb = lg[H + h:H + h + 1, :]
            mf = jnp.where(rel >= 0, jnp.exp(jnp.maximum(rel, 0.0) * lf), 0.0)
            mb = jnp.where(rel <= 0, jnp.exp(jnp.maximum(-rel, 0.0) * lb), 0.0)
            mask_scr[h] = scale * (mf + mb)
            lfe, lbe = lf[:, 0:E], lb[:, 0:E]
            vec_scr[h, 0] = jnp.exp((ri + 1.0) * lfe)
            vec_scr[h, 1] = jnp.exp((float(C) - ri) * lbe)
            vec_scr[h, 2] = scale * jnp.exp((float(C) - 1.0 - ri) * lfe)
            vec_scr[h, 3] = scale * jnp.exp(ri * lbe)


def _ret_core(hn, w_ref, s0f_ref, s0b_ref, wo_ref, y_ref, sf_ref, sb_ref, mask_scr, vec_scr,
              cd_scr, g_scr, *, L, C, nb, has_init, emit_state):
    n = L // C
    H, E = RET_HEADS, HEAD_DIM
    tdims = (((0,), (0,)), ((), ()))
    ndims = (((1,), (1,)), ((), ()))
    chains = [(s, h) for s in range(nb) for h in range(H)]
    rows = [slice(c * C, (c + 1) * C) for c in range(n)]
    qkvg = [_dot(hn[s], w_ref[...]) for s in range(nb)]

    def cols(s, part, h):
        return qkvg[s][:, part * RET_W + h * E:part * RET_W + (h + 1) * E]

    qb = [cols(s, 0, h).astype(BF16) for s, h in chains]
    kf = [cols(s, 1, h) for s, h in chains]
    kb = [k.astype(BF16) for k in kf]
    vb = [cols(s, 2, h).astype(BF16) for s, h in chains]
    att = [[lax.dot_general(qb[i][r], kb[i][r], ndims, preferred_element_type=F32) for r in rows]
           for i in range(len(chains))]
    prob = [[(att[i][c] * mask_scr[h]).astype(BF16) for c in range(n)]
            for i, (s, h) in enumerate(chains)]
    out = [[_dot(prob[i][c], vb[i][rows[c]]) for c in range(n)] for i in range(len(chains))]
    kv = []
    for i, (s, h) in enumerate(chains):
        dk2 = jnp.concatenate([vec_scr[h, 2], vec_scr[h, 3]], axis=1)
        per_c = []
        for r in rows:
            k2 = (jnp.concatenate([kf[i][r], kf[i][r]], axis=1) * dk2).astype(BF16)
            per_c.append(lax.dot_general(k2, vb[i][r], tdims, preferred_element_type=F32))
        kv.append(per_c)
    for i, (s, h) in enumerate(chains):
        cdf = cd_scr[h:h + 1, :]
        cdb = cd_scr[H + h:H + h + 1, :]
        sf_in, sb_in = [None] * n, [None] * n
        st = s0f_ref[s, h] if has_init else None
        for c in range(n):
            sf_in[c] = st
            kvc = kv[i][c][0:E]
            st = kvc if st is None else st * cdf + kvc
        if emit_state:
            sf_ref[s, h] = st
        st = s0b_ref[s, h] if has_init else None
        for c in range(n - 1, -1, -1):
            sb_in[c] = st
            kvc = kv[i][c][E:2 * E]
            st = kvc if st is None else st * cdb + kvc
        if emit_state:
            sb_ref[s, h] = st
        for c in range(n):
            if sf_in[c] is not None and sb_in[c] is not None:
                s2 = jnp.concatenate([sf_in[c], sb_in[c]], axis=1).astype(BF16)
                inter = _dot(qb[i][rows[c]], s2)
                out[i][c] = (out[i][c] + inter[:, 0:E] * vec_scr[h, 0]
                             + inter[:, E:2 * E] * vec_scr[h, 1])
            elif sf_in[c] is not None:
                out[i][c] = (out[i][c]
                             + _dot(qb[i][rows[c]], sf_in[c].astype(BF16)) * vec_scr[h, 0])
            elif sb_in[c] is not None:
                out[i][c] = (out[i][c]
                             + _dot(qb[i][rows[c]], sb_in[c].astype(BF16)) * vec_scr[h, 1])
    for i, (s, h) in enumerate(chains):
        for c in range(n):
            o = out[i][c]
            mu = jnp.mean(o, axis=-1, keepdims=True)
            d = o - mu
            var = jnp.mean(d * d, axis=-1, keepdims=True)
            on = d * lax.rsqrt(var + EPS)
            gg = cols(s, 3, h)[rows[c]]
            g_scr[s, rows[c], h * E:(h + 1) * E] = (gg * jax.nn.sigmoid(gg) * on).astype(BF16)
    for s in range(nb):
        y_ref[s] = _dot(g_scr[s], wo_ref[...])


def _hy_core(hn, w_ref, cw_ref, cb_ref, fw_ref, bw_ref, fa_ref, fb_ref, fd_ref, hb_ref, wo_ref,
             y_ref, *, L, W, b, nb):
    m = L // b
    CB = HY_CBLK
    nblk = HY_W // CB
    pos = lax.broadcasted_iota(jnp.int32, (L, CB), 0) % W
    first = pos == 0
    last = pos == W - 1
    chains = [(s, blk) for s in range(nb) for blk in range(nblk)]

    def short_conv(s, base, blk):
        cs = slice(base + blk * CB, base + (blk + 1) * CB)
        ug = _dot(hn[s], w_ref[:, cs])
        prev = jnp.where(first, 0.0, pltpu.roll(ug, 1, axis=0))
        nxt = jnp.where(last, 0.0, pltpu.roll(ug, L - 1, axis=0))
        taps = [cw_ref[:, t * N_HY + cs.start:t * N_HY + cs.stop] for t in range(3)]
        u = prev * taps[0] + ug * taps[1] + nxt * taps[2] + cb_ref[:, cs]
        return [u[j * b:(j + 1) * b] for j in range(m)]

    def long_conv(sigs, o):
        spec = [[_dot(fw_ref[...], sj.astype(BF16)) for sj in sig] for sig in sigs]
        prods = []
        for (s, blk), sp in zip(chains, spec):
            cs = slice(blk * CB, (blk + 1) * CB)
            per_i = []
            for i in range(m):
                yre = yim = yim8 = None
                for j in range(m):
                    d = i - j + m - 1
                    sre, sim = sp[j][0:b], sp[j][b:2 * b]
                    ka, kb = fa_ref[o, d, :, cs], fb_ref[o, d, :, cs]
                    tre = sre * ka - sim * kb
                    tim = sre * kb + sim * ka
                    t8 = sre[0:8] * kb[0:8] + sim[0:8] * fd_ref[o, d, :, cs]
                    yre = tre if yre is None else yre + tre
                    yim = tim if yim is None else yim + tim
                    yim8 = t8 if yim8 is None else yim8 + t8
                yim = jnp.concatenate([yim8, yim[8:]], axis=0)
                per_i.append((yre.astype(BF16), yim.astype(BF16)))
            prods.append(per_i)
        return [[_dot(bw_ref[:, 0:b], yre) + _dot(bw_ref[:, b:2 * b], yim) for yre, yim in per_i]
                for per_i in prods]

    hv = [short_conv(s, 0, blk) for s, blk in chains]
    hx1 = [short_conv(s, HY_W, blk) for s, blk in chains]
    hx2 = [short_conv(s, 2 * HY_W, blk) for s, blk in chains]

    def gate(hx, conv, sig, o):
        out = []
        for (s, blk), hxc, cc, sc in zip(chains, hx, conv, sig):
            bias = hb_ref[o:o + 1, blk * CB:(blk + 1) * CB]
            out.append([hxc[i] * (cc[i] + sc[i] * bias) for i in range(m)])
        return out

    z = gate(hx1, long_conv(hv, 0), hv, 0)
    z = gate(hx2, long_conv(z, 1), z, 1)
    for s in range(nb):
        for i in range(m):
            acc = None
            for blk in range(nblk):
                zc = z[chains.index((s, blk))][i].astype(BF16)
                part = _dot(zc, wo_ref[blk * CB:(blk + 1) * CB, :])
                acc = part if acc is None else acc + part
            y_ref[s, i * b:(i + 1) * b, :] = acc


def _mix_kernel(*refs, L, C, W, b, nb, do_ret, do_hy, has_init, emit_state, casts):
    it = iter(refs)
    x_ref, mod_ref, n1_ref = next(it), next(it), next(it)
    s0f_ref = s0b_ref = sf_ref = sb_ref = None
    if do_ret:
        wq_ref, decf_ref, decb_ref = next(it), next(it), next(it)
        if has_init:
            s0f_ref, s0b_ref = next(it), next(it)
        wo_ret_ref = next(it)
    if do_hy:
        hy_in = [next(it) for _ in range(10)]
    cast_srcs = [next(it) for _ in casts]
    if do_ret:
        y_ret_ref = next(it)
        if emit_state:
            sf_ref, sb_ref = next(it), next(it)
    if do_hy:
        y_hy_ref = next(it)
    cast_dsts = [next(it) for _ in range(_n_cast_outputs(casts))]
    if do_ret:
        ret_scr = [next(it) for _ in range(4)]
        _ret_init(decf_ref, decb_ref, ret_scr[0], ret_scr[1], ret_scr[2], C)
    _do_casts(casts, cast_srcs, cast_dsts)
    mod = mod_ref[0]
    hn = [_modnorm(x_ref[s], n1_ref[...], _mod(mod, 1), _mod(mod, 0)).astype(BF16)
          for s in range(nb)]
    if do_hy:
        _hy_core(hn, *hy_in, y_hy_ref, L=L, W=W, b=b, nb=nb)
    if do_ret:
        _ret_core(hn, wq_ref, s0f_ref, s0b_ref, wo_ret_ref, y_ret_ref, sf_ref, sb_ref, *ret_scr,
                  L=L, C=C, nb=nb, has_init=has_init, emit_state=emit_state)


def _mixer(x, mods, mod_row, norm1, *, nb, ret=None, hy=None, casts=()):
    B, L, D = x.shape
    H, E = RET_HEADS, HEAD_DIM
    C = min(RET_CHUNK, L)
    seq_spec = pl.BlockSpec((nb, L, D), lambda g: (g, 0, 0))
    in_specs = [seq_spec,
                pl.BlockSpec((1, 1, N_MOD * D), lambda g: (mod_row(g * nb), 0, 0)),
                _const_spec((1, D))]
    args = [x, mods, norm1]
    out_specs, out_shape, scratch = [], [], []
    has_init = emit_state = False
    W = b = None
    if ret is not None:
        w_qkvg, dec_f, dec_b, s0f, s0b, w_o, emit_state = ret
        has_init = s0f is not None
        st_spec = pl.BlockSpec((nb, H, E, E), lambda g: (g, 0, 0, 0))
        smem = pl.BlockSpec(memory_space=pltpu.SMEM)
        in_specs += [_const_spec((D, N_QKVG)), smem, smem]
        args += [w_qkvg, dec_f, dec_b]
        if has_init:
            in_specs += [st_spec, st_spec]
            args += [s0f, s0b]
        in_specs.append(_const_spec((RET_W, D)))
        args.append(w_o)
        out_specs.append(seq_spec)
        out_shape.append(jax.ShapeDtypeStruct((B, L, D), F32))
        if emit_state:
            out_specs += [st_spec, st_spec]
            out_shape += [jax.ShapeDtypeStruct((B, H, E, E), F32)] * 2
        scratch = [pltpu.VMEM((H, C, C), F32), pltpu.VMEM((H, 4, C, E), F32),
                   pltpu.VMEM((8, E), F32), pltpu.VMEM((nb, L, RET_W), BF16)]
    if hy is not None:
        w_hy, conv_w, conv_b, fw, bw, (fa, fb, fd), hy_bias, w_o, W, b = hy
        nd = fa.shape[1]
        in_specs += [_const_spec((D, N_HY)), _const_spec((1, 3 * N_HY)), _const_spec((1, N_HY)),
                     _const_spec((2 * b, b)), _const_spec((b, 2 * b)),
                     _const_spec((HY_ORDER, nd, b, HY_W)), _const_spec((HY_ORDER, nd, b, HY_W)),
                     _const_spec((HY_ORDER, nd, 8, HY_W)), _const_spec((HY_ORDER, HY_W)),
                     _const_spec((HY_W, D))]
        args += [w_hy, conv_w, conv_b, fw, bw, fa, fb, fd, hy_bias, w_o]
        out_specs.append(seq_spec)
        out_shape.append(jax.ShapeDtypeStruct((B, L, D), F32))
    c_in, c_out, c_shape, c_args = _cast_specs(casts, B // nb)
    name = ("ret" if ret is not None else "") + ("hy" if hy is not None else "")
    return pl.pallas_call(
        functools.partial(_mix_kernel, L=L, C=C, W=W, b=b, nb=nb, do_ret=ret is not None,
                          do_hy=hy is not None, has_init=has_init, emit_state=emit_state,
                          casts=tuple(cs for _, cs in casts)),
        grid=(B // nb,),
        in_specs=in_specs + c_in,
        out_specs=out_specs + c_out,
        out_shape=out_shape + c_shape,
        scratch_shapes=scratch,
        compiler_params=_params(1),
        name=f"{name}{L}",
    )(*args, *c_args)


def _mlp_kernel(x_ref, yr_ref, yh_ref, modp_ref, modq_ref, n1_ref, n2_ref, fg_ref, wg_ref,
                wout_ref, wfi_hbm, wfo_hbm, y_ref, x1_scr, h2_scr, wfi_ref, wfo_ref, sem, *,
                n_tiles):
    i = pl.program_id(0)
    wr = i % 2
    rd = 1 - wr
    nq = N_GATE // 4

    def pre_stages():
        mp = modp_ref[0]
        st = {}

        def p1():
            st["x"] = x_ref[...]
            st["hn"] = _modnorm(st["x"], n1_ref[...], _mod(mp, 1), _mod(mp, 0)).astype(BF16)

        def p2(q):
            def f():
                st["g%d" % q] = _dot(st["hn"], wg_ref[:, q * nq:(q + 1) * nq])
            return f

        def p3(h):
            def f():
                cs = slice(h * nq, (h + 1) * nq)
                st["mix%d" % h] = (jax.nn.sigmoid(st["g%d" % h]) * yr_ref[:, cs]
                                   + jax.nn.sigmoid(st["g%d" % (2 + h)]) * yh_ref[:, cs]
                                   ).astype(BF16)
            return f

        def p4():
            upd = (_dot(st["mix0"], wout_ref[0:nq, :]) + _dot(st["mix1"], wout_ref[nq:2 * nq, :]))
            st["x1"] = st["x"] + _mod(mp, 2) * upd

        def p5():
            x1_scr[wr] = st["x1"]
            h2_scr[wr] = _modnorm(st["x1"], n2_ref[...], _mod(mp, 4), _mod(mp, 3)).astype(BF16)

        return [p1, p2(0), p2(1), p2(2), p2(3), p3(0), p3(1), p4, p5]

    def ffn_stages():
        mq = modq_ref[0]
        st = {"acc": None}

        def f(j):
            def g():
                cs = slice(j * FF_CHUNK, (j + 1) * FF_CHUNK)
                h2 = h2_scr[rd]
                a = _dot(h2, wfi_ref[:, cs])
                b = _dot(h2, wfi_ref[:, D_FF + j * FF_CHUNK:D_FF + (j + 1) * FF_CHUNK])
                ff = (a * jax.nn.sigmoid(a) * b).astype(BF16)
                part = _dot(ff, wfo_ref[cs, :])
                st["acc"] = part if st["acc"] is None else st["acc"] + part
            return g

        def e():
            x2 = x1_scr[rd] + _mod(mq, 5) * st["acc"]
            ms = jnp.mean(x2 * x2, axis=-1, keepdims=True)
            y_ref[...] = x2 * lax.rsqrt(ms + EPS) * fg_ref[...]

        return [f(j) for j in range(D_FF // FF_CHUNK)] + [e]

    @pl.when(i == 0)
    def _():
        copies = [pltpu.make_async_copy(wfi_hbm, wfi_ref, sem.at[0]),
                  pltpu.make_async_copy(wfo_hbm, wfo_ref, sem.at[1])]
        for cp in copies:
            cp.start()
        for stage in pre_stages():
            stage()
        for cp in copies:
            cp.wait()

    @pl.when(jnp.logical_and(i > 0, i < n_tiles))
    def _():
        pre, ffn = pre_stages(), ffn_stages()
        order = []
        while pre or ffn:
            if ffn:
                order.append(ffn.pop(0))
            if pre:
                order.append(pre.pop(0))
        for stage in order:
            stage()

    @pl.when(i == n_tiles)
    def _():
        for stage in ffn_stages():
            stage()


def _mlp(x, y_ret, y_hy, mods, mod_row, norm1, norm2, final_g, w_gate, w_out, w_fi, w_fo):
    B, L, D = x.shape
    T = MLP_ROWS
    n_tiles = B * L // T
    flat = lambda a: a.reshape(B * L, D)
    pre_tile = lambda i: jnp.minimum(i, n_tiles - 1)
    post_tile = lambda i: jnp.maximum(i - 1, 0)
    act = pl.BlockSpec((T, D), lambda i: (pre_tile(i), 0))
    y = pl.pallas_call(
        functools.partial(_mlp_kernel, n_tiles=n_tiles),
        grid=(n_tiles + 1,),
        in_specs=[act, act, act,
                  pl.BlockSpec((1, 1, N_MOD * D),
                               lambda i: (mod_row((pre_tile(i) * T) // L), 0, 0)),
                  pl.BlockSpec((1, 1, N_MOD * D),
                               lambda i: (mod_row((post_tile(i) * T) // L), 0, 0)),
                  _const_spec((1, D)), _const_spec((1, D)), _const_spec((1, D)),
                  _const_spec((D, N_GATE)),
                  _const_spec((D, D)),
                  pl.BlockSpec(memory_space=pl.ANY),
                  pl.BlockSpec(memory_space=pl.ANY)],
        out_specs=pl.BlockSpec((T, D), lambda i: (post_tile(i), 0)),
        out_shape=jax.ShapeDtypeStruct((B * L, D), F32),
        scratch_shapes=[pltpu.VMEM((2, T, D), F32), pltpu.VMEM((2, T, D), BF16),
                        pltpu.VMEM((D, 2 * D_FF), BF16), pltpu.VMEM((D_FF, D), BF16),
                        pltpu.SemaphoreType.DMA((2,))],
        compiler_params=_params(1),
        name=f"mlp{L}",
    )(flat(x), flat(y_ret), flat(y_hy), mods, mods, norm1, norm2, final_g, w_gate, w_out, w_fi,
      w_fo)
    return y.reshape(B, L, D)


def kernel(x_prompt, x_sample, state_ret_fwd, state_ret_bwd, c, c_ctx, norm1_g, norm2_g, w_ada,
           b_ada, w_in, ret_decay_fwd, ret_decay_bwd, hy_conv_w, hy_conv_b, hy_pos_w1, hy_pos_b1,
           hy_pos_w2, hy_pos_b2, hy_pos_w3, hy_sin_freq, hy_bias, w_ret_o, w_hy_o, w_out,
           w_ffn_in, w_ffn_out, final_g):
    assert w_in.shape[0] == 1, "single-layer configuration"
    nb_lat = x_sample.shape[0]
    l_ctx = x_prompt.shape[1]

    assert 1 + nb_lat <= MOD_ROWS
    norm1 = norm1_g[0][None, :]
    norm2 = norm2_g[0][None, :]
    fg = final_g[None, :]
    conv_w = hy_conv_w[0].reshape(1, 3 * N_HY)
    conv_b = hy_conv_b[0][None, :]
    filt_params = (hy_pos_w1[0], hy_pos_b1, hy_pos_w2[0], hy_pos_b2, hy_pos_w3[0], hy_sin_freq)

    n_hy_end = N_QKVG + N_HY
    w_in_parts = (slice(0, N_QKVG), slice(N_QKVG, n_hy_end), slice(n_hy_end, N_IN))
    groups = []
    for L in (l_ctx, x_sample.shape[1]):
        blk = min(HY_TBLK, L)
        fw, _ = _dft_mats(blk)
        z, rates = _filter_consts(L)
        groups.append((L, blk, jnp.asarray(z.T), jnp.asarray(rates), fw))
    (mods, fa_c, fb_c, fd_c, fa_l, fb_l, fd_l, w_qkvg, w_hy, w_gate, w_ret_o_b,
     w_hy_o_b) = _ada(c_ctx[None, :], c, w_ada[0], b_ada, filt_params, groups,
                      casts=[(w_in[0], w_in_parts), (w_ret_o[0], None), (w_hy_o[0], None)])

    def branches(x, filt, s0f, s0b, grid_w, emit_state):
        blk = min(HY_TBLK, x.shape[1])
        fw, bw = _dft_mats(blk)
        ret = (w_qkvg, ret_decay_fwd[0], ret_decay_bwd[0], s0f, s0b, w_ret_o_b, emit_state)
        hy = (w_hy, conv_w, conv_b, fw, bw, filt, hy_bias[0], w_hy_o_b, grid_w, blk)
        return ret, hy

    ctx_row = lambda b: 0
    lat_row = lambda b: b + 1
    ret, hy = branches(x_prompt, (fa_c, fb_c, fd_c), None, None, l_ctx, True)
    y_ret_c, s_f, s_b, y_hy_c, w_fi_b, w_fo_b, w_out_b = _mixer(
        x_prompt, mods, ctx_row, norm1, nb=CTX_SEQS, ret=ret, hy=hy,
        casts=[(w_ffn_in[0], None), (w_ffn_out[0], None), (w_out[0], None)])
    y_prompt = _mlp(x_prompt, y_ret_c, y_hy_c, mods, ctx_row, norm1, norm2, fg, w_gate, w_out_b,
                    w_fi_b, w_fo_b)
    ret, hy = branches(x_sample, (fa_l, fb_l, fd_l), state_ret_fwd[:, 0], state_ret_bwd[:, 0],
                       GRID_W, False)
    y_ret_l, = _mixer(x_sample, mods, lat_row, norm1, nb=1, ret=ret)
    y_hy_l, = _mixer(x_sample, mods, lat_row, norm1, nb=1, hy=hy)
    y_sample = _mlp(x_sample, y_ret_l, y_hy_l, mods, lat_row, norm1, norm2, fg, w_gate,
                    w_out_b, w_fi_b, w_fo_b)
    return (y_prompt, y_sample, s_f[:, None], s_b[:, None])
```

```python
import functools
import math

import numpy as np
import jax
import jax.numpy as jnp
from jax import lax
from jax.experimental import pallas as pl
from jax.experimental.pallas import tpu as pltpu

F32 = jnp.float32
BF16 = jnp.bfloat16

D_MODEL = 1024
RET_HEADS = 4
HEAD_DIM = 128
RET_W = RET_HEADS * HEAD_DIM
HY_W = 512
HY_ORDER = 2
HY_BANDS = 16
HY_EMB = 1 + 2 * HY_BANDS
HY_EMB_PAD = 40
HY_HIDDEN = 64
HY_FAST_DECAY = 0.3
HY_SLOW_DECAY = 1.5
HY_TARGET = 1e-2
D_FF = 2816
N_QKVG = 4 * RET_W
N_HY = 3 * HY_W
N_GATE = 2 * D_MODEL
N_IN = N_QKVG + N_HY + N_GATE
N_MOD = 6
MOD_ROWS = 8
EPS = 1e-6
GRID_W = 64
RET_CHUNK = 256
HY_CBLK = 256
HY_TBLK = 512
CTX_SEQS = 2
MLP_ROWS = 512
FF_CHUNK = 256
FILTER_ONE_STEP_LEN = 256
ADA_STEPS = 8
ADA_RING = 4
VMEM_LIMIT = 56 * 1024 * 1024


def _const_spec(shape):
    nd = len(shape)
    return pl.BlockSpec(shape, lambda *_: (0,) * nd, pipeline_mode=pl.Buffered(1))


def _params(n_axes):
    return pltpu.CompilerParams(dimension_semantics=("arbitrary",) * n_axes,
                                vmem_limit_bytes=VMEM_LIMIT)


def _modnorm(x, g, scale, shift):
    ms = jnp.mean(x * x, axis=-1, keepdims=True)
    return (x * lax.rsqrt(ms + EPS) * g) * (1.0 + scale) + shift


def _mod(mod, k):
    return mod[:, k * D_MODEL:(k + 1) * D_MODEL]


def _dot(a, b):
    return jnp.dot(a, b, preferred_element_type=F32)


def _cast_specs(casts, steps):
    in_specs, out_specs, out_shape, args = [], [], [], []
    for arr, col_slices in casts:
        rows, width = arr.shape
        rb = rows // steps
        assert rb * steps == rows and rb % 16 == 0
        in_specs.append(pl.BlockSpec((rb, width), lambda g: (g, 0)))
        args.append(arr)
        for cs in col_slices or (slice(0, width),):
            cols = cs.stop - cs.start
            out_specs.append(pl.BlockSpec((rb, cols), lambda g: (g, 0)))
            out_shape.append(jax.ShapeDtypeStruct((rows, cols), BF16))
    return in_specs, out_specs, out_shape, args


def _n_cast_outputs(col_slices_per_src):
    return sum(1 if s is None else len(s) for s in col_slices_per_src)


def _do_casts(col_slices_per_src, srcs, dsts):
    dsts = iter(dsts)
    for col_slices, src in zip(col_slices_per_src, srcs):
        if col_slices is None:
            next(dsts)[...] = src[...].astype(BF16)
        else:
            for cs in col_slices:
                next(dsts)[...] = src[:, cs].astype(BF16)


def _ada_kernel(*refs, casts, groups, n_steps):
    it = iter(refs)
    cctx_ref, c_ref, w_hbm, b_ref = next(it), next(it), next(it), next(it)
    mlp_refs = [next(it) for _ in range(6)]
    g_in = [[next(it) for _ in range(3)] for _ in groups]
    cast_srcs = [next(it) for _ in casts]
    o_ref = next(it)
    g_out = [[next(it) for _ in range(3)] for _ in groups]
    cast_dsts = [next(it) for _ in range(_n_cast_outputs(casts))]
    h_scr, cond_scr, w1_scr, w2_scr = next(it), next(it), next(it), next(it)
    g_scr = [[next(it) for _ in range(3)] for _ in groups]
    sem = next(it)
    wbuf, acc_scr, wsem = next(it), next(it), next(it)
    j = pl.program_id(0)
    kt = D_MODEL // n_steps

    def w_copy(t):
        slot = t % ADA_RING
        return pltpu.make_async_copy(w_hbm.at[pl.ds(t * kt, kt), :], wbuf.at[slot], wsem.at[slot])

    @pl.when(j == 0)
    def _():
        for t in range(ADA_RING - 1):
            w_copy(t).start()

    @pl.when(j + ADA_RING - 1 < n_steps)
    def _():
        w_copy(j + ADA_RING - 1).start()

    _do_casts(casts, cast_srcs, cast_dsts)
    nlat = c_ref.shape[0]
    cond_scr[...] = jnp.zeros_like(cond_scr)
    cond_scr[0:1, :] = cctx_ref[...]
    cond_scr[1:1 + nlat, :] = c_ref[...]
    c = cond_scr[:, pl.ds(pl.multiple_of(j * kt, kt), kt)]
    s = (c * jax.nn.sigmoid(c)).astype(BF16)
    w_copy(j).wait()
    part = _dot(s, wbuf[j % ADA_RING].astype(BF16))

    @pl.when(j == 0)
    def _():
        acc_scr[...] = part + b_ref[...]

    @pl.when(j > 0)
    def _():
        acc_scr[...] += part

    @pl.when(j == n_steps - 1)
    def _():
        res = acc_scr[...]
        for r in range(res.shape[0]):
            o_ref[r] = res[r:r + 1, :]

    def writeback(g, o):
        return [pltpu.make_async_copy(scr.at[o], out.at[o], sem.at[(g * HY_ORDER + o) * 3 + k])
                for k, (scr, out) in enumerate(zip(g_scr[g], g_out[g]))]

    step = 0
    for g, ((L, blk), (zt_ref, rate_ref, fw_ref)) in enumerate(zip(groups, g_in)):
        def mlp_job(zt_ref=zt_ref, L=L):
            h_scr[0:L, :] = _filter_mlp(zt_ref, *mlp_refs, w1_scr, w2_scr)

        def order_job(o, g=g, rate_ref=rate_ref, fw_ref=fw_ref, L=L, blk=blk):
            _filter_spectra(h_scr, o, rate_ref, fw_ref, *g_scr[g], L=L, b=blk)
            for cp in writeback(g, o):
                cp.start()

        def all_job(mlp_job=mlp_job, order_job=order_job):
            mlp_job()
            for o in range(HY_ORDER):
                order_job(o)

        if L <= FILTER_ONE_STEP_LEN:
            jobs = [all_job]
        else:
            jobs = [mlp_job] + [functools.partial(order_job, o) for o in range(HY_ORDER)]
        for job in jobs:
            pl.when(pl.program_id(0) == step)(job)
            step += 1

    @pl.when(pl.program_id(0) == n_steps - 1)
    def _():
        for g in range(len(groups)):
            for o in range(HY_ORDER):
                for cp in writeback(g, o):
                    cp.wait()


def _ada(c_ctx, c, w, b, filt_params, groups, casts=()):
    n = w.shape[1]
    steps = ADA_STEPS
    kt = D_MODEL // steps
    c_in, c_out, c_shape, c_args = _cast_specs(casts, steps)
    in_specs = [_const_spec(c_ctx.shape), _const_spec(c.shape),
                pl.BlockSpec(memory_space=pl.ANY), _const_spec((1, n))]
    args = [c_ctx, c, w, b]
    for p in filt_params:
        in_specs.append(_const_spec(p.shape))
        args.append(p)
    out_specs = [pl.BlockSpec((MOD_ROWS, 1, n), lambda j: (0, 0, 0))]
    out_shape = [jax.ShapeDtypeStruct((MOD_ROWS, 1, n), F32)]
    max_len = 8
    spectra_scratch = []
    for L, blk, *consts in groups:
        nd = 2 * (L // blk) - 1
        max_len = max(max_len, L)
        for cst in consts:
            in_specs.append(_const_spec(cst.shape))
            args.append(cst)
        for shp in ((HY_ORDER, nd, blk, HY_W), (HY_ORDER, nd, blk, HY_W), (HY_ORDER, nd, 8, HY_W)):
            out_specs.append(pl.BlockSpec(memory_space=pl.ANY))
            out_shape.append(jax.ShapeDtypeStruct(shp, F32))
            spectra_scratch.append(pltpu.VMEM(shp, F32))
    n_jobs = sum(1 if L <= FILTER_ONE_STEP_LEN else 1 + HY_ORDER for L, *_ in groups)
    assert n_jobs <= steps
    return pl.pallas_call(
        functools.partial(_ada_kernel, casts=tuple(cs for _, cs in casts),
                          groups=tuple((L, blk) for L, blk, *_ in groups), n_steps=steps),
        grid=(steps,),
        in_specs=in_specs + c_in,
        out_specs=out_specs + c_out,
        out_shape=out_shape + c_shape,
        scratch_shapes=[pltpu.VMEM((max_len, HY_ORDER * 2 * HY_W), F32),
                        pltpu.VMEM((MOD_ROWS, D_MODEL), F32),
                        pltpu.VMEM((HY_EMB_PAD, HY_HIDDEN), F32),
                        pltpu.VMEM((HY_HIDDEN + 8, HY_HIDDEN), F32)]
        + spectra_scratch + [pltpu.SemaphoreType.DMA((len(groups) * HY_ORDER * 3,)),
                             pltpu.VMEM((ADA_RING, kt, n), F32), pltpu.VMEM((MOD_ROWS, n), F32),
                             pltpu.SemaphoreType.DMA((ADA_RING,))],
        compiler_params=_params(1),
        name="ada",
    )(*args, *c_args)


@functools.lru_cache(maxsize=None)
def _dft_mats(L):
    n = 2 * L
    t = np.arange(L, dtype=np.int64)
    f = np.arange(L, dtype=np.int64)
    ang = 2.0 * np.pi * ((f[:, None] * t[None, :]) % n).astype(np.float64) / n
    cos = np.cos(ang)
    sin = np.sin(ang)
    nyq = np.where(t % 2 == 0, 1.0, -1.0)
    fwd = np.concatenate([cos, -sin], axis=0)
    fwd[L] = nyq
    wre = np.full((L,), 2.0 / n)
    wre[0] = 1.0 / n
    inv = np.concatenate([cos.T * wre[None, :], -sin.T * (2.0 / n)], axis=1)
    inv[:, L] = nyq / n
    return jnp.asarray(fwd, dtype=BF16), jnp.asarray(inv, dtype=BF16)


@functools.lru_cache(maxsize=None)
def _filter_consts(L):
    t = np.linspace(0.0, 1.0, L)[:, None]
    ang = 2.0 * np.pi * np.arange(L, dtype=np.float64)[:, None] / L
    bands = np.linspace(1e-4, HY_BANDS - 1, HY_BANDS)[None]
    z = np.concatenate([t, np.cos(bands * ang), -np.sin(bands * ang)], axis=-1)
    z = np.pad(z, ((0, 0), (0, HY_EMB_PAD - HY_EMB)))
    z[:, HY_EMB] = 1.0
    max_decay = math.log(HY_TARGET) / HY_FAST_DECAY
    min_decay = math.log(HY_TARGET) / HY_SLOW_DECAY
    rates = np.abs(np.linspace(min_decay, max_decay, HY_W))[None, :]
    return np.asarray(z, np.float32), np.asarray(rates, np.float32)


def _filter_mlp(zt_ref, w1_ref, b1_ref, w2_ref, b2_ref, w3_ref, fr_ref, w1_scr, w2_scr):
    hi = lax.Precision.HIGHEST
    tdims = (((0,), (0,)), ((), ()))
    w1_scr[...] = jnp.zeros_like(w1_scr)
    w1_scr[0:HY_EMB, :] = w1_ref[...]
    w1_scr[HY_EMB:HY_EMB + 1, :] = b1_ref[...]
    w2_scr[...] = jnp.zeros_like(w2_scr)
    w2_scr[0:HY_HIDDEN, :] = w2_ref[...]
    w2_scr[HY_HIDDEN:HY_HIDDEN + 1, :] = b2_ref[...]
    fr = fr_ref[...]
    zt = zt_ref[...]
    h1 = jnp.sin(lax.dot_general(w1_scr[...] * fr, zt, tdims, precision=hi,
                                 preferred_element_type=F32))
    h1 = jnp.concatenate([h1, jnp.ones((8, zt.shape[1]), F32)], axis=0)
    h2 = jnp.sin(lax.dot_general(w2_scr[...] * fr, h1, tdims, precision=hi,
                                 preferred_element_type=F32))
    h2_hi = h2.astype(BF16)
    h2_lo = (h2 - h2_hi.astype(F32)).astype(BF16)
    w3 = w3_ref[...]
    w3_hi = w3.astype(BF16)
    w3_lo = (w3 - w3_hi.astype(F32)).astype(BF16)
    lhs = jnp.concatenate([h2_hi, h2_lo, h2_hi, jnp.zeros_like(h2_hi)], axis=0)
    rhs = jnp.concatenate([w3_hi, w3_hi, w3_lo, jnp.zeros_like(w3_hi)], axis=0)
    h = lax.dot_general(lhs, rhs, tdims, preferred_element_type=F32)
    return h


def _filter_spectra(h_ref, o, rate_ref, fw_ref, oa_ref, ob_ref, od_ref, *, L, b):
    m = L // b
    row_l = lax.broadcasted_iota(jnp.int32, (L, HY_W), 0)
    row_b = lax.broadcasted_iota(jnp.int32, (b, HY_W), 0)
    t = row_l.astype(F32) * (1.0 / (L - 1))
    win = jnp.exp(-t * rate_ref[...])
    sg = jnp.where(row_b % 2 == 0, 1.0, -1.0)
    row0_l = row_l == 0
    row0_b = row_b == 0
    row0_8 = lax.broadcasted_iota(jnp.int32, (8, HY_W), 0) == 0
    base = o * 2 * HY_W
    fwd = h_ref[0:L, base:base + HY_W] * win
    bwd = jnp.where(row0_l, 0.0, h_ref[0:L, base + HY_W:base + 2 * HY_W] * win)
    nrm = (jnp.sum(jnp.abs(fwd), axis=0, keepdims=True)
           + jnp.sum(jnp.abs(bwd), axis=0, keepdims=True))
    inv = 1.0 / nrm
    fn = fwd * inv
    bn = bwd * inv
    xr, xn, xi, wr, wn, wi = [], [], [], [], [], []
    for r in range(m):
        p = _dot(fw_ref[...], fn[r * b:(r + 1) * b].astype(BF16))
        q = _dot(fw_ref[...], bn[r * b:(r + 1) * b].astype(BF16))
        xr.append(p[0:b])
        xn.append(p[b:b + 1])
        xi.append(jnp.where(row0_b, 0.0, p[b:2 * b]))
        wr.append(q[0:b])
        wn.append(q[b:b + 1])
        wi.append(jnp.where(row0_b, 0.0, -q[b:2 * b]))

    def emit(d, ka, kn, kb):
        oa_ref[o, d + m - 1] = ka
        ob_ref[o, d + m - 1] = kb
        od_ref[o, d + m - 1] = jnp.where(row0_8, kn, ka[0:8])

    emit(0, xr[0] + wr[0], xn[0] + wn[0], xi[0] + wi[0])
    for d in range(1, m):
        f0 = fn[(d - 1) * b:(d - 1) * b + 1]
        b0 = bn[(d - 1) * b:(d - 1) * b + 1]
        emit(d, xr[d] + sg * (xr[d - 1] - f0), xn[d] + (xn[d - 1] - f0),
             xi[d] + sg * xi[d - 1])
        emit(-d, wr[d] + sg * (wr[d - 1] - b0), wn[d] + (wn[d - 1] - b0),
             wi[d] + sg * wi[d - 1])


def _ret_init(decf_ref, decb_ref, mask_scr, vec_scr, cd_scr, C):
    H, E = RET_HEADS, HEAD_DIM
    scale = float(E) ** -0.5

    @pl.when(pl.program_id(0) == 0)
    def _():
        dec = jnp.concatenate([jnp.full((1, C), ref[h], F32)
                               for ref in (decf_ref, decb_ref) for h in range(H)], axis=0)
        lg = jnp.log(jax.nn.sigmoid(dec))
        cd_scr[...] = jnp.exp(float(C) * lg[:, 0:E])
        ii = lax.broadcasted_iota(jnp.int32, (C, C), 0)
        jj = lax.broadcasted_iota(jnp.int32, (C, C), 1)
        rel = (ii - jj).astype(F32)
        ri = lax.broadcasted_iota(jnp.int32, (C, E), 0).astype(F32)
        for h in range(H):
            lf = lg[h:h + 1, :]
            lb = lg[H + h:H + h + 1, :]
            mf = jnp.where(rel >= 0, jnp.exp(jnp.maximum(rel, 0.0) * lf), 0.0)
            mb = jnp.where(rel <= 0, jnp.exp(jnp.maximum(-rel, 0.0) * lb), 0.0)
            mask_scr[h] = scale * (mf + mb)
            lfe, lbe = lf[:, 0:E], lb[:, 0:E]
            vec_scr[h, 0] = jnp.exp((ri + 1.0) * lfe)
            vec_scr[h, 1] = jnp.exp((float(C) - ri) * lbe)
            vec_scr[h, 2] = scale * jnp.exp((float(C) - 1.0 - ri) * lfe)
            vec_scr[h, 3] = scale * jnp.exp(ri * lbe)


def _ret_core(hn, w_ref, s0f_ref, s0b_ref, wo_ref, y_ref, sf_ref, sb_ref, mask_scr, vec_scr,
              cd_scr, g_scr, *, L, C, nb, has_init, emit_state):
    n = L // C
    H, E = RET_HEADS, HEAD_DIM
    tdims = (((0,), (0,)), ((), ()))
    ndims = (((1,), (1,)), ((), ()))
    chains = [(s, h) for s in range(nb) for h in range(H)]
    rows = [slice(c * C, (c + 1) * C) for c in range(n)]
    qkvg = [_dot(hn[s], w_ref[...]) for s in range(nb)]

    def cols(s, part, h):
        return qkvg[s][:, part * RET_W + h * E:part * RET_W + (h + 1) * E]

    qb = [cols(s, 0, h).astype(BF16) for s, h in chains]
    kf = [cols(s, 1, h) for s, h in chains]
    kb = [k.astype(BF16) for k in kf]
    vb = [cols(s, 2, h).astype(BF16) for s, h in chains]
    att = [[lax.dot_general(qb[i][r], kb[i][r], ndims, preferred_element_type=F32) for r in rows]
           for i in range(len(chains))]
    prob = [[(att[i][c] * mask_scr[h]).astype(BF16) for c in range(n)]
            for i, (s, h) in enumerate(chains)]
    out = [[_dot(prob[i][c], vb[i][rows[c]]) for c in range(n)] for i in range(len(chains))]
    kv = []
    for i, (s, h) in enumerate(chains):
        dk2 = jnp.concatenate([vec_scr[h, 2], vec_scr[h, 3]], axis=1)
        per_c = []
        for r in rows:
            k2 = (jnp.concatenate([kf[i][r], kf[i][r]], axis=1) * dk2).astype(BF16)
            per_c.append(lax.dot_general(k2, vb[i][r], tdims, preferred_element_type=F32))
        kv.append(per_c)
    for i, (s, h) in enumerate(chains):
        cdf = cd_scr[h:h + 1, :]
        cdb = cd_scr[H + h:H + h + 1, :]
        sf_in, sb_in = [None] * n, [None] * n
        st = s0f_ref[s, h] if has_init else None
        for c in range(n):
            sf_in[c] = st
            kvc = kv[i][c][0:E]
            st = kvc if st is None else st * cdf + kvc
        if emit_state:
            sf_ref[s, h] = st
        st = s0b_ref[s, h] if has_init else None
        for c in range(n - 1, -1, -1):
            sb_in[c] = st
            kvc = kv[i][c][E:2 * E]
            st = kvc if st is None else st * cdb + kvc
        if emit_state:
            sb_ref[s, h] = st
        for c in range(n):
            if sf_in[c] is not None and sb_in[c] is not None:
                s2 = jnp.concatenate([sf_in[c], sb_in[c]], axis=1).astype(BF16)
                inter = _dot(qb[i][rows[c]], s2)
                out[i][c] = (out[i][c] + inter[:, 0:E] * vec_scr[h, 0]
                             + inter[:, E:2 * E] * vec_scr[h, 1])
            elif sf_in[c] is not None:
                out[i][c] = (out[i][c]
                             + _dot(qb[i][rows[c]], sf_in[c].astype(BF16)) * vec_scr[h, 0])
            elif sb_in[c] is not None:
                out[i][c] = (out[i][c]
                             + _dot(qb[i][rows[c]], sb_in[c].astype(BF16)) * vec_scr[h, 1])
    for i, (s, h) in enumerate(chains):
        for c in range(n):
            o = out[i][c]
            mu = jnp.mean(o, axis=-1, keepdims=True)
            d = o - mu
            var = jnp.mean(d * d, axis=-1, keepdims=True)
            on = d * lax.rsqrt(var + EPS)
            gg = cols(s, 3, h)[rows[c]]
            g_scr[s, rows[c], h * E:(h + 1) * E] = (gg * jax.nn.sigmoid(gg) * on).astype(BF16)
    for s in range(nb):
        y_ref[s] = _dot(g_scr[s], wo_ref[...])


def _hy_core(hn, w_ref, cw_ref, cb_ref, fw_ref, bw_ref, fa_ref, fb_ref, fd_ref, hb_ref, wo_ref,
             y_ref, *, L, W, b, nb):
    m = L // b
    CB = HY_CBLK
    nblk = HY_W // CB
    pos = lax.broadcasted_iota(jnp.int32, (L, CB), 0) % W
    first = pos == 0
    last = pos == W - 1
    chains = [(s, blk) for s in range(nb) for blk in range(nblk)]

    def short_conv(s, base, blk):
        cs = slice(base + blk * CB, base + (blk + 1) * CB)
        ug = _dot(hn[s], w_ref[:, cs])
        prev = jnp.where(first, 0.0, pltpu.roll(ug, 1, axis=0))
        nxt = jnp.where(last, 0.0, pltpu.roll(ug, L - 1, axis=0))
        taps = [cw_ref[:, t * N_HY + cs.start:t * N_HY + cs.stop] for t in range(3)]
        u = prev * taps[0] + ug * taps[1] + nxt * taps[2] + cb_ref[:, cs]
        return [u[j * b:(j + 1) * b] for j in range(m)]

    def long_conv(sigs, o):
        spec = [[_dot(fw_ref[...], sj.astype(BF16)) for sj in sig] for sig in sigs]
        prods = []
        for (s, blk), sp in zip(chains, spec):
            cs = slice(blk * CB, (blk + 1) * CB)
            per_i = []
            for i in range(m):
                yre = yim = yim8 = None
                for j in range(m):
                    d = i - j + m - 1
                    sre, sim = sp[j][0:b], sp[j][b:2 * b]
                    ka, kb = fa_ref[o, d, :, cs], fb_ref[o, d, :, cs]
                    tre = sre * ka - sim * kb
                    tim = sre * kb + sim * ka
                    t8 = sre[0:8] * kb[0:8] + sim[0:8] * fd_ref[o, d, :, cs]
                    yre = tre if yre is None else yre + tre
                    yim = tim if yim is None else yim + tim
                    yim8 = t8 if yim8 is None else yim8 + t8
                yim = jnp.concatenate([yim8, yim[8:]], axis=0)
                per_i.append((yre.astype(BF16), yim.astype(BF16)))
            prods.append(per_i)
        return [[_dot(bw_ref[:, 0:b], yre) + _dot(bw_ref[:, b:2 * b], yim) for yre, yim in per_i]
                for per_i in prods]

    hv = [short_conv(s, 0, blk) for s, blk in chains]
    hx1 = [short_conv(s, HY_W, blk) for s, blk in chains]
    hx2 = [short_conv(s, 2 * HY_W, blk) for s, blk in chains]

    def gate(hx, conv, sig, o):
        out = []
        for (s, blk), hxc, cc, sc in zip(chains, hx, conv, sig):
            bias = hb_ref[o:o + 1, blk * CB:(blk + 1) * CB]
            out.append([hxc[i] * (cc[i] + sc[i] * bias) for i in range(m)])
        return out

    z = gate(hx1, long_conv(hv, 0), hv, 0)
    z = gate(hx2, long_conv(z, 1), z, 1)
    for s in range(nb):
        for i in range(m):
            acc = None
            for blk in range(nblk):
                zc = z[chains.index((s, blk))][i].astype(BF16)
                part = _dot(zc, wo_ref[blk * CB:(blk + 1) * CB, :])
                acc = part if acc is None else acc + part
            y_ref[s, i * b:(i + 1) * b, :] = acc


def _mix_kernel(*refs, L, C, W, b, nb, do_ret, do_hy, has_init, emit_state, casts):
    it = iter(refs)
    x_ref, mod_ref, n1_ref = next(it), next(it), next(it)
    s0f_ref = s0b_ref = sf_ref = sb_ref = None
    if do_ret:
        wq_ref, decf_ref, decb_ref = next(it), next(it), next(it)
        if has_init:
            s0f_ref, s0b_ref = next(it), next(it)
        wo_ret_ref = next(it)
    if do_hy:
        hy_in = [next(it) for _ in range(10)]
    cast_srcs = [next(it) for _ in casts]
    if do_ret:
        y_ret_ref = next(it)
        if emit_state:
            sf_ref, sb_ref = next(it), next(it)
    if do_hy:
        y_hy_ref = next(it)
    cast_dsts = [next(it) for _ in range(_n_cast_outputs(casts))]
    if do_ret:
        ret_scr = [next(it) for _ in range(4)]
        _ret_init(decf_ref, decb_ref, ret_scr[0], ret_scr[1], ret_scr[2], C)
    _do_casts(casts, cast_srcs, cast_dsts)
    mod = mod_ref[0]
    hn = [_modnorm(x_ref[s], n1_ref[...], _mod(mod, 1), _mod(mod, 0)).astype(BF16)
          for s in range(nb)]
    if do_hy:
        _hy_core(hn, *hy_in, y_hy_ref, L=L, W=W, b=b, nb=nb)
    if do_ret:
        _ret_core(hn, wq_ref, s0f_ref, s0b_ref, wo_ret_ref, y_ret_ref, sf_ref, sb_ref, *ret_scr,
                  L=L, C=C, nb=nb, has_init=has_init, emit_state=emit_state)


def _mixer(x, mods, mod_row, norm1, *, nb, ret=None, hy=None, casts=()):
    B, L, D = x.shape
    H, E = RET_HEADS, HEAD_DIM
    C = min(RET_CHUNK, L)
    seq_spec = pl.BlockSpec((nb, L, D), lambda g: (g, 0, 0))
    in_specs = [seq_spec,
                pl.BlockSpec((1, 1, N_MOD * D), lambda g: (mod_row(g * nb), 0, 0)),
                _const_spec((1, D))]
    args = [x, mods, norm1]
    out_specs, out_shape, scratch = [], [], []
    has_init = emit_state = False
    W = b = None
    if ret is not None:
        w_qkvg, dec_f, dec_b, s0f, s0b, w_o, emit_state = ret
        has_init = s0f is not None
        st_spec = pl.BlockSpec((nb, H, E, E), lambda g: (g, 0, 0, 0))
        smem = pl.BlockSpec(memory_space=pltpu.SMEM)
        in_specs += [_const_spec((D, N_QKVG)), smem, smem]
        args += [w_qkvg, dec_f, dec_b]
        if has_init:
            in_specs += [st_spec, st_spec]
            args += [s0f, s0b]
        in_specs.append(_const_spec((RET_W, D)))
        args.append(w_o)
        out_specs.append(seq_spec)
        out_shape.append(jax.ShapeDtypeStruct((B, L, D), F32))
        if emit_state:
            out_specs += [st_spec, st_spec]
            out_shape += [jax.ShapeDtypeStruct((B, H, E, E), F32)] * 2
        scratch = [pltpu.VMEM((H, C, C), F32), pltpu.VMEM((H, 4, C, E), F32),
                   pltpu.VMEM((8, E), F32), pltpu.VMEM((nb, L, RET_W), BF16)]
    if hy is not None:
        w_hy, conv_w, conv_b, fw, bw, (fa, fb, fd), hy_bias, w_o, W, b = hy
        nd = fa.shape[1]
        in_specs += [_const_spec((D, N_HY)), _const_spec((1, 3 * N_HY)), _const_spec((1, N_HY)),
                     _const_spec((2 * b, b)), _const_spec((b, 2 * b)),
                     _const_spec((HY_ORDER, nd, b, HY_W)), _const_spec((HY_ORDER, nd, b, HY_W)),
                     _const_spec((HY_ORDER, nd, 8, HY_W)), _const_spec((HY_ORDER, HY_W)),
                     _const_spec((HY_W, D))]
        args += [w_hy, conv_w, conv_b, fw, bw, fa, fb, fd, hy_bias, w_o]
        out_specs.append(seq_spec)
        out_shape.append(jax.ShapeDtypeStruct((B, L, D), F32))
    c_in, c_out, c_shape, c_args = _cast_specs(casts, B // nb)
    name = ("ret" if ret is not None else "") + ("hy" if hy is not None else "")
    return pl.pallas_call(
        functools.partial(_mix_kernel, L=L, C=C, W=W, b=b, nb=nb, do_ret=ret is not None,
                          do_hy=hy is not None, has_init=has_init, emit_state=emit_state,
                          casts=tuple(cs for _, cs in casts)),
        grid=(B // nb,),
        in_specs=in_specs + c_in,
        out_specs=out_specs + c_out,
        out_shape=out_shape + c_shape,
        scratch_shapes=scratch,
        compiler_params=_params(1),
        name=f"{name}{L}",
    )(*args, *c_args)


def _mlp_kernel(x_ref, yr_ref, yh_ref, modp_ref, modq_ref, n1_ref, n2_ref, fg_ref, wg_ref,
                wout_ref, wfi_hbm, wfo_hbm, y_ref, x1_scr, h2_scr, wfi_ref, wfo_ref, sem, *,
                n_tiles):
    i = pl.program_id(0)
    wr = i % 2
    rd = 1 - wr
    nq = N_GATE // 4

    def pre_stages():
        mp = modp_ref[0]
        st = {}

        def p1():
            st["x"] = x_ref[...]
            st["hn"] = _modnorm(st["x"], n1_ref[...], _mod(mp, 1), _mod(mp, 0)).astype(BF16)

        def p2(q):
            def f():
                st["g%d" % q] = _dot(st["hn"], wg_ref[:, q * nq:(q + 1) * nq])
            return f

        def p3(h):
            def f():
                cs = slice(h * nq, (h + 1) * nq)
                st["mix%d" % h] = (jax.nn.sigmoid(st["g%d" % h]) * yr_ref[:, cs]
                                   + jax.nn.sigmoid(st["g%d" % (2 + h)]) * yh_ref[:, cs]
                                   ).astype(BF16)
            return f

        def p4():
            upd = (_dot(st["mix0"], wout_ref[0:nq, :]) + _dot(st["mix1"], wout_ref[nq:2 * nq, :]))
            st["x1"] = st["x"] + _mod(mp, 2) * upd

        def p5():
            x1_scr[wr] = st["x1"]
            h2_scr[wr] = _modnorm(st["x1"], n2_ref[...], _mod(mp, 4), _mod(mp, 3)).astype(BF16)

        return [p1, p2(0), p2(1), p2(2), p2(3), p3(0), p3(1), p4, p5]

    def ffn_stages():
        mq = modq_ref[0]
        st = {"acc": None}

        def f(j):
            def g():
                cs = slice(j * FF_CHUNK, (j + 1) * FF_CHUNK)
                h2 = h2_scr[rd]
                a = _dot(h2, wfi_ref[:, cs])
                b = _dot(h2, wfi_ref[:, D_FF + j * FF_CHUNK:D_FF + (j + 1) * FF_CHUNK])
                ff = (a * jax.nn.sigmoid(a) * b).astype(BF16)
                part = _dot(ff, wfo_ref[cs, :])
                st["acc"] = part if st["acc"] is None else st["acc"] + part
            return g

        def e():
            x2 = x1_scr[rd] + _mod(mq, 5) * st["acc"]
            ms = jnp.mean(x2 * x2, axis=-1, keepdims=True)
            y_ref[...] = x2 * lax.rsqrt(ms + EPS) * fg_ref[...]

        return [f(j) for j in range(D_FF // FF_CHUNK)] + [e]

    @pl.when(i == 0)
    def _():
        copies = [pltpu.make_async_copy(wfi_hbm, wfi_ref, sem.at[0]),
                  pltpu.make_async_copy(wfo_hbm, wfo_ref, sem.at[1])]
        for cp in copies:
            cp.start()
        for stage in pre_stages():
            stage()
        for cp in copies:
            cp.wait()

    @pl.when(jnp.logical_and(i > 0, i < n_tiles))
    def _():
        pre, ffn = pre_stages(), ffn_stages()
        order = []
        while pre or ffn:
            if ffn:
                order.append(ffn.pop(0))
            if pre:
                order.append(pre.pop(0))
        for stage in order:
            stage()

    @pl.when(i == n_tiles)
    def _():
        for stage in ffn_stages():
            stage()


def _mlp(x, y_ret, y_hy, mods, mod_row, norm1, norm2, final_g, w_gate, w_out, w_fi, w_fo):
    B, L, D = x.shape
    T = MLP_ROWS
    n_tiles = B * L // T
    flat = lambda a: a.reshape(B * L, D)
    pre_tile = lambda i: jnp.minimum(i, n_tiles - 1)
    post_tile = lambda i: jnp.maximum(i - 1, 0)
    act = pl.BlockSpec((T, D), lambda i: (pre_tile(i), 0))
    y = pl.pallas_call(
        functools.partial(_mlp_kernel, n_tiles=n_tiles),
        grid=(n_tiles + 1,),
        in_specs=[act, act, act,
                  pl.BlockSpec((1, 1, N_MOD * D),
                               lambda i: (mod_row((pre_tile(i) * T) // L), 0, 0)),
                  pl.BlockSpec((1, 1, N_MOD * D),
                               lambda i: (mod_row((post_tile(i) * T) // L), 0, 0)),
                  _const_spec((1, D)), _const_spec((1, D)), _const_spec((1, D)),
                  _const_spec((D, N_GATE)),
                  _const_spec((D, D)),
                  pl.BlockSpec(memory_space=pl.ANY),
                  pl.BlockSpec(memory_space=pl.ANY)],
        out_specs=pl.BlockSpec((T, D), lambda i: (post_tile(i), 0)),
        out_shape=jax.ShapeDtypeStruct((B * L, D), F32),
        scratch_shapes=[pltpu.VMEM((2, T, D), F32), pltpu.VMEM((2, T, D), BF16),
                        pltpu.VMEM((D, 2 * D_FF), BF16), pltpu.VMEM((D_FF, D), BF16),
                        pltpu.SemaphoreType.DMA((2,))],
        compiler_params=_params(1),
        name=f"mlp{L}",
    )(flat(x), flat(y_ret), flat(y_hy), mods, mods, norm1, norm2, final_g, w_gate, w_out, w_fi,
      w_fo)
    return y.reshape(B, L, D)


def kernel(x_prompt, x_sample, state_ret_fwd, state_ret_bwd, c, c_ctx, norm1_g, norm2_g, w_ada,
           b_ada, w_in, ret_decay_fwd, ret_decay_bwd, hy_conv_w, hy_conv_b, hy_pos_w1, hy_pos_b1,
           hy_pos_w2, hy_pos_b2, hy_pos_w3, hy_sin_freq, hy_bias, w_ret_o, w_hy_o, w_out,
           w_ffn_in, w_ffn_out, final_g):
    assert w_in.shape[0] == 1, "single-layer configuration"
    nb_lat = x_sample.shape[0]
    l_ctx = x_prompt.shape[1]

    assert 1 + nb_lat <= MOD_ROWS
    norm1 = norm1_g[0][None, :]
    norm2 = norm2_g[0][None, :]
    fg = final_g[None, :]
    conv_w = hy_conv_w[0].reshape(1, 3 * N_HY)
    conv_b = hy_conv_b[0][None, :]
    filt_params = (hy_pos_w1[0], hy_pos_b1, hy_pos_w2[0], hy_pos_b2, hy_pos_w3[0], hy_sin_freq)

    n_hy_end = N_QKVG + N_HY
    w_in_parts = (slice(0, N_QKVG), slice(N_QKVG, n_hy_end), slice(n_hy_end, N_IN))
    groups = []
    for L in (l_ctx, x_sample.shape[1]):
        blk = min(HY_TBLK, L)
        fw, _ = _dft_mats(blk)
        z, rates = _filter_consts(L)
        groups.append((L, blk, jnp.asarray(z.T), jnp.asarray(rates), fw))
    (mods, fa_c, fb_c, fd_c, fa_l, fb_l, fd_l, w_qkvg, w_hy, w_gate, w_ret_o_b,
     w_hy_o_b) = _ada(c_ctx[None, :], c, w_ada[0], b_ada, filt_params, groups,
                      casts=[(w_in[0], w_in_parts), (w_ret_o[0], None), (w_hy_o[0], None)])

    def branches(x, filt, s0f, s0b, grid_w, emit_state):
        blk = min(HY_TBLK, x.shape[1])
        fw, bw = _dft_mats(blk)
        ret = (w_qkvg, ret_decay_fwd[0], ret_decay_bwd[0], s0f, s0b, w_ret_o_b, emit_state)
        hy = (w_hy, conv_w, conv_b, fw, bw, filt, hy_bias[0], w_hy_o_b, grid_w, blk)
        return ret, hy

    ctx_row = lambda b: 0
    lat_row = lambda b: b + 1
    ret, hy = branches(x_prompt, (fa_c, fb_c, fd_c), None, None, l_ctx, True)
    y_ret_c, s_f, s_b, y_hy_c, w_fi_b, w_fo_b, w_out_b = _mixer(
        x_prompt, mods, ctx_row, norm1, nb=CTX_SEQS, ret=ret, hy=hy,
        casts=[(w_ffn_in[0], None), (w_ffn_out[0], None), (w_out[0], None)])
    y_prompt = _mlp(x_prompt, y_ret_c, y_hy_c, mods, ctx_row, norm1, norm2, fg, w_gate, w_out_b,
                    w_fi_b, w_fo_b)
    ret, hy = branches(x_sample, (fa_l, fb_l, fd_l), state_ret_fwd[:, 0], state_ret_bwd[:, 0],
                       GRID_W, False)
    y_ret_l, = _mixer(x_sample, mods, lat_row, norm1, nb=1, ret=ret)
    y_hy_l, = _mixer(x_sample, mods, lat_row, norm1, nb=1, hy=hy)
    y_sample = _mlp(x_sample, y_ret_l, y_hy_l, mods, lat_row, norm1, norm2, fg, w_gate,
                    w_out_b, w_fi_b, w_fo_b)
    return (y_prompt, y_sample, s_f[:, None], s_b[:, None])
```

```python
import functools
import math

import numpy as np
import jax
import jax.numpy as jnp
from jax import lax
from jax.experimental import pallas as pl
from jax.experimental.pallas import tpu as pltpu

F32 = jnp.float32
BF16 = jnp.bfloat16

D_MODEL = 1024
RET_HEADS = 4
HEAD_DIM = 128
RET_W = RET_HEADS * HEAD_DIM
HY_W = 512
HY_ORDER = 2
HY_BANDS = 16
HY_EMB = 1 + 2 * HY_BANDS
HY_EMB_PAD = 40
HY_HIDDEN = 64
HY_FAST_DECAY = 0.3
HY_SLOW_DECAY = 1.5
HY_TARGET = 1e-2
D_FF = 2816
N_QKVG = 4 * RET_W
N_HY = 3 * HY_W
N_GATE = 2 * D_MODEL
N_IN = N_QKVG + N_HY + N_GATE
N_MOD = 6
MOD_ROWS = 8
EPS = 1e-6
GRID_W = 64
RET_CHUNK = 256
HY_CBLK = 256
HY_TBLK = 512
CTX_SEQS = 2
MLP_ROWS = 512
FF_CHUNK = 256
FILTER_ONE_STEP_LEN = 256
ADA_STEPS = 8
ADA_RING = 5
VMEM_LIMIT = 56 * 1024 * 1024


def _const_spec(shape):
    nd = len(shape)
    return pl.BlockSpec(shape, lambda *_: (0,) * nd, pipeline_mode=pl.Buffered(1))


def _params(n_axes):
    return pltpu.CompilerParams(dimension_semantics=("arbitrary",) * n_axes,
                                vmem_limit_bytes=VMEM_LIMIT)


def _modnorm(x, g, scale, shift):
    ms = jnp.mean(x * x, axis=-1, keepdims=True)
    return (x * lax.rsqrt(ms + EPS) * g) * (1.0 + scale) + shift


def _mod(mod, k):
    return mod[:, k * D_MODEL:(k + 1) * D_MODEL]


def _dot(a, b):
    return jnp.dot(a, b, preferred_element_type=F32)


def _cast_specs(casts, steps):
    in_specs, out_specs, out_shape, args = [], [], [], []
    for arr, col_slices in casts:
        rows, width = arr.shape
        rb = rows // steps
        assert rb * steps == rows and rb % 16 == 0
        in_specs.append(pl.BlockSpec((rb, width), lambda g: (g, 0)))
        args.append(arr)
        for cs in col_slices or (slice(0, width),):
            cols = cs.stop - cs.start
            out_specs.append(pl.BlockSpec((rb, cols), lambda g: (g, 0)))
            out_shape.append(jax.ShapeDtypeStruct((rows, cols), BF16))
    return in_specs, out_specs, out_shape, args


def _n_cast_outputs(col_slices_per_src):
    return sum(1 if s is None else len(s) for s in col_slices_per_src)


def _do_casts(col_slices_per_src, srcs, dsts):
    dsts = iter(dsts)
    for col_slices, src in zip(col_slices_per_src, srcs):
        if col_slices is None:
            next(dsts)[...] = src[...].astype(BF16)
        else:
            for cs in col_slices:
                next(dsts)[...] = src[:, cs].astype(BF16)


def _ada_kernel(*refs, casts, groups, n_steps):
    it = iter(refs)
    cctx_ref, c_ref, w_hbm, b_ref = next(it), next(it), next(it), next(it)
    mlp_refs = [next(it) for _ in range(6)]
    g_in = [[next(it) for _ in range(3)] for _ in groups]
    cast_srcs = [next(it) for _ in casts]
    o_ref = next(it)
    g_out = [[next(it) for _ in range(3)] for _ in groups]
    cast_dsts = [next(it) for _ in range(_n_cast_outputs(casts))]
    h_scr, cond_scr, w1_scr, w2_scr = next(it), next(it), next(it), next(it)
    g_scr = [[next(it) for _ in range(3)] for _ in groups]
    sem = next(it)
    wbuf, acc_scr, wsem = next(it), next(it), next(it)
    j = pl.program_id(0)
    kt = D_MODEL // n_steps

    def w_copy(t):
        slot = t % ADA_RING
        return pltpu.make_async_copy(w_hbm.at[pl.ds(t * kt, kt), :], wbuf.at[slot], wsem.at[slot])

    @pl.when(j == 0)
    def _():
        for t in range(ADA_RING - 1):
            w_copy(t).start()

    @pl.when(j + ADA_RING - 1 < n_steps)
    def _():
        w_copy(j + ADA_RING - 1).start()

    _do_casts(casts, cast_srcs, cast_dsts)
    nlat = c_ref.shape[0]
    cond_scr[...] = jnp.zeros_like(cond_scr)
    cond_scr[0:1, :] = cctx_ref[...]
    cond_scr[1:1 + nlat, :] = c_ref[...]
    c = cond_scr[:, pl.ds(pl.multiple_of(j * kt, kt), kt)]
    s = (c * jax.nn.sigmoid(c)).astype(BF16)
    w_copy(j).wait()
    part = _dot(s, wbuf[j % ADA_RING].astype(BF16))

    @pl.when(j == 0)
    def _():
        acc_scr[...] = part + b_ref[...]

    @pl.when(j > 0)
    def _():
        acc_scr[...] += part

    @pl.when(j == n_steps - 1)
    def _():
        res = acc_scr[...]
        for r in range(res.shape[0]):
            o_ref[r] = res[r:r + 1, :]

    def writeback(g, o):
        return [pltpu.make_async_copy(scr.at[o], out.at[o], sem.at[(g * HY_ORDER + o) * 3 + k])
                for k, (scr, out) in enumerate(zip(g_scr[g], g_out[g]))]

    step = 0
    for g, ((L, blk), (zt_ref, rate_ref, fw_ref)) in enumerate(zip(groups, g_in)):
        def mlp_job(zt_ref=zt_ref, L=L):
            h_scr[0:L, :] = _filter_mlp(zt_ref, *mlp_refs, w1_scr, w2_scr)

        def order_job(o, g=g, rate_ref=rate_ref, fw_ref=fw_ref, L=L, blk=blk):
            _filter_spectra(h_scr, o, rate_ref, fw_ref, *g_scr[g], L=L, b=blk)
            for cp in writeback(g, o):
                cp.start()

        def all_job(mlp_job=mlp_job, order_job=order_job):
            mlp_job()
            for o in range(HY_ORDER):
                order_job(o)

        if L <= FILTER_ONE_STEP_LEN:
            jobs = [all_job]
        else:
            jobs = [mlp_job] + [functools.partial(order_job, o) for o in range(HY_ORDER)]
        for job in jobs:
            pl.when(pl.program_id(0) == step)(job)
            step += 1

    @pl.when(pl.program_id(0) == n_steps - 1)
    def _():
        for g in range(len(groups)):
            for o in range(HY_ORDER):
                for cp in writeback(g, o):
                    cp.wait()


def _ada(c_ctx, c, w, b, filt_params, groups, casts=()):
    n = w.shape[1]
    steps = ADA_STEPS
    kt = D_MODEL // steps
    c_in, c_out, c_shape, c_args = _cast_specs(casts, steps)
    in_specs = [_const_spec(c_ctx.shape), _const_spec(c.shape),
                pl.BlockSpec(memory_space=pl.ANY), _const_spec((1, n))]
    args = [c_ctx, c, w, b]
    for p in filt_params:
        in_specs.append(_const_spec(p.shape))
        args.append(p)
    out_specs = [pl.BlockSpec((MOD_ROWS, 1, n), lambda j: (0, 0, 0))]
    out_shape = [jax.ShapeDtypeStruct((MOD_ROWS, 1, n), F32)]
    max_len = 8
    spectra_scratch = []
    for L, blk, *consts in groups:
        nd = 2 * (L // blk) - 1
        max_len = max(max_len, L)
        for cst in consts:
            in_specs.append(_const_spec(cst.shape))
            args.append(cst)
        for shp in ((HY_ORDER, nd, blk, HY_W), (HY_ORDER, nd, blk, HY_W), (HY_ORDER, nd, 8, HY_W)):
            out_specs.append(pl.BlockSpec(memory_space=pl.ANY))
            out_shape.append(jax.ShapeDtypeStruct(shp, F32))
            spectra_scratch.append(pltpu.VMEM(shp, F32))
    n_jobs = sum(1 if L <= FILTER_ONE_STEP_LEN else 1 + HY_ORDER for L, *_ in groups)
    assert n_jobs <= steps
    return pl.pallas_call(
        functools.partial(_ada_kernel, casts=tuple(cs for _, cs in casts),
                          groups=tuple((L, blk) for L, blk, *_ in groups), n_steps=steps),
        grid=(steps,),
        in_specs=in_specs + c_in,
        out_specs=out_specs + c_out,
        out_shape=out_shape + c_shape,
        scratch_shapes=[pltpu.VMEM((max_len, HY_ORDER * 2 * HY_W), F32),
                        pltpu.VMEM((MOD_ROWS, D_MODEL), F32),
                        pltpu.VMEM((HY_EMB_PAD, HY_HIDDEN), F32),
                        pltpu.VMEM((HY_HIDDEN + 8, HY_HIDDEN), F32)]
        + spectra_scratch + [pltpu.SemaphoreType.DMA((len(groups) * HY_ORDER * 3,)),
                             pltpu.VMEM((ADA_RING, kt, n), F32), pltpu.VMEM((MOD_ROWS, n), F32),
                             pltpu.SemaphoreType.DMA((ADA_RING,))],
        compiler_params=_params(1),
        name="ada",
    )(*args, *c_args)


@functools.lru_cache(maxsize=None)
def _dft_mats(L):
    n = 2 * L
    t = np.arange(L, dtype=np.int64)
    f = np.arange(L, dtype=np.int64)
    ang = 2.0 * np.pi * ((f[:, None] * t[None, :]) % n).astype(np.float64) / n
    cos = np.cos(ang)
    sin = np.sin(ang)
    nyq = np.where(t % 2 == 0, 1.0, -1.0)
    fwd = np.concatenate([cos, -sin], axis=0)
    fwd[L] = nyq
    wre = np.full((L,), 2.0 / n)
    wre[0] = 1.0 / n
    inv = np.concatenate([cos.T * wre[None, :], -sin.T * (2.0 / n)], axis=1)
    inv[:, L] = nyq / n
    return jnp.asarray(fwd, dtype=BF16), jnp.asarray(inv, dtype=BF16)


@functools.lru_cache(maxsize=None)
def _filter_consts(L):
    t = np.linspace(0.0, 1.0, L)[:, None]
    ang = 2.0 * np.pi * np.arange(L, dtype=np.float64)[:, None] / L
    bands = np.linspace(1e-4, HY_BANDS - 1, HY_BANDS)[None]
    z = np.concatenate([t, np.cos(bands * ang), -np.sin(bands * ang)], axis=-1)
    z = np.pad(z, ((0, 0), (0, HY_EMB_PAD - HY_EMB)))
    z[:, HY_EMB] = 1.0
    max_decay = math.log(HY_TARGET) / HY_FAST_DECAY
    min_decay = math.log(HY_TARGET) / HY_SLOW_DECAY
    rates = np.abs(np.linspace(min_decay, max_decay, HY_W))[None, :]
    return np.asarray(z, np.float32), np.asarray(rates, np.float32)


def _filter_mlp(zt_ref, w1_ref, b1_ref, w2_ref, b2_ref, w3_ref, fr_ref, w1_scr, w2_scr):
    hi = lax.Precision.HIGHEST
    tdims = (((0,), (0,)), ((), ()))
    w1_scr[...] = jnp.zeros_like(w1_scr)
    w1_scr[0:HY_EMB, :] = w1_ref[...]
    w1_scr[HY_EMB:HY_EMB + 1, :] = b1_ref[...]
    w2_scr[...] = jnp.zeros_like(w2_scr)
    w2_scr[0:HY_HIDDEN, :] = w2_ref[...]
    w2_scr[HY_HIDDEN:HY_HIDDEN + 1, :] = b2_ref[...]
    fr = fr_ref[...]
    zt = zt_ref[...]
    h1 = jnp.sin(lax.dot_general(w1_scr[...] * fr, zt, tdims, precision=hi,
                                 preferred_element_type=F32))
    h1 = jnp.concatenate([h1, jnp.ones((8, zt.shape[1]), F32)], axis=0)
    h2 = jnp.sin(lax.dot_general(w2_scr[...] * fr, h1, tdims, precision=hi,
                                 preferred_element_type=F32))
    h2_hi = h2.astype(BF16)
    h2_lo = (h2 - h2_hi.astype(F32)).astype(BF16)
    w3 = w3_ref[...]
    w3_hi = w3.astype(BF16)
    w3_lo = (w3 - w3_hi.astype(F32)).astype(BF16)
    lhs = jnp.concatenate([h2_hi, h2_lo, h2_hi, jnp.zeros_like(h2_hi)], axis=0)
    rhs = jnp.concatenate([w3_hi, w3_hi, w3_lo, jnp.zeros_like(w3_hi)], axis=0)
    h = lax.dot_general(lhs, rhs, tdims, preferred_element_type=F32)
    return h


def _filter_spectra(h_ref, o, rate_ref, fw_ref, oa_ref, ob_ref, od_ref, *, L, b):
    m = L // b
    row_l = lax.broadcasted_iota(jnp.int32, (L, HY_W), 0)
    row_b = lax.broadcasted_iota(jnp.int32, (b, HY_W), 0)
    t = row_l.astype(F32) * (1.0 / (L - 1))
    win = jnp.exp(-t * rate_ref[...])
    sg = jnp.where(row_b % 2 == 0, 1.0, -1.0)
    row0_l = row_l == 0
    row0_b = row_b == 0
    row0_8 = lax.broadcasted_iota(jnp.int32, (8, HY_W), 0) == 0
    base = o * 2 * HY_W
    fwd = h_ref[0:L, base:base + HY_W] * win
    bwd = jnp.where(row0_l, 0.0, h_ref[0:L, base + HY_W:base + 2 * HY_W] * win)
    nrm = (jnp.sum(jnp.abs(fwd), axis=0, keepdims=True)
           + jnp.sum(jnp.abs(bwd), axis=0, keepdims=True))
    inv = 1.0 / nrm
    fn = fwd * inv
    bn = bwd * inv
    xr, xn, xi, wr, wn, wi = [], [], [], [], [], []
    for r in range(m):
        p = _dot(fw_ref[...], fn[r * b:(r + 1) * b].astype(BF16))
        q = _dot(fw_ref[...], bn[r * b:(r + 1) * b].astype(BF16))
        xr.append(p[0:b])
        xn.append(p[b:b + 1])
        xi.append(jnp.where(row0_b, 0.0, p[b:2 * b]))
        wr.append(q[0:b])
        wn.append(q[b:b + 1])
        wi.append(jnp.where(row0_b, 0.0, -q[b:2 * b]))

    def emit(d, ka, kn, kb):
        oa_ref[o, d + m - 1] = ka
        ob_ref[o, d + m - 1] = kb
        od_ref[o, d + m - 1] = jnp.where(row0_8, kn, ka[0:8])

    emit(0, xr[0] + wr[0], xn[0] + wn[0], xi[0] + wi[0])
    for d in range(1, m):
        f0 = fn[(d - 1) * b:(d - 1) * b + 1]
        b0 = bn[(d - 1) * b:(d - 1) * b + 1]
        emit(d, xr[d] + sg * (xr[d - 1] - f0), xn[d] + (xn[d - 1] - f0),
             xi[d] + sg * xi[d - 1])
        emit(-d, wr[d] + sg * (wr[d - 1] - b0), wn[d] + (wn[d - 1] - b0),
             wi[d] + sg * wi[d - 1])


def _ret_init(decf_ref, decb_ref, mask_scr, vec_scr, cd_scr, C):
    H, E = RET_HEADS, HEAD_DIM
    scale = float(E) ** -0.5

    @pl.when(pl.program_id(0) == 0)
    def _():
        dec = jnp.concatenate([jnp.full((1, C), ref[h], F32)
                               for ref in (decf_ref, decb_ref) for h in range(H)], axis=0)
        lg = jnp.log(jax.nn.sigmoid(dec))
        cd_scr[...] = jnp.exp(float(C) * lg[:, 0:E])
        ii = lax.broadcasted_iota(jnp.int32, (C, C), 0)
        jj = lax.broadcasted_iota(jnp.int32, (C, C), 1)
        rel = (ii - jj).astype(F32)
        ri = lax.broadcasted_iota(jnp.int32, (C, E), 0).astype(F32)
        for h in range(H):
            lf = lg[h:h + 1, :]
            lb = lg[H + h:H + h + 1, :]
            mf = jnp.where(rel >= 0, jnp.exp(jnp.maximum(rel, 0.0) * lf), 0.0)
            mb = jnp.where(rel <= 0, jnp.exp(jnp.maximum(-rel, 0.0) * lb), 0.0)
            mask_scr[h] = scale * (mf + mb)
            lfe, lbe = lf[:, 0:E], lb[:, 0:E]
            vec_scr[h, 0] = jnp.exp((ri + 1.0) * lfe)
            vec_scr[h, 1] = jnp.exp((float(C) - ri) * lbe)
            vec_scr[h, 2] = scale * jnp.exp((float(C) - 1.0 - ri) * lfe)
            vec_scr[h, 3] = scale * jnp.exp(ri * lbe)


def _ret_core(hn, w_ref, s0f_ref, s0b_ref, wo_ref, y_ref, sf_ref, sb_ref, mask_scr, vec_scr,
              cd_scr, g_scr, *, L, C, nb, has_init, emit_state):
    n = L // C
    H, E = RET_HEADS, HEAD_DIM
    tdims = (((0,), (0,)), ((), ()))
    ndims = (((1,), (1,)), ((), ()))
    chains = [(s, h) for s in range(nb) for h in range(H)]
    rows = [slice(c * C, (c + 1) * C) for c in range(n)]
    qkvg = [_dot(hn[s], w_ref[...]) for s in range(nb)]

    def cols(s, part, h):
        return qkvg[s][:, part * RET_W + h * E:part * RET_W + (h + 1) * E]

    qb = [cols(s, 0, h).astype(BF16) for s, h in chains]
    kf = [cols(s, 1, h) for s, h in chains]
    kb = [k.astype(BF16) for k in kf]
    vb = [cols(s, 2, h).astype(BF16) for s, h in chains]
    att = [[lax.dot_general(qb[i][r], kb[i][r], ndims, preferred_element_type=F32) for r in rows]
           for i in range(len(chains))]
    prob = [[(att[i][c] * mask_scr[h]).astype(BF16) for c in range(n)]
            for i, (s, h) in enumerate(chains)]
    out = [[_dot(prob[i][c], vb[i][rows[c]]) for c in range(n)] for i in range(len(chains))]
    kv = []
    for i, (s, h) in enumerate(chains):
        dk2 = jnp.concatenate([vec_scr[h, 2], vec_scr[h, 3]], axis=1)
        per_c = []
        for r in rows:
            k2 = (jnp.concatenate([kf[i][r], kf[i][r]], axis=1) * dk2).astype(BF16)
            per_c.append(lax.dot_general(k2, vb[i][r], tdims, preferred_element_type=F32))
        kv.append(per_c)
    for i, (s, h) in enumerate(chains):
        cdf = cd_scr[h:h + 1, :]
        cdb = cd_scr[H + h:H + h + 1, :]
        sf_in, sb_in = [None] * n, [None] * n
        st = s0f_ref[s, h] if has_init else None
        for c in range(n):
            sf_in[c] = st
            kvc = kv[i][c][0:E]
            st = kvc if st is None else st * cdf + kvc
        if emit_state:
            sf_ref[s, h] = st
        st = s0b_ref[s, h] if has_init else None
        for c in range(n - 1, -1, -1):
            sb_in[c] = st
            kvc = kv[i][c][E:2 * E]
            st = kvc if st is None else st * cdb + kvc
        if emit_state:
            sb_ref[s, h] = st
        for c in range(n):
            if sf_in[c] is not None and sb_in[c] is not None:
                s2 = jnp.concatenate([sf_in[c], sb_in[c]], axis=1).astype(BF16)
                inter = _dot(qb[i][rows[c]], s2)
                out[i][c] = (out[i][c] + inter[:, 0:E] * vec_scr[h, 0]
                             + inter[:, E:2 * E] * vec_scr[h, 1])
            elif sf_in[c] is not None:
                out[i][c] = (out[i][c]
                             + _dot(qb[i][rows[c]], sf_in[c].astype(BF16)) * vec_scr[h, 0])
            elif sb_in[c] is not None:
                out[i][c] = (out[i][c]
                             + _dot(qb[i][rows[c]], sb_in[c].astype(BF16)) * vec_scr[h, 1])
    for i, (s, h) in enumerate(chains):
        for c in range(n):
            o = out[i][c]
            mu = jnp.mean(o, axis=-1, keepdims=True)
            d = o - mu
            var = jnp.mean(d * d, axis=-1, keepdims=True)
            on = d * lax.rsqrt(var + EPS)
            gg = cols(s, 3, h)[rows[c]]
            g_scr[s, rows[c], h * E:(h + 1) * E] = (gg * jax.nn.sigmoid(gg) * on).astype(BF16)
    for s in range(nb):
        y_ref[s] = _dot(g_scr[s], wo_ref[...])


def _hy_core(hn, w_ref, cw_ref, cb_ref, fw_ref, bw_ref, fa_ref, fb_ref, fd_ref, hb_ref, wo_ref,
             y_ref, *, L, W, b, nb):
    m = L // b
    CB = HY_CBLK
    nblk = HY_W // CB
    pos = lax.broadcasted_iota(jnp.int32, (L, CB), 0) % W
    first = pos == 0
    last = pos == W - 1
    chains = [(s, blk) for s in range(nb) for blk in range(nblk)]

    def short_conv(s, base, blk):
        cs = slice(base + blk * CB, base + (blk + 1) * CB)
        ug = _dot(hn[s], w_ref[:, cs])
        prev = jnp.where(first, 0.0, pltpu.roll(ug, 1, axis=0))
        nxt = jnp.where(last, 0.0, pltpu.roll(ug, L - 1, axis=0))
        taps = [cw_ref[:, t * N_HY + cs.start:t * N_HY + cs.stop] for t in range(3)]
        u = prev * taps[0] + ug * taps[1] + nxt * taps[2] + cb_ref[:, cs]
        return [u[j * b:(j + 1) * b] for j in range(m)]

    def long_conv(sigs, o):
        spec = [[_dot(fw_ref[...], sj.astype(BF16)) for sj in sig] for sig in sigs]
        prods = []
        for (s, blk), sp in zip(chains, spec):
            cs = slice(blk * CB, (blk + 1) * CB)
            per_i = []
            for i in range(m):
                yre = yim = yim8 = None
                for j in range(m):
                    d = i - j + m - 1
                    sre, sim = sp[j][0:b], sp[j][b:2 * b]
                    ka, kb = fa_ref[o, d, :, cs], fb_ref[o, d, :, cs]
                    tre = sre * ka - sim * kb
                    tim = sre * kb + sim * ka
                    t8 = sre[0:8] * kb[0:8] + sim[0:8] * fd_ref[o, d, :, cs]
                    yre = tre if yre is None else yre + tre
                    yim = tim if yim is None else yim + tim
                    yim8 = t8 if yim8 is None else yim8 + t8
                yim = jnp.concatenate([yim8, yim[8:]], axis=0)
                per_i.append((yre.astype(BF16), yim.astype(BF16)))
            prods.append(per_i)
        return [[_dot(bw_ref[:, 0:b], yre) + _dot(bw_ref[:, b:2 * b], yim) for yre, yim in per_i]
                for per_i in prods]

    hv = [short_conv(s, 0, blk) for s, blk in chains]
    hx1 = [short_conv(s, HY_W, blk) for s, blk in chains]
    hx2 = [short_conv(s, 2 * HY_W, blk) for s, blk in chains]

    def gate(hx, conv, sig, o):
        out = []
        for (s, blk), hxc, cc, sc in zip(chains, hx, conv, sig):
            bias = hb_ref[o:o + 1, blk * CB:(blk + 1) * CB]
            out.append([hxc[i] * (cc[i] + sc[i] * bias) for i in range(m)])
        return out

    z = gate(hx1, long_conv(hv, 0), hv, 0)
    z = gate(hx2, long_conv(z, 1), z, 1)
    for s in range(nb):
        for i in range(m):
            acc = None
            for blk in range(nblk):
                zc = z[chains.index((s, blk))][i].astype(BF16)
                part = _dot(zc, wo_ref[blk * CB:(blk + 1) * CB, :])
                acc = part if acc is None else acc + part
            y_ref[s, i * b:(i + 1) * b, :] = acc


def _mix_kernel(*refs, L, C, W, b, nb, do_ret, do_hy, has_init, emit_state, casts):
    it = iter(refs)
    x_ref, mod_ref, n1_ref = next(it), next(it), next(it)
    s0f_ref = s0b_ref = sf_ref = sb_ref = None
    if do_ret:
        wq_ref, decf_ref, decb_ref = next(it), next(it), next(it)
        if has_init:
            s0f_ref, s0b_ref = next(it), next(it)
        wo_ret_ref = next(it)
    if do_hy:
        hy_in = [next(it) for _ in range(10)]
    cast_srcs = [next(it) for _ in casts]
    if do_ret:
        y_ret_ref = next(it)
        if emit_state:
            sf_ref, sb_ref = next(it), next(it)
    if do_hy:
        y_hy_ref = next(it)
    cast_dsts = [next(it) for _ in range(_n_cast_outputs(casts))]
    if do_ret:
        ret_scr = [next(it) for _ in range(4)]
        _ret_init(decf_ref, decb_ref, ret_scr[0], ret_scr[1], ret_scr[2], C)
    _do_casts(casts, cast_srcs, cast_dsts)
    mod = mod_ref[0]
    hn = [_modnorm(x_ref[s], n1_ref[...], _mod(mod, 1), _mod(mod, 0)).astype(BF16)
          for s in range(nb)]
    if do_hy:
        _hy_core(hn, *hy_in, y_hy_ref, L=L, W=W, b=b, nb=nb)
    if do_ret:
        _ret_core(hn, wq_ref, s0f_ref, s0b_ref, wo_ret_ref, y_ret_ref, sf_ref, sb_ref, *ret_scr,
                  L=L, C=C, nb=nb, has_init=has_init, emit_state=emit_state)


def _mixer(x, mods, mod_row, norm1, *, nb, ret=None, hy=None, casts=()):
    B, L, D = x.shape
    H, E = RET_HEADS, HEAD_DIM
    C = min(RET_CHUNK, L)
    seq_spec = pl.BlockSpec((nb, L, D), lambda g: (g, 0, 0))
    in_specs = [seq_spec,
                pl.BlockSpec((1, 1, N_MOD * D), lambda g: (mod_row(g * nb), 0, 0)),
                _const_spec((1, D))]
    args = [x, mods, norm1]
    out_specs, out_shape, scratch = [], [], []
    has_init = emit_state = False
    W = b = None
    if ret is not None:
        w_qkvg, dec_f, dec_b, s0f, s0b, w_o, emit_state = ret
        has_init = s0f is not None
        st_spec = pl.BlockSpec((nb, H, E, E), lambda g: (g, 0, 0, 0))
        smem = pl.BlockSpec(memory_space=pltpu.SMEM)
        in_specs += [_const_spec((D, N_QKVG)), smem, smem]
        args += [w_qkvg, dec_f, dec_b]
        if has_init:
            in_specs += [st_spec, st_spec]
            args += [s0f, s0b]
        in_specs.append(_const_spec((RET_W, D)))
        args.append(w_o)
        out_specs.append(seq_spec)
        out_shape.append(jax.ShapeDtypeStruct((B, L, D), F32))
        if emit_state:
            out_specs += [st_spec, st_spec]
            out_shape += [jax.ShapeDtypeStruct((B, H, E, E), F32)] * 2
        scratch = [pltpu.VMEM((H, C, C), F32), pltpu.VMEM((H, 4, C, E), F32),
                   pltpu.VMEM((8, E), F32), pltpu.VMEM((nb, L, RET_W), BF16)]
    if hy is not None:
        w_hy, conv_w, conv_b, fw, bw, (fa, fb, fd), hy_bias, w_o, W, b = hy
        nd = fa.shape[1]
        in_specs += [_const_spec((D, N_HY)), _const_spec((1, 3 * N_HY)), _const_spec((1, N_HY)),
                     _const_spec((2 * b, b)), _const_spec((b, 2 * b)),
                     _const_spec((HY_ORDER, nd, b, HY_W)), _const_spec((HY_ORDER, nd, b, HY_W)),
                     _const_spec((HY_ORDER, nd, 8, HY_W)), _const_spec((HY_ORDER, HY_W)),
                     _const_spec((HY_W, D))]
        args += [w_hy, conv_w, conv_b, fw, bw, fa, fb, fd, hy_bias, w_o]
        out_specs.append(seq_spec)
        out_shape.append(jax.ShapeDtypeStruct((B, L, D), F32))
    c_in, c_out, c_shape, c_args = _cast_specs(casts, B // nb)
    name = ("ret" if ret is not None else "") + ("hy" if hy is not None else "")
    return pl.pallas_call(
        functools.partial(_mix_kernel, L=L, C=C, W=W, b=b, nb=nb, do_ret=ret is not None,
                          do_hy=hy is not None, has_init=has_init, emit_state=emit_state,
                          casts=tuple(cs for _, cs in casts)),
        grid=(B // nb,),
        in_specs=in_specs + c_in,
        out_specs=out_specs + c_out,
        out_shape=out_shape + c_shape,
        scratch_shapes=scratch,
        compiler_params=_params(1),
        name=f"{name}{L}",
    )(*args, *c_args)


def _mlp_kernel(x_ref, yr_ref, yh_ref, modp_ref, modq_ref, n1_ref, n2_ref, fg_ref, wg_ref,
                wout_ref, wfi_hbm, wfo_hbm, y_ref, x1_scr, h2_scr, wfi_ref, wfo_ref, sem, *,
                n_tiles):
    i = pl.program_id(0)
    wr = i % 2
    rd = 1 - wr
    nq = N_GATE // 4

    def pre_stages():
        mp = modp_ref[0]
        st = {}

        def p1():
            st["x"] = x_ref[...]
            st["hn"] = _modnorm(st["x"], n1_ref[...], _mod(mp, 1), _mod(mp, 0)).astype(BF16)

        def p2(q):
            def f():
                st["g%d" % q] = _dot(st["hn"], wg_ref[:, q * nq:(q + 1) * nq])
            return f

        def p3(h):
            def f():
                cs = slice(h * nq, (h + 1) * nq)
                st["mix%d" % h] = (jax.nn.sigmoid(st["g%d" % h]) * yr_ref[:, cs]
                                   + jax.nn.sigmoid(st["g%d" % (2 + h)]) * yh_ref[:, cs]
                                   ).astype(BF16)
            return f

        def p4():
            upd = (_dot(st["mix0"], wout_ref[0:nq, :]) + _dot(st["mix1"], wout_ref[nq:2 * nq, :]))
            st["x1"] = st["x"] + _mod(mp, 2) * upd

        def p5():
            x1_scr[wr] = st["x1"]
            h2_scr[wr] = _modnorm(st["x1"], n2_ref[...], _mod(mp, 4), _mod(mp, 3)).astype(BF16)

        return [p1, p2(0), p2(1), p2(2), p2(3), p3(0), p3(1), p4, p5]

    def ffn_stages():
        mq = modq_ref[0]
        st = {"acc": None}

        def f(j):
            def g():
                cs = slice(j * FF_CHUNK, (j + 1) * FF_CHUNK)
                h2 = h2_scr[rd]
                a = _dot(h2, wfi_ref[:, cs])
                b = _dot(h2, wfi_ref[:, D_FF + j * FF_CHUNK:D_FF + (j + 1) * FF_CHUNK])
                ff = (a * jax.nn.sigmoid(a) * b).astype(BF16)
                part = _dot(ff, wfo_ref[cs, :])
                st["acc"] = part if st["acc"] is None else st["acc"] + part
            return g

        def e():
            x2 = x1_scr[rd] + _mod(mq, 5) * st["acc"]
            ms = jnp.mean(x2 * x2, axis=-1, keepdims=True)
            y_ref[...] = x2 * lax.rsqrt(ms + EPS) * fg_ref[...]

        return [f(j) for j in range(D_FF // FF_CHUNK)] + [e]

    @pl.when(i == 0)
    def _():
        copies = [pltpu.make_async_copy(wfi_hbm, wfi_ref, sem.at[0]),
                  pltpu.make_async_copy(wfo_hbm, wfo_ref, sem.at[1])]
        for cp in copies:
            cp.start()
        for stage in pre_stages():
            stage()
        for cp in copies:
            cp.wait()

    @pl.when(jnp.logical_and(i > 0, i < n_tiles))
    def _():
        pre, ffn = pre_stages(), ffn_stages()
        order = []
        while pre or ffn:
            if ffn:
                order.append(ffn.pop(0))
            if pre:
                order.append(pre.pop(0))
        for stage in order:
            stage()

    @pl.when(i == n_tiles)
    def _():
        for stage in ffn_stages():
            stage()


def _mlp(x, y_ret, y_hy, mods, mod_row, norm1, norm2, final_g, w_gate, w_out, w_fi, w_fo):
    B, L, D = x.shape
    T = MLP_ROWS
    n_tiles = B * L // T
    flat = lambda a: a.reshape(B * L, D)
    pre_tile = lambda i: jnp.minimum(i, n_tiles - 1)
    post_tile = lambda i: jnp.maximum(i - 1, 0)
    act = pl.BlockSpec((T, D), lambda i: (pre_tile(i), 0))
    y = pl.pallas_call(
        functools.partial(_mlp_kernel, n_tiles=n_tiles),
        grid=(n_tiles + 1,),
        in_specs=[act, act, act,
                  pl.BlockSpec((1, 1, N_MOD * D),
                               lambda i: (mod_row((pre_tile(i) * T) // L), 0, 0)),
                  pl.BlockSpec((1, 1, N_MOD * D),
                               lambda i: (mod_row((post_tile(i) * T) // L), 0, 0)),
                  _const_spec((1, D)), _const_spec((1, D)), _const_spec((1, D)),
                  _const_spec((D, N_GATE)),
                  _const_spec((D, D)),
                  pl.BlockSpec(memory_space=pl.ANY),
                  pl.BlockSpec(memory_space=pl.ANY)],
        out_specs=pl.BlockSpec((T, D), lambda i: (post_tile(i), 0)),
        out_shape=jax.ShapeDtypeStruct((B * L, D), F32),
        scratch_shapes=[pltpu.VMEM((2, T, D), F32), pltpu.VMEM((2, T, D), BF16),
                        pltpu.VMEM((D, 2 * D_FF), BF16), pltpu.VMEM((D_FF, D), BF16),
                        pltpu.SemaphoreType.DMA((2,))],
        compiler_params=_params(1),
        name=f"mlp{L}",
    )(flat(x), flat(y_ret), flat(y_hy), mods, mods, norm1, norm2, final_g, w_gate, w_out, w_fi,
      w_fo)
    return y.reshape(B, L, D)


def kernel(x_prompt, x_sample, state_ret_fwd, state_ret_bwd, c, c_ctx, norm1_g, norm2_g, w_ada,
           b_ada, w_in, ret_decay_fwd, ret_decay_bwd, hy_conv_w, hy_conv_b, hy_pos_w1, hy_pos_b1,
           hy_pos_w2, hy_pos_b2, hy_pos_w3, hy_sin_freq, hy_bias, w_ret_o, w_hy_o, w_out,
           w_ffn_in, w_ffn_out, final_g):
    assert w_in.shape[0] == 1, "single-layer configuration"
    nb_lat = x_sample.shape[0]
    l_ctx = x_prompt.shape[1]

    assert 1 + nb_lat <= MOD_ROWS
    norm1 = norm1_g[0][None, :]
    norm2 = norm2_g[0][None, :]
    fg = final_g[None, :]
    conv_w = hy_conv_w[0].reshape(1, 3 * N_HY)
    conv_b = hy_conv_b[0][None, :]
    filt_params = (hy_pos_w1[0], hy_pos_b1, hy_pos_w2[0], hy_pos_b2, hy_pos_w3[0], hy_sin_freq)

    n_hy_end = N_QKVG + N_HY
    w_in_parts = (slice(0, N_QKVG), slice(N_QKVG, n_hy_end), slice(n_hy_end, N_IN))
    groups = []
    for L in (l_ctx, x_sample.shape[1]):
        blk = min(HY_TBLK, L)
        fw, _ = _dft_mats(blk)
        z, rates = _filter_consts(L)
        groups.append((L, blk, jnp.asarray(z.T), jnp.asarray(rates), fw))
    (mods, fa_c, fb_c, fd_c, fa_l, fb_l, fd_l, w_qkvg, w_hy, w_gate, w_ret_o_b,
     w_hy_o_b) = _ada(c_ctx[None, :], c, w_ada[0], b_ada, filt_params, groups,
                      casts=[(w_in[0], w_in_parts), (w_ret_o[0], None), (w_hy_o[0], None)])

    def branches(x, filt, s0f, s0b, grid_w, emit_state):
        blk = min(HY_TBLK, x.shape[1])
        fw, bw = _dft_mats(blk)
        ret = (w_qkvg, ret_decay_fwd[0], ret_decay_bwd[0], s0f, s0b, w_ret_o_b, emit_state)
        hy = (w_hy, conv_w, conv_b, fw, bw, filt, hy_bias[0], w_hy_o_b, grid_w, blk)
        return ret, hy

    ctx_row = lambda b: 0
    lat_row = lambda b: b + 1
    ret, hy = branches(x_prompt, (fa_c, fb_c, fd_c), None, None, l_ctx, True)
    y_ret_c, s_f, s_b, y_hy_c, w_fi_b, w_fo_b, w_out_b = _mixer(
        x_prompt, mods, ctx_row, norm1, nb=CTX_SEQS, ret=ret, hy=hy,
        casts=[(w_ffn_in[0], None), (w_ffn_out[0], None), (w_out[0], None)])
    y_prompt = _mlp(x_prompt, y_ret_c, y_hy_c, mods, ctx_row, norm1, norm2, fg, w_gate, w_out_b,
                    w_fi_b, w_fo_b)
    ret, hy = branches(x_sample, (fa_l, fb_l, fd_l), state_ret_fwd[:, 0], state_ret_bwd[:, 0],
                       GRID_W, False)
    y_ret_l, = _mixer(x_sample, mods, lat_row, norm1, nb=1, ret=ret)
    y_hy_l, = _mixer(x_sample, mods, lat_row, norm1, nb=1, hy=hy)
    y_sample = _mlp(x_sample, y_ret_l, y_hy_l, mods, lat_row, norm1, norm2, fg, w_gate,
                    w_out_b, w_fi_b, w_fo_b)
    return (y_prompt, y_sample, s_f[:, None], s_b[:, None])
```
